```python
import math
import functools
import jax
import jax.numpy as jnp
from jax import lax
import numpy as np

D_MODEL = 1024
BATCH = 2
SEQ = 8192
DEPTH = 1
DEC_BATCH = 32
DEC_SEQ = 4
PAST_LEN = 16384
PAGE_SIZE = 128

HEAD_DIM = 64
N_HEADS_ATTN = 8
ATTN_WIDTH = N_HEADS_ATTN * HEAD_DIM
DILATED_BRANCHES = ((128, 1), (512, 4), (2048, 16))
MAX_WINDOW = 2048
ATTN_BLOCK = 128
N_HEADS_DN = 8
DN_DK = 64
DN_DV = 64
DN_QK = N_HEADS_DN * DN_DK
DN_V = N_HEADS_DN * DN_DV
CONV_WIDTH = 4
CONV_CH = 2 * DN_QK + DN_V
DN_CHUNK = 64
MIX_WIDTH = ATTN_WIDTH + DN_V
IN_SPLITS = (ATTN_WIDTH, ATTN_WIDTH, ATTN_WIDTH, CONV_CH, DN_V, N_HEADS_DN, N_HEADS_DN)
IN_COLS = 3 * ATTN_WIDTH + CONV_CH + DN_V + 2 * N_HEADS_DN
N_EXPERTS = 64
N_EXPERT_GROUPS = 8
TOPK_GROUPS = 4
TOP_K = 8
EXPERT_DIM = 256
SHARED_DIM = 256
ROUTED_SCALE = 2.5
MOE_BLOCK = 64
PLE_DIM = 256
LN_EPS = 1e-5
NORM_EPS = 1e-6
DEEPNORM_ALPHA = (2.0 * DEPTH) ** 0.25
DEEPNORM_BETA = (8.0 * DEPTH) ** -0.25

kernel_name = 'hymba_dilated_deltanet_moe_step'


def layer_norm(x, g, b):
    xf = x.astype(jnp.float32)
    mu = xf.mean(-1, keepdims=True)
    var = jnp.square(xf - mu).mean(-1, keepdims=True)
    return ((xf - mu) * lax.rsqrt(var + LN_EPS) * g.astype(jnp.float32) + b.astype(jnp.float32)).astype(x.dtype)


def alibi_slopes():
    return jnp.asarray([2.0 ** (-8.0 * (h + 1) / N_HEADS_ATTN) for h in range(N_HEADS_ATTN)], jnp.float32)


def window_band_attention(q, k, v, n_back, slope_step):
    B, R, L, H, Dh = q.shape
    nb = -(-L // ATTN_BLOCK)
    pad = nb * ATTN_BLOCK - L
    padw = ((0, 0), (0, 0), (0, pad), (0, 0), (0, 0))
    qb, kb, vb = (jnp.pad(t, padw).reshape(B, R, nb, ATTN_BLOCK, H, Dh) for t in (q, k, v))

    def band(t):
        prev = jnp.pad(t[:, :, :-1], ((0, 0), (0, 0), (1, 0), (0, 0), (0, 0), (0, 0)))
        return jnp.concatenate([prev, t], axis=3)

    kk, vv = band(kb), band(vb)
    s = jnp.einsum('brnqhd,brnkhd->brnhqk', qb, kk, preferred_element_type=jnp.float32) * (Dh ** -0.5)
    qi = jnp.arange(ATTN_BLOCK)[:, None]
    ki = jnp.arange(2 * ATTN_BLOCK)[None, :]
    dist = qi + ATTN_BLOCK - ki
    blk = jnp.arange(nb)[:, None, None]
    valid = (dist >= 0) & (dist <= n_back) & (blk * ATTN_BLOCK - ATTN_BLOCK + ki >= 0)
    s = s - slope_step[:, None, None] * dist.astype(jnp.float32)
    s = jnp.where(valid[:, None], s, -jnp.inf)
    m = s.max(-1, keepdims=True)
    pexp = jnp.exp(s - m)
    den = pexp.sum(-1, keepdims=True)
    o = jnp.einsum('brnhqk,brnkhd->brnqhd', pexp / den, vv.astype(jnp.float32))
    lse = (m + jnp.log(den))[..., 0]
    o = o.reshape(B, R, nb * ATTN_BLOCK, H, Dh)[:, :, :L]
    lse = lse.transpose(0, 1, 2, 4, 3).reshape(B, R, nb * ATTN_BLOCK, H)[:, :, :L]
    return o, lse


def combine_branches(outs, lses):
    w = jax.nn.softmax(jnp.stack(lses, 0), axis=0)
    return jnp.einsum('gbth,gbthd->bthd', w, jnp.stack(outs, 0))


def dilated_attention_prompt(q, k, v):
    B, S, H, Dh = q.shape
    slopes = alibi_slopes()
    outs, lses = [], []
    for window, dil in DILATED_BRANCHES:
        L = S // dil

        def to_strided(t):
            return t.reshape(B, L, dil, H, Dh).transpose(0, 2, 1, 3, 4)

        o, lse = window_band_attention(to_strided(q), to_strided(k), to_strided(v), window // dil, slopes * dil)
        outs.append(o.transpose(0, 2, 1, 3, 4).reshape(B, S, H, Dh))
        lses.append(lse.transpose(0, 2, 1, 3).reshape(B, S, H))
    return combine_branches(outs, lses)


def dilated_attention_sample(q, k_all, v_all):
    B, T, H, Dh = q.shape
    W = k_all.shape[1] - T
    slopes = alibi_slopes()
    outs, lses = [], []
    for window, dil in DILATED_BRANCHES:
        steps = jnp.arange(window // dil + 1)
        idx = W + jnp.arange(T)[:, None] - steps[None, :] * dil
        valid = idx >= 0
        kg = k_all[:, jnp.maximum(idx, 0)]
        vg = v_all[:, jnp.maximum(idx, 0)]
        s = jnp.einsum('bthd,btjhd->bhtj', q, kg, preferred_element_type=jnp.float32) * (Dh ** -0.5)
        s = s - (slopes * dil)[:, None, None] * steps.astype(jnp.float32)
        s = jnp.where(valid, s, -jnp.inf)
        lse = jax.nn.logsumexp(s, axis=-1)
        p = jnp.exp(s - lse[..., None])
        outs.append(jnp.einsum('bhtj,btjhd->bthd', p, vg.astype(jnp.float32)))
        lses.append(lse.transpose(0, 2, 1))
    return combine_branches(outs, lses)


def attend_prompt(q, k, v):
    keep = min(MAX_WINDOW, q.shape[1])
    return dilated_attention_prompt(q, k, v), k[:, -keep:], v[:, -keep:]


def attend_sample(q, k, v, k_buf, v_buf):
    T = q.shape[1]
    k_all = jnp.concatenate([k_buf.astype(k.dtype), k], axis=1)
    v_all = jnp.concatenate([v_buf.astype(v.dtype), v], axis=1)
    return dilated_attention_sample(q, k_all, v_all), k_all[:, T:], v_all[:, T:]


def l2_normalize(t):
    tf = t.astype(jnp.float32)
    return tf * lax.rsqrt(jnp.sum(tf * tf, axis=-1, keepdims=True) + NORM_EPS)


def causal_short_conv(u, buf, w):
    L = u.shape[1]
    up = jnp.concatenate([buf.astype(u.dtype), u], axis=1)
    y = sum(up[:, i:i + L] * w[i] for i in range(CONV_WIDTH))
    return jax.nn.silu(y), up[:, L:]


def gated_delta_rule(q, k, v, beta, g, state):
    B, L, H, DK = q.shape
    DV = v.shape[-1]
    C = DN_CHUNK
    nc = -(-L // C)
    pad = nc * C - L

    def chunks(t):
        t = jnp.pad(t, ((0, 0), (0, pad)) + ((0, 0),) * (t.ndim - 2))
        t = t.reshape((B, nc, C) + t.shape[2:])
        return jnp.moveaxis(t, (1, 3), (0, 2))

    qc, kc, vc, bc, gc = (chunks(t) for t in (q, k, v, beta, g))
    gcum = jnp.cumsum(gc, axis=-1)
    ii = jnp.arange(C)[:, None]
    jj = jnp.arange(C)[None, :]
    tril = ii >= jj
    strict = ii > jj
    diff = gcum[..., :, None] - gcum[..., None, :]
    decay = jnp.where(tril, jnp.exp(jnp.where(tril, diff, 0.0)), 0.0)
    kb = kc * bc[..., None]
    a_low = jnp.where(strict, jnp.einsum('nbhik,nbhjk->nbhij', kb, kc) * decay, 0.0)
    tmat = a_low + jnp.eye(C, dtype=jnp.float32)
    rhs = jnp.concatenate([vc * bc[..., None], kb * jnp.exp(gcum)[..., None]], axis=-1)
    sol = lax.linalg.triangular_solve(tmat, rhs, left_side=True, lower=True, unit_diagonal=True)
    u_c, w_c = sol[..., :DV], sol[..., DV:]
    attn_qk = jnp.where(tril, jnp.einsum('nbhik,nbhjk->nbhij', qc, kc) * decay, 0.0)
    q_dec = qc * jnp.exp(gcum)[..., None]
    k_dec = kc * jnp.exp(gcum[..., -1:] - gcum)[..., None]
    g_tot = jnp.exp(gcum[..., -1])

    def step(S, xs):
        u, w, aqk, qd, kd, gt = xs
        v_new = u - jnp.einsum('bhck,bhkv->bhcv', w, S)
        o = jnp.einsum('bhck,bhkv->bhcv', qd, S) + jnp.einsum('bhcj,bhjv->bhcv', aqk, v_new)
        S = S * gt[..., None, None] + jnp.einsum('bhck,bhcv->bhkv', kd, v_new)
        return S, o

    S, o = lax.scan(step, state, (u_c, w_c, attn_qk, q_dec, k_dec, g_tot))
    o = jnp.transpose(o, (1, 0, 3, 2, 4)).reshape(B, nc * C, H, DV)[:, :L]
    return o, S


def gated_deltanet(dn_in, z, b_logit, a_logit, conv_buf, state, conv_w, a_log, dt_bias, norm_g):
    B, T, _ = dn_in.shape
    y, new_buf = causal_short_conv(dn_in, conv_buf, conv_w)
    q = l2_normalize(y[..., :DN_QK].reshape(B, T, N_HEADS_DN, DN_DK)) * (DN_DK ** -0.5)
    k = l2_normalize(y[..., DN_QK:2 * DN_QK].reshape(B, T, N_HEADS_DN, DN_DK))
    v = y[..., 2 * DN_QK:].reshape(B, T, N_HEADS_DN, DN_DV).astype(jnp.float32)
    beta = jax.nn.sigmoid(b_logit.astype(jnp.float32))
    g = -jnp.exp(a_log.astype(jnp.float32)) * jax.nn.softplus(a_logit.astype(jnp.float32) + dt_bias.astype(jnp.float32))
    o, new_state = gated_delta_rule(q, k, v, beta, g, state.astype(jnp.float32))
    o = o * lax.rsqrt(jnp.mean(o * o, axis=-1, keepdims=True) + NORM_EPS) * norm_g.astype(jnp.float32)
    o = o * jax.nn.silu(z.reshape(B, T, N_HEADS_DN, DN_DV).astype(jnp.float32))
    return o.reshape(B, T, DN_V).astype(dn_in.dtype), new_buf, new_state.astype(state.dtype)


def swiglu(x, wg, wu, wd):
    return (jax.nn.silu(x @ wg) * (x @ wu)) @ wd


def moe_ffn(x, w_router, router_bias, we_gate, we_up, we_down, ws_gate, ws_up, ws_down):
    B, T, D = x.shape
    n_tok = B * T
    xf = x.reshape(n_tok, D)
    scores = jax.nn.sigmoid(jnp.einsum('nd,de->ne', xf, w_router, preferred_element_type=jnp.float32))
    choice = scores + router_bias.astype(jnp.float32)
    group_score = lax.top_k(choice.reshape(n_tok, N_EXPERT_GROUPS, N_EXPERTS // N_EXPERT_GROUPS), 2)[0].sum(-1)
    _, top_groups = lax.top_k(group_score, TOPK_GROUPS)
    group_mask = jax.nn.one_hot(top_groups, N_EXPERT_GROUPS, dtype=jnp.float32).sum(1) > 0
    expert_mask = jnp.repeat(group_mask, N_EXPERTS // N_EXPERT_GROUPS, axis=1)
    _, top_e = lax.top_k(jnp.where(expert_mask, choice, -jnp.inf), TOP_K)
    gate = jnp.take_along_axis(scores, top_e, axis=1)
    gate = gate / gate.sum(-1, keepdims=True) * ROUTED_SCALE
    n_assign = n_tok * TOP_K
    flat_e = top_e.reshape(-1)
    order = jnp.argsort(flat_e)
    sorted_e = flat_e[order]
    counts = jnp.zeros((N_EXPERTS,), jnp.int32).at[flat_e].add(1)
    starts = jnp.cumsum(counts) - counts
    pcounts = (counts + MOE_BLOCK - 1) // MOE_BLOCK * MOE_BLOCK
    pends = jnp.cumsum(pcounts)
    pstarts = pends - pcounts
    slot = pstarts[sorted_e] + (jnp.arange(n_assign) - starts[sorted_e])
    n_blocks = -(-n_assign // MOE_BLOCK) + N_EXPERTS
    cap = n_blocks * MOE_BLOCK
    slot_token = jnp.full((cap,), n_tok, jnp.int32).at[slot].set((order // TOP_K).astype(jnp.int32))
    slot_gate = jnp.zeros((cap,), jnp.float32).at[slot].set(gate.reshape(-1)[order])
    block_start = jnp.arange(n_blocks) * MOE_BLOCK
    block_expert = jnp.minimum(jnp.sum(pends[None, :] <= block_start[:, None], axis=1), N_EXPERTS - 1)
    x_pad = jnp.concatenate([xf, jnp.zeros((1, D), xf.dtype)], axis=0)

    def run_block(args):
        tok, e = args
        return swiglu(x_pad[tok], we_gate[e], we_up[e], we_down[e])

    yb = lax.map(run_block, (slot_token.reshape(n_blocks, MOE_BLOCK), block_expert))
    routed = jax.ops.segment_sum(yb.reshape(cap, D).astype(jnp.float32) * slot_gate[:, None],
                                 slot_token, num_segments=n_tok + 1)[:n_tok]
    shared = swiglu(xf, ws_gate, ws_up, ws_down).astype(jnp.float32)
    return (routed + shared).astype(x.dtype).reshape(B, T, D)


def hybrid_layer(x, p, attend, conv_buf, delta_state, w_in, conv_w, a_log, dt_bias, dn_norm_g, w_out,
                 ln1_g, ln1_b, w_router, router_bias, we_gate, we_up, we_down, ws_gate, ws_up, ws_down,
                 ln2_g, ln2_b, w_ple_gate, w_ple_proj, ln3_g, ln3_b):
    B, T, _ = x.shape
    u = jnp.einsum('btd,dc->btc', x, w_in)
    qa, ka, va, dn_in, z, b_logit, a_logit = jnp.split(u, list(np.cumsum(IN_SPLITS)[:-1]), axis=-1)
    heads = (B, T, N_HEADS_ATTN, HEAD_DIM)
    att, k_rows, v_rows = attend(qa.reshape(heads), ka.reshape(heads), va.reshape(heads))
    dn, new_buf, new_state = gated_deltanet(dn_in, z, b_logit, a_logit, conv_buf, delta_state,
                                            conv_w, a_log, dt_bias, dn_norm_g)
    mix = jnp.concatenate([att.reshape(B, T, ATTN_WIDTH).astype(x.dtype), dn], axis=-1) @ w_out
    x = layer_norm(DEEPNORM_ALPHA * x + mix, ln1_g, ln1_b)
    x = layer_norm(DEEPNORM_ALPHA * x + moe_ffn(x, w_router, router_bias, we_gate, we_up, we_down,
                                                 ws_gate, ws_up, ws_down), ln2_g, ln2_b)
    ple = jax.nn.sigmoid(x @ w_ple_gate) * (p.astype(x.dtype) @ w_ple_proj)
    x = layer_norm(DEEPNORM_ALPHA * x + ple, ln3_g, ln3_b)
    return x, k_rows, v_rows, new_buf, new_state


def setup_inputs(seed: int = 0) -> dict:
    key = jax.random.key(seed)
    ks = jax.random.split(key, 32)
    f32 = jnp.float32
    w_buf = min(MAX_WINDOW, PAST_LEN)

    def nrm(k, shape, scale):
        return jax.random.normal(k, shape, f32) * scale

    dt = jnp.exp(jax.random.uniform(ks[10], (DEPTH, N_HEADS_DN), f32, math.log(1e-3), math.log(1e-1)))
    return {
        'x_prompt': nrm(ks[0], (BATCH, SEQ, D_MODEL), 1.0),
        'x_sample': nrm(ks[1], (DEC_BATCH, DEC_SEQ, D_MODEL), 1.0),
        'cache_k_win': nrm(ks[2], (DEPTH, DEC_BATCH, w_buf, N_HEADS_ATTN, HEAD_DIM), 1.0),
        'cache_v_win': nrm(ks[3], (DEPTH, DEC_BATCH, w_buf, N_HEADS_ATTN, HEAD_DIM), 1.0),
        'state_conv': nrm(ks[4], (DEPTH, DEC_BATCH, CONV_WIDTH - 1, CONV_CH), 1.0),
        'state_delta': nrm(ks[5], (DEPTH, DEC_BATCH, N_HEADS_DN, DN_DK, DN_DV), 0.1),
        'p_prompt': nrm(ks[6], (DEPTH, BATCH, SEQ, PLE_DIM), 1.0),
        'p_sample': nrm(ks[7], (DEPTH, DEC_BATCH, DEC_SEQ, PLE_DIM), 1.0),
        'w_in': nrm(ks[8], (DEPTH, D_MODEL, IN_COLS), D_MODEL ** -0.5),
        'conv_w': nrm(ks[9], (DEPTH, CONV_WIDTH, CONV_CH), CONV_WIDTH ** -0.5),
        'a_log': jnp.log(jax.random.uniform(ks[11], (DEPTH, N_HEADS_DN), f32, 1.0, 16.0)),
        'dt_bias': dt + jnp.log(-jnp.expm1(-dt)),
        'dn_norm_g': 1.0 + nrm(ks[12], (DEPTH, DN_DV), 0.02),
        'w_out': nrm(ks[13], (DEPTH, MIX_WIDTH, D_MODEL), MIX_WIDTH ** -0.5 * DEEPNORM_BETA),
        'ln1_g': 1.0 + nrm(ks[14], (DEPTH, D_MODEL), 0.02),
        'ln1_b': nrm(ks[15], (DEPTH, D_MODEL), 0.02),
        'w_router': nrm(ks[16], (DEPTH, D_MODEL, N_EXPERTS), D_MODEL ** -0.5),
        'router_bias': nrm(ks[17], (DEPTH, N_EXPERTS), 0.01),
        'we_gate': nrm(ks[18], (DEPTH, N_EXPERTS, D_MODEL, EXPERT_DIM), D_MODEL ** -0.5),
        'we_up': nrm(ks[19], (DEPTH, N_EXPERTS, D_MODEL, EXPERT_DIM), D_MODEL ** -0.5),
        'we_down': nrm(ks[20], (DEPTH, N_EXPERTS, EXPERT_DIM, D_MODEL), EXPERT_DIM ** -0.5 * DEEPNORM_BETA),
        'ws_gate': nrm(ks[21], (DEPTH, D_MODEL, SHARED_DIM), D_MODEL ** -0.5),
        'ws_up': nrm(ks[22], (DEPTH, D_MODEL, SHARED_DIM), D_MODEL ** -0.5),
        'ws_down': nrm(ks[23], (DEPTH, SHARED_DIM, D_MODEL), SHARED_DIM ** -0.5 * DEEPNORM_BETA),
        'ln2_g': 1.0 + nrm(ks[24], (DEPTH, D_MODEL), 0.02),
        'ln2_b': nrm(ks[25], (DEPTH, D_MODEL), 0.02),
        'w_ple_gate': nrm(ks[26], (DEPTH, D_MODEL, D_MODEL), D_MODEL ** -0.5),
        'w_ple_proj': nrm(ks[27], (DEPTH, PLE_DIM, D_MODEL), PLE_DIM ** -0.5 * DEEPNORM_BETA),
        'ln3_g': 1.0 + nrm(ks[28], (DEPTH, D_MODEL), 0.02),
        'ln3_b': nrm(ks[29], (DEPTH, D_MODEL), 0.02),
    }


def reference(x_prompt, x_sample, cache_k_win, cache_v_win, state_conv, state_delta, p_prompt, p_sample,
              w_in, conv_w, a_log, dt_bias, dn_norm_g, w_out, ln1_g, ln1_b, w_router, router_bias,
              we_gate, we_up, we_down, ws_gate, ws_up, ws_down, ln2_g, ln2_b, w_ple_gate, w_ple_proj,
              ln3_g, ln3_b):
    xp, xs = x_prompt, x_sample
    kp, vp, cp, dp = [], [], [], []
    ksm, vsm, csm, dsm = [], [], [], []
    for i in range(DEPTH):
        lw = (w_in[i], conv_w[i], a_log[i], dt_bias[i], dn_norm_g[i], w_out[i], ln1_g[i], ln1_b[i],
              w_router[i], router_bias[i], we_gate[i], we_up[i], we_down[i], ws_gate[i], ws_up[i], ws_down[i],
              ln2_g[i], ln2_b[i], w_ple_gate[i], w_ple_proj[i], ln3_g[i], ln3_b[i])
        conv0 = jnp.zeros((xp.shape[0], CONV_WIDTH - 1, CONV_CH), xp.dtype)
        delta0 = jnp.zeros((xp.shape[0], N_HEADS_DN, DN_DK, DN_DV), state_delta.dtype)
        xp, k_new, v_new, c_new, d_new = hybrid_layer(xp, p_prompt[i], attend_prompt, conv0, delta0, *lw)
        kp.append(k_new)
        vp.append(v_new)
        cp.append(c_new)
        dp.append(d_new)
        attend_s = functools.partial(attend_sample, k_buf=cache_k_win[i], v_buf=cache_v_win[i])
        xs, k_new, v_new, c_new, d_new = hybrid_layer(xs, p_sample[i], attend_s, state_conv[i], state_delta[i], *lw)
        ksm.append(k_new)
        vsm.append(v_new)
        csm.append(c_new)
        dsm.append(d_new)
    return (xp, xs, jnp.stack(kp), jnp.stack(vp), jnp.stack(cp), jnp.stack(dp),
            jnp.stack(ksm), jnp.stack(vsm), jnp.stack(csm), jnp.stack(dsm))
```

```python
import functools

import jax
import jax.numpy as jnp
import numpy as np
from jax import lax
from jax.experimental import pallas as pl
from jax.experimental.pallas import tpu as pltpu

F32 = jnp.float32
BF16 = jnp.bfloat16
I32 = jnp.int32

D_MODEL = 1024
N_PROMPT_B, SEQ = 2, 8192
N_SAMPLE_B, DEC_SEQ = 32, 4
W_BUF = 2048
N_HEADS = 8
HEAD_DIM = 64
ATTN_W = 512
CONV_CH = 1536
CONV_W = 4
DN_W = 512
N_EXPERTS = 64
TOP_K = 8
EXPERT_DIM = 256
PLE_DIM = 256
BRANCHES = ((128, 1), (512, 4), (2048, 16))
N_BACK = 128
ROUTED_SCALE = 2.5
LN_EPS = 1e-5
NORM_EPS = 1e-6
ALPHA = 2.0 ** 0.25

LANES = 128
HALF = 64
VMEM_LIMIT = 56 * 1024 * 1024

N_P = N_PROMPT_B * SEQ
N_S = N_SAMPLE_B * DEC_SEQ
N_REAL = N_P + N_S
N_PAD = N_P + 512
TM = 256
N_TILES = N_PAD // TM

SEG_ALIGN = 16
M_T = 3072
M_CHUNK = 512
BM = 256
R_MAX = N_PAD * TOP_K + N_TILES * N_EXPERTS * (SEG_ALIGN - 1) + N_EXPERTS * (BM - 1)
NB_MAX = -(-R_MAX // BM)
SEG_BITS = (16, 8, 4, 2, 1)

DN_C = 64
SEQ_S = 64
S_TOK0 = 8


def _dot(a, b):
    return jnp.dot(a, b, preferred_element_type=F32)


def _dot_nt(a, b):
    return lax.dot_general(a, b, (((1,), (1,)), ((), ())), preferred_element_type=F32)


def _split2(x):
    hi = x.astype(BF16)
    lo = (x - hi.astype(F32)).astype(BF16)
    return hi, lo


def _split3(x):
    a = x.astype(BF16)
    r = x - a.astype(F32)
    b = r.astype(BF16)
    c = (r - b.astype(F32)).astype(BF16)
    return a, b, c


def _dot3_l01(m01, x):
    a, b, c = _split3(x)
    return _dot(m01, a) + _dot(m01, b) + _dot(m01, c)


def _dot3_r01(x, m01):
    a, b, c = _split3(x)
    return _dot(a, m01) + _dot(b, m01) + _dot(c, m01)


def _sigmoid(x):
    return 1.0 / (1.0 + jnp.exp(-x))


def _layer_norm(x, g, b):
    mu = jnp.mean(x, axis=-1, keepdims=True)
    xc = x - mu
    var = jnp.mean(xc * xc, axis=-1, keepdims=True)
    return xc * lax.rsqrt(var + LN_EPS) * g + b


def _cparams(sem=None):
    return pltpu.CompilerParams(dimension_semantics=sem, vmem_limit_bytes=VMEM_LIMIT)


def _full(shape):
    return pl.BlockSpec(shape, lambda *_: (0,) * len(shape))


def _proj_kernel(x_ref, wqkv_ref, wdn_ref, wz_ref, wbh_ref, wbl_ref, wah_ref, wal_ref,
                 qkv_ref, dn_ref, z_ref, b_ref, a_ref):
    x = x_ref[...]
    xh, xl = _split2(x)
    qkv_ref[...] = _dot(xh, wqkv_ref[...])
    dn_ref[...] = _dot(xh, wdn_ref[...])
    z_ref[...] = _dot(xh, wz_ref[...])
    b_ref[...] = _dot(xh, wbh_ref[...]) + _dot(xl, wbh_ref[...]) + _dot(xh, wbl_ref[...])
    a_ref[...] = _dot(xh, wah_ref[...]) + _dot(xl, wah_ref[...]) + _dot(xh, wal_ref[...])


def _project(x_all, w_in):
    wqkv = w_in[:, :3 * ATTN_W].astype(BF16)
    wdn = w_in[:, 3 * ATTN_W:3 * ATTN_W + CONV_CH].astype(BF16)
    c0 = 3 * ATTN_W + CONV_CH
    wz = w_in[:, c0:c0 + DN_W].astype(BF16)
    wb = w_in[:, c0 + DN_W:c0 + DN_W + N_HEADS]
    wa = w_in[:, c0 + DN_W + N_HEADS:]
    wbh, wbl = _split2(wb)
    wah, wal = _split2(wa)
    row = lambda w: pl.BlockSpec((TM, w), lambda i: (i, 0))
    outs = (3 * ATTN_W, CONV_CH, DN_W, N_HEADS, N_HEADS)
    return pl.pallas_call(
        _proj_kernel,
        name="proj",
        grid=(N_TILES,),
        in_specs=[row(D_MODEL), _full(wqkv.shape), _full(wdn.shape), _full(wz.shape),
                  _full(wbh.shape), _full(wbl.shape), _full(wah.shape), _full(wal.shape)],
        out_specs=[row(w) for w in outs],
        out_shape=[jax.ShapeDtypeStruct((N_PAD, w), F32) for w in outs],
        compiler_params=_cparams(("arbitrary",)),
    )(x_all, wqkv, wdn, wz, wbh, wbl, wah, wal)


QT = 2048


def _attn_bias():
    qi = np.arange(N_BACK)[:, None]
    ki = np.arange(2 * N_BACK)[None, :]
    out = np.zeros((2, len(BRANCHES), N_HEADS, N_BACK, 2 * N_BACK), np.float32)
    for var, off in enumerate((N_BACK, 0)):
        dist = qi - ki + off
        valid = (dist >= 0) & (dist <= N_BACK)
        for di, (_, dil) in enumerate(BRANCHES):
            for h in range(N_HEADS):
                slope = 2.0 ** (-8.0 * (h + 1) / N_HEADS) * dil
                out[var, di, h] = np.where(valid, -slope * dist, -1e30)
    return out


def _attn_prompt_kernel(q_ref, k_ref, v_ref, bias_ref, o_ref, acc_ref, m_ref, l_ref):
    qt = pl.program_id(2)
    lane = lax.broadcasted_iota(I32, (1, LANES), 1)
    h0 = lane < HALF
    for di, (_, dil) in enumerate(BRANCHES):
        per_r = QT // (N_BACK * dil)

        def body(i, carry, di=di, dil=dil, per_r=per_r):
            r = i // per_r
            jb = i % per_r
            qloc = r + dil * N_BACK * jb
            qglob = qt * QT + qloc
            first = jnp.logical_and(qt == 0, jb == 0)
            kstart = jnp.where(first, r, qglob - N_BACK * dil)
            var = jnp.where(first, 1, 0)
            if dil == 1:
                q = q_ref[pl.ds(qloc, N_BACK), :]
                k = k_ref[pl.ds(kstart, 2 * N_BACK), :]
                v = v_ref[pl.ds(kstart, 2 * N_BACK), :]
            else:
                q = q_ref[pl.ds(qloc, N_BACK, stride=dil), :]
                k = k_ref[pl.ds(kstart, 2 * N_BACK, stride=dil), :]
                v = v_ref[pl.ds(kstart, 2 * N_BACK, stride=dil), :]
            kb = k.astype(BF16)
            res = []
            mx = []
            for hh in range(2):
                keep = h0 if hh == 0 else jnp.logical_not(h0)
                qm = jnp.where(keep, q * (HEAD_DIM ** -0.5), 0.0).astype(BF16)
                s = _dot_nt(qm, kb) + bias_ref[var, di, hh]
                m = jnp.max(s, axis=-1, keepdims=True)
                p = jnp.exp(s - m).astype(BF16)
                vaug = jnp.where(keep, v, 1.0).astype(BF16)
                res.append(_dot(p, vaug))
                mx.append(m)
            o = jnp.where(h0, res[0], res[1])
            lsw = jnp.where(h0, res[1], res[0])
            mp = jnp.where(h0, mx[0], mx[1])
            if dil == 1:
                rows = pl.ds(qloc, N_BACK)
            else:
                rows = pl.ds(qloc, N_BACK, stride=dil)
            acc_ref[di, rows, :] = o
            l_ref[di, rows, :] = lsw
            m_ref[di, rows, :] = mp
            return carry

        lax.fori_loop(0, QT // N_BACK, body, 0)

    mall = m_ref[...]
    mtot = jnp.max(mall, axis=0)
    num = jnp.zeros((QT, LANES), F32)
    den = jnp.zeros((QT, LANES), F32)
    for di in range(len(BRANCHES)):
        w = jnp.exp(mall[di] - mtot)
        num = num + w * acc_ref[di]
        den = den + w * pltpu.roll(l_ref[di], HALF, axis=1)
    o_ref[...] = num / den


def _attn_prompt(qkv):
    bias = jnp.asarray(_attn_bias())
    npair = N_HEADS // 2
    nqt = SEQ // QT
    kv_rows = SEQ
    q_spec = pl.BlockSpec((QT, LANES), lambda b, hp, t: (b * nqt + t, hp))
    k_spec = pl.BlockSpec((kv_rows, LANES), lambda b, hp, t: (b, npair + hp))
    v_spec = pl.BlockSpec((kv_rows, LANES), lambda b, hp, t: (b, 2 * npair + hp))
    bias_spec = pl.BlockSpec((2, len(BRANCHES), 2, N_BACK, 2 * N_BACK), lambda b, hp, t: (0, 0, hp, 0, 0))
    return pl.pallas_call(
        _attn_prompt_kernel,
        name="attn_prompt",
        grid=(N_PROMPT_B, npair, nqt),
        in_specs=[q_spec, k_spec, v_spec, bias_spec],
        out_specs=pl.BlockSpec((QT, LANES), lambda b, hp, t: (b * nqt + t, hp)),
        out_shape=jax.ShapeDtypeStruct((N_P, ATTN_W), F32),
        scratch_shapes=[pltpu.VMEM((len(BRANCHES), QT, LANES), F32)] * 3,
        compiler_params=_cparams(("arbitrary", "arbitrary", "arbitrary")),
    )(qkv, qkv, qkv, bias)


ROWS_S = 8


def _attn_sample_kernel(q_ref, kn_ref, vn_ref, ck_ref, cv_ref, e_ref, et_ref, slope_ref, o_ref, kall_ref, vall_ref):
    npair = N_HEADS // 2
    for hp in range(npair):
        sl = slice(hp * LANES, (hp + 1) * LANES)
        kall_ref[hp, 0:W_BUF, :] = ck_ref[0, :, sl]
        vall_ref[hp, 0:W_BUF, :] = cv_ref[0, :, sl]
        kall_ref[hp, W_BUF:W_BUF + ROWS_S, :] = kn_ref[0, :, sl]
        vall_ref[hp, W_BUF:W_BUF + ROWS_S, :] = vn_ref[0, :, sl]

    def rows(ref, start, n, dil):
        idx = pl.ds(start, n) if dil == 1 else pl.ds(start, n, stride=dil)
        return jnp.concatenate([ref[hp, idx, :] for hp in range(npair)], axis=1)

    e01 = e_ref[...]
    et01 = et_ref[...]
    slope = slope_ref[...]
    jrev = (N_BACK - lax.broadcasted_iota(I32, (N_BACK, 1), 0)).astype(F32)
    inew = lax.broadcasted_iota(I32, (ROWS_S, 1), 0)
    scale = HEAD_DIM ** -0.5
    knew = rows(kall_ref, W_BUF, ROWS_S, 1)
    vnew = rows(vall_ref, W_BUF, ROWS_S, 1).astype(BF16).astype(F32)

    def scores(kr, qt):
        ph, plo = _split2(kr * qt)
        return (_dot(ph, e01) + _dot(plo, e01)) * scale

    outs = []
    for t in range(DEC_SEQ):
        qt = q_ref[0, t:t + 1, :]
        s_self = jnp.where(inew == t, scores(knew, qt), -1e30)
        s_br = []
        v_br = []
        for _, dil in BRANCHES:
            start = W_BUF + t - N_BACK * dil
            s_br.append(scores(rows(kall_ref, start, N_BACK, dil), qt) - (slope * dil) * jrev)
            v_br.append(rows(vall_ref, start, N_BACK, dil))
        m = jnp.max(s_self, axis=0, keepdims=True)
        for s in s_br:
            m = jnp.maximum(m, jnp.max(s, axis=0, keepdims=True))
        pb = (jnp.exp(s_self - m) * float(len(BRANCHES))).astype(BF16)
        den = jnp.sum(pb.astype(F32), axis=0, keepdims=True)
        num = jnp.sum(_dot(pb, et01) * vnew, axis=0, keepdims=True)
        for s, vr in zip(s_br, v_br):
            pb = jnp.exp(s - m).astype(BF16)
            den = den + jnp.sum(pb.astype(F32), axis=0, keepdims=True)
            num = num + jnp.sum(_dot(pb, et01) * vr.astype(BF16).astype(F32), axis=0, keepdims=True)
        denx = _dot3_r01(jnp.broadcast_to(den, (ROWS_S, N_HEADS)), et01)[0:1]
        outs.append(num / denx)
    pad = jnp.zeros((ROWS_S - DEC_SEQ, ATTN_W), F32)
    o_ref[0] = jnp.concatenate(outs + [pad], axis=0)


def _head_expand():
    e = np.zeros((ATTN_W, N_HEADS), np.float32)
    for h in range(N_HEADS):
        e[h * HEAD_DIM:(h + 1) * HEAD_DIM, h] = 1.0
    return e


def _attn_sample(q_s, k_s, v_s, cache_k, cache_v):
    e = _head_expand()
    e01 = jnp.asarray(e, BF16)
    et01 = jnp.asarray(e.T, BF16)
    slopes = jnp.asarray([[2.0 ** (-8.0 * (h + 1) / N_HEADS) for h in range(N_HEADS)]], F32)
    row = pl.BlockSpec((1, ROWS_S, ATTN_W), lambda b: (b, 0, 0))
    cache = pl.BlockSpec((1, W_BUF, ATTN_W), lambda b: (b, 0, 0))
    return pl.pallas_call(
        _attn_sample_kernel,
        name="attn_sample",
        grid=(N_SAMPLE_B,),
        in_specs=[row, row, row, cache, cache, _full(e01.shape), _full(et01.shape), _full(slopes.shape)],
        out_specs=row,
        out_shape=jax.ShapeDtypeStruct((N_SAMPLE_B, ROWS_S, ATTN_W), F32),
        scratch_shapes=[pltpu.VMEM((N_HEADS // 2, W_BUF + ROWS_S, LANES), F32)] * 2,
        compiler_params=_cparams(("arbitrary",)),
    )(q_s, k_s, v_s, cache_k, cache_v, e01, et01, slopes)


HALO = 8


def _dn_pre_kernel(dn_ref, halo_ref, b_ref, a_ref, valid_ref, cw_ref, alog_ref, dtb_ref, e_ref, et_ref, ltri_ref,
                   q_ref, k_ref, v_ref, bx_ref, gx_ref, buf_ref):
    ts = dn_ref.shape[0]
    buf_ref[0:HALO, :] = halo_ref[0]
    buf_ref[HALO:HALO + ts, :] = dn_ref[...]
    y = jnp.zeros((ts, CONV_CH), F32)
    for i in range(CONV_W):
        y = y + buf_ref[pl.ds(HALO - (CONV_W - 1) + i, ts), :] * cw_ref[i:i + 1, :]
    y = y * _sigmoid(y)
    valid = valid_ref[...]
    e01 = e_ref[...]
    et01 = et_ref[...]

    def l2n(t):
        ss = _dot3_r01(t * t, e01)
        inv = lax.rsqrt(ss + NORM_EPS)
        return t * _dot3_r01(inv, et01)

    q_ref[...] = l2n(y[:, :DN_W]) * (HEAD_DIM ** -0.5)
    k_ref[...] = l2n(y[:, DN_W:2 * DN_W]) * valid
    v_ref[...] = y[:, 2 * DN_W:] * valid
    beta = _sigmoid(b_ref[...]) * valid
    sp_in = a_ref[...] + dtb_ref[...]
    softplus = jnp.maximum(sp_in, 0.0) + jnp.log1p(jnp.exp(-jnp.abs(sp_in)))
    g = -jnp.exp(alog_ref[...]) * softplus * valid
    bx_ref[...] = _dot3_r01(beta, et01)
    gx_ref[...] = _dot3_l01(ltri_ref[...], _dot3_r01(g, et01))


def _dn_pre(dn, halo, b_logit, a_logit, valid, conv_w, a_log, dt_bias):
    rows = dn.shape[0]
    nt = rows // TM
    e = _head_expand()
    e01 = jnp.asarray(e, BF16)
    et01 = jnp.asarray(e.T, BF16)
    ii = np.arange(TM)
    ltri = jnp.asarray(((ii[:, None] >= ii[None, :]) & (ii[:, None] // DN_C == ii[None, :] // DN_C)), BF16)
    row = lambda w: pl.BlockSpec((TM, w), lambda i: (i, 0))
    return pl.pallas_call(
        _dn_pre_kernel,
        name="dn_pre",
        grid=(nt,),
        in_specs=[row(CONV_CH), pl.BlockSpec((1, HALO, CONV_CH), lambda i: (i, 0, 0)), row(N_HEADS), row(N_HEADS),
                  row(1), _full(conv_w.shape), _full((1, N_HEADS)), _full((1, N_HEADS)),
                  _full(e01.shape), _full(et01.shape), _full(ltri.shape)],
        out_specs=[row(DN_W)] * 5,
        out_shape=[jax.ShapeDtypeStruct((rows, DN_W), F32)] * 5,
        scratch_shapes=[pltpu.VMEM((HALO + TM, CONV_CH), F32)],
        compiler_params=_cparams(("arbitrary",)),
    )(dn, halo, b_logit, a_logit, valid, conv_w, a_log.reshape(1, N_HEADS), dt_bias.reshape(1, N_HEADS),
      e01, et01, ltri)


def _delta_chunk(q, k, v, bx, gcx, z, s_bd, normg, bavg):
    c = DN_C
    lane = lax.broadcasted_iota(I32, (1, LANES), 1)
    h0 = lane < HALF
    m0 = h0.astype(F32)
    m1 = 1.0 - m0

    def stack(x):
        return jnp.concatenate([x * m0, x * m1], axis=0)

    def fold(x2):
        return x2[:c] + x2[c:]

    eg = jnp.exp(gcx)
    kb = k * bx
    glast = gcx[c - 1:c, :]
    kd = k * jnp.exp(glast - gcx)
    gt = jnp.exp(glast)
    k2 = stack(k).astype(BF16)
    kb2 = stack(kb).astype(BF16)
    q2 = stack(q).astype(BF16)
    gsw = pltpu.roll(gcx, HALF, axis=1)
    row_b = jnp.concatenate([jnp.where(h0, gcx, gsw), jnp.where(h0, gsw, gcx)], axis=0)
    diff = row_b - row_b.T
    ii = lax.broadcasted_iota(I32, (2 * c, 2 * c), 0)
    jj = lax.broadcasted_iota(I32, (2 * c, 2 * c), 1)
    same = jnp.where(ii < c, 0, 1) == jnp.where(jj < c, 0, 1)
    order = jnp.where(same, ii - jj, -1)
    tril = order >= 0
    strict = order > 0
    decay = jnp.where(tril, jnp.exp(jnp.where(tril, diff, 0.0)), 0.0)
    a_low = jnp.where(strict, _dot_nt(kb2, k2) * decay, 0.0)
    aqk = jnp.where(tril, _dot_nt(q2, k2) * decay, 0.0)
    tinv = jnp.where(ii == jj, 1.0, 0.0) - a_low
    apow = a_low
    for _ in range(5):
        ab = apow.astype(BF16)
        apow = _dot(ab, ab)
        tinv = tinv + _dot(tinv.astype(BF16), apow.astype(BF16))
    rhs = jnp.concatenate([stack(v * bx), stack(kb * eg)], axis=1).astype(BF16)
    uw = _dot(tinv.astype(BF16), rhs)
    u = fold(uw[:, :LANES])
    w = fold(uw[:, LANES:])
    sb = s_bd.astype(BF16)
    ws = _dot(jnp.concatenate([w, q * eg], axis=0).astype(BF16), sb)
    v_new = u - ws[:c]
    o = ws[c:] + fold(_dot(aqk.astype(BF16), stack(v_new).astype(BF16)))
    upd = _dot(kd.T.astype(BF16), v_new.astype(BF16))
    s_new = s_bd * gt + jnp.where(same, upd, 0.0)
    oh, ol = _split2(o * o)
    ms = _dot(oh, bavg) + _dot(ol, bavg)
    o = o * lax.rsqrt(ms + NORM_EPS) * normg
    o = o * (z * _sigmoid(z))
    return o, s_new


def _delta_kernel(q_ref, k_ref, v_ref, bx_ref, gx_ref, z_ref, s0_ref, ng_ref, bavg_ref, o_ref, sout_ref, s_ref):
    ci = pl.program_id(1)
    nb = q_ref.shape[0]
    npair = N_HEADS // 2

    @pl.when(ci == 0)
    def _():
        s_ref[...] = s0_ref[...]

    normg = ng_ref[...]
    bavg = bavg_ref[...]
    for b in range(nb):
        for hp in range(npair):
            sl = slice(hp * LANES, (hp + 1) * LANES)
            o, s_new = _delta_chunk(q_ref[b, :, sl], k_ref[b, :, sl], v_ref[b, :, sl], bx_ref[b, :, sl],
                                    gx_ref[b, :, sl], z_ref[b, :, sl], s_ref[b, hp], normg, bavg)
            o_ref[b, :, sl] = o
            s_ref[b, hp] = s_new

    @pl.when(ci == pl.num_programs(1) - 1)
    def _():
        sout_ref[...] = s_ref[...]


def _delta(q, k, v, bx, gx, z, s0_bd, norm_g, bblk):
    nbatch, length, _ = q.shape
    npair = N_HEADS // 2
    normg = jnp.tile(norm_g.reshape(1, HEAD_DIM), (1, 2))
    bavg = jnp.asarray(np.kron(np.eye(2), np.ones((HALF, HALF))) / HALF, BF16)
    seq = pl.BlockSpec((bblk, DN_C, DN_W), lambda g, c: (g, c, 0))
    st = pl.BlockSpec((bblk, npair, LANES, LANES), lambda g, c: (g, 0, 0, 0))
    return pl.pallas_call(
        _delta_kernel,
        name="delta",
        grid=(nbatch // bblk, length // DN_C),
        in_specs=[seq] * 6 + [st, _full(normg.shape), _full(bavg.shape)],
        out_specs=[seq, st],
        out_shape=[jax.ShapeDtypeStruct((nbatch, length, DN_W), F32),
                   jax.ShapeDtypeStruct((nbatch, npair, LANES, LANES), F32)],
        scratch_shapes=[pltpu.VMEM((bblk, npair, LANES, LANES), F32)],
        compiler_params=_cparams(("arbitrary", "arbitrary")),
    )(q, k, v, bx, gx, z, s0_bd, normg, bavg)


def _state_to_bd(s):
    b = s.shape[0]
    s = s.reshape(b, N_HEADS // 2, 2, HALF, HALF)
    zero = jnp.zeros_like(s[:, :, 0])
    top = jnp.concatenate([s[:, :, 0], zero], axis=-1)
    bot = jnp.concatenate([zero, s[:, :, 1]], axis=-1)
    return jnp.concatenate([top, bot], axis=-2)


def _state_from_bd(s):
    b = s.shape[0]
    return jnp.stack([s[:, :, :HALF, :HALF], s[:, :, HALF:, HALF:]], axis=2).reshape(b, N_HEADS, HALF, HALF)


def _out_kernel(x_ref, att_ref, dn_ref, wa_ref, wd_ref, g_ref, b_ref, o_ref):
    mix = _dot(att_ref[...].astype(BF16), wa_ref[...]) + _dot(dn_ref[...].astype(BF16), wd_ref[...])
    o_ref[...] = _layer_norm(ALPHA * x_ref[...] + mix, g_ref[...], b_ref[...])


def _out_proj(x_all, att, dn, w_out, g, b):
    wa = w_out[:ATTN_W].astype(BF16)
    wd = w_out[ATTN_W:].astype(BF16)
    row = lambda w: pl.BlockSpec((TM, w), lambda i: (i, 0))
    return pl.pallas_call(
        _out_kernel,
        name="out_proj",
        grid=(N_TILES,),
        in_specs=[row(D_MODEL), row(ATTN_W), row(DN_W), _full(wa.shape), _full(wd.shape),
                  _full((1, D_MODEL)), _full((1, D_MODEL))],
        out_specs=row(D_MODEL),
        out_shape=jax.ShapeDtypeStruct((N_PAD, D_MODEL), F32),
        compiler_params=_cparams(("arbitrary",)),
    )(x_all, att, dn, wa, wd, g.reshape(1, -1), b.reshape(1, -1))


N_GROUPS = 8
GROUP = N_EXPERTS // N_GROUPS
TOPK_GROUPS = 4
NEG = -jnp.inf


def _route_kernel(x_ref, wh_ref, wl_ref, bias_ref, upper_ref, lows_ref, gate_ref, slot_ref, cnt_ref):
    t = pl.program_id(0)
    xh, xl = _split2(x_ref[...])
    wh = wh_ref[...]
    logits = _dot_nt(wh, xh) + _dot_nt(wh, xl) + _dot_nt(wl_ref[...], xh)
    scores = _sigmoid(logits)
    choice = scores + bias_ref[...]
    i8 = lax.broadcasted_iota(I32, (GROUP, TM), 0)

    def first_max(vals):
        m = jnp.max(vals, axis=0, keepdims=True)
        idx = jnp.min(jnp.where(vals == m, i8, GROUP), axis=0, keepdims=True)
        return m, idx

    pieces = [choice[g * GROUP:(g + 1) * GROUP, :] for g in range(N_GROUPS)]
    spieces = [scores[g * GROUP:(g + 1) * GROUP, :] for g in range(N_GROUPS)]
    gsc = jnp.zeros((N_GROUPS, TM), F32)
    for g in range(N_GROUPS):
        m1, idx1 = first_max(pieces[g])
        m2 = jnp.max(jnp.where(i8 == idx1, NEG, pieces[g]), axis=0, keepdims=True)
        gsc = jnp.where(i8 == g, m1 + m2, gsc)
    gsel = jnp.zeros((N_GROUPS, TM), F32)
    for _ in range(TOPK_GROUPS):
        _, idx = first_max(gsc)
        hit = i8 == idx
        gsel = jnp.where(hit, 1.0, gsel)
        gsc = jnp.where(hit, NEG, gsc)
    masked = [jnp.where(gsel[g:g + 1, :] > 0.5, pieces[g], NEG) for g in range(N_GROUPS)]
    eidx = [i8 + g * GROUP for g in range(N_GROUPS)]
    member = [jnp.zeros((GROUP, TM), F32) for _ in range(N_GROUPS)]
    tops = []
    graw = []
    for _ in range(TOP_K):
        mm = masked[0]
        for g in range(1, N_GROUPS):
            mm = jnp.maximum(mm, masked[g])
        m = jnp.max(mm, axis=0, keepdims=True)
        cand = jnp.where(masked[0] == m, eidx[0], N_EXPERTS)
        for g in range(1, N_GROUPS):
            cand = jnp.minimum(cand, jnp.where(masked[g] == m, eidx[g], N_EXPERTS))
        idx = jnp.min(cand, axis=0, keepdims=True)
        gsum = jnp.zeros((GROUP, TM), F32)
        for g in range(N_GROUPS):
            hit = eidx[g] == idx
            gsum = gsum + jnp.where(hit, spieces[g], 0.0)
            member[g] = jnp.where(hit, 1.0, member[g])
            masked[g] = jnp.where(hit, NEG, masked[g])
        tops.append(idx)
        graw.append(jnp.sum(gsum, axis=0, keepdims=True))
    gtot = graw[0]
    for j in range(1, TOP_K):
        gtot = gtot + graw[j]
    gates = jnp.zeros((TOP_K, TM), F32)
    for j in range(TOP_K):
        gates = jnp.where(i8 == j, graw[j] / gtot * ROUTED_SCALE, gates)
    gate_ref[...] = gates

    tok = t * TM + lax.broadcasted_iota(I32, (1, TM), 1)
    real = tok < N_REAL
    mem = jnp.concatenate(member, axis=0) * jnp.where(real, 1.0, 0.0)
    memb = mem.astype(BF16)
    rank = _dot(memb, upper_ref[...])
    cnt = jnp.sum(mem, axis=1, keepdims=True)
    cpad = jnp.floor((cnt + (SEG_ALIGN - 1)) * (1.0 / SEG_ALIGN))
    loc = _dot(lows_ref[...], jnp.broadcast_to(cpad, (N_EXPERTS, TM)).astype(BF16)) * SEG_ALIGN
    slot_e = loc + rank
    slots = jnp.zeros((TOP_K, TM), F32)
    for j in range(TOP_K):
        acc = jnp.zeros((GROUP, TM), F32)
        for g in range(N_GROUPS):
            acc = acc + jnp.where(eidx[g] == tops[j], slot_e[g * GROUP:(g + 1) * GROUP, :], 0.0)
        slots = jnp.where(i8 == j, jnp.sum(acc, axis=0, keepdims=True), slots)
    slots = jnp.where(real, slots, float(M_T))
    slot_ref[...] = slots.astype(I32)
    ones = jnp.ones((GROUP, TM), BF16)
    cnt_ref[0] = _dot_nt(ones, memb).astype(I32)


def _route(x1, w_router, router_bias):
    wt = w_router.T
    wh, wl = _split2(wt)
    ii = np.arange(TM)
    upper = jnp.asarray(ii[:, None] < ii[None, :], BF16)
    ee = np.arange(N_EXPERTS)
    lows = jnp.asarray(ee[:, None] > ee[None, :], BF16)
    return pl.pallas_call(
        _route_kernel,
        name="route",
        grid=(N_TILES,),
        in_specs=[pl.BlockSpec((TM, D_MODEL), lambda i: (i, 0)), _full(wh.shape), _full(wl.shape),
                  _full((N_EXPERTS, 1)), _full(upper.shape), _full(lows.shape)],
        out_specs=[pl.BlockSpec((TOP_K, TM), lambda i: (0, i)), pl.BlockSpec((TOP_K, TM), lambda i: (0, i)),
                   pl.BlockSpec((1, GROUP, N_EXPERTS), lambda i: (i, 0, 0))],
        out_shape=[jax.ShapeDtypeStruct((TOP_K, N_PAD), F32), jax.ShapeDtypeStruct((TOP_K, N_PAD), I32),
                   jax.ShapeDtypeStruct((N_TILES, GROUP, N_EXPERTS), I32)],
        compiler_params=_cparams(("arbitrary",)),
    )(x1, wh, wl, router_bias.reshape(N_EXPERTS, 1), upper, lows)


def _segment_tables(cnt):
    seg = (cnt + SEG_ALIGN - 1) // SEG_ALIGN
    cpad = seg * SEG_ALIGN
    loc_off = jnp.cumsum(cpad, axis=1) - cpad
    rows_e = jnp.sum(cpad, axis=0)
    blocks_e = (rows_e + BM - 1) // BM
    bend = jnp.cumsum(blocks_e)
    bstart = bend - blocks_e
    glob_off = (bstart * BM)[None, :] + jnp.cumsum(cpad, axis=0) - cpad
    nb = bend[-1]
    bidx = jnp.arange(NB_MAX, dtype=I32)
    bexp = jnp.minimum(jnp.sum(bend[None, :] <= jnp.minimum(bidx, nb - 1)[:, None], axis=1), N_EXPERTS - 1).astype(I32)
    pad_off = bstart * BM + rows_e
    pad_seg = (blocks_e * BM - rows_e) // SEG_ALIGN
    i32 = lambda a: a.reshape(-1).astype(I32)
    return (i32(seg), i32(loc_off), i32(glob_off), i32(pad_seg), i32(pad_off), bexp, i32(nb))


def _copy_by_bits(k, bits, src_at, dst_at, sem, wait):
    off = jnp.int32(0)
    for bit in bits:
        rows = bit * SEG_ALIGN
        take = (k & bit) != 0

        @pl.when(take)
        def _(off=off, rows=rows):
            cp = pltpu.make_async_copy(src_at(off, rows), dst_at(off, rows), sem)
            if wait:
                cp.wait()
            else:
                cp.start()

        off = off + jnp.where(take, rows, 0)


def _segment_copies(seg_ref, loc_ref, glob_ref, t, src_of, dst_of, sem, wait):
    def body(e, carry):
        k = seg_ref[t * N_EXPERTS + e]
        lo = loc_ref[t * N_EXPERTS + e]
        go = glob_ref[t * N_EXPERTS + e]
        at = lambda fn: (lambda off, rows: fn(pl.multiple_of(lo + off, SEG_ALIGN), pl.multiple_of(go + off, SEG_ALIGN), rows))
        _copy_by_bits(k, SEG_BITS, at(src_of), at(dst_of), sem, wait)
        return carry

    lax.fori_loop(0, N_EXPERTS, body, 0)


def _dispatch_kernel(seg_ref, loc_ref, glob_ref, pseg_ref, poff_ref, nb_ref, x_ref, slot_ref, xs_ref, buf_ref, sem):
    t = pl.program_id(0)
    xb = x_ref[...].astype(BF16)
    for mc in range(M_T // M_CHUNK):
        mi = lax.broadcasted_iota(I32, (M_CHUNK, TM), 0) + mc * M_CHUNK
        onehot = jnp.zeros((M_CHUNK, TM), F32)
        for j in range(TOP_K):
            onehot = jnp.where(mi == slot_ref[j:j + 1, :], 1.0, onehot)
        buf_ref[mc * M_CHUNK:(mc + 1) * M_CHUNK, :] = _dot(onehot.astype(BF16), xb).astype(BF16)

    src = lambda lo, go, rows: buf_ref.at[pl.ds(lo, rows), :]
    dst = lambda lo, go, rows: xs_ref.at[pl.ds(go, rows), :]
    _segment_copies(seg_ref, loc_ref, glob_ref, t, src, dst, sem, wait=False)
    _segment_copies(seg_ref, loc_ref, glob_ref, t, src, dst, sem, wait=True)

    @pl.when(t == N_TILES - 1)
    def _():
        buf_ref[0:BM, :] = jnp.zeros((BM, D_MODEL), BF16)

        def fill(wait):
            def pad(e, carry):
                go = poff_ref[e]
                _copy_by_bits(pseg_ref[e], SEG_BITS[1:], lambda off, rows: buf_ref.at[pl.ds(0, rows), :],
                              lambda off, rows: xs_ref.at[pl.ds(pl.multiple_of(go + off, SEG_ALIGN), rows), :], sem, wait)
                return carry

            lax.fori_loop(0, N_EXPERTS, pad, 0)

            def tail(b, carry):
                cp = pltpu.make_async_copy(buf_ref.at[pl.ds(0, BM), :], xs_ref.at[pl.ds(pl.multiple_of(b * BM, BM), BM), :], sem)
                if wait:
                    cp.wait()
                else:
                    cp.start()
                return carry

            lax.fori_loop(nb_ref[0], NB_MAX, tail, 0)

        fill(False)
        fill(True)


def _dispatch(x1, slot_t, tables):
    grid_spec = pltpu.PrefetchScalarGridSpec(
        num_scalar_prefetch=len(tables),
        grid=(N_TILES,),
        in_specs=[pl.BlockSpec((TM, D_MODEL), lambda i, *_: (i, 0)), pl.BlockSpec((TOP_K, TM), lambda i, *_: (0, i))],
        out_specs=pl.BlockSpec(memory_space=pl.ANY),
        scratch_shapes=[pltpu.VMEM((M_T, D_MODEL), BF16), pltpu.SemaphoreType.DMA],
    )
    return pl.pallas_call(
        _dispatch_kernel,
        name="dispatch",
        grid_spec=grid_spec,
        out_shape=jax.ShapeDtypeStruct((NB_MAX * BM, D_MODEL), BF16),
        compiler_params=_cparams(("arbitrary",)),
    )(*tables, x1, slot_t)


def _expert_kernel(bexp_ref, nb_ref, xs_ref, wg_ref, wu_ref, wd_ref, ys_ref, wgu_s, wd_s):
    b = pl.program_id(0)
    nb = nb_ref[0]
    prev = bexp_ref[jnp.maximum(b - 1, 0)]
    fresh = jnp.logical_or(b == 0, bexp_ref[b] != prev)

    @pl.when(fresh)
    def _():
        wgu_s[:, :EXPERT_DIM] = wg_ref[0].astype(BF16)
        wgu_s[:, EXPERT_DIM:] = wu_ref[0].astype(BF16)
        wd_s[...] = wd_ref[0].astype(BF16)

    @pl.when(b < nb)
    def _():
        h = _dot(xs_ref[...], wgu_s[...])
        g = h[:, :EXPERT_DIM]
        a = (g * _sigmoid(g) * h[:, EXPERT_DIM:]).astype(BF16)
        ys_ref[...] = _dot(a, wd_s[...]).astype(BF16)

    @pl.when(b >= nb)
    def _():
        ys_ref[...] = jnp.zeros((BM, D_MODEL), BF16)


def _experts(xs, we_gate, we_up, we_down, bexp, nb):
    last = lambda b, nb_ref: jnp.minimum(b, nb_ref[0] - 1)
    grid_spec = pltpu.PrefetchScalarGridSpec(
        num_scalar_prefetch=2,
        grid=(NB_MAX,),
        in_specs=[pl.BlockSpec((BM, D_MODEL), lambda b, be, nb_ref: (last(b, nb_ref), 0)),
                  pl.BlockSpec((1, D_MODEL, EXPERT_DIM), lambda b, be, nb_ref: (be[b], 0, 0)),
                  pl.BlockSpec((1, D_MODEL, EXPERT_DIM), lambda b, be, nb_ref: (be[b], 0, 0)),
                  pl.BlockSpec((1, EXPERT_DIM, D_MODEL), lambda b, be, nb_ref: (be[b], 0, 0))],
        out_specs=pl.BlockSpec((BM, D_MODEL), lambda b, be, nb_ref: (b, 0)),
        scratch_shapes=[pltpu.VMEM((D_MODEL, 2 * EXPERT_DIM), BF16), pltpu.VMEM((EXPERT_DIM, D_MODEL), BF16)],
    )
    return pl.pallas_call(
        _expert_kernel,
        name="expert",
        grid_spec=grid_spec,
        out_shape=jax.ShapeDtypeStruct((NB_MAX * BM, D_MODEL), BF16),
        compiler_params=_cparams(("arbitrary",)),
    )(bexp, nb, xs, we_gate, we_up, we_down)


def _combine_kernel(seg_ref, loc_ref, glob_ref, x_ref, slot_ref, gate_ref, p_ref, ys_ref,
                    wsgu_ref, wsd_ref, g2_ref, b2_ref, wpg_ref, wpp_ref, g3_ref, b3_ref, o_ref, buf_ref, sem):
    t = pl.program_id(0)

    @pl.when(t == 0)
    def _():
        buf_ref[...] = jnp.zeros((M_T, D_MODEL), BF16)

    src = lambda lo, go, rows: ys_ref.at[pl.ds(go, rows), :]
    dst = lambda lo, go, rows: buf_ref.at[pl.ds(lo, rows), :]
    _segment_copies(seg_ref, loc_ref, glob_ref, t, src, dst, sem, wait=False)
    _segment_copies(seg_ref, loc_ref, glob_ref, t, src, dst, sem, wait=True)

    x = x_ref[...]
    routed = jnp.zeros((TM, D_MODEL), F32)
    for mc in range(M_T // M_CHUNK):
        mi = lax.broadcasted_iota(I32, (TM, M_CHUNK), 1) + mc * M_CHUNK
        pg = jnp.zeros((TM, M_CHUNK), F32)
        for j in range(TOP_K):
            pg = jnp.where(mi == slot_ref[:, j:j + 1], gate_ref[:, j:j + 1], pg)
        ph, plo = _split2(pg)
        rows = buf_ref[mc * M_CHUNK:(mc + 1) * M_CHUNK, :]
        routed = routed + _dot(ph, rows) + _dot(plo, rows)
    xb = x.astype(BF16)
    h = _dot(xb, wsgu_ref[...])
    g = h[:, :EXPERT_DIM]
    shared = _dot((g * _sigmoid(g) * h[:, EXPERT_DIM:]).astype(BF16), wsd_ref[...])
    x2 = _layer_norm(ALPHA * x + (routed + shared), g2_ref[...], b2_ref[...])
    ple = _sigmoid(_dot(x2.astype(BF16), wpg_ref[...])) * _dot(p_ref[...].astype(BF16), wpp_ref[...])
    o_ref[...] = _layer_norm(ALPHA * x2 + ple, g3_ref[...], b3_ref[...])


def _combine(x1, slot, gate, p_all, ys, seg, loc_off, glob_off, ws_gate, ws_up, ws_down, ln2_g, ln2_b,
             w_ple_gate, w_ple_proj, ln3_g, ln3_b):
    wsgu = jnp.concatenate([ws_gate, ws_up], axis=1).astype(BF16)
    wsd = ws_down.astype(BF16)
    wpg = w_ple_gate.astype(BF16)
    wpp = w_ple_proj.astype(BF16)
    row = lambda w: pl.BlockSpec((TM, w), lambda i, *_: (i, 0))
    full = lambda shape: pl.BlockSpec(shape, lambda i, *_: (0,) * len(shape))
    vec = full((1, D_MODEL))
    grid_spec = pltpu.PrefetchScalarGridSpec(
        num_scalar_prefetch=3,
        grid=(N_TILES,),
        in_specs=[row(D_MODEL), row(TOP_K), row(TOP_K), row(PLE_DIM), pl.BlockSpec(memory_space=pl.ANY),
                  full(wsgu.shape), full(wsd.shape), vec, vec, full(wpg.shape), full(wpp.shape), vec, vec],
        out_specs=row(D_MODEL),
        scratch_shapes=[pltpu.VMEM((M_T, D_MODEL), BF16), pltpu.SemaphoreType.DMA],
    )
    r = lambda a: a.reshape(1, -1)
    return pl.pallas_call(
        _combine_kernel,
        name="combine",
        grid_spec=grid_spec,
        out_shape=jax.ShapeDtypeStruct((N_PAD, D_MODEL), F32),
        compiler_params=_cparams(("arbitrary",)),
    )(seg, loc_off, glob_off, x1, slot, gate, p_all, ys, wsgu, wsd, r(ln2_g), r(ln2_b), wpg, wpp, r(ln3_g), r(ln3_b))


def _pad_rows(a, rows):
    return jnp.concatenate([a, jnp.zeros((rows - a.shape[0],) + a.shape[1:], a.dtype)], axis=0)


def kernel(x_prompt, x_sample, cache_k_win, cache_v_win, state_conv, state_delta, p_prompt, p_sample, w_in, conv_w, a_log, dt_bias, dn_norm_g, w_out, ln1_g, ln1_b, w_router, router_bias, we_gate, we_up, we_down, ws_gate, ws_up, ws_down, ln2_g, ln2_b, w_ple_gate, w_ple_proj, ln3_g, ln3_b):
    x_all = _pad_rows(jnp.concatenate([x_prompt.reshape(N_P, D_MODEL), x_sample.reshape(N_S, D_MODEL)], axis=0), N_PAD)
    p_all = _pad_rows(jnp.concatenate([p_prompt[0].reshape(N_P, PLE_DIM), p_sample[0].reshape(N_S, PLE_DIM)], axis=0), N_PAD)

    qkv, dn_in, z, b_logit, a_logit = _project(x_all, w_in[0])

    att_p = _attn_prompt(qkv)
    qkv_s = qkv[N_P:N_REAL].reshape(N_SAMPLE_B, DEC_SEQ, 3 * ATTN_W)
    qkv_s8 = jnp.pad(qkv_s, ((0, 0), (0, ROWS_S - DEC_SEQ), (0, 0)))
    ck = cache_k_win[0].reshape(N_SAMPLE_B, W_BUF, ATTN_W)
    cv = cache_v_win[0].reshape(N_SAMPLE_B, W_BUF, ATTN_W)
    att_s = _attn_sample(qkv_s8[:, :, :ATTN_W], qkv_s8[:, :, ATTN_W:2 * ATTN_W], qkv_s8[:, :, 2 * ATTN_W:], ck, cv)
    att_all = _pad_rows(jnp.concatenate([att_p, att_s[:, :DEC_SEQ].reshape(N_S, ATTN_W)], axis=0), N_PAD)

    nt_p = N_P // TM
    dn_p = dn_in[:N_P]
    tails = dn_p.reshape(nt_p, TM, CONV_CH)[:, TM - HALO:]
    halo_p = jnp.concatenate([jnp.zeros((1, HALO, CONV_CH), F32), tails[:-1]], axis=0)
    seq_start = (jnp.arange(nt_p) % (SEQ // TM) == 0)[:, None, None]
    halo_p = jnp.where(seq_start, 0.0, halo_p)
    q_p, k_p, v_p, bx_p, gx_p = _dn_pre(dn_p, halo_p, b_logit[:N_P], a_logit[:N_P], jnp.ones((N_P, 1), F32),
                                        conv_w[0], a_log[0], dt_bias[0])
    shp = (N_PROMPT_B, SEQ, DN_W)
    s0_p = jnp.zeros((N_PROMPT_B, N_HEADS // 2, LANES, LANES), F32)
    o_p, s_p = _delta(q_p.reshape(shp), k_p.reshape(shp), v_p.reshape(shp), bx_p.reshape(shp), gx_p.reshape(shp),
                      z[:N_P].reshape(shp), s0_p, dn_norm_g[0], N_PROMPT_B)

    def seq_pad(tok, state=None):
        w = tok.shape[-1]
        tok = tok.reshape(N_SAMPLE_B, DEC_SEQ, w)
        head = jnp.zeros((N_SAMPLE_B, S_TOK0, w), F32)
        if state is not None:
            head = head.at[:, S_TOK0 - (CONV_W - 1):].set(state)
        tail = jnp.zeros((N_SAMPLE_B, SEQ_S - S_TOK0 - DEC_SEQ, w), F32)
        return jnp.concatenate([head, tok, tail], axis=1).reshape(N_SAMPLE_B * SEQ_S, w)

    dn_s = seq_pad(dn_in[N_P:N_REAL], state_conv[0])
    rows_s = N_SAMPLE_B * SEQ_S
    valid_s = seq_pad(jnp.ones((N_S, 1), F32))
    halo_s = jnp.zeros((rows_s // TM, HALO, CONV_CH), F32)
    q_s, k_s, v_s, bx_s, gx_s = _dn_pre(dn_s, halo_s, seq_pad(b_logit[N_P:N_REAL]), seq_pad(a_logit[N_P:N_REAL]),
                                        valid_s, conv_w[0], a_log[0], dt_bias[0])
    shs = (N_SAMPLE_B, SEQ_S, DN_W)
    o_s, s_s = _delta(q_s.reshape(shs), k_s.reshape(shs), v_s.reshape(shs), bx_s.reshape(shs), gx_s.reshape(shs),
                      seq_pad(z[N_P:N_REAL]).reshape(shs), _state_to_bd(state_delta[0]), dn_norm_g[0], 4)
    dn_all = _pad_rows(jnp.concatenate([o_p.reshape(N_P, DN_W), o_s[:, S_TOK0:S_TOK0 + DEC_SEQ].reshape(N_S, DN_W)], axis=0), N_PAD)

    x1 = _out_proj(x_all, att_all, dn_all, w_out[0], ln1_g[0], ln1_b[0])

    gate_t, slot_t, cnt = _route(x1, w_router[0], router_bias[0])
    seg, loc_off, glob_off, pad_seg, pad_off, bexp, nb = _segment_tables(cnt[:, 0, :])
    xs = _dispatch(x1, slot_t, (seg, loc_off, glob_off, pad_seg, pad_off, nb))
    ys = _experts(xs, we_gate[0], we_up[0], we_down[0], bexp, nb)
    y = _combine(x1, slot_t.T, gate_t.T, p_all, ys, seg, loc_off, glob_off, ws_gate[0], ws_up[0], ws_down[0],
                 ln2_g[0], ln2_b[0], w_ple_gate[0], w_ple_proj[0], ln3_g[0], ln3_b[0])

    y_prompt = y[:N_P].reshape(N_PROMPT_B, SEQ, D_MODEL)
    y_sample = y[N_P:N_REAL].reshape(N_SAMPLE_B, DEC_SEQ, D_MODEL)
    heads = (N_HEADS, HEAD_DIM)
    k_pr = qkv[:N_P, ATTN_W:2 * ATTN_W].reshape(N_PROMPT_B, SEQ, *heads)[:, SEQ - W_BUF:]
    v_pr = qkv[:N_P, 2 * ATTN_W:].reshape(N_PROMPT_B, SEQ, *heads)[:, SEQ - W_BUF:]
    conv_p = dn_p.reshape(N_PROMPT_B, SEQ, CONV_CH)[:, SEQ - (CONV_W - 1):]
    k_new = qkv_s[:, :, ATTN_W:2 * ATTN_W].reshape(N_SAMPLE_B, DEC_SEQ, *heads)
    v_new = qkv_s[:, :, 2 * ATTN_W:].reshape(N_SAMPLE_B, DEC_SEQ, *heads)
    k_sm = jnp.concatenate([cache_k_win[0][:, DEC_SEQ:], k_new], axis=1)
    v_sm = jnp.concatenate([cache_v_win[0][:, DEC_SEQ:], v_new], axis=1)
    conv_s = dn_in[N_P:N_REAL].reshape(N_SAMPLE_B, DEC_SEQ, CONV_CH)[:, DEC_SEQ - (CONV_W - 1):]
    return (y_prompt, y_sample, k_pr[None], v_pr[None], conv_p[None], _state_from_bd(s_p)[None],
            k_sm[None], v_sm[None], conv_s[None], _state_from_bd(s_s)[None])
```

```python
import functools

import jax
import jax.numpy as jnp
import numpy as np
from jax import lax
from jax.experimental import pallas as pl
from jax.experimental.pallas import tpu as pltpu

F32 = jnp.float32
BF16 = jnp.bfloat16
I32 = jnp.int32

D_MODEL = 1024
N_PROMPT_B, SEQ = 2, 8192
N_SAMPLE_B, DEC_SEQ = 32, 4
W_BUF = 2048
N_HEADS = 8
HEAD_DIM = 64
ATTN_W = 512
CONV_CH = 1536
CONV_W = 4
DN_W = 512
N_EXPERTS = 64
TOP_K = 8
EXPERT_DIM = 256
PLE_DIM = 256
BRANCHES = ((128, 1), (512, 4), (2048, 16))
N_BACK = 128
ROUTED_SCALE = 2.5
LN_EPS = 1e-5
NORM_EPS = 1e-6
ALPHA = 2.0 ** 0.25

LANES = 128
HALF = 64
VMEM_LIMIT = 56 * 1024 * 1024

N_P = N_PROMPT_B * SEQ
N_S = N_SAMPLE_B * DEC_SEQ
N_REAL = N_P + N_S
N_PAD = N_P + 512
TM = 256
N_TILES = N_PAD // TM
NT_P = N_P // TM
N_SP = N_PAD - N_P

SEG_ALIGN = 16
M_T = 3072
M_CHUNK = 256
C_CHUNK = 256
BM = 512
R_MAX = N_PAD * TOP_K + N_TILES * N_EXPERTS * (SEG_ALIGN - 1) + N_EXPERTS * (BM - 1)
NB_MAX = -(-R_MAX // BM)
SEG_BITS = (16, 8, 4, 2, 1)

DN_C = 64
SEQ_S = 64
S_TOK0 = 8


def _dot(a, b):
    return jnp.dot(a, b, preferred_element_type=F32)


def _dot_nt(a, b):
    return lax.dot_general(a, b, (((1,), (1,)), ((), ())), preferred_element_type=F32)


def _split2(x):
    hi = x.astype(BF16)
    lo = (x - hi.astype(F32)).astype(BF16)
    return hi, lo


def _split3(x):
    a = x.astype(BF16)
    r = x - a.astype(F32)
    b = r.astype(BF16)
    c = (r - b.astype(F32)).astype(BF16)
    return a, b, c


def _dot3_l01(m01, x):
    a, b, c = _split3(x)
    return _dot(m01, a) + _dot(m01, b) + _dot(m01, c)


def _dot3_r01(x, m01):
    a, b, c = _split3(x)
    return _dot(a, m01) + _dot(b, m01) + _dot(c, m01)


def _sigmoid(x):
    return 1.0 / (1.0 + jnp.exp(-x))


def _layer_norm(x, g, b):
    mu = jnp.mean(x, axis=-1, keepdims=True)
    xc = x - mu
    var = jnp.mean(xc * xc, axis=-1, keepdims=True)
    return xc * lax.rsqrt(var + LN_EPS) * g + b


def _cparams(sem=None):
    return pltpu.CompilerParams(dimension_semantics=sem, vmem_limit_bytes=VMEM_LIMIT)


def _full(shape):
    return pl.BlockSpec(shape, lambda *_: (0,) * len(shape))


def _row2(w):
    return [pl.BlockSpec((TM, w), lambda i, *_: (jnp.minimum(i, NT_P - 1), 0)),
            pl.BlockSpec((TM, w), lambda i, *_: (jnp.maximum(i - NT_P, 0), 0))]


def _pick(p_ref, s_ref):
    return jnp.where(pl.program_id(0) < NT_P, p_ref[...], s_ref[...])


def _proj_kernel(xp_ref, xs_ref, wqkv_ref, wdn_ref, wz_ref, wbh_ref, wbl_ref, wah_ref, wal_ref,
                 qkv_ref, dn_ref, z_ref, b_ref, a_ref):
    x = _pick(xp_ref, xs_ref)
    xh, xl = _split2(x)
    qkv_ref[...] = _dot(xh, wqkv_ref[...])
    dn_ref[...] = _dot(xh, wdn_ref[...])
    z_ref[...] = _dot(xh, wz_ref[...])
    b_ref[...] = _dot(xh, wbh_ref[...]) + _dot(xl, wbh_ref[...]) + _dot(xh, wbl_ref[...])
    a_ref[...] = _dot(xh, wah_ref[...]) + _dot(xl, wah_ref[...]) + _dot(xh, wal_ref[...])


def _project(x_p, x_s, w_in):
    wqkv = w_in[:, :3 * ATTN_W].astype(BF16)
    wdn = w_in[:, 3 * ATTN_W:3 * ATTN_W + CONV_CH].astype(BF16)
    c0 = 3 * ATTN_W + CONV_CH
    wz = w_in[:, c0:c0 + DN_W].astype(BF16)
    wb = w_in[:, c0 + DN_W:c0 + DN_W + N_HEADS]
    wa = w_in[:, c0 + DN_W + N_HEADS:]
    wbh, wbl = _split2(wb)
    wah, wal = _split2(wa)
    row = lambda w: pl.BlockSpec((TM, w), lambda i: (i, 0))
    outs = (3 * ATTN_W, CONV_CH, DN_W, N_HEADS, N_HEADS)
    return pl.pallas_call(
        _proj_kernel,
        name="proj",
        grid=(N_TILES,),
        in_specs=_row2(D_MODEL) + [_full(wqkv.shape), _full(wdn.shape), _full(wz.shape),
                                   _full(wbh.shape), _full(wbl.shape), _full(wah.shape), _full(wal.shape)],
        out_specs=[row(w) for w in outs],
        out_shape=[jax.ShapeDtypeStruct((N_PAD, w), F32) for w in outs],
        compiler_params=_cparams(("arbitrary",)),
    )(x_p, x_s, wqkv, wdn, wz, wbh, wbl, wah, wal)


QT = 2048
ATTN_UNROLL = 4


def _attn_bias():
    qi = np.arange(N_BACK)[:, None]
    ki = np.arange(2 * N_BACK)[None, :]
    out = np.zeros((2, len(BRANCHES), N_HEADS, N_BACK, 2 * N_BACK), np.float32)
    for var, off in enumerate((N_BACK, 0)):
        dist = qi - ki + off
        valid = (dist >= 0) & (dist <= N_BACK)
        for di, (_, dil) in enumerate(BRANCHES):
            for h in range(N_HEADS):
                slope = 2.0 ** (-8.0 * (h + 1) / N_HEADS) * dil
                out[var, di, h] = np.where(valid, -slope * dist, -1e30)
    return out


def _attn_prompt_kernel(q_ref, k_ref, v_ref, bias_ref, o_ref, acc_ref, m_ref, l_ref):
    qt = pl.program_id(2)
    lane = lax.broadcasted_iota(I32, (1, LANES), 1)
    h0 = lane < HALF
    for di, (_, dil) in enumerate(BRANCHES):
        per_r = QT // (N_BACK * dil)

        def body(i, carry, di=di, dil=dil, per_r=per_r):
            def ds(start, n):
                return pl.ds(start, n) if dil == 1 else pl.ds(start, n, stride=dil)

            blocks = []
            for u in range(ATTN_UNROLL):
                blk = i * ATTN_UNROLL + u
                r = blk // per_r
                jb = blk % per_r
                qloc = r + dil * N_BACK * jb
                first = jnp.logical_and(qt == 0, jb == 0)
                kstart = jnp.where(first, r, qt * QT + qloc - N_BACK * dil)
                blocks.append((qloc, kstart, jnp.where(first, 1, 0)))
            q8 = [q_ref[ds(qloc, N_BACK), :] * (HEAD_DIM ** -0.5) for qloc, _, _ in blocks]
            kb = [k_ref[ds(kstart, 2 * N_BACK), :].astype(BF16) for _, kstart, _ in blocks]
            vv = [v_ref[ds(kstart, 2 * N_BACK), :] for _, kstart, _ in blocks]
            heads = [(u, hh) for u in range(ATTN_UNROLL) for hh in range(2)]
            keep = lambda hh: h0 if hh == 0 else jnp.logical_not(h0)
            s = [_dot_nt(jnp.where(keep(hh), q8[u], 0.0).astype(BF16), kb[u]) + bias_ref[blocks[u][2], di, hh]
                 for u, hh in heads]
            m = [jnp.max(x, axis=-1, keepdims=True) for x in s]
            p = [jnp.exp(x - mm).astype(BF16) for x, mm in zip(s, m)]
            res = [_dot(pp, jnp.where(keep(hh), vv[u], 1.0).astype(BF16)) for pp, (u, hh) in zip(p, heads)]
            for u, (qloc, _, _) in enumerate(blocks):
                rows = ds(qloc, N_BACK)
                acc_ref[di, rows, :] = jnp.where(h0, res[2 * u], res[2 * u + 1])
                l_ref[di, rows, :] = jnp.where(h0, res[2 * u + 1], res[2 * u])
                m_ref[di, rows, :] = jnp.where(h0, m[2 * u], m[2 * u + 1])
            return carry

        lax.fori_loop(0, QT // (N_BACK * ATTN_UNROLL), body, 0)

    mall = m_ref[...]
    mtot = jnp.max(mall, axis=0)
    num = jnp.zeros((QT, LANES), F32)
    den = jnp.zeros((QT, LANES), F32)
    for di in range(len(BRANCHES)):
        w = jnp.exp(mall[di] - mtot)
        num = num + w * acc_ref[di]
        den = den + w * pltpu.roll(l_ref[di], HALF, axis=1)
    o_ref[...] = num / den


def _attn_prompt(qkv):
    bias = jnp.asarray(_attn_bias())
    npair = N_HEADS // 2
    nqt = SEQ // QT
    kv_rows = SEQ
    q_spec = pl.BlockSpec((QT, LANES), lambda b, hp, t: (b * nqt + t, hp))
    k_spec = pl.BlockSpec((kv_rows, LANES), lambda b, hp, t: (b, npair + hp))
    v_spec = pl.BlockSpec((kv_rows, LANES), lambda b, hp, t: (b, 2 * npair + hp))
    bias_spec = pl.BlockSpec((2, len(BRANCHES), 2, N_BACK, 2 * N_BACK), lambda b, hp, t: (0, 0, hp, 0, 0))
    return pl.pallas_call(
        _attn_prompt_kernel,
        name="attn_prompt",
        grid=(N_PROMPT_B, npair, nqt),
        in_specs=[q_spec, k_spec, v_spec, bias_spec],
        out_specs=pl.BlockSpec((QT, LANES), lambda b, hp, t: (b * nqt + t, hp)),
        out_shape=jax.ShapeDtypeStruct((N_P, ATTN_W), F32),
        scratch_shapes=[pltpu.VMEM((len(BRANCHES), QT, LANES), F32)] * 3,
        compiler_params=_cparams(("arbitrary", "arbitrary", "arbitrary")),
    )(qkv, qkv, qkv, bias)


ROWS_S = 8


def _attn_sample_kernel(q_ref, kn_ref, vn_ref, ck_ref, cv_ref, e_ref, et_ref, slope_ref, o_ref, kall_ref, vall_ref):
    npair = N_HEADS // 2
    for hp in range(npair):
        sl = slice(hp * LANES, (hp + 1) * LANES)
        kall_ref[hp, 0:W_BUF, :] = ck_ref[0, :, sl]
        vall_ref[hp, 0:W_BUF, :] = cv_ref[0, :, sl]
        kall_ref[hp, W_BUF:W_BUF + ROWS_S, :] = kn_ref[0, :, sl]
        vall_ref[hp, W_BUF:W_BUF + ROWS_S, :] = vn_ref[0, :, sl]

    def rows(ref, start, n, dil):
        idx = pl.ds(start, n) if dil == 1 else pl.ds(start, n, stride=dil)
        return jnp.concatenate([ref[hp, idx, :] for hp in range(npair)], axis=1)

    e01 = e_ref[...]
    et01 = et_ref[...]
    slope = slope_ref[...]
    jrev = (N_BACK - lax.broadcasted_iota(I32, (N_BACK, 1), 0)).astype(F32)
    inew = lax.broadcasted_iota(I32, (ROWS_S, 1), 0)
    scale = HEAD_DIM ** -0.5
    knew = rows(kall_ref, W_BUF, ROWS_S, 1)
    vnew = rows(vall_ref, W_BUF, ROWS_S, 1).astype(BF16).astype(F32)

    def scores(kr, qt):
        ph, plo = _split2(kr * qt)
        return (_dot(ph, e01) + _dot(plo, e01)) * scale

    outs = []
    for t in range(DEC_SEQ):
        qt = q_ref[0, t:t + 1, :]
        s_self = jnp.where(inew == t, scores(knew, qt), -1e30)
        s_br = []
        v_br = []
        for _, dil in BRANCHES:
            start = W_BUF + t - N_BACK * dil
            s_br.append(scores(rows(kall_ref, start, N_BACK, dil), qt) - (slope * dil) * jrev)
            v_br.append(rows(vall_ref, start, N_BACK, dil))
        m = jnp.max(s_self, axis=0, keepdims=True)
        for s in s_br:
            m = jnp.maximum(m, jnp.max(s, axis=0, keepdims=True))
        pb = (jnp.exp(s_self - m) * float(len(BRANCHES))).astype(BF16)
        den = jnp.sum(pb.astype(F32), axis=0, keepdims=True)
        num = jnp.sum(_dot(pb, et01) * vnew, axis=0, keepdims=True)
        for s, vr in zip(s_br, v_br):
            pb = jnp.exp(s - m).astype(BF16)
            den = den + jnp.sum(pb.astype(F32), axis=0, keepdims=True)
            num = num + jnp.sum(_dot(pb, et01) * vr.astype(BF16).astype(F32), axis=0, keepdims=True)
        denx = _dot3_r01(jnp.broadcast_to(den, (ROWS_S, N_HEADS)), et01)[0:1]
        outs.append(num / denx)
    pad = jnp.zeros((ROWS_S - DEC_SEQ, ATTN_W), F32)
    o_ref[0] = jnp.concatenate(outs + [pad], axis=0)


def _head_expand():
    e = np.zeros((ATTN_W, N_HEADS), np.float32)
    for h in range(N_HEADS):
        e[h * HEAD_DIM:(h + 1) * HEAD_DIM, h] = 1.0
    return e


def _attn_sample(q_s, k_s, v_s, cache_k, cache_v):
    e = _head_expand()
    e01 = jnp.asarray(e, BF16)
    et01 = jnp.asarray(e.T, BF16)
    slopes = jnp.asarray([[2.0 ** (-8.0 * (h + 1) / N_HEADS) for h in range(N_HEADS)]], F32)
    row = pl.BlockSpec((1, ROWS_S, ATTN_W), lambda b: (b, 0, 0))
    cache = pl.BlockSpec((1, W_BUF, ATTN_W), lambda b: (b, 0, 0))
    return pl.pallas_call(
        _attn_sample_kernel,
        name="attn_sample",
        grid=(N_SAMPLE_B,),
        in_specs=[row, row, row, cache, cache, _full(e01.shape), _full(et01.shape), _full(slopes.shape)],
        out_specs=row,
        out_shape=jax.ShapeDtypeStruct((N_SAMPLE_B, ROWS_S, ATTN_W), F32),
        scratch_shapes=[pltpu.VMEM((N_HEADS // 2, W_BUF + ROWS_S, LANES), F32)] * 2,
        compiler_params=_cparams(("arbitrary",)),
    )(q_s, k_s, v_s, cache_k, cache_v, e01, et01, slopes)


HALO = 8


def _dn_pre_kernel(dn_ref, halo_ref, b_ref, a_ref, valid_ref, cw_ref, alog_ref, dtb_ref, e_ref, et_ref, ltri_ref,
                   q_ref, k_ref, v_ref, bx_ref, gx_ref, buf_ref):
    ts = dn_ref.shape[0]
    buf_ref[0:HALO, :] = halo_ref[0]
    buf_ref[HALO:HALO + ts, :] = dn_ref[...]
    y = jnp.zeros((ts, CONV_CH), F32)
    for i in range(CONV_W):
        y = y + buf_ref[pl.ds(HALO - (CONV_W - 1) + i, ts), :] * cw_ref[i:i + 1, :]
    y = y * _sigmoid(y)
    valid = valid_ref[...]
    e01 = e_ref[...]
    et01 = et_ref[...]

    def l2n(t):
        ss = _dot3_r01(t * t, e01)
        inv = lax.rsqrt(ss + NORM_EPS)
        return t * _dot3_r01(inv, et01)

    q_ref[...] = l2n(y[:, :DN_W]) * (HEAD_DIM ** -0.5)
    k_ref[...] = l2n(y[:, DN_W:2 * DN_W]) * valid
    v_ref[...] = y[:, 2 * DN_W:] * valid
    beta = _sigmoid(b_ref[...]) * valid
    sp_in = a_ref[...] + dtb_ref[...]
    softplus = jnp.maximum(sp_in, 0.0) + jnp.log1p(jnp.exp(-jnp.abs(sp_in)))
    g = -jnp.exp(alog_ref[...]) * softplus * valid
    bx_ref[...] = _dot3_r01(beta, et01)
    gx_ref[...] = _dot3_l01(ltri_ref[...], _dot3_r01(g, et01))


def _dn_pre(dn, halo, b_logit, a_logit, valid, conv_w, a_log, dt_bias):
    nt = halo.shape[0]
    rows = nt * TM
    e = _head_expand()
    e01 = jnp.asarray(e, BF16)
    et01 = jnp.asarray(e.T, BF16)
    ii = np.arange(TM)
    ltri = jnp.asarray(((ii[:, None] >= ii[None, :]) & (ii[:, None] // DN_C == ii[None, :] // DN_C)), BF16)
    row = lambda w: pl.BlockSpec((TM, w), lambda i: (i, 0))
    return pl.pallas_call(
        _dn_pre_kernel,
        name="dn_pre",
        grid=(nt,),
        in_specs=[row(CONV_CH), pl.BlockSpec((1, HALO, CONV_CH), lambda i: (i, 0, 0)), row(N_HEADS), row(N_HEADS),
                  row(1), _full(conv_w.shape), _full((1, N_HEADS)), _full((1, N_HEADS)),
                  _full(e01.shape), _full(et01.shape), _full(ltri.shape)],
        out_specs=[row(DN_W)] * 5,
        out_shape=[jax.ShapeDtypeStruct((rows, DN_W), F32)] * 5,
        scratch_shapes=[pltpu.VMEM((HALO + TM, CONV_CH), F32)],
        compiler_params=_cparams(("arbitrary",)),
    )(dn, halo, b_logit, a_logit, valid, conv_w, a_log.reshape(1, N_HEADS), dt_bias.reshape(1, N_HEADS),
      e01, et01, ltri)


def _delta_streams(q, k, v, bx, gcx, s_bd):
    c = DN_C
    n = len(q)
    each = range(n)
    lane = lax.broadcasted_iota(I32, (1, LANES), 1)
    h0 = lane < HALF
    m0 = h0.astype(F32)
    m1 = 1.0 - m0
    ii = lax.broadcasted_iota(I32, (2 * c, 2 * c), 0)
    jj = lax.broadcasted_iota(I32, (2 * c, 2 * c), 1)
    same = jnp.where(ii < c, 0, 1) == jnp.where(jj < c, 0, 1)
    order = jnp.where(same, ii - jj, -1)
    tril = order >= 0
    strict = order > 0
    eye = jnp.where(ii == jj, 1.0, 0.0)

    def stack(x):
        return jnp.concatenate([x * m0, x * m1], axis=0)

    def fold(x2):
        return x2[:c] + x2[c:]

    def decay_of(g):
        gsw = pltpu.roll(g, HALF, axis=1)
        row_b = jnp.concatenate([jnp.where(h0, g, gsw), jnp.where(h0, gsw, g)], axis=0)
        return jnp.where(tril, jnp.exp(jnp.where(tril, row_b - row_b.T, 0.0)), 0.0)

    eg = [jnp.exp(gcx[i]) for i in each]
    kb = [k[i] * bx[i] for i in each]
    glast = [gcx[i][c - 1:c, :] for i in each]
    decay = [decay_of(gcx[i]) for i in each]
    k2 = [stack(k[i]).astype(BF16) for i in each]
    lhs = [jnp.concatenate([stack(kb[i]), stack(q[i])], axis=0).astype(BF16) for i in each]
    aq = [_dot_nt(lhs[i], k2[i]) for i in each]
    a_low = [jnp.where(strict, aq[i][:2 * c] * decay[i], 0.0) for i in each]
    aqk = [jnp.where(tril, aq[i][2 * c:] * decay[i], 0.0).astype(BF16) for i in each]
    tinv = [eye - a_low[i] for i in each]
    apow = [a_low[i].astype(BF16) for i in each]
    for _ in range(5):
        apow = [_dot(apow[i], apow[i]).astype(BF16) for i in each]
        tinv = [tinv[i] + _dot(tinv[i].astype(BF16), apow[i]) for i in each]
    rhs = [jnp.concatenate([stack(v[i] * bx[i]), stack(kb[i] * eg[i])], axis=1).astype(BF16) for i in each]
    uw = [_dot(tinv[i].astype(BF16), rhs[i]) for i in each]
    u = [fold(uw[i][:, :LANES]) for i in each]
    w = [fold(uw[i][:, LANES:]) for i in each]
    ws = [_dot(jnp.concatenate([w[i], q[i] * eg[i]], axis=0).astype(BF16), s_bd[i].astype(BF16)) for i in each]
    v_new = [u[i] - ws[i][:c] for i in each]
    intra = [_dot(aqk[i], stack(v_new[i]).astype(BF16)) for i in each]
    kd = [(k[i] * jnp.exp(glast[i] - gcx[i])).T.astype(BF16) for i in each]
    upd = [_dot(kd[i], v_new[i].astype(BF16)) for i in each]
    o = [ws[i][c:] + fold(intra[i]) for i in each]
    s_new = [s_bd[i] * jnp.exp(glast[i]) + jnp.where(same, upd[i], 0.0) for i in each]
    return o, s_new


def _delta_kernel(q_ref, k_ref, v_ref, bx_ref, gx_ref, s0_ref, o_ref, sout_ref, s_ref):
    ci = pl.program_id(1)
    nb = q_ref.shape[0]
    npair = N_HEADS // 2

    @pl.when(ci == 0)
    def _():
        s_ref[...] = s0_ref[...]

    streams = [(b, hp) for b in range(nb) for hp in range(npair)]
    lanes = lambda hp: slice(hp * LANES, (hp + 1) * LANES)
    take = lambda ref: [ref[b, :, lanes(hp)] for b, hp in streams]
    o, s_new = _delta_streams(take(q_ref), take(k_ref), take(v_ref), take(bx_ref), take(gx_ref),
                              [s_ref[b, hp] for b, hp in streams])
    for i, (b, hp) in enumerate(streams):
        o_ref[b, :, lanes(hp)] = o[i]
        s_ref[b, hp] = s_new[i]

    @pl.when(ci == pl.num_programs(1) - 1)
    def _():
        sout_ref[...] = s_ref[...]


def _delta(q, k, v, bx, gx, s0_bd, bblk):
    nbatch, length, _ = q.shape
    npair = N_HEADS // 2
    seq = pl.BlockSpec((bblk, DN_C, DN_W), lambda g, c: (g, c, 0))
    st = pl.BlockSpec((bblk, npair, LANES, LANES), lambda g, c: (g, 0, 0, 0))
    return pl.pallas_call(
        _delta_kernel,
        name="delta",
        grid=(nbatch // bblk, length // DN_C),
        in_specs=[seq] * 5 + [st],
        out_specs=[seq, st],
        out_shape=[jax.ShapeDtypeStruct((nbatch, length, DN_W), F32),
                   jax.ShapeDtypeStruct((nbatch, npair, LANES, LANES), F32)],
        scratch_shapes=[pltpu.VMEM((bblk, npair, LANES, LANES), F32)],
        compiler_params=_cparams(("arbitrary", "arbitrary")),
    )(q, k, v, bx, gx, s0_bd)


def _state_to_bd(s):
    b = s.shape[0]
    s = s.reshape(b, N_HEADS // 2, 2, HALF, HALF)
    zero = jnp.zeros_like(s[:, :, 0])
    top = jnp.concatenate([s[:, :, 0], zero], axis=-1)
    bot = jnp.concatenate([zero, s[:, :, 1]], axis=-1)
    return jnp.concatenate([top, bot], axis=-2)


def _state_from_bd(s):
    b = s.shape[0]
    return jnp.stack([s[:, :, :HALF, :HALF], s[:, :, HALF:, HALF:]], axis=2).reshape(b, N_HEADS, HALF, HALF)


def _out_kernel(xp_ref, xs_ref, ap_ref, as_ref, dp_ref, ds_ref, z_ref, ng_ref, bavg_ref, wa_ref, wd_ref, g_ref, b_ref,
                o_ref):
    o = _pick(dp_ref, ds_ref)
    oh, ol = _split2(o * o)
    ms = _dot(oh, bavg_ref[...]) + _dot(ol, bavg_ref[...])
    z = z_ref[...]
    dn = o * lax.rsqrt(ms + NORM_EPS) * ng_ref[...] * (z * _sigmoid(z))
    mix = _dot(_pick(ap_ref, as_ref).astype(BF16), wa_ref[...]) + _dot(dn.astype(BF16), wd_ref[...])
    o_ref[...] = _layer_norm(ALPHA * _pick(xp_ref, xs_ref) + mix, g_ref[...], b_ref[...])


def _out_proj(x_p, x_s, att_p, att_s, dn_p, dn_s, z, norm_g, w_out, g, b):
    wa = w_out[:ATTN_W].astype(BF16)
    wd = w_out[ATTN_W:].astype(BF16)
    normg = jnp.tile(norm_g.reshape(1, HEAD_DIM), (1, N_HEADS))
    bavg = jnp.asarray(np.kron(np.eye(N_HEADS), np.ones((HALF, HALF))) / HALF, BF16)
    row = lambda w: pl.BlockSpec((TM, w), lambda i: (i, 0))
    return pl.pallas_call(
        _out_kernel,
        name="out_proj",
        grid=(N_TILES,),
        in_specs=_row2(D_MODEL) + _row2(ATTN_W) + _row2(DN_W) + [
            row(DN_W), _full(normg.shape), _full(bavg.shape),
            _full(wa.shape), _full(wd.shape), _full((1, D_MODEL)), _full((1, D_MODEL))],
        out_specs=row(D_MODEL),
        out_shape=jax.ShapeDtypeStruct((N_PAD, D_MODEL), F32),
        compiler_params=_cparams(("arbitrary",)),
    )(x_p, x_s, att_p, att_s, dn_p, dn_s, z, normg, bavg, wa, wd, g.reshape(1, -1), b.reshape(1, -1))


N_GROUPS = 8
GROUP = N_EXPERTS // N_GROUPS
TOPK_GROUPS = 4
NEG = -jnp.inf


def _route_kernel(x_ref, wh_ref, wl_ref, bias_ref, upper_ref, lows_ref, gate_ref, slot_ref, cnt_ref):
    t = pl.program_id(0)
    xh, xl = _split2(x_ref[...])
    wh = wh_ref[...]
    logits = _dot_nt(wh, xh) + _dot_nt(wh, xl) + _dot_nt(wl_ref[...], xh)
    scores = _sigmoid(logits)
    choice = scores + bias_ref[...]
    i8 = lax.broadcasted_iota(I32, (GROUP, TM), 0)

    def first_max(vals):
        m = jnp.max(vals, axis=0, keepdims=True)
        idx = jnp.min(jnp.where(vals == m, i8, GROUP), axis=0, keepdims=True)
        return m, idx

    pieces = [choice[g * GROUP:(g + 1) * GROUP, :] for g in range(N_GROUPS)]
    spieces = [scores[g * GROUP:(g + 1) * GROUP, :] for g in range(N_GROUPS)]
    gsc = jnp.zeros((N_GROUPS, TM), F32)
    for g in range(N_GROUPS):
        m1, idx1 = first_max(pieces[g])
        m2 = jnp.max(jnp.where(i8 == idx1, NEG, pieces[g]), axis=0, keepdims=True)
        gsc = jnp.where(i8 == g, m1 + m2, gsc)
    gsel = jnp.zeros((N_GROUPS, TM), F32)
    for _ in range(TOPK_GROUPS):
        _, idx = first_max(gsc)
        hit = i8 == idx
        gsel = jnp.where(hit, 1.0, gsel)
        gsc = jnp.where(hit, NEG, gsc)
    masked = [jnp.where(gsel[g:g + 1, :] > 0.5, pieces[g], NEG) for g in range(N_GROUPS)]
    eidx = [i8 + g * GROUP for g in range(N_GROUPS)]
    member = [jnp.zeros((GROUP, TM), F32) for _ in range(N_GROUPS)]
    tops = []
    graw = []
    for _ in range(TOP_K):
        mm = masked[0]
        for g in range(1, N_GROUPS):
            mm = jnp.maximum(mm, masked[g])
        m = jnp.max(mm, axis=0, keepdims=True)
        cand = jnp.where(masked[0] == m, eidx[0], N_EXPERTS)
        for g in range(1, N_GROUPS):
            cand = jnp.minimum(cand, jnp.where(masked[g] == m, eidx[g], N_EXPERTS))
        idx = jnp.min(cand, axis=0, keepdims=True)
        gsum = jnp.zeros((GROUP, TM), F32)
        for g in range(N_GROUPS):
            hit = eidx[g] == idx
            gsum = gsum + jnp.where(hit, spieces[g], 0.0)
            member[g] = jnp.where(hit, 1.0, member[g])
            masked[g] = jnp.where(hit, NEG, masked[g])
        tops.append(idx)
        graw.append(jnp.sum(gsum, axis=0, keepdims=True))
    gtot = graw[0]
    for j in range(1, TOP_K):
        gtot = gtot + graw[j]
    gates = jnp.zeros((TOP_K, TM), F32)
    for j in range(TOP_K):
        gates = jnp.where(i8 == j, graw[j] / gtot * ROUTED_SCALE, gates)
    gate_ref[...] = gates

    tok = t * TM + lax.broadcasted_iota(I32, (1, TM), 1)
    real = tok < N_REAL
    mem = jnp.concatenate(member, axis=0) * jnp.where(real, 1.0, 0.0)
    memb = mem.astype(BF16)
    rank = _dot(memb, upper_ref[...])
    cnt = jnp.sum(mem, axis=1, keepdims=True)
    cpad = jnp.floor((cnt + (SEG_ALIGN - 1)) * (1.0 / SEG_ALIGN))
    loc = _dot(lows_ref[...], jnp.broadcast_to(cpad, (N_EXPERTS, TM)).astype(BF16)) * SEG_ALIGN
    slot_e = loc + rank
    slots = jnp.zeros((TOP_K, TM), F32)
    for j in range(TOP_K):
        acc = jnp.zeros((GROUP, TM), F32)
        for g in range(N_GROUPS):
            acc = acc + jnp.where(eidx[g] == tops[j], slot_e[g * GROUP:(g + 1) * GROUP, :], 0.0)
        slots = jnp.where(i8 == j, jnp.sum(acc, axis=0, keepdims=True), slots)
    slots = jnp.where(real, slots, float(M_T))
    slot_ref[...] = slots.astype(I32)
    ones = jnp.ones((GROUP, TM), BF16)
    cnt_ref[0] = _dot_nt(ones, memb).astype(I32)


def _route(x1, w_router, router_bias):
    wt = w_router.T
    wh, wl = _split2(wt)
    ii = np.arange(TM)
    upper = jnp.asarray(ii[:, None] < ii[None, :], BF16)
    ee = np.arange(N_EXPERTS)
    lows = jnp.asarray(ee[:, None] > ee[None, :], BF16)
    return pl.pallas_call(
        _route_kernel,
        name="route",
        grid=(N_TILES,),
        in_specs=[pl.BlockSpec((TM, D_MODEL), lambda i: (i, 0)), _full(wh.shape), _full(wl.shape),
                  _full((N_EXPERTS, 1)), _full(upper.shape), _full(lows.shape)],
        out_specs=[pl.BlockSpec((TOP_K, TM), lambda i: (0, i)), pl.BlockSpec((TOP_K, TM), lambda i: (0, i)),
                   pl.BlockSpec((1, GROUP, N_EXPERTS), lambda i: (i, 0, 0))],
        out_shape=[jax.ShapeDtypeStruct((TOP_K, N_PAD), F32), jax.ShapeDtypeStruct((TOP_K, N_PAD), I32),
                   jax.ShapeDtypeStruct((N_TILES, GROUP, N_EXPERTS), I32)],
        compiler_params=_cparams(("arbitrary",)),
    )(x1, wh, wl, router_bias.reshape(N_EXPERTS, 1), upper, lows)


def _segment_tables(cnt):
    seg = (cnt + SEG_ALIGN - 1) // SEG_ALIGN
    cpad = seg * SEG_ALIGN
    loc_off = jnp.cumsum(cpad, axis=1) - cpad
    rows_e = jnp.sum(cpad, axis=0)
    blocks_e = (rows_e + BM - 1) // BM
    bend = jnp.cumsum(blocks_e)
    bstart = bend - blocks_e
    glob_off = (bstart * BM)[None, :] + jnp.cumsum(cpad, axis=0) - cpad
    nb = bend[-1]
    bidx = jnp.arange(NB_MAX, dtype=I32)
    bexp = jnp.minimum(jnp.sum(bend[None, :] <= jnp.minimum(bidx, nb - 1)[:, None], axis=1), N_EXPERTS - 1).astype(I32)
    pad_off = bstart * BM + rows_e
    pad_seg = (blocks_e * BM - rows_e) // SEG_ALIGN
    i32 = lambda a: a.reshape(-1).astype(I32)
    return (i32(seg), i32(loc_off), i32(glob_off), i32(pad_seg), i32(pad_off), bexp, i32(nb))


def _copy_by_bits(k, bits, src_at, dst_at, sem, wait):
    off = jnp.int32(0)
    for bit in bits:
        rows = bit * SEG_ALIGN
        take = (k & bit) != 0

        @pl.when(take)
        def _(off=off, rows=rows):
            cp = pltpu.make_async_copy(src_at(off, rows), dst_at(off, rows), sem)
            if wait:
                cp.wait()
            else:
                cp.start()

        off = off + jnp.where(take, rows, 0)


def _segment_copies(seg_ref, loc_ref, glob_ref, t, src_of, dst_of, sem, wait):
    def body(e, carry):
        k = seg_ref[t * N_EXPERTS + e]
        lo = loc_ref[t * N_EXPERTS + e]
        go = glob_ref[t * N_EXPERTS + e]
        at = lambda fn: (lambda off, rows: fn(pl.multiple_of(lo + off, SEG_ALIGN), pl.multiple_of(go + off, SEG_ALIGN), rows))
        _copy_by_bits(k, SEG_BITS, at(src_of), at(dst_of), sem, wait)
        return carry

    lax.fori_loop(0, N_EXPERTS, body, 0)


def _dispatch_kernel(seg_ref, loc_ref, glob_ref, pseg_ref, poff_ref, nb_ref, x_ref, slot_ref, xs_ref, buf_ref, sem):
    t = pl.program_id(0)
    last = t * N_EXPERTS + N_EXPERTS - 1
    used = loc_ref[last] + seg_ref[last] * SEG_ALIGN
    xb = x_ref[...].astype(BF16)
    for mc in range(M_T // M_CHUNK):
        @pl.when(mc * M_CHUNK < used)
        def _(mc=mc):
            mi = lax.broadcasted_iota(I32, (M_CHUNK, TM), 0) + mc * M_CHUNK
            onehot = jnp.zeros((M_CHUNK, TM), F32)
            for j in range(TOP_K):
                onehot = jnp.where(mi == slot_ref[j:j + 1, :], 1.0, onehot)
            buf_ref[mc * M_CHUNK:(mc + 1) * M_CHUNK, :] = _dot(onehot.astype(BF16), xb).astype(BF16)

    src = lambda lo, go, rows: buf_ref.at[pl.ds(lo, rows), :]
    dst = lambda lo, go, rows: xs_ref.at[pl.ds(go, rows), :]
    _segment_copies(seg_ref, loc_ref, glob_ref, t, src, dst, sem, wait=False)
    _segment_copies(seg_ref, loc_ref, glob_ref, t, src, dst, sem, wait=True)

    @pl.when(t == N_TILES - 1)
    def _():
        buf_ref[0:BM, :] = jnp.zeros((BM, D_MODEL), BF16)

        def fill(wait):
            def pad(e, carry):
                go = poff_ref[e]
                _copy_by_bits(pseg_ref[e], SEG_BITS, lambda off, rows: buf_ref.at[pl.ds(0, rows), :],
                              lambda off, rows: xs_ref.at[pl.ds(pl.multiple_of(go + off, SEG_ALIGN), rows), :], sem, wait)
                return carry

            lax.fori_loop(0, N_EXPERTS, pad, 0)

            def tail(b, carry):
                cp = pltpu.make_async_copy(buf_ref.at[pl.ds(0, BM), :], xs_ref.at[pl.ds(pl.multiple_of(b * BM, BM), BM), :], sem)
                if wait:
                    cp.wait()
                else:
                    cp.start()
                return carry

            lax.fori_loop(nb_ref[0], NB_MAX, tail, 0)

        fill(False)
        fill(True)


def _dispatch(x1, slot_t, tables):
    grid_spec = pltpu.PrefetchScalarGridSpec(
        num_scalar_prefetch=len(tables),
        grid=(N_TILES,),
        in_specs=[pl.BlockSpec((TM, D_MODEL), lambda i, *_: (i, 0)), pl.BlockSpec((TOP_K, TM), lambda i, *_: (0, i))],
        out_specs=pl.BlockSpec(memory_space=pl.ANY),
        scratch_shapes=[pltpu.VMEM((M_T, D_MODEL), BF16), pltpu.SemaphoreType.DMA],
    )
    return pl.pallas_call(
        _dispatch_kernel,
        name="dispatch",
        grid_spec=grid_spec,
        out_shape=jax.ShapeDtypeStruct((NB_MAX * BM, D_MODEL), BF16),
        compiler_params=_cparams(("arbitrary",)),
    )(*tables, x1, slot_t)


def _expert_kernel(bexp_ref, nb_ref, xs_ref, wg_ref, wu_ref, wd_ref, ys_ref, wgu_s, wd_s):
    b = pl.program_id(0)
    nb = nb_ref[0]
    prev = bexp_ref[jnp.maximum(b - 1, 0)]
    fresh = jnp.logical_or(b == 0, bexp_ref[b] != prev)

    @pl.when(fresh)
    def _():
        wgu_s[:, :EXPERT_DIM] = wg_ref[0].astype(BF16)
        wgu_s[:, EXPERT_DIM:] = wu_ref[0].astype(BF16)
        wd_s[...] = wd_ref[0].astype(BF16)

    @pl.when(b < nb)
    def _():
        h = _dot(xs_ref[...], wgu_s[...])
        g = h[:, :EXPERT_DIM]
        a = (g * _sigmoid(g) * h[:, EXPERT_DIM:]).astype(BF16)
        ys_ref[...] = _dot(a, wd_s[...]).astype(BF16)

    @pl.when(b >= nb)
    def _():
        ys_ref[...] = jnp.zeros((BM, D_MODEL), BF16)


def _experts(xs, we_gate, we_up, we_down, bexp, nb):
    last = lambda b, nb_ref: jnp.minimum(b, nb_ref[0] - 1)
    grid_spec = pltpu.PrefetchScalarGridSpec(
        num_scalar_prefetch=2,
        grid=(NB_MAX,),
        in_specs=[pl.BlockSpec((BM, D_MODEL), lambda b, be, nb_ref: (last(b, nb_ref), 0)),
                  pl.BlockSpec((1, D_MODEL, EXPERT_DIM), lambda b, be, nb_ref: (be[b], 0, 0)),
                  pl.BlockSpec((1, D_MODEL, EXPERT_DIM), lambda b, be, nb_ref: (be[b], 0, 0)),
                  pl.BlockSpec((1, EXPERT_DIM, D_MODEL), lambda b, be, nb_ref: (be[b], 0, 0))],
        out_specs=pl.BlockSpec((BM, D_MODEL), lambda b, be, nb_ref: (b, 0)),
        scratch_shapes=[pltpu.VMEM((D_MODEL, 2 * EXPERT_DIM), BF16), pltpu.VMEM((EXPERT_DIM, D_MODEL), BF16)],
    )
    return pl.pallas_call(
        _expert_kernel,
        name="expert",
        grid_spec=grid_spec,
        out_shape=jax.ShapeDtypeStruct((NB_MAX * BM, D_MODEL), BF16),
        compiler_params=_cparams(("arbitrary",)),
    )(bexp, nb, xs, we_gate, we_up, we_down)


def _combine_kernel(seg_ref, loc_ref, glob_ref, x_ref, slot_ref, gate_ref, pp_ref, ps_ref, ys_ref,
                    wsgu_ref, wsd_ref, g2_ref, b2_ref, wpg_ref, wpp_ref, g3_ref, b3_ref, op_ref, os_ref, buf_ref, sem):
    t = pl.program_id(0)
    last = t * N_EXPERTS + N_EXPERTS - 1
    used = loc_ref[last] + seg_ref[last] * SEG_ALIGN

    @pl.when(t == 0)
    def _():
        buf_ref[...] = jnp.zeros((M_T, D_MODEL), BF16)

    src = lambda lo, go, rows: ys_ref.at[pl.ds(go, rows), :]
    dst = lambda lo, go, rows: buf_ref.at[pl.ds(lo, rows), :]
    _segment_copies(seg_ref, loc_ref, glob_ref, t, src, dst, sem, wait=False)
    _segment_copies(seg_ref, loc_ref, glob_ref, t, src, dst, sem, wait=True)

    x = x_ref[...]
    routed = jnp.zeros((TM, D_MODEL), F32)
    for mc in range(M_T // C_CHUNK):
        def add(routed=routed, mc=mc):
            mi = lax.broadcasted_iota(I32, (TM, C_CHUNK), 1) + mc * C_CHUNK
            pg = jnp.zeros((TM, C_CHUNK), F32)
            for j in range(TOP_K):
                pg = jnp.where(mi == slot_ref[:, j:j + 1], gate_ref[:, j:j + 1], pg)
            return routed + _dot(pg.astype(BF16), buf_ref[mc * C_CHUNK:(mc + 1) * C_CHUNK, :])

        routed = lax.cond(mc * C_CHUNK < used, add, lambda routed=routed: routed)
    xb = x.astype(BF16)
    h = _dot(xb, wsgu_ref[...])
    g = h[:, :EXPERT_DIM]
    shared = _dot((g * _sigmoid(g) * h[:, EXPERT_DIM:]).astype(BF16), wsd_ref[...])
    x2 = _layer_norm(ALPHA * x + (routed + shared), g2_ref[...], b2_ref[...])
    ple = _sigmoid(_dot(x2.astype(BF16), wpg_ref[...])) * _dot(_pick(pp_ref, ps_ref).astype(BF16), wpp_ref[...])
    y = _layer_norm(ALPHA * x2 + ple, g3_ref[...], b3_ref[...])

    @pl.when(t < NT_P)
    def _():
        op_ref[...] = y

    @pl.when(t >= NT_P)
    def _():
        os_ref[...] = y


def _combine(x1, slot, gate, p_p, p_s, ys, seg, loc_off, glob_off, ws_gate, ws_up, ws_down, ln2_g, ln2_b,
             w_ple_gate, w_ple_proj, ln3_g, ln3_b):
    wsgu = jnp.concatenate([ws_gate, ws_up], axis=1).astype(BF16)
    wsd = ws_down.astype(BF16)
    wpg = w_ple_gate.astype(BF16)
    wpp = w_ple_proj.astype(BF16)
    row = lambda w: pl.BlockSpec((TM, w), lambda i, *_: (i, 0))
    full = lambda shape: pl.BlockSpec(shape, lambda i, *_: (0,) * len(shape))
    vec = full((1, D_MODEL))
    grid_spec = pltpu.PrefetchScalarGridSpec(
        num_scalar_prefetch=3,
        grid=(N_TILES,),
        in_specs=[row(D_MODEL), row(TOP_K), row(TOP_K)] + _row2(PLE_DIM) + [
            pl.BlockSpec(memory_space=pl.ANY),
            full(wsgu.shape), full(wsd.shape), vec, vec, full(wpg.shape), full(wpp.shape), vec, vec],
        out_specs=_row2(D_MODEL),
        scratch_shapes=[pltpu.VMEM((M_T, D_MODEL), BF16), pltpu.SemaphoreType.DMA],
    )
    r = lambda a: a.reshape(1, -1)
    return pl.pallas_call(
        _combine_kernel,
        name="combine",
        grid_spec=grid_spec,
        out_shape=[jax.ShapeDtypeStruct((N_P, D_MODEL), F32), jax.ShapeDtypeStruct((N_SP, D_MODEL), F32)],
        compiler_params=_cparams(("arbitrary",)),
    )(seg, loc_off, glob_off, x1, slot, gate, p_p, p_s, ys, wsgu, wsd, r(ln2_g), r(ln2_b), wpg, wpp,
      r(ln3_g), r(ln3_b))


def _pad_rows(a, rows):
    return jnp.concatenate([a, jnp.zeros((rows - a.shape[0],) + a.shape[1:], a.dtype)], axis=0)


def kernel(x_prompt, x_sample, cache_k_win, cache_v_win, state_conv, state_delta, p_prompt, p_sample, w_in, conv_w, a_log, dt_bias, dn_norm_g, w_out, ln1_g, ln1_b, w_router, router_bias, we_gate, we_up, we_down, ws_gate, ws_up, ws_down, ln2_g, ln2_b, w_ple_gate, w_ple_proj, ln3_g, ln3_b):
    x_p = x_prompt.reshape(N_P, D_MODEL)
    x_s = _pad_rows(x_sample.reshape(N_S, D_MODEL), N_SP)
    p_p = p_prompt[0].reshape(N_P, PLE_DIM)
    p_s = _pad_rows(p_sample[0].reshape(N_S, PLE_DIM), N_SP)

    qkv, dn_in, z, b_logit, a_logit = _project(x_p, x_s, w_in[0])

    att_p = _attn_prompt(qkv)
    qkv_s = qkv[N_P:N_REAL].reshape(N_SAMPLE_B, DEC_SEQ, 3 * ATTN_W)
    qkv_s8 = jnp.pad(qkv_s, ((0, 0), (0, ROWS_S - DEC_SEQ), (0, 0)))
    ck = cache_k_win[0].reshape(N_SAMPLE_B, W_BUF, ATTN_W)
    cv = cache_v_win[0].reshape(N_SAMPLE_B, W_BUF, ATTN_W)
    att_s = _attn_sample(qkv_s8[:, :, :ATTN_W], qkv_s8[:, :, ATTN_W:2 * ATTN_W], qkv_s8[:, :, 2 * ATTN_W:], ck, cv)
    att_s = _pad_rows(att_s[:, :DEC_SEQ].reshape(N_S, ATTN_W), N_SP)

    tails = dn_in.reshape(N_TILES, TM, CONV_CH)[:NT_P - 1, TM - HALO:]
    halo_p = jnp.concatenate([jnp.zeros((1, HALO, CONV_CH), F32), tails], axis=0)
    seq_start = (jnp.arange(NT_P) % (SEQ // TM) == 0)[:, None, None]
    halo_p = jnp.where(seq_start, 0.0, halo_p)
    q_p, k_p, v_p, bx_p, gx_p = _dn_pre(dn_in, halo_p, b_logit, a_logit, jnp.ones((N_P, 1), F32),
                                        conv_w[0], a_log[0], dt_bias[0])
    shp = (N_PROMPT_B, SEQ, DN_W)
    s0_p = jnp.zeros((N_PROMPT_B, N_HEADS // 2, LANES, LANES), F32)
    o_p, s_p = _delta(q_p.reshape(shp), k_p.reshape(shp), v_p.reshape(shp), bx_p.reshape(shp), gx_p.reshape(shp),
                      s0_p, N_PROMPT_B)

    def seq_pad(tok, state=None):
        w = tok.shape[-1]
        tok = tok.reshape(N_SAMPLE_B, DEC_SEQ, w)
        head = jnp.zeros((N_SAMPLE_B, S_TOK0, w), F32)
        if state is not None:
            head = head.at[:, S_TOK0 - (CONV_W - 1):].set(state)
        tail = jnp.zeros((N_SAMPLE_B, SEQ_S - S_TOK0 - DEC_SEQ, w), F32)
        return jnp.concatenate([head, tok, tail], axis=1).reshape(N_SAMPLE_B * SEQ_S, w)

    dn_tok_s = dn_in[N_P:N_REAL]
    dn_s = seq_pad(dn_tok_s, state_conv[0])
    rows_s = N_SAMPLE_B * SEQ_S
    valid_s = seq_pad(jnp.ones((N_S, 1), F32))
    halo_s = jnp.zeros((rows_s // TM, HALO, CONV_CH), F32)
    q_s, k_s, v_s, bx_s, gx_s = _dn_pre(dn_s, halo_s, seq_pad(b_logit[N_P:N_REAL]), seq_pad(a_logit[N_P:N_REAL]),
                                        valid_s, conv_w[0], a_log[0], dt_bias[0])
    shs = (N_SAMPLE_B, SEQ_S, DN_W)
    o_s, s_s = _delta(q_s.reshape(shs), k_s.reshape(shs), v_s.reshape(shs), bx_s.reshape(shs), gx_s.reshape(shs),
                      _state_to_bd(state_delta[0]), 4)
    dn_o_s = _pad_rows(o_s[:, S_TOK0:S_TOK0 + DEC_SEQ].reshape(N_S, DN_W), N_SP)

    x1 = _out_proj(x_p, x_s, att_p, att_s, o_p.reshape(N_P, DN_W), dn_o_s, z, dn_norm_g[0], w_out[0], ln1_g[0], ln1_b[0])

    gate_t, slot_t, cnt = _route(x1, w_router[0], router_bias[0])
    seg, loc_off, glob_off, pad_seg, pad_off, bexp, nb = _segment_tables(cnt[:, 0, :])
    xs = _dispatch(x1, slot_t, (seg, loc_off, glob_off, pad_seg, pad_off, nb))
    ys = _experts(xs, we_gate[0], we_up[0], we_down[0], bexp, nb)
    y_p, y_s = _combine(x1, slot_t.T, gate_t.T, p_p, p_s, ys, seg, loc_off, glob_off, ws_gate[0], ws_up[0], ws_down[0],
                        ln2_g[0], ln2_b[0], w_ple_gate[0], w_ple_proj[0], ln3_g[0], ln3_b[0])

    y_prompt = y_p.reshape(N_PROMPT_B, SEQ, D_MODEL)
    y_sample = y_s[:N_S].reshape(N_SAMPLE_B, DEC_SEQ, D_MODEL)
    heads = (N_HEADS, HEAD_DIM)
    win = lambda c0: jnp.stack([qkv[(b + 1) * SEQ - W_BUF:(b + 1) * SEQ, c0:c0 + ATTN_W] for b in range(N_PROMPT_B)])
    k_pr = win(ATTN_W).reshape(N_PROMPT_B, W_BUF, *heads)
    v_pr = win(2 * ATTN_W).reshape(N_PROMPT_B, W_BUF, *heads)
    conv_p = jnp.stack([dn_in[(b + 1) * SEQ - (CONV_W - 1):(b + 1) * SEQ] for b in range(N_PROMPT_B)])
    k_new = qkv_s[:, :, ATTN_W:2 * ATTN_W].reshape(N_SAMPLE_B, DEC_SEQ, *heads)
    v_new = qkv_s[:, :, 2 * ATTN_W:].reshape(N_SAMPLE_B, DEC_SEQ, *heads)
    k_sm = jnp.concatenate([cache_k_win[0][:, DEC_SEQ:], k_new], axis=1)
    v_sm = jnp.concatenate([cache_v_win[0][:, DEC_SEQ:], v_new], axis=1)
    conv_s = dn_tok_s.reshape(N_SAMPLE_B, DEC_SEQ, CONV_CH)[:, DEC_SEQ - (CONV_W - 1):]
    return (y_prompt, y_sample, k_pr[None], v_pr[None], conv_p[None], _state_from_bd(s_p)[None],
            k_sm[None], v_sm[None], conv_s[None], _state_from_bd(s_s)[None])
```

```python
import functools

import jax
import jax.numpy as jnp
import numpy as np
from jax import lax
from jax.experimental import pallas as pl
from jax.experimental.pallas import tpu as pltpu

F32 = jnp.float32
BF16 = jnp.bfloat16
I32 = jnp.int32

D_MODEL = 1024
N_PROMPT_B, SEQ = 2, 8192
N_SAMPLE_B, DEC_SEQ = 32, 4
W_BUF = 2048
N_HEADS = 8
HEAD_DIM = 64
ATTN_W = 512
CONV_CH = 1536
CONV_W = 4
DN_W = 512
N_EXPERTS = 64
TOP_K = 8
EXPERT_DIM = 256
PLE_DIM = 256
BRANCHES = ((128, 1), (512, 4), (2048, 16))
N_BACK = 128
ROUTED_SCALE = 2.5
LN_EPS = 1e-5
NORM_EPS = 1e-6
ALPHA = 2.0 ** 0.25

LANES = 128
HALF = 64
VMEM_LIMIT = 56 * 1024 * 1024

N_P = N_PROMPT_B * SEQ
N_S = N_SAMPLE_B * DEC_SEQ
N_REAL = N_P + N_S
N_PAD = N_P + 512
TM = 256
N_TILES = N_PAD // TM
NT_P = N_P // TM
N_SP = N_PAD - N_P

SEG_ALIGN = 16
M_T = 3072
M_CHUNK = 256
M_MAIN = 2560
BM = 512
R_MAX = N_PAD * TOP_K + N_TILES * N_EXPERTS * (SEG_ALIGN - 1) + N_EXPERTS * (BM - 1)
NB_MAX = -(-R_MAX // BM)
SEG_BITS = (16, 8, 4, 2, 1)

DN_C = 64
SEQ_S = 64
S_TOK0 = 8


def _dot(a, b):
    return jnp.dot(a, b, preferred_element_type=F32)


def _dot_nt(a, b):
    return lax.dot_general(a, b, (((1,), (1,)), ((), ())), preferred_element_type=F32)


def _split2(x):
    hi = x.astype(BF16)
    lo = (x - hi.astype(F32)).astype(BF16)
    return hi, lo


def _split3(x):
    a = x.astype(BF16)
    r = x - a.astype(F32)
    b = r.astype(BF16)
    c = (r - b.astype(F32)).astype(BF16)
    return a, b, c


def _dot3_l01(m01, x):
    a, b, c = _split3(x)
    return _dot(m01, a) + _dot(m01, b) + _dot(m01, c)


def _dot3_r01(x, m01):
    a, b, c = _split3(x)
    return _dot(a, m01) + _dot(b, m01) + _dot(c, m01)


def _sigmoid(x):
    return 1.0 / (1.0 + jnp.exp(-x))


def _layer_norm(x, g, b):
    mu = jnp.mean(x, axis=-1, keepdims=True)
    xc = x - mu
    var = jnp.mean(xc * xc, axis=-1, keepdims=True)
    return xc * lax.rsqrt(var + LN_EPS) * g + b


def _cparams(sem=None):
    return pltpu.CompilerParams(dimension_semantics=sem, vmem_limit_bytes=VMEM_LIMIT)


def _full(shape):
    return pl.BlockSpec(shape, lambda *_: (0,) * len(shape))


def _row2(w):
    return [pl.BlockSpec((TM, w), lambda i, *_: (jnp.minimum(i, NT_P - 1), 0)),
            pl.BlockSpec((TM, w), lambda i, *_: (jnp.maximum(i - NT_P, 0), 0))]


def _pick(p_ref, s_ref):
    return jnp.where(pl.program_id(0) < NT_P, p_ref[...], s_ref[...])


def _proj_kernel(xp_ref, xs_ref, wqkv_ref, wdn_ref, wz_ref, wbh_ref, wbl_ref, wah_ref, wal_ref,
                 qkv_ref, dn_ref, z_ref, b_ref, a_ref):
    x = _pick(xp_ref, xs_ref)
    xh, xl = _split2(x)
    qkv_ref[...] = _dot(xh, wqkv_ref[...])
    dn_ref[...] = _dot(xh, wdn_ref[...])
    z_ref[...] = _dot(xh, wz_ref[...])
    b_ref[...] = _dot(xh, wbh_ref[...]) + _dot(xl, wbh_ref[...]) + _dot(xh, wbl_ref[...])
    a_ref[...] = _dot(xh, wah_ref[...]) + _dot(xl, wah_ref[...]) + _dot(xh, wal_ref[...])


def _project(x_p, x_s, w_in):
    wqkv = w_in[:, :3 * ATTN_W].astype(BF16)
    wdn = w_in[:, 3 * ATTN_W:3 * ATTN_W + CONV_CH].astype(BF16)
    c0 = 3 * ATTN_W + CONV_CH
    wz = w_in[:, c0:c0 + DN_W].astype(BF16)
    wb = w_in[:, c0 + DN_W:c0 + DN_W + N_HEADS]
    wa = w_in[:, c0 + DN_W + N_HEADS:]
    wbh, wbl = _split2(wb)
    wah, wal = _split2(wa)
    row = lambda w: pl.BlockSpec((TM, w), lambda i: (i, 0))
    outs = (3 * ATTN_W, CONV_CH, DN_W, N_HEADS, N_HEADS)
    return pl.pallas_call(
        _proj_kernel,
        name="proj",
        grid=(N_TILES,),
        in_specs=_row2(D_MODEL) + [_full(wqkv.shape), _full(wdn.shape), _full(wz.shape),
                                   _full(wbh.shape), _full(wbl.shape), _full(wah.shape), _full(wal.shape)],
        out_specs=[row(w) for w in outs],
        out_shape=[jax.ShapeDtypeStruct((N_PAD, w), F32) for w in outs],
        compiler_params=_cparams(("arbitrary",)),
    )(x_p, x_s, wqkv, wdn, wz, wbh, wbl, wah, wal)


QT = 2048
ATTN_UNROLL = 4


def _attn_bias():
    qi = np.arange(N_BACK)[:, None]
    ki = np.arange(2 * N_BACK)[None, :]
    out = np.zeros((2, len(BRANCHES), N_HEADS, N_BACK, 2 * N_BACK), np.float32)
    for var, off in enumerate((N_BACK, 0)):
        dist = qi - ki + off
        valid = (dist >= 0) & (dist <= N_BACK)
        for di, (_, dil) in enumerate(BRANCHES):
            for h in range(N_HEADS):
                slope = 2.0 ** (-8.0 * (h + 1) / N_HEADS) * dil
                out[var, di, h] = np.where(valid, -slope * dist, -1e30)
    return out


def _attn_prompt_kernel(q_ref, k_ref, v_ref, bias_ref, o_ref, acc_ref, m_ref, l_ref):
    qt = pl.program_id(2)
    lane = lax.broadcasted_iota(I32, (1, LANES), 1)
    h0 = lane < HALF
    for di, (_, dil) in enumerate(BRANCHES):
        per_r = QT // (N_BACK * dil)

        def body(i, carry, di=di, dil=dil, per_r=per_r):
            def ds(start, n):
                return pl.ds(start, n) if dil == 1 else pl.ds(start, n, stride=dil)

            blocks = []
            for u in range(ATTN_UNROLL):
                blk = i * ATTN_UNROLL + u
                r = blk // per_r
                jb = blk % per_r
                qloc = r + dil * N_BACK * jb
                first = jnp.logical_and(qt == 0, jb == 0)
                kstart = jnp.where(first, r, qt * QT + qloc - N_BACK * dil)
                blocks.append((qloc, kstart, jnp.where(first, 1, 0)))
            q8 = [q_ref[ds(qloc, N_BACK), :] * (HEAD_DIM ** -0.5) for qloc, _, _ in blocks]
            kb = [k_ref[ds(kstart, 2 * N_BACK), :].astype(BF16) for _, kstart, _ in blocks]
            vv = [v_ref[ds(kstart, 2 * N_BACK), :] for _, kstart, _ in blocks]
            heads = [(u, hh) for u in range(ATTN_UNROLL) for hh in range(2)]
            keep = lambda hh: h0 if hh == 0 else jnp.logical_not(h0)
            s = [_dot_nt(jnp.where(keep(hh), q8[u], 0.0).astype(BF16), kb[u]) + bias_ref[blocks[u][2], di, hh]
                 for u, hh in heads]
            m = [jnp.max(x, axis=-1, keepdims=True) for x in s]
            p = [jnp.exp(x - mm).astype(BF16) for x, mm in zip(s, m)]
            res = [_dot(pp, jnp.where(keep(hh), vv[u], 1.0).astype(BF16)) for pp, (u, hh) in zip(p, heads)]
            for u, (qloc, _, _) in enumerate(blocks):
                rows = ds(qloc, N_BACK)
                acc_ref[di, rows, :] = jnp.where(h0, res[2 * u], res[2 * u + 1])
                l_ref[di, rows, :] = jnp.where(h0, res[2 * u + 1], res[2 * u])
                m_ref[di, rows, :] = jnp.where(h0, m[2 * u], m[2 * u + 1])
            return carry

        lax.fori_loop(0, QT // (N_BACK * ATTN_UNROLL), body, 0)

    mall = m_ref[...]
    mtot = jnp.max(mall, axis=0)
    num = jnp.zeros((QT, LANES), F32)
    den = jnp.zeros((QT, LANES), F32)
    for di in range(len(BRANCHES)):
        w = jnp.exp(mall[di] - mtot)
        num = num + w * acc_ref[di]
        den = den + w * pltpu.roll(l_ref[di], HALF, axis=1)
    o_ref[...] = num / den


def _attn_prompt(qkv):
    bias = jnp.asarray(_attn_bias())
    npair = N_HEADS // 2
    nqt = SEQ // QT
    kv_rows = SEQ
    q_spec = pl.BlockSpec((QT, LANES), lambda b, hp, t: (b * nqt + t, hp))
    k_spec = pl.BlockSpec((kv_rows, LANES), lambda b, hp, t: (b, npair + hp))
    v_spec = pl.BlockSpec((kv_rows, LANES), lambda b, hp, t: (b, 2 * npair + hp))
    bias_spec = pl.BlockSpec((2, len(BRANCHES), 2, N_BACK, 2 * N_BACK), lambda b, hp, t: (0, 0, hp, 0, 0))
    return pl.pallas_call(
        _attn_prompt_kernel,
        name="attn_prompt",
        grid=(N_PROMPT_B, npair, nqt),
        in_specs=[q_spec, k_spec, v_spec, bias_spec],
        out_specs=pl.BlockSpec((QT, LANES), lambda b, hp, t: (b * nqt + t, hp)),
        out_shape=jax.ShapeDtypeStruct((N_P, ATTN_W), F32),
        scratch_shapes=[pltpu.VMEM((len(BRANCHES), QT, LANES), F32)] * 3,
        compiler_params=_cparams(("arbitrary", "arbitrary", "arbitrary")),
    )(qkv, qkv, qkv, bias)


ROWS_S = 8


def _attn_sample_kernel(q_ref, kn_ref, vn_ref, ck_ref, cv_ref, e_ref, et_ref, slope_ref, o_ref, kall_ref, vall_ref):
    npair = N_HEADS // 2
    for hp in range(npair):
        sl = slice(hp * LANES, (hp + 1) * LANES)
        kall_ref[hp, 0:W_BUF, :] = ck_ref[0, :, sl]
        vall_ref[hp, 0:W_BUF, :] = cv_ref[0, :, sl]
        kall_ref[hp, W_BUF:W_BUF + ROWS_S, :] = kn_ref[0, :, sl]
        vall_ref[hp, W_BUF:W_BUF + ROWS_S, :] = vn_ref[0, :, sl]

    def rows(ref, start, n, dil):
        idx = pl.ds(start, n) if dil == 1 else pl.ds(start, n, stride=dil)
        return jnp.concatenate([ref[hp, idx, :] for hp in range(npair)], axis=1)

    e01 = e_ref[...]
    et01 = et_ref[...]
    slope = slope_ref[...]
    jrev = (N_BACK - lax.broadcasted_iota(I32, (N_BACK, 1), 0)).astype(F32)
    inew = lax.broadcasted_iota(I32, (ROWS_S, 1), 0)
    scale = HEAD_DIM ** -0.5
    knew = rows(kall_ref, W_BUF, ROWS_S, 1)
    vnew = rows(vall_ref, W_BUF, ROWS_S, 1).astype(BF16).astype(F32)

    def scores(kr, qt):
        ph, plo = _split2(kr * qt)
        return (_dot(ph, e01) + _dot(plo, e01)) * scale

    outs = []
    for t in range(DEC_SEQ):
        qt = q_ref[0, t:t + 1, :]
        s_self = jnp.where(inew == t, scores(knew, qt), -1e30)
        s_br = []
        v_br = []
        for _, dil in BRANCHES:
            start = W_BUF + t - N_BACK * dil
            s_br.append(scores(rows(kall_ref, start, N_BACK, dil), qt) - (slope * dil) * jrev)
            v_br.append(rows(vall_ref, start, N_BACK, dil))
        m = jnp.max(s_self, axis=0, keepdims=True)
        for s in s_br:
            m = jnp.maximum(m, jnp.max(s, axis=0, keepdims=True))
        pb = (jnp.exp(s_self - m) * float(len(BRANCHES))).astype(BF16)
        den = jnp.sum(pb.astype(F32), axis=0, keepdims=True)
        num = jnp.sum(_dot(pb, et01) * vnew, axis=0, keepdims=True)
        for s, vr in zip(s_br, v_br):
            pb = jnp.exp(s - m).astype(BF16)
            den = den + jnp.sum(pb.astype(F32), axis=0, keepdims=True)
            num = num + jnp.sum(_dot(pb, et01) * vr.astype(BF16).astype(F32), axis=0, keepdims=True)
        denx = _dot3_r01(jnp.broadcast_to(den, (ROWS_S, N_HEADS)), et01)[0:1]
        outs.append(num / denx)
    pad = jnp.zeros((ROWS_S - DEC_SEQ, ATTN_W), F32)
    o_ref[0] = jnp.concatenate(outs + [pad], axis=0)


def _head_expand():
    e = np.zeros((ATTN_W, N_HEADS), np.float32)
    for h in range(N_HEADS):
        e[h * HEAD_DIM:(h + 1) * HEAD_DIM, h] = 1.0
    return e


def _attn_sample(q_s, k_s, v_s, cache_k, cache_v):
    e = _head_expand()
    e01 = jnp.asarray(e, BF16)
    et01 = jnp.asarray(e.T, BF16)
    slopes = jnp.asarray([[2.0 ** (-8.0 * (h + 1) / N_HEADS) for h in range(N_HEADS)]], F32)
    row = pl.BlockSpec((1, ROWS_S, ATTN_W), lambda b: (b, 0, 0))
    cache = pl.BlockSpec((1, W_BUF, ATTN_W), lambda b: (b, 0, 0))
    return pl.pallas_call(
        _attn_sample_kernel,
        name="attn_sample",
        grid=(N_SAMPLE_B,),
        in_specs=[row, row, row, cache, cache, _full(e01.shape), _full(et01.shape), _full(slopes.shape)],
        out_specs=row,
        out_shape=jax.ShapeDtypeStruct((N_SAMPLE_B, ROWS_S, ATTN_W), F32),
        scratch_shapes=[pltpu.VMEM((N_HEADS // 2, W_BUF + ROWS_S, LANES), F32)] * 2,
        compiler_params=_cparams(("arbitrary",)),
    )(q_s, k_s, v_s, cache_k, cache_v, e01, et01, slopes)


HALO = 8


def _dn_pre_kernel(dn_ref, halo_ref, b_ref, a_ref, valid_ref, cw_ref, alog_ref, dtb_ref, e_ref, et_ref, ltri_ref,
                   q_ref, k_ref, v_ref, bx_ref, gx_ref, buf_ref):
    ts = dn_ref.shape[0]
    buf_ref[0:HALO, :] = halo_ref[0]
    buf_ref[HALO:HALO + ts, :] = dn_ref[...]
    y = jnp.zeros((ts, CONV_CH), F32)
    for i in range(CONV_W):
        y = y + buf_ref[pl.ds(HALO - (CONV_W - 1) + i, ts), :] * cw_ref[i:i + 1, :]
    y = y * _sigmoid(y)
    valid = valid_ref[...]
    e01 = e_ref[...]
    et01 = et_ref[...]

    def l2n(t):
        ss = _dot3_r01(t * t, e01)
        inv = lax.rsqrt(ss + NORM_EPS)
        return t * _dot3_r01(inv, et01)

    q_ref[...] = l2n(y[:, :DN_W]) * (HEAD_DIM ** -0.5)
    k_ref[...] = l2n(y[:, DN_W:2 * DN_W]) * valid
    v_ref[...] = y[:, 2 * DN_W:] * valid
    beta = _sigmoid(b_ref[...]) * valid
    sp_in = a_ref[...] + dtb_ref[...]
    softplus = jnp.maximum(sp_in, 0.0) + jnp.log1p(jnp.exp(-jnp.abs(sp_in)))
    g = -jnp.exp(alog_ref[...]) * softplus * valid
    bx_ref[...] = _dot3_r01(beta, et01)
    gx_ref[...] = _dot3_l01(ltri_ref[...], _dot3_r01(g, et01))


def _dn_pre(dn, halo, b_logit, a_logit, valid, conv_w, a_log, dt_bias):
    nt = halo.shape[0]
    rows = nt * TM
    e = _head_expand()
    e01 = jnp.asarray(e, BF16)
    et01 = jnp.asarray(e.T, BF16)
    ii = np.arange(TM)
    ltri = jnp.asarray(((ii[:, None] >= ii[None, :]) & (ii[:, None] // DN_C == ii[None, :] // DN_C)), BF16)
    row = lambda w: pl.BlockSpec((TM, w), lambda i: (i, 0))
    return pl.pallas_call(
        _dn_pre_kernel,
        name="dn_pre",
        grid=(nt,),
        in_specs=[row(CONV_CH), pl.BlockSpec((1, HALO, CONV_CH), lambda i: (i, 0, 0)), row(N_HEADS), row(N_HEADS),
                  row(1), _full(conv_w.shape), _full((1, N_HEADS)), _full((1, N_HEADS)),
                  _full(e01.shape), _full(et01.shape), _full(ltri.shape)],
        out_specs=[row(DN_W)] * 5,
        out_shape=[jax.ShapeDtypeStruct((rows, DN_W), F32)] * 5,
        scratch_shapes=[pltpu.VMEM((HALO + TM, CONV_CH), F32)],
        compiler_params=_cparams(("arbitrary",)),
    )(dn, halo, b_logit, a_logit, valid, conv_w, a_log.reshape(1, N_HEADS), dt_bias.reshape(1, N_HEADS),
      e01, et01, ltri)


def _delta_streams(q, k, v, bx, gcx, s_bd):
    c = DN_C
    n = len(q)
    each = range(n)
    lane = lax.broadcasted_iota(I32, (1, LANES), 1)
    h0 = lane < HALF
    m0 = h0.astype(F32)
    m1 = 1.0 - m0
    ii = lax.broadcasted_iota(I32, (2 * c, 2 * c), 0)
    jj = lax.broadcasted_iota(I32, (2 * c, 2 * c), 1)
    same = jnp.where(ii < c, 0, 1) == jnp.where(jj < c, 0, 1)
    order = jnp.where(same, ii - jj, -1)
    tril = order >= 0
    strict = order > 0
    eye = jnp.where(ii == jj, 1.0, 0.0)

    def stack(x):
        return jnp.concatenate([x * m0, x * m1], axis=0)

    def fold(x2):
        return x2[:c] + x2[c:]

    def decay_of(g):
        gsw = pltpu.roll(g, HALF, axis=1)
        row_b = jnp.concatenate([jnp.where(h0, g, gsw), jnp.where(h0, gsw, g)], axis=0)
        return jnp.where(tril, jnp.exp(jnp.where(tril, row_b - row_b.T, 0.0)), 0.0)

    eg = [jnp.exp(gcx[i]) for i in each]
    kb = [k[i] * bx[i] for i in each]
    glast = [gcx[i][c - 1:c, :] for i in each]
    decay = [decay_of(gcx[i]) for i in each]
    k2 = [stack(k[i]).astype(BF16) for i in each]
    lhs = [jnp.concatenate([stack(kb[i]), stack(q[i])], axis=0).astype(BF16) for i in each]
    aq = [_dot_nt(lhs[i], k2[i]) for i in each]
    a_low = [jnp.where(strict, aq[i][:2 * c] * decay[i], 0.0) for i in each]
    aqk = [jnp.where(tril, aq[i][2 * c:] * decay[i], 0.0).astype(BF16) for i in each]
    tinv = [eye - a_low[i] for i in each]
    apow = [a_low[i].astype(BF16) for i in each]
    for _ in range(5):
        apow = [_dot(apow[i], apow[i]).astype(BF16) for i in each]
        tinv = [tinv[i] + _dot(tinv[i].astype(BF16), apow[i]) for i in each]
    rhs = [jnp.concatenate([stack(v[i] * bx[i]), stack(kb[i] * eg[i])], axis=1).astype(BF16) for i in each]
    uw = [_dot(tinv[i].astype(BF16), rhs[i]) for i in each]
    u = [fold(uw[i][:, :LANES]) for i in each]
    w = [fold(uw[i][:, LANES:]) for i in each]
    ws = [_dot(jnp.concatenate([w[i], q[i] * eg[i]], axis=0).astype(BF16), s_bd[i].astype(BF16)) for i in each]
    v_new = [u[i] - ws[i][:c] for i in each]
    intra = [_dot(aqk[i], stack(v_new[i]).astype(BF16)) for i in each]
    kd = [(k[i] * jnp.exp(glast[i] - gcx[i])).T.astype(BF16) for i in each]
    upd = [_dot(kd[i], v_new[i].astype(BF16)) for i in each]
    o = [ws[i][c:] + fold(intra[i]) for i in each]
    s_new = [s_bd[i] * jnp.exp(glast[i]) + jnp.where(same, upd[i], 0.0) for i in each]
    return o, s_new


def _delta_kernel(q_ref, k_ref, v_ref, bx_ref, gx_ref, s0_ref, o_ref, sout_ref, s_ref):
    ci = pl.program_id(1)
    nb = q_ref.shape[0]
    npair = N_HEADS // 2

    @pl.when(ci == 0)
    def _():
        s_ref[...] = s0_ref[...]

    streams = [(b, hp) for b in range(nb) for hp in range(npair)]
    lanes = lambda hp: slice(hp * LANES, (hp + 1) * LANES)
    take = lambda ref: [ref[b, :, lanes(hp)] for b, hp in streams]
    o, s_new = _delta_streams(take(q_ref), take(k_ref), take(v_ref), take(bx_ref), take(gx_ref),
                              [s_ref[b, hp] for b, hp in streams])
    for i, (b, hp) in enumerate(streams):
        o_ref[b, :, lanes(hp)] = o[i]
        s_ref[b, hp] = s_new[i]

    @pl.when(ci == pl.num_programs(1) - 1)
    def _():
        sout_ref[...] = s_ref[...]


def _delta(q, k, v, bx, gx, s0_bd, bblk):
    nbatch, length, _ = q.shape
    npair = N_HEADS // 2
    seq = pl.BlockSpec((bblk, DN_C, DN_W), lambda g, c: (g, c, 0))
    st = pl.BlockSpec((bblk, npair, LANES, LANES), lambda g, c: (g, 0, 0, 0))
    return pl.pallas_call(
        _delta_kernel,
        name="delta",
        grid=(nbatch // bblk, length // DN_C),
        in_specs=[seq] * 5 + [st],
        out_specs=[seq, st],
        out_shape=[jax.ShapeDtypeStruct((nbatch, length, DN_W), F32),
                   jax.ShapeDtypeStruct((nbatch, npair, LANES, LANES), F32)],
        scratch_shapes=[pltpu.VMEM((bblk, npair, LANES, LANES), F32)],
        compiler_params=_cparams(("arbitrary", "arbitrary")),
    )(q, k, v, bx, gx, s0_bd)


def _state_to_bd(s):
    b = s.shape[0]
    s = s.reshape(b, N_HEADS // 2, 2, HALF, HALF)
    zero = jnp.zeros_like(s[:, :, 0])
    top = jnp.concatenate([s[:, :, 0], zero], axis=-1)
    bot = jnp.concatenate([zero, s[:, :, 1]], axis=-1)
    return jnp.concatenate([top, bot], axis=-2)


def _state_from_bd(s):
    b = s.shape[0]
    return jnp.stack([s[:, :, :HALF, :HALF], s[:, :, HALF:, HALF:]], axis=2).reshape(b, N_HEADS, HALF, HALF)


def _out_kernel(xp_ref, xs_ref, ap_ref, as_ref, dp_ref, ds_ref, z_ref, ng_ref, bavg_ref, wa_ref, wd_ref, g_ref, b_ref,
                o_ref):
    o = _pick(dp_ref, ds_ref)
    oh, ol = _split2(o * o)
    ms = _dot(oh, bavg_ref[...]) + _dot(ol, bavg_ref[...])
    z = z_ref[...]
    dn = o * lax.rsqrt(ms + NORM_EPS) * ng_ref[...] * (z * _sigmoid(z))
    mix = _dot(_pick(ap_ref, as_ref).astype(BF16), wa_ref[...]) + _dot(dn.astype(BF16), wd_ref[...])
    o_ref[...] = _layer_norm(ALPHA * _pick(xp_ref, xs_ref) + mix, g_ref[...], b_ref[...])


def _out_proj(x_p, x_s, att_p, att_s, dn_p, dn_s, z, norm_g, w_out, g, b):
    wa = w_out[:ATTN_W].astype(BF16)
    wd = w_out[ATTN_W:].astype(BF16)
    normg = jnp.tile(norm_g.reshape(1, HEAD_DIM), (1, N_HEADS))
    bavg = jnp.asarray(np.kron(np.eye(N_HEADS), np.ones((HALF, HALF))) / HALF, BF16)
    row = lambda w: pl.BlockSpec((TM, w), lambda i: (i, 0))
    return pl.pallas_call(
        _out_kernel,
        name="out_proj",
        grid=(N_TILES,),
        in_specs=_row2(D_MODEL) + _row2(ATTN_W) + _row2(DN_W) + [
            row(DN_W), _full(normg.shape), _full(bavg.shape),
            _full(wa.shape), _full(wd.shape), _full((1, D_MODEL)), _full((1, D_MODEL))],
        out_specs=row(D_MODEL),
        out_shape=jax.ShapeDtypeStruct((N_PAD, D_MODEL), F32),
        compiler_params=_cparams(("arbitrary",)),
    )(x_p, x_s, att_p, att_s, dn_p, dn_s, z, normg, bavg, wa, wd, g.reshape(1, -1), b.reshape(1, -1))


N_GROUPS = 8
GROUP = N_EXPERTS // N_GROUPS
SLOT_RADIX = 64
_ET_SPEC = pl.BlockSpec((1, 2 * N_EXPERTS, TM), lambda i, *_: (i, 0, 0))
_SEG_SPEC = pl.BlockSpec((1, 1, 2 * N_EXPERTS), lambda i, *_: (i, 0, 0))
TOPK_GROUPS = 4
NEG = -jnp.inf


def _route_kernel(x_ref, wh_ref, wl_ref, bias_ref, upper_ref, lows_ref, st_ref, ge_ref, cnt_ref):
    t = pl.program_id(0)
    xh, xl = _split2(x_ref[...])
    wh = wh_ref[...]
    logits = _dot_nt(wh, xh) + _dot_nt(wh, xl) + _dot_nt(wl_ref[...], xh)
    scores = _sigmoid(logits)
    choice = scores + bias_ref[...]
    i8 = lax.broadcasted_iota(I32, (GROUP, TM), 0)

    def first_max(vals):
        m = jnp.max(vals, axis=0, keepdims=True)
        idx = jnp.min(jnp.where(vals == m, i8, GROUP), axis=0, keepdims=True)
        return m, idx

    pieces = [choice[g * GROUP:(g + 1) * GROUP, :] for g in range(N_GROUPS)]
    gsc = jnp.zeros((N_GROUPS, TM), F32)
    for g in range(N_GROUPS):
        m1, idx1 = first_max(pieces[g])
        m2 = jnp.max(jnp.where(i8 == idx1, NEG, pieces[g]), axis=0, keepdims=True)
        gsc = jnp.where(i8 == g, m1 + m2, gsc)
    gsel = jnp.zeros((N_GROUPS, TM), F32)
    for _ in range(TOPK_GROUPS):
        _, idx = first_max(gsc)
        hit = i8 == idx
        gsel = jnp.where(hit, 1.0, gsel)
        gsc = jnp.where(hit, NEG, gsc)
    masked = [jnp.where(gsel[g:g + 1, :] > 0.5, pieces[g], NEG) for g in range(N_GROUPS)]
    eidx = [i8 + g * GROUP for g in range(N_GROUPS)]
    member = [jnp.zeros((GROUP, TM), F32) for _ in range(N_GROUPS)]
    for _ in range(TOP_K):
        mm = masked[0]
        for g in range(1, N_GROUPS):
            mm = jnp.maximum(mm, masked[g])
        m = jnp.max(mm, axis=0, keepdims=True)
        cand = jnp.where(masked[0] == m, eidx[0], N_EXPERTS)
        for g in range(1, N_GROUPS):
            cand = jnp.minimum(cand, jnp.where(masked[g] == m, eidx[g], N_EXPERTS))
        idx = jnp.min(cand, axis=0, keepdims=True)
        for g in range(N_GROUPS):
            hit = eidx[g] == idx
            member[g] = jnp.where(hit, 1.0, member[g])
            masked[g] = jnp.where(hit, NEG, masked[g])

    tok = t * TM + lax.broadcasted_iota(I32, (1, TM), 1)
    real = jnp.where(tok < N_REAL, 1.0, 0.0)
    mem = jnp.concatenate(member, axis=0) * real
    sel = mem * scores
    gate = (sel / jnp.sum(sel + (1.0 - real), axis=0, keepdims=True) * ROUTED_SCALE).astype(BF16)
    ge_ref[0] = jnp.concatenate([gate, jnp.zeros_like(gate)], axis=0)
    memb = mem.astype(BF16)
    rank = _dot(memb, upper_ref[...])
    cnt = jnp.sum(mem, axis=1, keepdims=True)
    cpad = jnp.floor((cnt + (SEG_ALIGN - 1)) * (1.0 / SEG_ALIGN))
    loc = _dot(lows_ref[...], jnp.broadcast_to(cpad, (N_EXPERTS, TM)).astype(BF16)) * SEG_ALIGN
    slot_e = jnp.where(mem > 0.5, loc + rank, float(M_T))
    slot_hi = jnp.floor(slot_e * (1.0 / SLOT_RADIX)) * SLOT_RADIX
    st_ref[0] = jnp.concatenate([slot_hi, slot_e - slot_hi + 1.0], axis=0).astype(BF16)
    ones = jnp.ones((GROUP, TM), BF16)
    cnt_ref[0] = _dot_nt(ones, memb).astype(I32)


def _route(x1, w_router, router_bias):
    wt = w_router.T
    wh, wl = _split2(wt)
    ii = np.arange(TM)
    upper = jnp.asarray(ii[:, None] < ii[None, :], BF16)
    ee = np.arange(N_EXPERTS)
    lows = jnp.asarray(ee[:, None] > ee[None, :], BF16)
    return pl.pallas_call(
        _route_kernel,
        name="route",
        grid=(N_TILES,),
        in_specs=[pl.BlockSpec((TM, D_MODEL), lambda i: (i, 0)), _full(wh.shape), _full(wl.shape),
                  _full((N_EXPERTS, 1)), _full(upper.shape), _full(lows.shape)],
        out_specs=[_ET_SPEC, _ET_SPEC, pl.BlockSpec((1, GROUP, N_EXPERTS), lambda i: (i, 0, 0))],
        out_shape=[jax.ShapeDtypeStruct((N_TILES, 2 * N_EXPERTS, TM), BF16)] * 2 + [
            jax.ShapeDtypeStruct((N_TILES, GROUP, N_EXPERTS), I32)],
        compiler_params=_cparams(("arbitrary",)),
    )(x1, wh, wl, router_bias.reshape(N_EXPERTS, 1), upper, lows)


def _segment_tables(cnt):
    seg = (cnt + SEG_ALIGN - 1) // SEG_ALIGN
    cpad = seg * SEG_ALIGN
    loc_off = jnp.cumsum(cpad, axis=1) - cpad
    rows_e = jnp.sum(cpad, axis=0)
    blocks_e = (rows_e + BM - 1) // BM
    bend = jnp.cumsum(blocks_e)
    bstart = bend - blocks_e
    glob_off = (bstart * BM)[None, :] + jnp.cumsum(cpad, axis=0) - cpad
    nb = bend[-1]
    bidx = jnp.arange(NB_MAX, dtype=I32)
    bexp = jnp.minimum(jnp.sum(bend[None, :] <= jnp.minimum(bidx, nb - 1)[:, None], axis=1), N_EXPERTS - 1).astype(I32)
    pad_off = bstart * BM + rows_e
    pad_seg = (blocks_e * BM - rows_e) // SEG_ALIGN
    i32 = lambda a: a.reshape(-1).astype(I32)
    bounds = lambda a: jnp.tile(a.astype(F32), (1, 2)).reshape(N_TILES, 1, 2 * N_EXPERTS)
    return (i32(seg), i32(loc_off), i32(glob_off), i32(pad_seg), i32(pad_off), bexp, i32(nb),
            bounds(loc_off), bounds(loc_off + cpad))


def _copy_by_bits(k, bits, src_at, dst_at, sem, wait):
    off = jnp.int32(0)
    for bit in bits:
        rows = bit * SEG_ALIGN
        take = (k & bit) != 0

        @pl.when(take)
        def _(off=off, rows=rows):
            cp = pltpu.make_async_copy(src_at(off, rows), dst_at(off, rows), sem)
            if wait:
                cp.wait()
            else:
                cp.start()

        off = off + jnp.where(take, rows, 0)


def _segment_copies(seg_ref, loc_ref, glob_ref, t, src_of, dst_of, sem, wait):
    def body(e, carry):
        k = seg_ref[t * N_EXPERTS + e]
        lo = loc_ref[t * N_EXPERTS + e]
        go = glob_ref[t * N_EXPERTS + e]
        at = lambda fn: (lambda off, rows: fn(pl.multiple_of(lo + off, SEG_ALIGN), pl.multiple_of(go + off, SEG_ALIGN), rows))
        _copy_by_bits(k, SEG_BITS, at(src_of), at(dst_of), sem, wait)
        return carry

    lax.fori_loop(0, N_EXPERTS, body, 0)


def _local_rows(m0, rows, lo, hi, slot_tab):
    mcol = (lax.broadcasted_iota(I32, (rows, 2 * N_EXPERTS), 0) + m0).astype(F32)
    e01 = jnp.where(mcol >= lo, jnp.where(mcol < hi, 1.0, 0.0), 0.0).astype(BF16)
    mrow = (lax.broadcasted_iota(I32, (rows, TM), 0) + (m0 + 1)).astype(F32)
    return _dot(e01, slot_tab) == mrow, e01


def _row_spans(used):
    yield None, 0, M_MAIN
    for m0 in range(M_MAIN, M_T, M_CHUNK):
        yield m0 < used, m0, M_CHUNK


def _dispatch_kernel(seg_ref, loc_ref, glob_ref, pseg_ref, poff_ref, nb_ref, x_ref, st_ref, lo_ref, hi_ref,
                     xs_ref, buf_ref, sem):
    t = pl.program_id(0)
    last = t * N_EXPERTS + N_EXPERTS - 1
    used = loc_ref[last] + seg_ref[last] * SEG_ALIGN
    for live, m0, rows in _row_spans(used):
        def sort_rows(m0=m0, rows=rows):
            hit, _ = _local_rows(m0, rows, lo_ref[0], hi_ref[0], st_ref[0])
            onehot = jnp.where(hit, 1.0, 0.0).astype(BF16)
            buf_ref[m0:m0 + rows, :] = _dot(onehot, x_ref[...].astype(BF16)).astype(BF16)

        if live is None:
            sort_rows()
        else:
            pl.when(live)(sort_rows)

    src = lambda lo, go, rows: buf_ref.at[pl.ds(lo, rows), :]
    dst = lambda lo, go, rows: xs_ref.at[pl.ds(go, rows), :]
    _segment_copies(seg_ref, loc_ref, glob_ref, t, src, dst, sem, wait=False)
    _segment_copies(seg_ref, loc_ref, glob_ref, t, src, dst, sem, wait=True)

    @pl.when(t == N_TILES - 1)
    def _():
        buf_ref[0:BM, :] = jnp.zeros((BM, D_MODEL), BF16)

        def fill(wait):
            def pad(e, carry):
                go = poff_ref[e]
                _copy_by_bits(pseg_ref[e], SEG_BITS, lambda off, rows: buf_ref.at[pl.ds(0, rows), :],
                              lambda off, rows: xs_ref.at[pl.ds(pl.multiple_of(go + off, SEG_ALIGN), rows), :], sem, wait)
                return carry

            lax.fori_loop(0, N_EXPERTS, pad, 0)

            def tail(b, carry):
                cp = pltpu.make_async_copy(buf_ref.at[pl.ds(0, BM), :], xs_ref.at[pl.ds(pl.multiple_of(b * BM, BM), BM), :], sem)
                if wait:
                    cp.wait()
                else:
                    cp.start()
                return carry

            lax.fori_loop(nb_ref[0], NB_MAX, tail, 0)

        fill(False)
        fill(True)


def _dispatch(x1, slot_tab, seg_lo, seg_hi, tables):
    grid_spec = pltpu.PrefetchScalarGridSpec(
        num_scalar_prefetch=len(tables),
        grid=(N_TILES,),
        in_specs=[pl.BlockSpec((TM, D_MODEL), lambda i, *_: (i, 0)), _ET_SPEC, _SEG_SPEC, _SEG_SPEC],
        out_specs=pl.BlockSpec(memory_space=pl.ANY),
        scratch_shapes=[pltpu.VMEM((M_T, D_MODEL), BF16), pltpu.SemaphoreType.DMA],
    )
    return pl.pallas_call(
        _dispatch_kernel,
        name="dispatch",
        grid_spec=grid_spec,
        out_shape=jax.ShapeDtypeStruct((NB_MAX * BM, D_MODEL), BF16),
        compiler_params=_cparams(("arbitrary",)),
    )(*tables, x1, slot_tab, seg_lo, seg_hi)


def _expert_kernel(bexp_ref, nb_ref, xs_ref, wg_ref, wu_ref, wd_ref, ys_ref, wgu_s, wd_s):
    b = pl.program_id(0)
    nb = nb_ref[0]
    prev = bexp_ref[jnp.maximum(b - 1, 0)]
    fresh = jnp.logical_or(b == 0, bexp_ref[b] != prev)

    @pl.when(fresh)
    def _():
        wgu_s[:, :EXPERT_DIM] = wg_ref[0].astype(BF16)
        wgu_s[:, EXPERT_DIM:] = wu_ref[0].astype(BF16)
        wd_s[...] = wd_ref[0].astype(BF16)

    @pl.when(b < nb)
    def _():
        h = _dot(xs_ref[...], wgu_s[...])
        g = h[:, :EXPERT_DIM]
        a = (g * _sigmoid(g) * h[:, EXPERT_DIM:]).astype(BF16)
        ys_ref[...] = _dot(a, wd_s[...]).astype(BF16)

    @pl.when(b >= nb)
    def _():
        ys_ref[...] = jnp.zeros((BM, D_MODEL), BF16)


def _experts(xs, we_gate, we_up, we_down, bexp, nb):
    last = lambda b, nb_ref: jnp.minimum(b, nb_ref[0] - 1)
    grid_spec = pltpu.PrefetchScalarGridSpec(
        num_scalar_prefetch=2,
        grid=(NB_MAX,),
        in_specs=[pl.BlockSpec((BM, D_MODEL), lambda b, be, nb_ref: (last(b, nb_ref), 0)),
                  pl.BlockSpec((1, D_MODEL, EXPERT_DIM), lambda b, be, nb_ref: (be[b], 0, 0)),
                  pl.BlockSpec((1, D_MODEL, EXPERT_DIM), lambda b, be, nb_ref: (be[b], 0, 0)),
                  pl.BlockSpec((1, EXPERT_DIM, D_MODEL), lambda b, be, nb_ref: (be[b], 0, 0))],
        out_specs=pl.BlockSpec((BM, D_MODEL), lambda b, be, nb_ref: (b, 0)),
        scratch_shapes=[pltpu.VMEM((D_MODEL, 2 * EXPERT_DIM), BF16), pltpu.VMEM((EXPERT_DIM, D_MODEL), BF16)],
    )
    return pl.pallas_call(
        _expert_kernel,
        name="expert",
        grid_spec=grid_spec,
        out_shape=jax.ShapeDtypeStruct((NB_MAX * BM, D_MODEL), BF16),
        compiler_params=_cparams(("arbitrary",)),
    )(bexp, nb, xs, we_gate, we_up, we_down)


def _combine_kernel(seg_ref, loc_ref, glob_ref, x_ref, st_ref, ge_ref, lo_ref, hi_ref, pp_ref, ps_ref, ys_ref,
                    wsgu_ref, wsd_ref, g2_ref, b2_ref, wpg_ref, wpp_ref, g3_ref, b3_ref, op_ref, os_ref,
                    buf_ref, acc_ref, sem):
    t = pl.program_id(0)
    last = t * N_EXPERTS + N_EXPERTS - 1
    used = loc_ref[last] + seg_ref[last] * SEG_ALIGN

    @pl.when(t == 0)
    def _():
        buf_ref[...] = jnp.zeros((M_T, D_MODEL), BF16)

    src = lambda lo, go, rows: ys_ref.at[pl.ds(go, rows), :]
    dst = lambda lo, go, rows: buf_ref.at[pl.ds(lo, rows), :]
    _segment_copies(seg_ref, loc_ref, glob_ref, t, src, dst, sem, wait=False)
    _segment_copies(seg_ref, loc_ref, glob_ref, t, src, dst, sem, wait=True)

    for live, m0, rows in _row_spans(used):
        def gather_rows(m0=m0, rows=rows):
            hit, e01 = _local_rows(m0, rows, lo_ref[0], hi_ref[0], st_ref[0])
            pg = jnp.where(hit, _dot(e01, ge_ref[0]), 0.0).astype(BF16)
            return lax.dot_general(pg, buf_ref[m0:m0 + rows, :], (((0,), (0,)), ((), ())), preferred_element_type=F32)

        if live is None:
            acc_ref[...] = gather_rows()
        else:
            @pl.when(live)
            def _(gather_rows=gather_rows):
                acc_ref[...] += gather_rows()

    x = x_ref[...]
    routed = acc_ref[...]
    xb = x.astype(BF16)
    h = _dot(xb, wsgu_ref[...])
    g = h[:, :EXPERT_DIM]
    shared = _dot((g * _sigmoid(g) * h[:, EXPERT_DIM:]).astype(BF16), wsd_ref[...])
    x2 = _layer_norm(ALPHA * x + (routed + shared), g2_ref[...], b2_ref[...])
    ple = _sigmoid(_dot(x2.astype(BF16), wpg_ref[...])) * _dot(_pick(pp_ref, ps_ref).astype(BF16), wpp_ref[...])
    y = _layer_norm(ALPHA * x2 + ple, g3_ref[...], b3_ref[...])

    @pl.when(t < NT_P)
    def _():
        op_ref[...] = y

    @pl.when(t >= NT_P)
    def _():
        os_ref[...] = y


def _combine(x1, slot_tab, gate_e, seg_lo, seg_hi, p_p, p_s, ys, seg, loc_off, glob_off, ws_gate, ws_up, ws_down,
             ln2_g, ln2_b, w_ple_gate, w_ple_proj, ln3_g, ln3_b):
    wsgu = jnp.concatenate([ws_gate, ws_up], axis=1).astype(BF16)
    wsd = ws_down.astype(BF16)
    wpg = w_ple_gate.astype(BF16)
    wpp = w_ple_proj.astype(BF16)
    row = lambda w: pl.BlockSpec((TM, w), lambda i, *_: (i, 0))
    full = lambda shape: pl.BlockSpec(shape, lambda i, *_: (0,) * len(shape))
    vec = full((1, D_MODEL))
    grid_spec = pltpu.PrefetchScalarGridSpec(
        num_scalar_prefetch=3,
        grid=(N_TILES,),
        in_specs=[row(D_MODEL), _ET_SPEC, _ET_SPEC, _SEG_SPEC, _SEG_SPEC] + _row2(PLE_DIM) + [
            pl.BlockSpec(memory_space=pl.ANY),
            full(wsgu.shape), full(wsd.shape), vec, vec, full(wpg.shape), full(wpp.shape), vec, vec],
        out_specs=_row2(D_MODEL),
        scratch_shapes=[pltpu.VMEM((M_T, D_MODEL), BF16), pltpu.VMEM((TM, D_MODEL), F32), pltpu.SemaphoreType.DMA],
    )
    r = lambda a: a.reshape(1, -1)
    return pl.pallas_call(
        _combine_kernel,
        name="combine",
        grid_spec=grid_spec,
        out_shape=[jax.ShapeDtypeStruct((N_P, D_MODEL), F32), jax.ShapeDtypeStruct((N_SP, D_MODEL), F32)],
        compiler_params=_cparams(("arbitrary",)),
    )(seg, loc_off, glob_off, x1, slot_tab, gate_e, seg_lo, seg_hi, p_p, p_s, ys, wsgu, wsd, r(ln2_g), r(ln2_b),
      wpg, wpp, r(ln3_g), r(ln3_b))


def _pad_rows(a, rows):
    return jnp.concatenate([a, jnp.zeros((rows - a.shape[0],) + a.shape[1:], a.dtype)], axis=0)


def kernel(x_prompt, x_sample, cache_k_win, cache_v_win, state_conv, state_delta, p_prompt, p_sample, w_in, conv_w, a_log, dt_bias, dn_norm_g, w_out, ln1_g, ln1_b, w_router, router_bias, we_gate, we_up, we_down, ws_gate, ws_up, ws_down, ln2_g, ln2_b, w_ple_gate, w_ple_proj, ln3_g, ln3_b):
    x_p = x_prompt.reshape(N_P, D_MODEL)
    x_s = _pad_rows(x_sample.reshape(N_S, D_MODEL), N_SP)
    p_p = p_prompt[0].reshape(N_P, PLE_DIM)
    p_s = _pad_rows(p_sample[0].reshape(N_S, PLE_DIM), N_SP)

    qkv, dn_in, z, b_logit, a_logit = _project(x_p, x_s, w_in[0])

    att_p = _attn_prompt(qkv)
    qkv_s = qkv[N_P:N_REAL].reshape(N_SAMPLE_B, DEC_SEQ, 3 * ATTN_W)
    qkv_s8 = jnp.pad(qkv_s, ((0, 0), (0, ROWS_S - DEC_SEQ), (0, 0)))
    ck = cache_k_win[0].reshape(N_SAMPLE_B, W_BUF, ATTN_W)
    cv = cache_v_win[0].reshape(N_SAMPLE_B, W_BUF, ATTN_W)
    att_s = _attn_sample(qkv_s8[:, :, :ATTN_W], qkv_s8[:, :, ATTN_W:2 * ATTN_W], qkv_s8[:, :, 2 * ATTN_W:], ck, cv)
    att_s = _pad_rows(att_s[:, :DEC_SEQ].reshape(N_S, ATTN_W), N_SP)

    tails = dn_in.reshape(N_TILES, TM, CONV_CH)[:NT_P - 1, TM - HALO:]
    halo_p = jnp.concatenate([jnp.zeros((1, HALO, CONV_CH), F32), tails], axis=0)
    seq_start = (jnp.arange(NT_P) % (SEQ // TM) == 0)[:, None, None]
    halo_p = jnp.where(seq_start, 0.0, halo_p)
    q_p, k_p, v_p, bx_p, gx_p = _dn_pre(dn_in, halo_p, b_logit, a_logit, jnp.ones((N_P, 1), F32),
                                        conv_w[0], a_log[0], dt_bias[0])
    shp = (N_PROMPT_B, SEQ, DN_W)
    s0_p = jnp.zeros((N_PROMPT_B, N_HEADS // 2, LANES, LANES), F32)
    o_p, s_p = _delta(q_p.reshape(shp), k_p.reshape(shp), v_p.reshape(shp), bx_p.reshape(shp), gx_p.reshape(shp),
                      s0_p, N_PROMPT_B)

    def seq_pad(tok, state=None):
        w = tok.shape[-1]
        tok = tok.reshape(N_SAMPLE_B, DEC_SEQ, w)
        head = jnp.zeros((N_SAMPLE_B, S_TOK0, w), F32)
        if state is not None:
            head = head.at[:, S_TOK0 - (CONV_W - 1):].set(state)
        tail = jnp.zeros((N_SAMPLE_B, SEQ_S - S_TOK0 - DEC_SEQ, w), F32)
        return jnp.concatenate([head, tok, tail], axis=1).reshape(N_SAMPLE_B * SEQ_S, w)

    dn_tok_s = dn_in[N_P:N_REAL]
    dn_s = seq_pad(dn_tok_s, state_conv[0])
    rows_s = N_SAMPLE_B * SEQ_S
    valid_s = seq_pad(jnp.ones((N_S, 1), F32))
    halo_s = jnp.zeros((rows_s // TM, HALO, CONV_CH), F32)
    q_s, k_s, v_s, bx_s, gx_s = _dn_pre(dn_s, halo_s, seq_pad(b_logit[N_P:N_REAL]), seq_pad(a_logit[N_P:N_REAL]),
                                        valid_s, conv_w[0], a_log[0], dt_bias[0])
    shs = (N_SAMPLE_B, SEQ_S, DN_W)
    o_s, s_s = _delta(q_s.reshape(shs), k_s.reshape(shs), v_s.reshape(shs), bx_s.reshape(shs), gx_s.reshape(shs),
                      _state_to_bd(state_delta[0]), 4)
    dn_o_s = _pad_rows(o_s[:, S_TOK0:S_TOK0 + DEC_SEQ].reshape(N_S, DN_W), N_SP)

    x1 = _out_proj(x_p, x_s, att_p, att_s, o_p.reshape(N_P, DN_W), dn_o_s, z, dn_norm_g[0], w_out[0], ln1_g[0], ln1_b[0])

    slot_tab, gate_e, cnt = _route(x1, w_router[0], router_bias[0])
    seg, loc_off, glob_off, pad_seg, pad_off, bexp, nb, seg_lo, seg_hi = _segment_tables(cnt[:, 0, :])
    xs = _dispatch(x1, slot_tab, seg_lo, seg_hi, (seg, loc_off, glob_off, pad_seg, pad_off, nb))
    ys = _experts(xs, we_gate[0], we_up[0], we_down[0], bexp, nb)
    y_p, y_s = _combine(x1, slot_tab, gate_e, seg_lo, seg_hi, p_p, p_s, ys, seg, loc_off, glob_off,
                        ws_gate[0], ws_up[0], ws_down[0], ln2_g[0], ln2_b[0], w_ple_gate[0], w_ple_proj[0],
                        ln3_g[0], ln3_b[0])

    y_prompt = y_p.reshape(N_PROMPT_B, SEQ, D_MODEL)
    y_sample = y_s[:N_S].reshape(N_SAMPLE_B, DEC_SEQ, D_MODEL)
    heads = (N_HEADS, HEAD_DIM)
    win = lambda c0: jnp.stack([qkv[(b + 1) * SEQ - W_BUF:(b + 1) * SEQ, c0:c0 + ATTN_W] for b in range(N_PROMPT_B)])
    k_pr = win(ATTN_W).reshape(N_PROMPT_B, W_BUF, *heads)
    v_pr = win(2 * ATTN_W).reshape(N_PROMPT_B, W_BUF, *heads)
    conv_p = jnp.stack([dn_in[(b + 1) * SEQ - (CONV_W - 1):(b + 1) * SEQ] for b in range(N_PROMPT_B)])
    k_new = qkv_s[:, :, ATTN_W:2 * ATTN_W].reshape(N_SAMPLE_B, DEC_SEQ, *heads)
    v_new = qkv_s[:, :, 2 * ATTN_W:].reshape(N_SAMPLE_B, DEC_SEQ, *heads)
    k_sm = jnp.concatenate([cache_k_win[0][:, DEC_SEQ:], k_new], axis=1)
    v_sm = jnp.concatenate([cache_v_win[0][:, DEC_SEQ:], v_new], axis=1)
    conv_s = dn_tok_s.reshape(N_SAMPLE_B, DEC_SEQ, CONV_CH)[:, DEC_SEQ - (CONV_W - 1):]
    return (y_prompt, y_sample, k_pr[None], v_pr[None], conv_p[None], _state_from_bd(s_p)[None],
            k_sm[None], v_sm[None], conv_s[None], _state_from_bd(s_s)[None])
```

```python
import functools

import jax
import jax.numpy as jnp
import numpy as np
from jax import lax
from jax.experimental import pallas as pl
from jax.experimental.pallas import tpu as pltpu

F32 = jnp.float32
BF16 = jnp.bfloat16
I32 = jnp.int32

D_MODEL = 1024
N_PROMPT_B, SEQ = 2, 8192
N_SAMPLE_B, DEC_SEQ = 32, 4
W_BUF = 2048
N_HEADS = 8
HEAD_DIM = 64
ATTN_W = 512
CONV_CH = 1536
CONV_W = 4
DN_W = 512
N_EXPERTS = 64
TOP_K = 8
EXPERT_DIM = 256
PLE_DIM = 256
BRANCHES = ((128, 1), (512, 4), (2048, 16))
N_BACK = 128
ROUTED_SCALE = 2.5
LN_EPS = 1e-5
NORM_EPS = 1e-6
ALPHA = 2.0 ** 0.25

LANES = 128
HALF = 64
VMEM_LIMIT = 56 * 1024 * 1024

N_P = N_PROMPT_B * SEQ
N_S = N_SAMPLE_B * DEC_SEQ
N_REAL = N_P + N_S
N_PAD = N_P + 512
TM = 256
N_TILES = N_PAD // TM
NT_P = N_P // TM
N_SP = N_PAD - N_P

SEG_ALIGN = 16
M_T = 3072
M_CHUNK = 256
M_MAIN = 2560
BM = 512
R_MAX = N_PAD * TOP_K + N_TILES * N_EXPERTS * (SEG_ALIGN - 1) + N_EXPERTS * (BM - 1)
NB_MAX = -(-R_MAX // BM)

DN_C = 64
SEQ_S = 64
S_TOK0 = 8


def _dot(a, b):
    return jnp.dot(a, b, preferred_element_type=F32)


def _dot_nt(a, b):
    return lax.dot_general(a, b, (((1,), (1,)), ((), ())), preferred_element_type=F32)


def _split2(x):
    hi = x.astype(BF16)
    lo = (x - hi.astype(F32)).astype(BF16)
    return hi, lo


def _split3(x):
    a = x.astype(BF16)
    r = x - a.astype(F32)
    b = r.astype(BF16)
    c = (r - b.astype(F32)).astype(BF16)
    return a, b, c


def _dot3_l01(m01, x):
    a, b, c = _split3(x)
    return _dot(m01, a) + _dot(m01, b) + _dot(m01, c)


def _dot3_r01(x, m01):
    a, b, c = _split3(x)
    return _dot(a, m01) + _dot(b, m01) + _dot(c, m01)


def _sigmoid(x):
    return 1.0 / (1.0 + jnp.exp(-x))


def _layer_norm(x, g, b):
    mu = jnp.mean(x, axis=-1, keepdims=True)
    xc = x - mu
    var = jnp.mean(xc * xc, axis=-1, keepdims=True)
    return xc * lax.rsqrt(var + LN_EPS) * g + b


def _cparams(sem=None):
    return pltpu.CompilerParams(dimension_semantics=sem, vmem_limit_bytes=VMEM_LIMIT)


def _full(shape):
    return pl.BlockSpec(shape, lambda *_: (0,) * len(shape))


def _row2(w):
    return [pl.BlockSpec((TM, w), lambda i, *_: (jnp.minimum(i, NT_P - 1), 0)),
            pl.BlockSpec((TM, w), lambda i, *_: (jnp.maximum(i - NT_P, 0), 0))]


def _pick(p_ref, s_ref):
    return jnp.where(pl.program_id(0) < NT_P, p_ref[...], s_ref[...])


def _proj_kernel(xp_ref, xs_ref, wqkv_ref, wdn_ref, wz_ref, wbh_ref, wbl_ref, wah_ref, wal_ref,
                 qkv_ref, dn_ref, z_ref, b_ref, a_ref):
    x = _pick(xp_ref, xs_ref)
    xh, xl = _split2(x)
    qkv_ref[...] = _dot(xh, wqkv_ref[...])
    dn_ref[...] = _dot(xh, wdn_ref[...])
    z_ref[...] = _dot(xh, wz_ref[...])
    b_ref[...] = _dot(xh, wbh_ref[...]) + _dot(xl, wbh_ref[...]) + _dot(xh, wbl_ref[...])
    a_ref[...] = _dot(xh, wah_ref[...]) + _dot(xl, wah_ref[...]) + _dot(xh, wal_ref[...])


def _project(x_p, x_s, w_in):
    wqkv = w_in[:, :3 * ATTN_W].astype(BF16)
    wdn = w_in[:, 3 * ATTN_W:3 * ATTN_W + CONV_CH].astype(BF16)
    c0 = 3 * ATTN_W + CONV_CH
    wz = w_in[:, c0:c0 + DN_W].astype(BF16)
    wb = w_in[:, c0 + DN_W:c0 + DN_W + N_HEADS]
    wa = w_in[:, c0 + DN_W + N_HEADS:]
    wbh, wbl = _split2(wb)
    wah, wal = _split2(wa)
    row = lambda w: pl.BlockSpec((TM, w), lambda i: (i, 0))
    outs = (3 * ATTN_W, CONV_CH, DN_W, N_HEADS, N_HEADS)
    return pl.pallas_call(
        _proj_kernel,
        name="proj",
        grid=(N_TILES,),
        in_specs=_row2(D_MODEL) + [_full(wqkv.shape), _full(wdn.shape), _full(wz.shape),
                                   _full(wbh.shape), _full(wbl.shape), _full(wah.shape), _full(wal.shape)],
        out_specs=[row(w) for w in outs],
        out_shape=[jax.ShapeDtypeStruct((N_PAD, w), F32) for w in outs],
        compiler_params=_cparams(("arbitrary",)),
    )(x_p, x_s, wqkv, wdn, wz, wbh, wbl, wah, wal)


QT = 2048
ATTN_UNROLL = 4


def _attn_bias():
    qi = np.arange(N_BACK)[:, None]
    ki = np.arange(2 * N_BACK)[None, :]
    out = np.zeros((2, len(BRANCHES), N_HEADS, N_BACK, 2 * N_BACK), np.float32)
    for var, off in enumerate((N_BACK, 0)):
        dist = qi - ki + off
        valid = (dist >= 0) & (dist <= N_BACK)
        for di, (_, dil) in enumerate(BRANCHES):
            for h in range(N_HEADS):
                slope = 2.0 ** (-8.0 * (h + 1) / N_HEADS) * dil
                out[var, di, h] = np.where(valid, -slope * dist, -1e30)
    return out


def _attn_prompt_kernel(q_ref, k_ref, v_ref, bias_ref, o_ref, acc_ref, m_ref, l_ref):
    qt = pl.program_id(2)
    lane = lax.broadcasted_iota(I32, (1, LANES), 1)
    h0 = lane < HALF
    for di, (_, dil) in enumerate(BRANCHES):
        per_r = QT // (N_BACK * dil)

        def body(i, carry, di=di, dil=dil, per_r=per_r):
            def ds(start, n):
                return pl.ds(start, n) if dil == 1 else pl.ds(start, n, stride=dil)

            blocks = []
            for u in range(ATTN_UNROLL):
                blk = i * ATTN_UNROLL + u
                r = blk // per_r
                jb = blk % per_r
                qloc = r + dil * N_BACK * jb
                first = jnp.logical_and(qt == 0, jb == 0)
                kstart = jnp.where(first, r, qt * QT + qloc - N_BACK * dil)
                blocks.append((qloc, kstart, jnp.where(first, 1, 0)))
            q8 = [q_ref[ds(qloc, N_BACK), :] * (HEAD_DIM ** -0.5) for qloc, _, _ in blocks]
            kb = [k_ref[ds(kstart, 2 * N_BACK), :].astype(BF16) for _, kstart, _ in blocks]
            vv = [v_ref[ds(kstart, 2 * N_BACK), :] for _, kstart, _ in blocks]
            heads = [(u, hh) for u in range(ATTN_UNROLL) for hh in range(2)]
            keep = lambda hh: h0 if hh == 0 else jnp.logical_not(h0)
            s = [_dot_nt(jnp.where(keep(hh), q8[u], 0.0).astype(BF16), kb[u]) + bias_ref[blocks[u][2], di, hh]
                 for u, hh in heads]
            m = [jnp.max(x, axis=-1, keepdims=True) for x in s]
            p = [jnp.exp(x - mm).astype(BF16) for x, mm in zip(s, m)]
            res = [_dot(pp, jnp.where(keep(hh), vv[u], 1.0).astype(BF16)) for pp, (u, hh) in zip(p, heads)]
            for u, (qloc, _, _) in enumerate(blocks):
                rows = ds(qloc, N_BACK)
                acc_ref[di, rows, :] = jnp.where(h0, res[2 * u], res[2 * u + 1])
                l_ref[di, rows, :] = jnp.where(h0, res[2 * u + 1], res[2 * u])
                m_ref[di, rows, :] = jnp.where(h0, m[2 * u], m[2 * u + 1])
            return carry

        lax.fori_loop(0, QT // (N_BACK * ATTN_UNROLL), body, 0)

    mall = m_ref[...]
    mtot = jnp.max(mall, axis=0)
    num = jnp.zeros((QT, LANES), F32)
    den = jnp.zeros((QT, LANES), F32)
    for di in range(len(BRANCHES)):
        w = jnp.exp(mall[di] - mtot)
        num = num + w * acc_ref[di]
        den = den + w * pltpu.roll(l_ref[di], HALF, axis=1)
    o_ref[...] = num / den


def _attn_prompt(qkv):
    bias = jnp.asarray(_attn_bias())
    npair = N_HEADS // 2
    nqt = SEQ // QT
    kv_rows = SEQ
    q_spec = pl.BlockSpec((QT, LANES), lambda b, hp, t: (b * nqt + t, hp))
    k_spec = pl.BlockSpec((kv_rows, LANES), lambda b, hp, t: (b, npair + hp))
    v_spec = pl.BlockSpec((kv_rows, LANES), lambda b, hp, t: (b, 2 * npair + hp))
    bias_spec = pl.BlockSpec((2, len(BRANCHES), 2, N_BACK, 2 * N_BACK), lambda b, hp, t: (0, 0, hp, 0, 0))
    return pl.pallas_call(
        _attn_prompt_kernel,
        name="attn_prompt",
        grid=(N_PROMPT_B, npair, nqt),
        in_specs=[q_spec, k_spec, v_spec, bias_spec],
        out_specs=pl.BlockSpec((QT, LANES), lambda b, hp, t: (b * nqt + t, hp)),
        out_shape=jax.ShapeDtypeStruct((N_P, ATTN_W), F32),
        scratch_shapes=[pltpu.VMEM((len(BRANCHES), QT, LANES), F32)] * 3,
        compiler_params=_cparams(("arbitrary", "arbitrary", "arbitrary")),
    )(qkv, qkv, qkv, bias)


ROWS_S = 8


def _attn_sample_kernel(q_ref, kn_ref, vn_ref, ck_ref, cv_ref, e_ref, et_ref, slope_ref, o_ref, kall_ref, vall_ref):
    npair = N_HEADS // 2
    for hp in range(npair):
        sl = slice(hp * LANES, (hp + 1) * LANES)
        kall_ref[hp, 0:W_BUF, :] = ck_ref[0, :, sl]
        vall_ref[hp, 0:W_BUF, :] = cv_ref[0, :, sl]
        kall_ref[hp, W_BUF:W_BUF + ROWS_S, :] = kn_ref[0, :, sl]
        vall_ref[hp, W_BUF:W_BUF + ROWS_S, :] = vn_ref[0, :, sl]

    def rows(ref, start, n, dil):
        idx = pl.ds(start, n) if dil == 1 else pl.ds(start, n, stride=dil)
        return jnp.concatenate([ref[hp, idx, :] for hp in range(npair)], axis=1)

    e01 = e_ref[...]
    et01 = et_ref[...]
    slope = slope_ref[...]
    jrev = (N_BACK - lax.broadcasted_iota(I32, (N_BACK, 1), 0)).astype(F32)
    inew = lax.broadcasted_iota(I32, (ROWS_S, 1), 0)
    scale = HEAD_DIM ** -0.5
    knew = rows(kall_ref, W_BUF, ROWS_S, 1)
    vnew = rows(vall_ref, W_BUF, ROWS_S, 1).astype(BF16).astype(F32)

    def scores(kr, qt):
        ph, plo = _split2(kr * qt)
        return (_dot(ph, e01) + _dot(plo, e01)) * scale

    outs = []
    for t in range(DEC_SEQ):
        qt = q_ref[0, t:t + 1, :]
        s_self = jnp.where(inew == t, scores(knew, qt), -1e30)
        s_br = []
        v_br = []
        for _, dil in BRANCHES:
            start = W_BUF + t - N_BACK * dil
            s_br.append(scores(rows(kall_ref, start, N_BACK, dil), qt) - (slope * dil) * jrev)
            v_br.append(rows(vall_ref, start, N_BACK, dil))
        m = jnp.max(s_self, axis=0, keepdims=True)
        for s in s_br:
            m = jnp.maximum(m, jnp.max(s, axis=0, keepdims=True))
        pb = (jnp.exp(s_self - m) * float(len(BRANCHES))).astype(BF16)
        den = jnp.sum(pb.astype(F32), axis=0, keepdims=True)
        num = jnp.sum(_dot(pb, et01) * vnew, axis=0, keepdims=True)
        for s, vr in zip(s_br, v_br):
            pb = jnp.exp(s - m).astype(BF16)
            den = den + jnp.sum(pb.astype(F32), axis=0, keepdims=True)
            num = num + jnp.sum(_dot(pb, et01) * vr.astype(BF16).astype(F32), axis=0, keepdims=True)
        denx = _dot3_r01(jnp.broadcast_to(den, (ROWS_S, N_HEADS)), et01)[0:1]
        outs.append(num / denx)
    pad = jnp.zeros((ROWS_S - DEC_SEQ, ATTN_W), F32)
    o_ref[0] = jnp.concatenate(outs + [pad], axis=0)


def _head_expand():
    e = np.zeros((ATTN_W, N_HEADS), np.float32)
    for h in range(N_HEADS):
        e[h * HEAD_DIM:(h + 1) * HEAD_DIM, h] = 1.0
    return e


def _attn_sample(q_s, k_s, v_s, cache_k, cache_v):
    e = _head_expand()
    e01 = jnp.asarray(e, BF16)
    et01 = jnp.asarray(e.T, BF16)
    slopes = jnp.asarray([[2.0 ** (-8.0 * (h + 1) / N_HEADS) for h in range(N_HEADS)]], F32)
    row = pl.BlockSpec((1, ROWS_S, ATTN_W), lambda b: (b, 0, 0))
    cache = pl.BlockSpec((1, W_BUF, ATTN_W), lambda b: (b, 0, 0))
    return pl.pallas_call(
        _attn_sample_kernel,
        name="attn_sample",
        grid=(N_SAMPLE_B,),
        in_specs=[row, row, row, cache, cache, _full(e01.shape), _full(et01.shape), _full(slopes.shape)],
        out_specs=row,
        out_shape=jax.ShapeDtypeStruct((N_SAMPLE_B, ROWS_S, ATTN_W), F32),
        scratch_shapes=[pltpu.VMEM((N_HEADS // 2, W_BUF + ROWS_S, LANES), F32)] * 2,
        compiler_params=_cparams(("arbitrary",)),
    )(q_s, k_s, v_s, cache_k, cache_v, e01, et01, slopes)


HALO = 8


def _dn_pre_kernel(dn_ref, halo_ref, b_ref, a_ref, valid_ref, cw_ref, alog_ref, dtb_ref, e_ref, et_ref, ltri_ref,
                   q_ref, k_ref, v_ref, bx_ref, gx_ref, buf_ref):
    ts = dn_ref.shape[0]
    buf_ref[0:HALO, :] = halo_ref[0]
    buf_ref[HALO:HALO + ts, :] = dn_ref[...]
    y = jnp.zeros((ts, CONV_CH), F32)
    for i in range(CONV_W):
        y = y + buf_ref[pl.ds(HALO - (CONV_W - 1) + i, ts), :] * cw_ref[i:i + 1, :]
    y = y * _sigmoid(y)
    valid = valid_ref[...]
    e01 = e_ref[...]
    et01 = et_ref[...]

    def l2n(t):
        ss = _dot3_r01(t * t, e01)
        inv = lax.rsqrt(ss + NORM_EPS)
        return t * _dot3_r01(inv, et01)

    q_ref[...] = l2n(y[:, :DN_W]) * (HEAD_DIM ** -0.5)
    k_ref[...] = l2n(y[:, DN_W:2 * DN_W]) * valid
    v_ref[...] = y[:, 2 * DN_W:] * valid
    beta = _sigmoid(b_ref[...]) * valid
    sp_in = a_ref[...] + dtb_ref[...]
    softplus = jnp.maximum(sp_in, 0.0) + jnp.log1p(jnp.exp(-jnp.abs(sp_in)))
    g = -jnp.exp(alog_ref[...]) * softplus * valid
    bx_ref[...] = _dot3_r01(beta, et01)
    gx_ref[...] = _dot3_l01(ltri_ref[...], _dot3_r01(g, et01))


def _dn_pre(dn, halo, b_logit, a_logit, valid, conv_w, a_log, dt_bias):
    nt = halo.shape[0]
    rows = nt * TM
    e = _head_expand()
    e01 = jnp.asarray(e, BF16)
    et01 = jnp.asarray(e.T, BF16)
    ii = np.arange(TM)
    ltri = jnp.asarray(((ii[:, None] >= ii[None, :]) & (ii[:, None] // DN_C == ii[None, :] // DN_C)), BF16)
    row = lambda w: pl.BlockSpec((TM, w), lambda i: (i, 0))
    return pl.pallas_call(
        _dn_pre_kernel,
        name="dn_pre",
        grid=(nt,),
        in_specs=[row(CONV_CH), pl.BlockSpec((1, HALO, CONV_CH), lambda i: (i, 0, 0)), row(N_HEADS), row(N_HEADS),
                  row(1), _full(conv_w.shape), _full((1, N_HEADS)), _full((1, N_HEADS)),
                  _full(e01.shape), _full(et01.shape), _full(ltri.shape)],
        out_specs=[row(DN_W)] * 5,
        out_shape=[jax.ShapeDtypeStruct((rows, DN_W), F32)] * 5,
        scratch_shapes=[pltpu.VMEM((HALO + TM, CONV_CH), F32)],
        compiler_params=_cparams(("arbitrary",)),
    )(dn, halo, b_logit, a_logit, valid, conv_w, a_log.reshape(1, N_HEADS), dt_bias.reshape(1, N_HEADS),
      e01, et01, ltri)


def _delta_streams(q, k, v, bx, gcx, s_bd):
    c = DN_C
    n = len(q)
    each = range(n)
    lane = lax.broadcasted_iota(I32, (1, LANES), 1)
    h0 = lane < HALF
    m0 = h0.astype(F32)
    m1 = 1.0 - m0
    ii = lax.broadcasted_iota(I32, (2 * c, 2 * c), 0)
    jj = lax.broadcasted_iota(I32, (2 * c, 2 * c), 1)
    same = jnp.where(ii < c, 0, 1) == jnp.where(jj < c, 0, 1)
    order = jnp.where(same, ii - jj, -1)
    tril = order >= 0
    strict = order > 0
    eye = jnp.where(ii == jj, 1.0, 0.0)

    def stack(x):
        return jnp.concatenate([x * m0, x * m1], axis=0)

    def fold(x2):
        return x2[:c] + x2[c:]

    def decay_of(g):
        gsw = pltpu.roll(g, HALF, axis=1)
        row_b = jnp.concatenate([jnp.where(h0, g, gsw), jnp.where(h0, gsw, g)], axis=0)
        return jnp.where(tril, jnp.exp(jnp.where(tril, row_b - row_b.T, 0.0)), 0.0)

    eg = [jnp.exp(gcx[i]) for i in each]
    kb = [k[i] * bx[i] for i in each]
    glast = [gcx[i][c - 1:c, :] for i in each]
    decay = [decay_of(gcx[i]) for i in each]
    k2 = [stack(k[i]).astype(BF16) for i in each]
    lhs = [jnp.concatenate([stack(kb[i]), stack(q[i])], axis=0).astype(BF16) for i in each]
    aq = [_dot_nt(lhs[i], k2[i]) for i in each]
    a_low = [jnp.where(strict, aq[i][:2 * c] * decay[i], 0.0) for i in each]
    aqk = [jnp.where(tril, aq[i][2 * c:] * decay[i], 0.0).astype(BF16) for i in each]
    tinv = [eye - a_low[i] for i in each]
    apow = [a_low[i].astype(BF16) for i in each]
    for _ in range(5):
        apow = [_dot(apow[i], apow[i]).astype(BF16) for i in each]
        tinv = [tinv[i] + _dot(tinv[i].astype(BF16), apow[i]) for i in each]
    rhs = [jnp.concatenate([stack(v[i] * bx[i]), stack(kb[i] * eg[i])], axis=1).astype(BF16) for i in each]
    uw = [_dot(tinv[i].astype(BF16), rhs[i]) for i in each]
    u = [fold(uw[i][:, :LANES]) for i in each]
    w = [fold(uw[i][:, LANES:]) for i in each]
    ws = [_dot(jnp.concatenate([w[i], q[i] * eg[i]], axis=0).astype(BF16), s_bd[i].astype(BF16)) for i in each]
    v_new = [u[i] - ws[i][:c] for i in each]
    intra = [_dot(aqk[i], stack(v_new[i]).astype(BF16)) for i in each]
    kd = [(k[i] * jnp.exp(glast[i] - gcx[i])).T.astype(BF16) for i in each]
    upd = [_dot(kd[i], v_new[i].astype(BF16)) for i in each]
    o = [ws[i][c:] + fold(intra[i]) for i in each]
    s_new = [s_bd[i] * jnp.exp(glast[i]) + jnp.where(same, upd[i], 0.0) for i in each]
    return o, s_new


def _delta_kernel(q_ref, k_ref, v_ref, bx_ref, gx_ref, s0_ref, o_ref, sout_ref, s_ref):
    ci = pl.program_id(1)
    nb = q_ref.shape[0]
    npair = N_HEADS // 2

    @pl.when(ci == 0)
    def _():
        s_ref[...] = s0_ref[...]

    streams = [(b, hp) for b in range(nb) for hp in range(npair)]
    lanes = lambda hp: slice(hp * LANES, (hp + 1) * LANES)
    take = lambda ref: [ref[b, :, lanes(hp)] for b, hp in streams]
    o, s_new = _delta_streams(take(q_ref), take(k_ref), take(v_ref), take(bx_ref), take(gx_ref),
                              [s_ref[b, hp] for b, hp in streams])
    for i, (b, hp) in enumerate(streams):
        o_ref[b, :, lanes(hp)] = o[i]
        s_ref[b, hp] = s_new[i]

    @pl.when(ci == pl.num_programs(1) - 1)
    def _():
        sout_ref[...] = s_ref[...]


def _delta(q, k, v, bx, gx, s0_bd, bblk):
    nbatch, length, _ = q.shape
    npair = N_HEADS // 2
    seq = pl.BlockSpec((bblk, DN_C, DN_W), lambda g, c: (g, c, 0))
    st = pl.BlockSpec((bblk, npair, LANES, LANES), lambda g, c: (g, 0, 0, 0))
    return pl.pallas_call(
        _delta_kernel,
        name="delta",
        grid=(nbatch // bblk, length // DN_C),
        in_specs=[seq] * 5 + [st],
        out_specs=[seq, st],
        out_shape=[jax.ShapeDtypeStruct((nbatch, length, DN_W), F32),
                   jax.ShapeDtypeStruct((nbatch, npair, LANES, LANES), F32)],
        scratch_shapes=[pltpu.VMEM((bblk, npair, LANES, LANES), F32)],
        compiler_params=_cparams(("arbitrary", "arbitrary")),
    )(q, k, v, bx, gx, s0_bd)


def _state_to_bd(s):
    b = s.shape[0]
    s = s.reshape(b, N_HEADS // 2, 2, HALF, HALF)
    zero = jnp.zeros_like(s[:, :, 0])
    top = jnp.concatenate([s[:, :, 0], zero], axis=-1)
    bot = jnp.concatenate([zero, s[:, :, 1]], axis=-1)
    return jnp.concatenate([top, bot], axis=-2)


def _state_from_bd(s):
    b = s.shape[0]
    return jnp.stack([s[:, :, :HALF, :HALF], s[:, :, HALF:, HALF:]], axis=2).reshape(b, N_HEADS, HALF, HALF)


def _out_kernel(xp_ref, xs_ref, ap_ref, as_ref, dp_ref, ds_ref, z_ref, ng_ref, bavg_ref, wa_ref, wd_ref, g_ref, b_ref,
                o_ref):
    o = _pick(dp_ref, ds_ref)
    oh, ol = _split2(o * o)
    ms = _dot(oh, bavg_ref[...]) + _dot(ol, bavg_ref[...])
    z = z_ref[...]
    dn = o * lax.rsqrt(ms + NORM_EPS) * ng_ref[...] * (z * _sigmoid(z))
    mix = _dot(_pick(ap_ref, as_ref).astype(BF16), wa_ref[...]) + _dot(dn.astype(BF16), wd_ref[...])
    o_ref[...] = _layer_norm(ALPHA * _pick(xp_ref, xs_ref) + mix, g_ref[...], b_ref[...])


def _out_proj(x_p, x_s, att_p, att_s, dn_p, dn_s, z, norm_g, w_out, g, b):
    wa = w_out[:ATTN_W].astype(BF16)
    wd = w_out[ATTN_W:].astype(BF16)
    normg = jnp.tile(norm_g.reshape(1, HEAD_DIM), (1, N_HEADS))
    bavg = jnp.asarray(np.kron(np.eye(N_HEADS), np.ones((HALF, HALF))) / HALF, BF16)
    row = lambda w: pl.BlockSpec((TM, w), lambda i: (i, 0))
    return pl.pallas_call(
        _out_kernel,
        name="out_proj",
        grid=(N_TILES,),
        in_specs=_row2(D_MODEL) + _row2(ATTN_W) + _row2(DN_W) + [
            row(DN_W), _full(normg.shape), _full(bavg.shape),
            _full(wa.shape), _full(wd.shape), _full((1, D_MODEL)), _full((1, D_MODEL))],
        out_specs=row(D_MODEL),
        out_shape=jax.ShapeDtypeStruct((N_PAD, D_MODEL), F32),
        compiler_params=_cparams(("arbitrary",)),
    )(x_p, x_s, att_p, att_s, dn_p, dn_s, z, normg, bavg, wa, wd, g.reshape(1, -1), b.reshape(1, -1))


N_GROUPS = 8
GROUP = N_EXPERTS // N_GROUPS
SLOT_RADIX = 64
_ET_SPEC = pl.BlockSpec((1, 2 * N_EXPERTS, TM), lambda i, *_: (i, 0, 0))
_SEG_SPEC = pl.BlockSpec((1, 1, 2 * N_EXPERTS), lambda i, *_: (i, 0, 0))
TOPK_GROUPS = 4
NEG = -jnp.inf


def _route_kernel(x_ref, wh_ref, wl_ref, bias_ref, upper_ref, lows_ref, st_ref, ge_ref, cnt_ref):
    t = pl.program_id(0)
    xh, xl = _split2(x_ref[...])
    wh = wh_ref[...]
    logits = _dot_nt(wh, xh) + _dot_nt(wh, xl) + _dot_nt(wl_ref[...], xh)
    scores = _sigmoid(logits)
    choice = scores + bias_ref[...]
    i8 = lax.broadcasted_iota(I32, (GROUP, TM), 0)

    def first_max(vals):
        m = jnp.max(vals, axis=0, keepdims=True)
        idx = jnp.min(jnp.where(vals == m, i8, GROUP), axis=0, keepdims=True)
        return m, idx

    pieces = [choice[g * GROUP:(g + 1) * GROUP, :] for g in range(N_GROUPS)]
    gsc = jnp.zeros((N_GROUPS, TM), F32)
    for g in range(N_GROUPS):
        m1, idx1 = first_max(pieces[g])
        m2 = jnp.max(jnp.where(i8 == idx1, NEG, pieces[g]), axis=0, keepdims=True)
        gsc = jnp.where(i8 == g, m1 + m2, gsc)
    gsel = jnp.zeros((N_GROUPS, TM), F32)
    for _ in range(TOPK_GROUPS):
        _, idx = first_max(gsc)
        hit = i8 == idx
        gsel = jnp.where(hit, 1.0, gsel)
        gsc = jnp.where(hit, NEG, gsc)
    masked = [jnp.where(gsel[g:g + 1, :] > 0.5, pieces[g], NEG) for g in range(N_GROUPS)]
    eidx = [i8 + g * GROUP for g in range(N_GROUPS)]
    member = [jnp.zeros((GROUP, TM), F32) for _ in range(N_GROUPS)]
    for _ in range(TOP_K):
        mm = masked[0]
        for g in range(1, N_GROUPS):
            mm = jnp.maximum(mm, masked[g])
        m = jnp.max(mm, axis=0, keepdims=True)
        cand = jnp.where(masked[0] == m, eidx[0], N_EXPERTS)
        for g in range(1, N_GROUPS):
            cand = jnp.minimum(cand, jnp.where(masked[g] == m, eidx[g], N_EXPERTS))
        idx = jnp.min(cand, axis=0, keepdims=True)
        for g in range(N_GROUPS):
            hit = eidx[g] == idx
            member[g] = jnp.where(hit, 1.0, member[g])
            masked[g] = jnp.where(hit, NEG, masked[g])

    tok = t * TM + lax.broadcasted_iota(I32, (1, TM), 1)
    real = jnp.where(tok < N_REAL, 1.0, 0.0)
    mem = jnp.concatenate(member, axis=0) * real
    sel = mem * scores
    gate = (sel / jnp.sum(sel + (1.0 - real), axis=0, keepdims=True) * ROUTED_SCALE).astype(BF16)
    ge_ref[0] = jnp.concatenate([gate, jnp.zeros_like(gate)], axis=0)
    memb = mem.astype(BF16)
    rank = _dot(memb, upper_ref[...])
    cnt = jnp.sum(mem, axis=1, keepdims=True)
    cpad = jnp.floor((cnt + (SEG_ALIGN - 1)) * (1.0 / SEG_ALIGN))
    loc = _dot(lows_ref[...], jnp.broadcast_to(cpad, (N_EXPERTS, TM)).astype(BF16)) * SEG_ALIGN
    slot_e = jnp.where(mem > 0.5, loc + rank, float(M_T))
    slot_hi = jnp.floor(slot_e * (1.0 / SLOT_RADIX)) * SLOT_RADIX
    st_ref[0] = jnp.concatenate([slot_hi, slot_e - slot_hi + 1.0], axis=0).astype(BF16)
    ones = jnp.ones((GROUP, TM), BF16)
    cnt_ref[0] = _dot_nt(ones, memb).astype(I32)


def _route(x1, w_router, router_bias):
    wt = w_router.T
    wh, wl = _split2(wt)
    ii = np.arange(TM)
    upper = jnp.asarray(ii[:, None] < ii[None, :], BF16)
    ee = np.arange(N_EXPERTS)
    lows = jnp.asarray(ee[:, None] > ee[None, :], BF16)
    return pl.pallas_call(
        _route_kernel,
        name="route",
        grid=(N_TILES,),
        in_specs=[pl.BlockSpec((TM, D_MODEL), lambda i: (i, 0)), _full(wh.shape), _full(wl.shape),
                  _full((N_EXPERTS, 1)), _full(upper.shape), _full(lows.shape)],
        out_specs=[_ET_SPEC, _ET_SPEC, pl.BlockSpec((1, GROUP, N_EXPERTS), lambda i: (i, 0, 0))],
        out_shape=[jax.ShapeDtypeStruct((N_TILES, 2 * N_EXPERTS, TM), BF16)] * 2 + [
            jax.ShapeDtypeStruct((N_TILES, GROUP, N_EXPERTS), I32)],
        compiler_params=_cparams(("arbitrary",)),
    )(x1, wh, wl, router_bias.reshape(N_EXPERTS, 1), upper, lows)


def _segment_tables(cnt):
    seg = (cnt + SEG_ALIGN - 1) // SEG_ALIGN
    cpad = seg * SEG_ALIGN
    loc_off = jnp.cumsum(cpad, axis=1) - cpad
    rows_e = jnp.sum(cpad, axis=0)
    blocks_e = (rows_e + BM - 1) // BM
    bend = jnp.cumsum(blocks_e)
    bstart = bend - blocks_e
    glob_off = (bstart * BM)[None, :] + jnp.cumsum(cpad, axis=0) - cpad
    nb = bend[-1]
    bidx = jnp.arange(NB_MAX, dtype=I32)
    bexp = jnp.minimum(jnp.sum(bend[None, :] <= jnp.minimum(bidx, nb - 1)[:, None], axis=1), N_EXPERTS - 1).astype(I32)
    pad_off = bstart * BM + rows_e
    pad_seg = (blocks_e * BM - rows_e) // SEG_ALIGN
    i32 = lambda a: a.reshape(-1).astype(I32)
    bounds = lambda a: jnp.tile(a.astype(F32), (1, 2)).reshape(N_TILES, 1, 2 * N_EXPERTS)
    return (i32(seg), i32(loc_off), i32(glob_off), i32(pad_seg), i32(pad_off), bexp, i32(nb),
            bounds(loc_off), bounds(loc_off + cpad))


def _rows_copy(src_of, dst_of, lo, go, rows, sem):
    lo, go, rows = (v if isinstance(v, int) else pl.multiple_of(v, SEG_ALIGN) for v in (lo, go, rows))
    return pltpu.make_async_copy(src_of(lo, go, rows), dst_of(lo, go, rows), sem)


def _segment_copies(seg_ref, loc_ref, glob_ref, t, src_of, dst_of, sem):
    def body(e, carry):
        k = seg_ref[t * N_EXPERTS + e]

        @pl.when(k > 0)
        def _():
            _rows_copy(src_of, dst_of, loc_ref[t * N_EXPERTS + e], glob_ref[t * N_EXPERTS + e], k * SEG_ALIGN, sem).start()

        return carry

    lax.fori_loop(0, N_EXPERTS, body, 0)


def _wait_rows(src_of, dst_of, rows, sem):
    @pl.when(rows > 0)
    def _():
        _rows_copy(src_of, dst_of, 0, 0, rows, sem).wait()


def _local_rows(m0, rows, lo, hi, slot_tab):
    mcol = (lax.broadcasted_iota(I32, (rows, 2 * N_EXPERTS), 0) + m0).astype(F32)
    e01 = jnp.where(mcol >= lo, jnp.where(mcol < hi, 1.0, 0.0), 0.0).astype(BF16)
    mrow = (lax.broadcasted_iota(I32, (rows, TM), 0) + (m0 + 1)).astype(F32)
    return _dot(e01, slot_tab) == mrow, e01


def _row_spans(used):
    yield None, 0, M_MAIN
    for m0 in range(M_MAIN, M_T, M_CHUNK):
        yield m0 < used, m0, M_CHUNK


def _tile_rows(seg_ref, loc_ref, t):
    last = t * N_EXPERTS + N_EXPERTS - 1
    return loc_ref[last] + seg_ref[last] * SEG_ALIGN


def _dispatch_kernel(seg_ref, loc_ref, glob_ref, pseg_ref, poff_ref, nb_ref, x_ref, st_ref, lo_ref, hi_ref,
                     xs_ref, buf_ref, sem):
    t = pl.program_id(0)
    s = t % 2
    used = _tile_rows(seg_ref, loc_ref, t)
    for live, m0, rows in _row_spans(used):
        def sort_rows(m0=m0, rows=rows):
            hit, _ = _local_rows(m0, rows, lo_ref[0], hi_ref[0], st_ref[0])
            onehot = jnp.where(hit, 1.0, 0.0).astype(BF16)
            buf_ref[s, m0:m0 + rows, :] = _dot(onehot, x_ref[...].astype(BF16)).astype(BF16)

        if live is None:
            sort_rows()
        else:
            pl.when(live)(sort_rows)

    src = lambda lo, go, rows: buf_ref.at[s, pl.ds(lo, rows), :]
    dst = lambda lo, go, rows: xs_ref.at[pl.ds(go, rows), :]
    _segment_copies(seg_ref, loc_ref, glob_ref, t, src, dst, sem.at[s])

    @pl.when(t > 0)
    def _():
        prev = lambda lo, go, rows: buf_ref.at[1 - s, pl.ds(lo, rows), :]
        _wait_rows(prev, dst, _tile_rows(seg_ref, loc_ref, t - 1), sem.at[1 - s])

    @pl.when(t == N_TILES - 1)
    def _():
        _wait_rows(src, dst, used, sem.at[s])
        buf_ref[s, 0:BM, :] = jnp.zeros((BM, D_MODEL), BF16)
        zeros = lambda lo, go, rows: buf_ref.at[s, pl.ds(0, rows), :]

        def fill(wait):
            def pad(e, carry):
                k = pseg_ref[e]

                @pl.when(k > 0)
                def _():
                    cp = _rows_copy(zeros, dst, 0, poff_ref[e], k * SEG_ALIGN, sem.at[s])
                    cp.wait() if wait else cp.start()

                return carry

            lax.fori_loop(0, N_EXPERTS, pad, 0)

            def tail(b, carry):
                cp = _rows_copy(zeros, dst, 0, b * BM, BM, sem.at[s])
                cp.wait() if wait else cp.start()
                return carry

            lax.fori_loop(nb_ref[0], NB_MAX, tail, 0)

        fill(False)
        fill(True)


def _dispatch(x1, slot_tab, seg_lo, seg_hi, tables):
    grid_spec = pltpu.PrefetchScalarGridSpec(
        num_scalar_prefetch=len(tables),
        grid=(N_TILES,),
        in_specs=[pl.BlockSpec((TM, D_MODEL), lambda i, *_: (i, 0)), _ET_SPEC, _SEG_SPEC, _SEG_SPEC],
        out_specs=pl.BlockSpec(memory_space=pl.ANY),
        scratch_shapes=[pltpu.VMEM((2, M_T, D_MODEL), BF16), pltpu.SemaphoreType.DMA((2,))],
    )
    return pl.pallas_call(
        _dispatch_kernel,
        name="dispatch",
        grid_spec=grid_spec,
        out_shape=jax.ShapeDtypeStruct((NB_MAX * BM, D_MODEL), BF16),
        compiler_params=_cparams(("arbitrary",)),
    )(*tables, x1, slot_tab, seg_lo, seg_hi)


def _expert_kernel(bexp_ref, nb_ref, xs_ref, wg_ref, wu_ref, wd_ref, ys_ref, wgu_s, wd_s):
    b = pl.program_id(0)
    nb = nb_ref[0]
    prev = bexp_ref[jnp.maximum(b - 1, 0)]
    fresh = jnp.logical_or(b == 0, bexp_ref[b] != prev)

    @pl.when(fresh)
    def _():
        wgu_s[:, :EXPERT_DIM] = wg_ref[0].astype(BF16)
        wgu_s[:, EXPERT_DIM:] = wu_ref[0].astype(BF16)
        wd_s[...] = wd_ref[0].astype(BF16)

    @pl.when(b < nb)
    def _():
        h = _dot(xs_ref[...], wgu_s[...])
        g = h[:, :EXPERT_DIM]
        a = (g * _sigmoid(g) * h[:, EXPERT_DIM:]).astype(BF16)
        ys_ref[...] = _dot(a, wd_s[...]).astype(BF16)

    @pl.when(b >= nb)
    def _():
        ys_ref[...] = jnp.zeros((BM, D_MODEL), BF16)


def _experts(xs, we_gate, we_up, we_down, bexp, nb):
    last = lambda b, nb_ref: jnp.minimum(b, nb_ref[0] - 1)
    grid_spec = pltpu.PrefetchScalarGridSpec(
        num_scalar_prefetch=2,
        grid=(NB_MAX,),
        in_specs=[pl.BlockSpec((BM, D_MODEL), lambda b, be, nb_ref: (last(b, nb_ref), 0)),
                  pl.BlockSpec((1, D_MODEL, EXPERT_DIM), lambda b, be, nb_ref: (be[b], 0, 0)),
                  pl.BlockSpec((1, D_MODEL, EXPERT_DIM), lambda b, be, nb_ref: (be[b], 0, 0)),
                  pl.BlockSpec((1, EXPERT_DIM, D_MODEL), lambda b, be, nb_ref: (be[b], 0, 0))],
        out_specs=pl.BlockSpec((BM, D_MODEL), lambda b, be, nb_ref: (b, 0)),
        scratch_shapes=[pltpu.VMEM((D_MODEL, 2 * EXPERT_DIM), BF16), pltpu.VMEM((EXPERT_DIM, D_MODEL), BF16)],
    )
    return pl.pallas_call(
        _expert_kernel,
        name="expert",
        grid_spec=grid_spec,
        out_shape=jax.ShapeDtypeStruct((NB_MAX * BM, D_MODEL), BF16),
        compiler_params=_cparams(("arbitrary",)),
    )(bexp, nb, xs, we_gate, we_up, we_down)


def _combine_kernel(seg_ref, loc_ref, glob_ref, x_ref, st_ref, ge_ref, lo_ref, hi_ref, pp_ref, ps_ref, ys_ref,
                    wsgu_ref, wsd_ref, g2_ref, b2_ref, wpg_ref, wpp_ref, g3_ref, b3_ref, op_ref, os_ref,
                    buf_ref, acc_ref, sem):
    t = pl.program_id(0)
    s = t % 2
    used = _tile_rows(seg_ref, loc_ref, t)
    src = lambda lo, go, rows: ys_ref.at[pl.ds(go, rows), :]
    into = lambda slot: (lambda lo, go, rows: buf_ref.at[slot, pl.ds(lo, rows), :])

    @pl.when(t == 0)
    def _():
        buf_ref[...] = jnp.zeros((2, M_T, D_MODEL), BF16)
        _segment_copies(seg_ref, loc_ref, glob_ref, t, src, into(s), sem.at[s])

    @pl.when(t + 1 < N_TILES)
    def _():
        _segment_copies(seg_ref, loc_ref, glob_ref, t + 1, src, into(1 - s), sem.at[1 - s])

    _wait_rows(src, into(s), used, sem.at[s])

    for live, m0, rows in _row_spans(used):
        def gather_rows(m0=m0, rows=rows):
            hit, e01 = _local_rows(m0, rows, lo_ref[0], hi_ref[0], st_ref[0])
            pg = jnp.where(hit, _dot(e01, ge_ref[0]), 0.0).astype(BF16)
            return lax.dot_general(pg, buf_ref[s, m0:m0 + rows, :], (((0,), (0,)), ((), ())),
                                   preferred_element_type=F32)

        if live is None:
            acc_ref[...] = gather_rows()
        else:
            @pl.when(live)
            def _(gather_rows=gather_rows):
                acc_ref[...] += gather_rows()

    x = x_ref[...]
    routed = acc_ref[...]
    xb = x.astype(BF16)
    h = _dot(xb, wsgu_ref[...])
    g = h[:, :EXPERT_DIM]
    shared = _dot((g * _sigmoid(g) * h[:, EXPERT_DIM:]).astype(BF16), wsd_ref[...])
    x2 = _layer_norm(ALPHA * x + (routed + shared), g2_ref[...], b2_ref[...])
    ple = _sigmoid(_dot(x2.astype(BF16), wpg_ref[...])) * _dot(_pick(pp_ref, ps_ref).astype(BF16), wpp_ref[...])
    y = _layer_norm(ALPHA * x2 + ple, g3_ref[...], b3_ref[...])

    @pl.when(t < NT_P)
    def _():
        op_ref[...] = y

    @pl.when(t >= NT_P)
    def _():
        os_ref[...] = y


def _combine(x1, slot_tab, gate_e, seg_lo, seg_hi, p_p, p_s, ys, seg, loc_off, glob_off, ws_gate, ws_up, ws_down,
             ln2_g, ln2_b, w_ple_gate, w_ple_proj, ln3_g, ln3_b):
    wsgu = jnp.concatenate([ws_gate, ws_up], axis=1).astype(BF16)
    wsd = ws_down.astype(BF16)
    wpg = w_ple_gate.astype(BF16)
    wpp = w_ple_proj.astype(BF16)
    row = lambda w: pl.BlockSpec((TM, w), lambda i, *_: (i, 0))
    full = lambda shape: pl.BlockSpec(shape, lambda i, *_: (0,) * len(shape))
    vec = full((1, D_MODEL))
    grid_spec = pltpu.PrefetchScalarGridSpec(
        num_scalar_prefetch=3,
        grid=(N_TILES,),
        in_specs=[row(D_MODEL), _ET_SPEC, _ET_SPEC, _SEG_SPEC, _SEG_SPEC] + _row2(PLE_DIM) + [
            pl.BlockSpec(memory_space=pl.ANY),
            full(wsgu.shape), full(wsd.shape), vec, vec, full(wpg.shape), full(wpp.shape), vec, vec],
        out_specs=_row2(D_MODEL),
        scratch_shapes=[pltpu.VMEM((2, M_T, D_MODEL), BF16), pltpu.VMEM((TM, D_MODEL), F32),
                        pltpu.SemaphoreType.DMA((2,))],
    )
    r = lambda a: a.reshape(1, -1)
    return pl.pallas_call(
        _combine_kernel,
        name="combine",
        grid_spec=grid_spec,
        out_shape=[jax.ShapeDtypeStruct((N_P, D_MODEL), F32), jax.ShapeDtypeStruct((N_SP, D_MODEL), F32)],
        compiler_params=_cparams(("arbitrary",)),
    )(seg, loc_off, glob_off, x1, slot_tab, gate_e, seg_lo, seg_hi, p_p, p_s, ys, wsgu, wsd, r(ln2_g), r(ln2_b),
      wpg, wpp, r(ln3_g), r(ln3_b))


def _pad_rows(a, rows):
    return jnp.concatenate([a, jnp.zeros((rows - a.shape[0],) + a.shape[1:], a.dtype)], axis=0)


def kernel(x_prompt, x_sample, cache_k_win, cache_v_win, state_conv, state_delta, p_prompt, p_sample, w_in, conv_w, a_log, dt_bias, dn_norm_g, w_out, ln1_g, ln1_b, w_router, router_bias, we_gate, we_up, we_down, ws_gate, ws_up, ws_down, ln2_g, ln2_b, w_ple_gate, w_ple_proj, ln3_g, ln3_b):
    x_p = x_prompt.reshape(N_P, D_MODEL)
    x_s = _pad_rows(x_sample.reshape(N_S, D_MODEL), N_SP)
    p_p = p_prompt[0].reshape(N_P, PLE_DIM)
    p_s = _pad_rows(p_sample[0].reshape(N_S, PLE_DIM), N_SP)

    qkv, dn_in, z, b_logit, a_logit = _project(x_p, x_s, w_in[0])

    att_p = _attn_prompt(qkv)
    qkv_s = qkv[N_P:N_REAL].reshape(N_SAMPLE_B, DEC_SEQ, 3 * ATTN_W)
    qkv_s8 = jnp.pad(qkv_s, ((0, 0), (0, ROWS_S - DEC_SEQ), (0, 0)))
    ck = cache_k_win[0].reshape(N_SAMPLE_B, W_BUF, ATTN_W)
    cv = cache_v_win[0].reshape(N_SAMPLE_B, W_BUF, ATTN_W)
    att_s = _attn_sample(qkv_s8[:, :, :ATTN_W], qkv_s8[:, :, ATTN_W:2 * ATTN_W], qkv_s8[:, :, 2 * ATTN_W:], ck, cv)
    att_s = _pad_rows(att_s[:, :DEC_SEQ].reshape(N_S, ATTN_W), N_SP)

    tails = dn_in.reshape(N_TILES, TM, CONV_CH)[:NT_P - 1, TM - HALO:]
    halo_p = jnp.concatenate([jnp.zeros((1, HALO, CONV_CH), F32), tails], axis=0)
    seq_start = (jnp.arange(NT_P) % (SEQ // TM) == 0)[:, None, None]
    halo_p = jnp.where(seq_start, 0.0, halo_p)
    q_p, k_p, v_p, bx_p, gx_p = _dn_pre(dn_in, halo_p, b_logit, a_logit, jnp.ones((N_P, 1), F32),
                                        conv_w[0], a_log[0], dt_bias[0])
    shp = (N_PROMPT_B, SEQ, DN_W)
    s0_p = jnp.zeros((N_PROMPT_B, N_HEADS // 2, LANES, LANES), F32)
    o_p, s_p = _delta(q_p.reshape(shp), k_p.reshape(shp), v_p.reshape(shp), bx_p.reshape(shp), gx_p.reshape(shp),
                      s0_p, N_PROMPT_B)

    def seq_pad(tok, state=None):
        w = tok.shape[-1]
        tok = tok.reshape(N_SAMPLE_B, DEC_SEQ, w)
        head = jnp.zeros((N_SAMPLE_B, S_TOK0, w), F32)
        if state is not None:
            head = head.at[:, S_TOK0 - (CONV_W - 1):].set(state)
        tail = jnp.zeros((N_SAMPLE_B, SEQ_S - S_TOK0 - DEC_SEQ, w), F32)
        return jnp.concatenate([head, tok, tail], axis=1).reshape(N_SAMPLE_B * SEQ_S, w)

    dn_tok_s = dn_in[N_P:N_REAL]
    dn_s = seq_pad(dn_tok_s, state_conv[0])
    rows_s = N_SAMPLE_B * SEQ_S
    valid_s = seq_pad(jnp.ones((N_S, 1), F32))
    halo_s = jnp.zeros((rows_s // TM, HALO, CONV_CH), F32)
    q_s, k_s, v_s, bx_s, gx_s = _dn_pre(dn_s, halo_s, seq_pad(b_logit[N_P:N_REAL]), seq_pad(a_logit[N_P:N_REAL]),
                                        valid_s, conv_w[0], a_log[0], dt_bias[0])
    shs = (N_SAMPLE_B, SEQ_S, DN_W)
    o_s, s_s = _delta(q_s.reshape(shs), k_s.reshape(shs), v_s.reshape(shs), bx_s.reshape(shs), gx_s.reshape(shs),
                      _state_to_bd(state_delta[0]), 4)
    dn_o_s = _pad_rows(o_s[:, S_TOK0:S_TOK0 + DEC_SEQ].reshape(N_S, DN_W), N_SP)

    x1 = _out_proj(x_p, x_s, att_p, att_s, o_p.reshape(N_P, DN_W), dn_o_s, z, dn_norm_g[0], w_out[0], ln1_g[0], ln1_b[0])

    slot_tab, gate_e, cnt = _route(x1, w_router[0], router_bias[0])
    seg, loc_off, glob_off, pad_seg, pad_off, bexp, nb, seg_lo, seg_hi = _segment_tables(cnt[:, 0, :])
    xs = _dispatch(x1, slot_tab, seg_lo, seg_hi, (seg, loc_off, glob_off, pad_seg, pad_off, nb))
    ys = _experts(xs, we_gate[0], we_up[0], we_down[0], bexp, nb)
    y_p, y_s = _combine(x1, slot_tab, gate_e, seg_lo, seg_hi, p_p, p_s, ys, seg, loc_off, glob_off,
                        ws_gate[0], ws_up[0], ws_down[0], ln2_g[0], ln2_b[0], w_ple_gate[0], w_ple_proj[0],
                        ln3_g[0], ln3_b[0])

    y_prompt = y_p.reshape(N_PROMPT_B, SEQ, D_MODEL)
    y_sample = y_s[:N_S].reshape(N_SAMPLE_B, DEC_SEQ, D_MODEL)
    heads = (N_HEADS, HEAD_DIM)
    win = lambda c0: jnp.stack([qkv[(b + 1) * SEQ - W_BUF:(b + 1) * SEQ, c0:c0 + ATTN_W] for b in range(N_PROMPT_B)])
    k_pr = win(ATTN_W).reshape(N_PROMPT_B, W_BUF, *heads)
    v_pr = win(2 * ATTN_W).reshape(N_PROMPT_B, W_BUF, *heads)
    conv_p = jnp.stack([dn_in[(b + 1) * SEQ - (CONV_W - 1):(b + 1) * SEQ] for b in range(N_PROMPT_B)])
    k_new = qkv_s[:, :, ATTN_W:2 * ATTN_W].reshape(N_SAMPLE_B, DEC_SEQ, *heads)
    v_new = qkv_s[:, :, 2 * ATTN_W:].reshape(N_SAMPLE_B, DEC_SEQ, *heads)
    k_sm = jnp.concatenate([cache_k_win[0][:, DEC_SEQ:], k_new], axis=1)
    v_sm = jnp.concatenate([cache_v_win[0][:, DEC_SEQ:], v_new], axis=1)
    conv_s = dn_tok_s.reshape(N_SAMPLE_B, DEC_SEQ, CONV_CH)[:, DEC_SEQ - (CONV_W - 1):]
    return (y_prompt, y_sample, k_pr[None], v_pr[None], conv_p[None], _state_from_bd(s_p)[None],
            k_sm[None], v_sm[None], conv_s[None], _state_from_bd(s_s)[None])
```

```python
import functools

import jax
import jax.numpy as jnp
import numpy as np
from jax import lax
from jax.experimental import pallas as pl
from jax.experimental.pallas import tpu as pltpu

F32 = jnp.float32
BF16 = jnp.bfloat16
I32 = jnp.int32

D_MODEL = 1024
N_PROMPT_B, SEQ = 2, 8192
N_SAMPLE_B, DEC_SEQ = 32, 4
W_BUF = 2048
N_HEADS = 8
HEAD_DIM = 64
ATTN_W = 512
CONV_CH = 1536
CONV_W = 4
DN_W = 512
N_EXPERTS = 64
TOP_K = 8
EXPERT_DIM = 256
PLE_DIM = 256
BRANCHES = ((128, 1), (512, 4), (2048, 16))
N_BACK = 128
ROUTED_SCALE = 2.5
LN_EPS = 1e-5
NORM_EPS = 1e-6
ALPHA = 2.0 ** 0.25

LANES = 128
HALF = 64
VMEM_LIMIT = 56 * 1024 * 1024

N_P = N_PROMPT_B * SEQ
N_S = N_SAMPLE_B * DEC_SEQ
N_REAL = N_P + N_S
N_PAD = N_P + 512
TM = 256
N_TILES = N_PAD // TM
NT_P = N_P // TM
N_SP = N_PAD - N_P

SEG_ALIGN = 16
M_T = 3072
M_CHUNK = 256
M_MAIN = 2560
BM = 512
R_MAX = N_PAD * TOP_K + N_TILES * N_EXPERTS * (SEG_ALIGN - 1) + N_EXPERTS * (BM - 1)
NB_MAX = -(-R_MAX // BM)

DN_C = 64
SEQ_S = 64
S_TOK0 = 8


def _dot(a, b):
    return jnp.dot(a, b, preferred_element_type=F32)


def _dot_nt(a, b):
    return lax.dot_general(a, b, (((1,), (1,)), ((), ())), preferred_element_type=F32)


def _split2(x):
    hi = x.astype(BF16)
    lo = (x - hi.astype(F32)).astype(BF16)
    return hi, lo


def _dot_l01(m01, x):
    hi, lo = _split2(x)
    return _dot(m01, hi) + _dot(m01, lo)


def _dot_r01(x, m01):
    hi, lo = _split2(x)
    return _dot(hi, m01) + _dot(lo, m01)


def _sigmoid(x):
    return 1.0 / (1.0 + jnp.exp(-x))


def _layer_norm(x, g, b):
    mu = jnp.mean(x, axis=-1, keepdims=True)
    xc = x - mu
    var = jnp.mean(xc * xc, axis=-1, keepdims=True)
    return xc * lax.rsqrt(var + LN_EPS) * g + b


def _cparams(sem=None):
    return pltpu.CompilerParams(dimension_semantics=sem, vmem_limit_bytes=VMEM_LIMIT)


def _full(shape):
    return pl.BlockSpec(shape, lambda *_: (0,) * len(shape))


def _row2(w):
    return [pl.BlockSpec((TM, w), lambda i, *_: (jnp.minimum(i, NT_P - 1), 0)),
            pl.BlockSpec((TM, w), lambda i, *_: (jnp.maximum(i - NT_P, 0), 0))]


def _pick(p_ref, s_ref):
    return jnp.where(pl.program_id(0) < NT_P, p_ref[...], s_ref[...])


def _proj_kernel(xp_ref, xs_ref, wqkv_ref, wdn_ref, wz_ref, wg_ref, qkv_ref, dn_ref, z_ref, b_ref, a_ref):
    x = _pick(xp_ref, xs_ref)
    xh, xl = _split2(x)
    qkv_ref[...] = _dot(xh, wqkv_ref[...])
    dn_ref[...] = _dot(xh, wdn_ref[...])
    z_ref[...] = _dot(xh, wz_ref[...])
    ng = 2 * N_HEADS
    o1 = _dot(xh, wg_ref[...])
    o2 = _dot(xl, wg_ref[...])
    gates = o1[:, :ng] + o1[:, ng:2 * ng] + o2[:, :ng]
    b_ref[...] = gates[:, :N_HEADS]
    a_ref[...] = gates[:, N_HEADS:]


def _project(x_p, x_s, w_in):
    wqkv = w_in[:, :3 * ATTN_W].astype(BF16)
    wdn = w_in[:, 3 * ATTN_W:3 * ATTN_W + CONV_CH].astype(BF16)
    c0 = 3 * ATTN_W + CONV_CH
    wz = w_in[:, c0:c0 + DN_W].astype(BF16)
    wgh, wgl = _split2(w_in[:, c0 + DN_W:])
    wg = jnp.concatenate([wgh, wgl, jnp.zeros((D_MODEL, LANES - 4 * N_HEADS), BF16)], axis=1)
    row = lambda w: pl.BlockSpec((TM, w), lambda i: (i, 0))
    outs = (3 * ATTN_W, CONV_CH, DN_W, N_HEADS, N_HEADS)
    return pl.pallas_call(
        _proj_kernel,
        name="proj",
        grid=(N_TILES,),
        in_specs=_row2(D_MODEL) + [_full(wqkv.shape), _full(wdn.shape), _full(wz.shape), _full(wg.shape)],
        out_specs=[row(w) for w in outs],
        out_shape=[jax.ShapeDtypeStruct((N_PAD, w), F32) for w in outs],
        compiler_params=_cparams(("arbitrary",)),
    )(x_p, x_s, wqkv, wdn, wz, wg)


QT = 2048
ATTN_UNROLL = 8


def _attn_bias():
    qi = np.arange(N_BACK)[:, None]
    ki = np.arange(2 * N_BACK)[None, :]
    out = np.zeros((2, len(BRANCHES), N_HEADS, N_BACK, 2 * N_BACK), np.float32)
    for var, off in enumerate((N_BACK, 0)):
        dist = qi - ki + off
        valid = (dist >= 0) & (dist <= N_BACK)
        for di, (_, dil) in enumerate(BRANCHES):
            for h in range(N_HEADS):
                slope = 2.0 ** (-8.0 * (h + 1) / N_HEADS) * dil
                out[var, di, h] = np.where(valid, -slope * dist, -1e30)
    return out


def _attn_prompt_kernel(q_ref, k_ref, v_ref, bias_ref, o_ref, acc_ref, m_ref, l_ref):
    qt = pl.program_id(2)
    lane = lax.broadcasted_iota(I32, (1, LANES), 1)
    h0 = lane < HALF
    for di, (_, dil) in enumerate(BRANCHES):
        per_r = QT // (N_BACK * dil)

        def body(i, carry, di=di, dil=dil, per_r=per_r):
            def ds(start, n):
                return pl.ds(start, n) if dil == 1 else pl.ds(start, n, stride=dil)

            blocks = []
            for u in range(ATTN_UNROLL):
                blk = i * ATTN_UNROLL + u
                r = blk // per_r
                jb = blk % per_r
                qloc = r + dil * N_BACK * jb
                first = jnp.logical_and(qt == 0, jb == 0)
                kstart = jnp.where(first, r, qt * QT + qloc - N_BACK * dil)
                blocks.append((qloc, kstart, jnp.where(first, 1, 0)))
            q8 = [q_ref[ds(qloc, N_BACK), :] * (HEAD_DIM ** -0.5) for qloc, _, _ in blocks]
            kb = [k_ref[ds(kstart, 2 * N_BACK), :].astype(BF16) for _, kstart, _ in blocks]
            vv = [v_ref[ds(kstart, 2 * N_BACK), :] for _, kstart, _ in blocks]
            heads = [(u, hh) for u in range(ATTN_UNROLL) for hh in range(2)]
            keep = lambda hh: h0 if hh == 0 else jnp.logical_not(h0)
            s = [_dot_nt(jnp.where(keep(hh), q8[u], 0.0).astype(BF16), kb[u]) + bias_ref[blocks[u][2], di, hh]
                 for u, hh in heads]
            m = [jnp.max(x, axis=-1, keepdims=True) for x in s]
            p = [jnp.exp(x - mm).astype(BF16) for x, mm in zip(s, m)]
            res = [_dot(pp, jnp.where(keep(hh), vv[u], 1.0).astype(BF16)) for pp, (u, hh) in zip(p, heads)]
            for u, (qloc, _, _) in enumerate(blocks):
                rows = ds(qloc, N_BACK)
                acc_ref[di, rows, :] = jnp.where(h0, res[2 * u], res[2 * u + 1])
                l_ref[di, rows, :] = jnp.where(h0, res[2 * u + 1], res[2 * u])
                m_ref[di, rows, :] = jnp.where(h0, m[2 * u], m[2 * u + 1])
            return carry

        lax.fori_loop(0, QT // (N_BACK * ATTN_UNROLL), body, 0)

    mall = m_ref[...]
    mtot = jnp.max(mall, axis=0)
    num = jnp.zeros((QT, LANES), F32)
    den = jnp.zeros((QT, LANES), F32)
    for di in range(len(BRANCHES)):
        w = jnp.exp(mall[di] - mtot)
        num = num + w * acc_ref[di]
        den = den + w * pltpu.roll(l_ref[di], HALF, axis=1)
    o_ref[...] = num / den


def _attn_prompt(qkv):
    bias = jnp.asarray(_attn_bias())
    npair = N_HEADS // 2
    nqt = SEQ // QT
    kv_rows = SEQ
    q_spec = pl.BlockSpec((QT, LANES), lambda b, hp, t: (b * nqt + t, hp))
    k_spec = pl.BlockSpec((kv_rows, LANES), lambda b, hp, t: (b, npair + hp))
    v_spec = pl.BlockSpec((kv_rows, LANES), lambda b, hp, t: (b, 2 * npair + hp))
    bias_spec = pl.BlockSpec((2, len(BRANCHES), 2, N_BACK, 2 * N_BACK), lambda b, hp, t: (0, 0, hp, 0, 0))
    return pl.pallas_call(
        _attn_prompt_kernel,
        name="attn_prompt",
        grid=(N_PROMPT_B, npair, nqt),
        in_specs=[q_spec, k_spec, v_spec, bias_spec],
        out_specs=pl.BlockSpec((QT, LANES), lambda b, hp, t: (b * nqt + t, hp)),
        out_shape=jax.ShapeDtypeStruct((N_P, ATTN_W), F32),
        scratch_shapes=[pltpu.VMEM((len(BRANCHES), QT, LANES), F32)] * 3,
        compiler_params=_cparams(("arbitrary", "arbitrary", "arbitrary")),
    )(qkv, qkv, qkv, bias)


ROWS_S = 8


def _attn_sample_kernel(q_ref, kn_ref, vn_ref, ck_ref, cv_ref, e_ref, et_ref, slope_ref, o_ref, kall_ref, vall_ref):
    npair = N_HEADS // 2
    for hp in range(npair):
        sl = slice(hp * LANES, (hp + 1) * LANES)
        kall_ref[hp, 0:W_BUF, :] = ck_ref[0, :, sl]
        vall_ref[hp, 0:W_BUF, :] = cv_ref[0, :, sl]
        kall_ref[hp, W_BUF:W_BUF + ROWS_S, :] = kn_ref[0, :, sl]
        vall_ref[hp, W_BUF:W_BUF + ROWS_S, :] = vn_ref[0, :, sl]

    def rows(ref, start, n, dil):
        idx = pl.ds(start, n) if dil == 1 else pl.ds(start, n, stride=dil)
        return jnp.concatenate([ref[hp, idx, :] for hp in range(npair)], axis=1)

    e01 = e_ref[...]
    et01 = et_ref[...]
    slope = slope_ref[...]
    jrev = (N_BACK - lax.broadcasted_iota(I32, (N_BACK, 1), 0)).astype(F32)
    inew = lax.broadcasted_iota(I32, (ROWS_S, 1), 0)
    scale = HEAD_DIM ** -0.5
    knew = rows(kall_ref, W_BUF, ROWS_S, 1)
    vnew = rows(vall_ref, W_BUF, ROWS_S, 1).astype(BF16).astype(F32)

    def scores(kr, qt):
        ph, plo = _split2(kr * qt)
        return (_dot(ph, e01) + _dot(plo, e01)) * scale

    outs = []
    for t in range(DEC_SEQ):
        qt = q_ref[0, t:t + 1, :]
        s_self = jnp.where(inew == t, scores(knew, qt), -1e30)
        s_br = []
        v_br = []
        for _, dil in BRANCHES:
            start = W_BUF + t - N_BACK * dil
            s_br.append(scores(rows(kall_ref, start, N_BACK, dil), qt) - (slope * dil) * jrev)
            v_br.append(rows(vall_ref, start, N_BACK, dil))
        m = jnp.max(s_self, axis=0, keepdims=True)
        for s in s_br:
            m = jnp.maximum(m, jnp.max(s, axis=0, keepdims=True))
        pb = (jnp.exp(s_self - m) * float(len(BRANCHES))).astype(BF16)
        den = jnp.sum(pb.astype(F32), axis=0, keepdims=True)
        num = jnp.sum(_dot(pb, et01) * vnew, axis=0, keepdims=True)
        for s, vr in zip(s_br, v_br):
            pb = jnp.exp(s - m).astype(BF16)
            den = den + jnp.sum(pb.astype(F32), axis=0, keepdims=True)
            num = num + jnp.sum(_dot(pb, et01) * vr.astype(BF16).astype(F32), axis=0, keepdims=True)
        denx = _dot_r01(jnp.broadcast_to(den, (ROWS_S, N_HEADS)), et01)[0:1]
        outs.append(num / denx)
    pad = jnp.zeros((ROWS_S - DEC_SEQ, ATTN_W), F32)
    o_ref[0] = jnp.concatenate(outs + [pad], axis=0)


def _head_expand():
    e = np.zeros((ATTN_W, N_HEADS), np.float32)
    for h in range(N_HEADS):
        e[h * HEAD_DIM:(h + 1) * HEAD_DIM, h] = 1.0
    return e


def _attn_sample(q_s, k_s, v_s, cache_k, cache_v):
    e = _head_expand()
    e01 = jnp.asarray(e, BF16)
    et01 = jnp.asarray(e.T, BF16)
    slopes = jnp.asarray([[2.0 ** (-8.0 * (h + 1) / N_HEADS) for h in range(N_HEADS)]], F32)
    row = pl.BlockSpec((1, ROWS_S, ATTN_W), lambda b: (b, 0, 0))
    cache = pl.BlockSpec((1, W_BUF, ATTN_W), lambda b: (b, 0, 0))
    return pl.pallas_call(
        _attn_sample_kernel,
        name="attn_sample",
        grid=(N_SAMPLE_B,),
        in_specs=[row, row, row, cache, cache, _full(e01.shape), _full(et01.shape), _full(slopes.shape)],
        out_specs=row,
        out_shape=jax.ShapeDtypeStruct((N_SAMPLE_B, ROWS_S, ATTN_W), F32),
        scratch_shapes=[pltpu.VMEM((N_HEADS // 2, W_BUF + ROWS_S, LANES), F32)] * 2,
        compiler_params=_cparams(("arbitrary",)),
    )(q_s, k_s, v_s, cache_k, cache_v, e01, et01, slopes)


HALO = 8


def _dn_pre_kernel(dn_ref, halo_ref, b_ref, a_ref, valid_ref, cw_ref, alog_ref, dtb_ref, e_ref, et_ref, ltri_ref,
                   q_ref, k_ref, v_ref, bx_ref, gx_ref, buf_ref):
    ts = dn_ref.shape[0]
    buf_ref[0:HALO, :] = halo_ref[0]
    buf_ref[HALO:HALO + ts, :] = dn_ref[...]
    y = jnp.zeros((ts, CONV_CH), F32)
    for i in range(CONV_W):
        y = y + buf_ref[pl.ds(HALO - (CONV_W - 1) + i, ts), :] * cw_ref[i:i + 1, :]
    y = y * _sigmoid(y)
    valid = valid_ref[...]
    e01 = e_ref[...]
    et01 = et_ref[...]

    def l2n(t):
        ss = _dot_r01(t * t, e01)
        inv = lax.rsqrt(ss + NORM_EPS)
        return t * _dot_r01(inv, et01)

    q_ref[...] = l2n(y[:, :DN_W]) * (HEAD_DIM ** -0.5)
    k_ref[...] = l2n(y[:, DN_W:2 * DN_W]) * valid
    v_ref[...] = y[:, 2 * DN_W:] * valid
    beta = _sigmoid(b_ref[...]) * valid
    sp_in = a_ref[...] + dtb_ref[...]
    softplus = jnp.maximum(sp_in, 0.0) + jnp.log1p(jnp.exp(-jnp.abs(sp_in)))
    g = -jnp.exp(alog_ref[...]) * softplus * valid
    bx_ref[...] = _dot_r01(beta, et01)
    gx_ref[...] = _dot_l01(ltri_ref[...], _dot_r01(g, et01))


def _dn_pre(dn, halo, b_logit, a_logit, valid, conv_w, a_log, dt_bias):
    nt = halo.shape[0]
    rows = nt * TM
    e = _head_expand()
    e01 = jnp.asarray(e, BF16)
    et01 = jnp.asarray(e.T, BF16)
    ii = np.arange(TM)
    ltri = jnp.asarray(((ii[:, None] >= ii[None, :]) & (ii[:, None] // DN_C == ii[None, :] // DN_C)), BF16)
    row = lambda w: pl.BlockSpec((TM, w), lambda i: (i, 0))
    return pl.pallas_call(
        _dn_pre_kernel,
        name="dn_pre",
        grid=(nt,),
        in_specs=[row(CONV_CH), pl.BlockSpec((1, HALO, CONV_CH), lambda i: (i, 0, 0)), row(N_HEADS), row(N_HEADS),
                  row(1), _full(conv_w.shape), _full((1, N_HEADS)), _full((1, N_HEADS)),
                  _full(e01.shape), _full(et01.shape), _full(ltri.shape)],
        out_specs=[row(DN_W)] * 5,
        out_shape=[jax.ShapeDtypeStruct((rows, DN_W), F32)] * 5,
        scratch_shapes=[pltpu.VMEM((HALO + TM, CONV_CH), F32)],
        compiler_params=_cparams(("arbitrary",)),
    )(dn, halo, b_logit, a_logit, valid, conv_w, a_log.reshape(1, N_HEADS), dt_bias.reshape(1, N_HEADS),
      e01, et01, ltri)


def _delta_streams(q, k, v, bx, gcx, s_bd):
    c = DN_C
    n = len(q)
    each = range(n)
    lane = lax.broadcasted_iota(I32, (1, LANES), 1)
    h0 = lane < HALF
    m0 = h0.astype(F32)
    m1 = 1.0 - m0
    ii = lax.broadcasted_iota(I32, (2 * c, 2 * c), 0)
    jj = lax.broadcasted_iota(I32, (2 * c, 2 * c), 1)
    same = jnp.where(ii < c, 0, 1) == jnp.where(jj < c, 0, 1)
    order = jnp.where(same, ii - jj, -1)
    tril = order >= 0
    strict = order > 0
    eye = jnp.where(ii == jj, 1.0, 0.0)

    def stack(x):
        return jnp.concatenate([x * m0, x * m1], axis=0)

    def fold(x2):
        return x2[:c] + x2[c:]

    def decay_of(g):
        gsw = pltpu.roll(g, HALF, axis=1)
        row_b = jnp.concatenate([jnp.where(h0, g, gsw), jnp.where(h0, gsw, g)], axis=0)
        return jnp.where(tril, jnp.exp(jnp.where(tril, row_b - row_b.T, 0.0)), 0.0)

    eg = [jnp.exp(gcx[i]) for i in each]
    kb = [k[i] * bx[i] for i in each]
    glast = [gcx[i][c - 1:c, :] for i in each]
    decay = [decay_of(gcx[i]) for i in each]
    k2 = [stack(k[i]).astype(BF16) for i in each]
    lhs = [jnp.concatenate([stack(kb[i]), stack(q[i])], axis=0).astype(BF16) for i in each]
    aq = [_dot_nt(lhs[i], k2[i]) for i in each]
    a_low = [jnp.where(strict, aq[i][:2 * c] * decay[i], 0.0) for i in each]
    aqk = [jnp.where(tril, aq[i][2 * c:] * decay[i], 0.0).astype(BF16) for i in each]
    tinv = [eye - a_low[i] for i in each]
    apow = [a_low[i].astype(BF16) for i in each]
    for _ in range(5):
        apow = [_dot(apow[i], apow[i]).astype(BF16) for i in each]
        tinv = [tinv[i] + _dot(tinv[i].astype(BF16), apow[i]) for i in each]
    rhs = [jnp.concatenate([stack(v[i] * bx[i]), stack(kb[i] * eg[i])], axis=1).astype(BF16) for i in each]
    uw = [_dot(tinv[i].astype(BF16), rhs[i]) for i in each]
    u = [fold(uw[i][:, :LANES]) for i in each]
    w = [fold(uw[i][:, LANES:]) for i in each]
    ws = [_dot(jnp.concatenate([w[i], q[i] * eg[i]], axis=0).astype(BF16), s_bd[i].astype(BF16)) for i in each]
    v_new = [u[i] - ws[i][:c] for i in each]
    intra = [_dot(aqk[i], stack(v_new[i]).astype(BF16)) for i in each]
    kd = [(k[i] * jnp.exp(glast[i] - gcx[i])).T.astype(BF16) for i in each]
    upd = [_dot(kd[i], v_new[i].astype(BF16)) for i in each]
    o = [ws[i][c:] + fold(intra[i]) for i in each]
    s_new = [s_bd[i] * jnp.exp(glast[i]) + jnp.where(same, upd[i], 0.0) for i in each]
    return o, s_new


def _delta_kernel(q_ref, k_ref, v_ref, bx_ref, gx_ref, s0_ref, o_ref, sout_ref, s_ref):
    ci = pl.program_id(1)
    nb = q_ref.shape[0]
    npair = N_HEADS // 2

    @pl.when(ci == 0)
    def _():
        s_ref[...] = s0_ref[...]

    streams = [(b, hp) for b in range(nb) for hp in range(npair)]
    lanes = lambda hp: slice(hp * LANES, (hp + 1) * LANES)
    take = lambda ref: [ref[b, :, lanes(hp)] for b, hp in streams]
    o, s_new = _delta_streams(take(q_ref), take(k_ref), take(v_ref), take(bx_ref), take(gx_ref),
                              [s_ref[b, hp] for b, hp in streams])
    for i, (b, hp) in enumerate(streams):
        o_ref[b, :, lanes(hp)] = o[i]
        s_ref[b, hp] = s_new[i]

    @pl.when(ci == pl.num_programs(1) - 1)
    def _():
        sout_ref[...] = s_ref[...]


def _delta(q, k, v, bx, gx, s0_bd, bblk):
    nbatch, length, _ = q.shape
    npair = N_HEADS // 2
    seq = pl.BlockSpec((bblk, DN_C, DN_W), lambda g, c: (g, c, 0))
    st = pl.BlockSpec((bblk, npair, LANES, LANES), lambda g, c: (g, 0, 0, 0))
    return pl.pallas_call(
        _delta_kernel,
        name="delta",
        grid=(nbatch // bblk, length // DN_C),
        in_specs=[seq] * 5 + [st],
        out_specs=[seq, st],
        out_shape=[jax.ShapeDtypeStruct((nbatch, length, DN_W), F32),
                   jax.ShapeDtypeStruct((nbatch, npair, LANES, LANES), F32)],
        scratch_shapes=[pltpu.VMEM((bblk, npair, LANES, LANES), F32)],
        compiler_params=_cparams(("arbitrary", "arbitrary")),
    )(q, k, v, bx, gx, s0_bd)


def _state_to_bd(s):
    b = s.shape[0]
    s = s.reshape(b, N_HEADS // 2, 2, HALF, HALF)
    zero = jnp.zeros_like(s[:, :, 0])
    top = jnp.concatenate([s[:, :, 0], zero], axis=-1)
    bot = jnp.concatenate([zero, s[:, :, 1]], axis=-1)
    return jnp.concatenate([top, bot], axis=-2)


def _state_from_bd(s):
    b = s.shape[0]
    return jnp.stack([s[:, :, :HALF, :HALF], s[:, :, HALF:, HALF:]], axis=2).reshape(b, N_HEADS, HALF, HALF)


def _out_kernel(xp_ref, xs_ref, ap_ref, as_ref, dp_ref, ds_ref, z_ref, ng_ref, bavg_ref, wa_ref, wd_ref, g_ref, b_ref,
                o_ref):
    o = _pick(dp_ref, ds_ref)
    oh, ol = _split2(o * o)
    ms = _dot(oh, bavg_ref[...]) + _dot(ol, bavg_ref[...])
    z = z_ref[...]
    dn = o * lax.rsqrt(ms + NORM_EPS) * ng_ref[...] * (z * _sigmoid(z))
    mix = _dot(_pick(ap_ref, as_ref).astype(BF16), wa_ref[...]) + _dot(dn.astype(BF16), wd_ref[...])
    o_ref[...] = _layer_norm(ALPHA * _pick(xp_ref, xs_ref) + mix, g_ref[...], b_ref[...])


def _out_proj(x_p, x_s, att_p, att_s, dn_p, dn_s, z, norm_g, w_out, g, b):
    wa = w_out[:ATTN_W].astype(BF16)
    wd = w_out[ATTN_W:].astype(BF16)
    normg = jnp.tile(norm_g.reshape(1, HEAD_DIM), (1, N_HEADS))
    bavg = jnp.asarray(np.kron(np.eye(N_HEADS), np.ones((HALF, HALF))) / HALF, BF16)
    row = lambda w: pl.BlockSpec((TM, w), lambda i: (i, 0))
    return pl.pallas_call(
        _out_kernel,
        name="out_proj",
        grid=(N_TILES,),
        in_specs=_row2(D_MODEL) + _row2(ATTN_W) + _row2(DN_W) + [
            row(DN_W), _full(normg.shape), _full(bavg.shape),
            _full(wa.shape), _full(wd.shape), _full((1, D_MODEL)), _full((1, D_MODEL))],
        out_specs=row(D_MODEL),
        out_shape=jax.ShapeDtypeStruct((N_PAD, D_MODEL), F32),
        compiler_params=_cparams(("arbitrary",)),
    )(x_p, x_s, att_p, att_s, dn_p, dn_s, z, normg, bavg, wa, wd, g.reshape(1, -1), b.reshape(1, -1))


N_GROUPS = 8
GROUP = N_EXPERTS // N_GROUPS
SLOT_RADIX = 64
_ET_SPEC = pl.BlockSpec((1, 2 * N_EXPERTS, TM), lambda i, *_: (i, 0, 0))
_SEG_SPEC = pl.BlockSpec((1, 1, 2 * N_EXPERTS), lambda i, *_: (i, 0, 0))
TOPK_GROUPS = 4
NEG = -jnp.inf


def _route_kernel(x_ref, wh_ref, wl_ref, bias_ref, upper_ref, lows_ref, st_ref, ge_ref, cnt_ref):
    t = pl.program_id(0)
    xh, xl = _split2(x_ref[...])
    wh = wh_ref[...]
    logits = _dot_nt(wh, xh) + _dot_nt(wh, xl) + _dot_nt(wl_ref[...], xh)
    scores = _sigmoid(logits)
    choice = scores + bias_ref[...]
    i8 = lax.broadcasted_iota(I32, (GROUP, TM), 0)

    def first_max(vals):
        m = jnp.max(vals, axis=0, keepdims=True)
        idx = jnp.min(jnp.where(vals == m, i8, GROUP), axis=0, keepdims=True)
        return m, idx

    pieces = [choice[g * GROUP:(g + 1) * GROUP, :] for g in range(N_GROUPS)]
    gsc = jnp.zeros((N_GROUPS, TM), F32)
    for g in range(N_GROUPS):
        m1, idx1 = first_max(pieces[g])
        m2 = jnp.max(jnp.where(i8 == idx1, NEG, pieces[g]), axis=0, keepdims=True)
        gsc = jnp.where(i8 == g, m1 + m2, gsc)
    gsel = jnp.zeros((N_GROUPS, TM), F32)
    for _ in range(TOPK_GROUPS):
        _, idx = first_max(gsc)
        hit = i8 == idx
        gsel = jnp.where(hit, 1.0, gsel)
        gsc = jnp.where(hit, NEG, gsc)
    masked = [jnp.where(gsel[g:g + 1, :] > 0.5, pieces[g], NEG) for g in range(N_GROUPS)]
    eidx = [i8 + g * GROUP for g in range(N_GROUPS)]
    member = [jnp.zeros((GROUP, TM), F32) for _ in range(N_GROUPS)]
    for _ in range(TOP_K):
        mm = masked[0]
        for g in range(1, N_GROUPS):
            mm = jnp.maximum(mm, masked[g])
        m = jnp.max(mm, axis=0, keepdims=True)
        cand = jnp.where(masked[0] == m, eidx[0], N_EXPERTS)
        for g in range(1, N_GROUPS):
            cand = jnp.minimum(cand, jnp.where(masked[g] == m, eidx[g], N_EXPERTS))
        idx = jnp.min(cand, axis=0, keepdims=True)
        for g in range(N_GROUPS):
            hit = eidx[g] == idx
            member[g] = jnp.where(hit, 1.0, member[g])
            masked[g] = jnp.where(hit, NEG, masked[g])

    tok = t * TM + lax.broadcasted_iota(I32, (1, TM), 1)
    real = jnp.where(tok < N_REAL, 1.0, 0.0)
    mem = jnp.concatenate(member, axis=0) * real
    sel = mem * scores
    gate = (sel / jnp.sum(sel + (1.0 - real), axis=0, keepdims=True) * ROUTED_SCALE).astype(BF16)
    ge_ref[0] = jnp.concatenate([gate, jnp.zeros_like(gate)], axis=0)
    memb = mem.astype(BF16)
    rank = _dot(memb, upper_ref[...])
    cnt = jnp.sum(mem, axis=1, keepdims=True)
    cpad = jnp.floor((cnt + (SEG_ALIGN - 1)) * (1.0 / SEG_ALIGN))
    loc = _dot(lows_ref[...], jnp.broadcast_to(cpad, (N_EXPERTS, TM)).astype(BF16)) * SEG_ALIGN
    slot_e = jnp.where(mem > 0.5, loc + rank, float(M_T))
    slot_hi = jnp.floor(slot_e * (1.0 / SLOT_RADIX)) * SLOT_RADIX
    st_ref[0] = jnp.concatenate([slot_hi, slot_e - slot_hi + 1.0], axis=0).astype(BF16)
    ones = jnp.ones((GROUP, TM), BF16)
    cnt_ref[0] = _dot_nt(ones, memb).astype(I32)


def _route(x1, w_router, router_bias):
    wt = w_router.T
    wh, wl = _split2(wt)
    ii = np.arange(TM)
    upper = jnp.asarray(ii[:, None] < ii[None, :], BF16)
    ee = np.arange(N_EXPERTS)
    lows = jnp.asarray(ee[:, None] > ee[None, :], BF16)
    return pl.pallas_call(
        _route_kernel,
        name="route",
        grid=(N_TILES,),
        in_specs=[pl.BlockSpec((TM, D_MODEL), lambda i: (i, 0)), _full(wh.shape), _full(wl.shape),
                  _full((N_EXPERTS, 1)), _full(upper.shape), _full(lows.shape)],
        out_specs=[_ET_SPEC, _ET_SPEC, pl.BlockSpec((1, GROUP, N_EXPERTS), lambda i: (i, 0, 0))],
        out_shape=[jax.ShapeDtypeStruct((N_TILES, 2 * N_EXPERTS, TM), BF16)] * 2 + [
            jax.ShapeDtypeStruct((N_TILES, GROUP, N_EXPERTS), I32)],
        compiler_params=_cparams(("arbitrary",)),
    )(x1, wh, wl, router_bias.reshape(N_EXPERTS, 1), upper, lows)


def _segment_tables(cnt):
    seg = (cnt + SEG_ALIGN - 1) // SEG_ALIGN
    cpad = seg * SEG_ALIGN
    loc_off = jnp.cumsum(cpad, axis=1) - cpad
    rows_e = jnp.sum(cpad, axis=0)
    blocks_e = (rows_e + BM - 1) // BM
    bend = jnp.cumsum(blocks_e)
    bstart = bend - blocks_e
    glob_off = (bstart * BM)[None, :] + jnp.cumsum(cpad, axis=0) - cpad
    nb = bend[-1]
    pad_off = bstart * BM + rows_e
    pad_seg = (blocks_e * BM - rows_e) // SEG_ALIGN
    i32 = lambda a: a.reshape(-1).astype(I32)
    bounds = lambda a: jnp.tile(a.astype(F32), (1, 2)).reshape(N_TILES, 1, 2 * N_EXPERTS)
    return (i32(seg), i32(loc_off), i32(glob_off), i32(pad_seg), i32(pad_off), i32(bstart), i32(blocks_e), i32(nb),
            bounds(loc_off), bounds(loc_off + cpad))


def _rows_copy(src_of, dst_of, lo, go, rows, sem):
    lo, go, rows = (v if isinstance(v, int) else pl.multiple_of(v, SEG_ALIGN) for v in (lo, go, rows))
    return pltpu.make_async_copy(src_of(lo, go, rows), dst_of(lo, go, rows), sem)


def _segment_copies(seg_ref, loc_ref, glob_ref, t, src_of, dst_of, sem):
    def body(e, carry):
        k = seg_ref[t * N_EXPERTS + e]

        @pl.when(k > 0)
        def _():
            _rows_copy(src_of, dst_of, loc_ref[t * N_EXPERTS + e], glob_ref[t * N_EXPERTS + e], k * SEG_ALIGN, sem).start()

        return carry

    lax.fori_loop(0, N_EXPERTS, body, 0)


def _wait_rows(src_of, dst_of, rows, sem):
    @pl.when(rows > 0)
    def _():
        _rows_copy(src_of, dst_of, 0, 0, rows, sem).wait()


def _local_rows(m0, rows, lo, hi, slot_tab):
    mcol = (lax.broadcasted_iota(I32, (rows, 2 * N_EXPERTS), 0) + m0).astype(F32)
    e01 = jnp.where(mcol >= lo, jnp.where(mcol < hi, 1.0, 0.0), 0.0).astype(BF16)
    mrow = (lax.broadcasted_iota(I32, (rows, TM), 0) + (m0 + 1)).astype(F32)
    return _dot(e01, slot_tab) == mrow, e01


def _row_spans(used):
    yield None, 0, M_MAIN
    for m0 in range(M_MAIN, M_T, M_CHUNK):
        yield m0 < used, m0, M_CHUNK


def _tile_rows(seg_ref, loc_ref, t):
    last = t * N_EXPERTS + N_EXPERTS - 1
    return loc_ref[last] + seg_ref[last] * SEG_ALIGN


def _dispatch_kernel(seg_ref, loc_ref, glob_ref, pseg_ref, poff_ref, nb_ref, x_ref, st_ref, lo_ref, hi_ref,
                     xs_ref, buf_ref, sem):
    t = pl.program_id(0)
    s = t % 2
    used = _tile_rows(seg_ref, loc_ref, t)
    for live, m0, rows in _row_spans(used):
        def sort_rows(m0=m0, rows=rows):
            hit, _ = _local_rows(m0, rows, lo_ref[0], hi_ref[0], st_ref[0])
            onehot = jnp.where(hit, 1.0, 0.0).astype(BF16)
            buf_ref[s, m0:m0 + rows, :] = _dot(onehot, x_ref[...].astype(BF16)).astype(BF16)

        if live is None:
            sort_rows()
        else:
            pl.when(live)(sort_rows)

    src = lambda lo, go, rows: buf_ref.at[s, pl.ds(lo, rows), :]
    dst = lambda lo, go, rows: xs_ref.at[pl.ds(go, rows), :]
    _segment_copies(seg_ref, loc_ref, glob_ref, t, src, dst, sem.at[s])

    @pl.when(t > 0)
    def _():
        prev = lambda lo, go, rows: buf_ref.at[1 - s, pl.ds(lo, rows), :]
        _wait_rows(prev, dst, _tile_rows(seg_ref, loc_ref, t - 1), sem.at[1 - s])

    @pl.when(t == N_TILES - 1)
    def _():
        _wait_rows(src, dst, used, sem.at[s])
        buf_ref[s, 0:BM, :] = jnp.zeros((BM, D_MODEL), BF16)
        zeros = lambda lo, go, rows: buf_ref.at[s, pl.ds(0, rows), :]

        def fill(wait):
            def pad(e, carry):
                k = pseg_ref[e]

                @pl.when(k > 0)
                def _():
                    cp = _rows_copy(zeros, dst, 0, poff_ref[e], k * SEG_ALIGN, sem.at[s])
                    cp.wait() if wait else cp.start()

                return carry

            lax.fori_loop(0, N_EXPERTS, pad, 0)

            def tail(b, carry):
                cp = _rows_copy(zeros, dst, 0, b * BM, BM, sem.at[s])
                cp.wait() if wait else cp.start()
                return carry

            lax.fori_loop(nb_ref[0], NB_MAX, tail, 0)

        fill(False)
        fill(True)


def _dispatch(x1, slot_tab, seg_lo, seg_hi, tables):
    grid_spec = pltpu.PrefetchScalarGridSpec(
        num_scalar_prefetch=len(tables),
        grid=(N_TILES,),
        in_specs=[pl.BlockSpec((TM, D_MODEL), lambda i, *_: (i, 0)), _ET_SPEC, _SEG_SPEC, _SEG_SPEC],
        out_specs=pl.BlockSpec(memory_space=pl.ANY),
        scratch_shapes=[pltpu.VMEM((2, M_T, D_MODEL), BF16), pltpu.SemaphoreType.DMA((2,))],
    )
    return pl.pallas_call(
        _dispatch_kernel,
        name="dispatch",
        grid_spec=grid_spec,
        out_shape=jax.ShapeDtypeStruct((NB_MAX * BM, D_MODEL), BF16),
        compiler_params=_cparams(("arbitrary",)),
    )(*tables, x1, slot_tab, seg_lo, seg_hi)


def _expert_kernel(bstart_ref, bcount_ref, nb_ref, xs_ref, wg_ref, wu_ref, wd_ref, ys_ref,
                   xbuf, ybuf, wgu_s, wd_s, sem_in, sem_out):
    e = pl.program_id(0)
    nb = nb_ref[0]
    b0 = bstart_ref[e]
    n = bcount_ref[e]

    def rows_of(ref, g):
        return ref.at[pl.ds(pl.multiple_of(g * BM, BM), BM), :]

    def x_copy(g):
        return pltpu.make_async_copy(rows_of(xs_ref, g), xbuf.at[g % 2], sem_in.at[g % 2])

    def y_copy(g):
        return pltpu.make_async_copy(ybuf.at[g % 2], rows_of(ys_ref, g), sem_out.at[g % 2])

    @pl.when(jnp.logical_and(e == 0, nb > 0))
    def _():
        x_copy(0).start()

    wgu_s[:, :EXPERT_DIM] = wg_ref[0].astype(BF16)
    wgu_s[:, EXPERT_DIM:] = wu_ref[0].astype(BF16)
    wd_s[...] = wd_ref[0].astype(BF16)

    def block(i, carry):
        g = b0 + i
        slot = g % 2

        @pl.when(g + 1 < nb)
        def _():
            x_copy(g + 1).start()

        x_copy(g).wait()

        @pl.when(g >= 2)
        def _():
            y_copy(g - 2).wait()

        h = _dot(xbuf[slot], wgu_s[...])
        gate = h[:, :EXPERT_DIM]
        a = (gate * _sigmoid(gate) * h[:, EXPERT_DIM:]).astype(BF16)
        ybuf[slot] = _dot(a, wd_s[...]).astype(BF16)
        y_copy(g).start()
        return carry

    lax.fori_loop(0, n, block, 0)

    @pl.when(e == N_EXPERTS - 1)
    def _():
        @pl.when(nb >= 2)
        def _():
            y_copy(nb - 2).wait()

        @pl.when(nb >= 1)
        def _():
            y_copy(nb - 1).wait()

        ybuf[0] = jnp.zeros((BM, D_MODEL), BF16)

        def tail(wait):
            def body(g, carry):
                cp = pltpu.make_async_copy(ybuf.at[0], rows_of(ys_ref, g), sem_out.at[0])
                cp.wait() if wait else cp.start()
                return carry

            lax.fori_loop(nb, NB_MAX, body, 0)

        tail(False)
        tail(True)


def _experts(xs, we_gate, we_up, we_down, bstart, bcount, nb):
    weight = lambda shape: pl.BlockSpec((1,) + shape, lambda e, *_: (e, 0, 0))
    grid_spec = pltpu.PrefetchScalarGridSpec(
        num_scalar_prefetch=3,
        grid=(N_EXPERTS,),
        in_specs=[pl.BlockSpec(memory_space=pl.ANY), weight((D_MODEL, EXPERT_DIM)), weight((D_MODEL, EXPERT_DIM)),
                  weight((EXPERT_DIM, D_MODEL))],
        out_specs=pl.BlockSpec(memory_space=pl.ANY),
        scratch_shapes=[pltpu.VMEM((2, BM, D_MODEL), BF16), pltpu.VMEM((2, BM, D_MODEL), BF16),
                        pltpu.VMEM((D_MODEL, 2 * EXPERT_DIM), BF16), pltpu.VMEM((EXPERT_DIM, D_MODEL), BF16),
                        pltpu.SemaphoreType.DMA((2,)), pltpu.SemaphoreType.DMA((2,))],
    )
    return pl.pallas_call(
        _expert_kernel,
        name="expert",
        grid_spec=grid_spec,
        out_shape=jax.ShapeDtypeStruct((NB_MAX * BM, D_MODEL), BF16),
        compiler_params=_cparams(("arbitrary",)),
    )(bstart, bcount, nb, xs, we_gate, we_up, we_down)


def _combine_kernel(seg_ref, loc_ref, glob_ref, x_ref, st_ref, ge_ref, lo_ref, hi_ref, pp_ref, ps_ref, ys_ref,
                    wsgu_ref, wsd_ref, g2_ref, b2_ref, wpg_ref, wpp_ref, g3_ref, b3_ref, op_ref, os_ref,
                    buf_ref, acc_ref, sem):
    t = pl.program_id(0)
    s = t % 2
    used = _tile_rows(seg_ref, loc_ref, t)
    src = lambda lo, go, rows: ys_ref.at[pl.ds(go, rows), :]
    into = lambda slot: (lambda lo, go, rows: buf_ref.at[slot, pl.ds(lo, rows), :])

    @pl.when(t == 0)
    def _():
        buf_ref[...] = jnp.zeros((2, M_T, D_MODEL), BF16)
        _segment_copies(seg_ref, loc_ref, glob_ref, t, src, into(s), sem.at[s])

    @pl.when(t + 1 < N_TILES)
    def _():
        _segment_copies(seg_ref, loc_ref, glob_ref, t + 1, src, into(1 - s), sem.at[1 - s])

    _wait_rows(src, into(s), used, sem.at[s])

    for live, m0, rows in _row_spans(used):
        def gather_rows(m0=m0, rows=rows):
            hit, e01 = _local_rows(m0, rows, lo_ref[0], hi_ref[0], st_ref[0])
            pg = jnp.where(hit, _dot(e01, ge_ref[0]), 0.0).astype(BF16)
            return lax.dot_general(pg, buf_ref[s, m0:m0 + rows, :], (((0,), (0,)), ((), ())),
                                   preferred_element_type=F32)

        if live is None:
            acc_ref[...] = gather_rows()
        else:
            @pl.when(live)
            def _(gather_rows=gather_rows):
                acc_ref[...] += gather_rows()

    x = x_ref[...]
    routed = acc_ref[...]
    xb = x.astype(BF16)
    h = _dot(xb, wsgu_ref[...])
    g = h[:, :EXPERT_DIM]
    shared = _dot((g * _sigmoid(g) * h[:, EXPERT_DIM:]).astype(BF16), wsd_ref[...])
    x2 = _layer_norm(ALPHA * x + (routed + shared), g2_ref[...], b2_ref[...])
    ple = _sigmoid(_dot(x2.astype(BF16), wpg_ref[...])) * _dot(_pick(pp_ref, ps_ref).astype(BF16), wpp_ref[...])
    y = _layer_norm(ALPHA * x2 + ple, g3_ref[...], b3_ref[...])

    @pl.when(t < NT_P)
    def _():
        op_ref[...] = y

    @pl.when(t >= NT_P)
    def _():
        os_ref[...] = y


def _combine(x1, slot_tab, gate_e, seg_lo, seg_hi, p_p, p_s, ys, seg, loc_off, glob_off, ws_gate, ws_up, ws_down,
             ln2_g, ln2_b, w_ple_gate, w_ple_proj, ln3_g, ln3_b):
    wsgu = jnp.concatenate([ws_gate, ws_up], axis=1).astype(BF16)
    wsd = ws_down.astype(BF16)
    wpg = w_ple_gate.astype(BF16)
    wpp = w_ple_proj.astype(BF16)
    row = lambda w: pl.BlockSpec((TM, w), lambda i, *_: (i, 0))
    full = lambda shape: pl.BlockSpec(shape, lambda i, *_: (0,) * len(shape))
    vec = full((1, D_MODEL))
    grid_spec = pltpu.PrefetchScalarGridSpec(
        num_scalar_prefetch=3,
        grid=(N_TILES,),
        in_specs=[row(D_MODEL), _ET_SPEC, _ET_SPEC, _SEG_SPEC, _SEG_SPEC] + _row2(PLE_DIM) + [
            pl.BlockSpec(memory_space=pl.ANY),
            full(wsgu.shape), full(wsd.shape), vec, vec, full(wpg.shape), full(wpp.shape), vec, vec],
        out_specs=_row2(D_MODEL),
        scratch_shapes=[pltpu.VMEM((2, M_T, D_MODEL), BF16), pltpu.VMEM((TM, D_MODEL), F32),
                        pltpu.SemaphoreType.DMA((2,))],
    )
    r = lambda a: a.reshape(1, -1)
    return pl.pallas_call(
        _combine_kernel,
        name="combine",
        grid_spec=grid_spec,
        out_shape=[jax.ShapeDtypeStruct((N_P, D_MODEL), F32), jax.ShapeDtypeStruct((N_SP, D_MODEL), F32)],
        compiler_params=_cparams(("arbitrary",)),
    )(seg, loc_off, glob_off, x1, slot_tab, gate_e, seg_lo, seg_hi, p_p, p_s, ys, wsgu, wsd, r(ln2_g), r(ln2_b),
      wpg, wpp, r(ln3_g), r(ln3_b))


def _pad_rows(a, rows):
    return jnp.concatenate([a, jnp.zeros((rows - a.shape[0],) + a.shape[1:], a.dtype)], axis=0)


def kernel(x_prompt, x_sample, cache_k_win, cache_v_win, state_conv, state_delta, p_prompt, p_sample, w_in, conv_w, a_log, dt_bias, dn_norm_g, w_out, ln1_g, ln1_b, w_router, router_bias, we_gate, we_up, we_down, ws_gate, ws_up, ws_down, ln2_g, ln2_b, w_ple_gate, w_ple_proj, ln3_g, ln3_b):
    x_p = x_prompt.reshape(N_P, D_MODEL)
    x_s = _pad_rows(x_sample.reshape(N_S, D_MODEL), N_SP)
    p_p = p_prompt[0].reshape(N_P, PLE_DIM)
    p_s = _pad_rows(p_sample[0].reshape(N_S, PLE_DIM), N_SP)

    qkv, dn_in, z, b_logit, a_logit = _project(x_p, x_s, w_in[0])

    att_p = _attn_prompt(qkv)
    qkv_s = qkv[N_P:N_REAL].reshape(N_SAMPLE_B, DEC_SEQ, 3 * ATTN_W)
    qkv_s8 = jnp.pad(qkv_s, ((0, 0), (0, ROWS_S - DEC_SEQ), (0, 0)))
    ck = cache_k_win[0].reshape(N_SAMPLE_B, W_BUF, ATTN_W)
    cv = cache_v_win[0].reshape(N_SAMPLE_B, W_BUF, ATTN_W)
    att_s = _attn_sample(qkv_s8[:, :, :ATTN_W], qkv_s8[:, :, ATTN_W:2 * ATTN_W], qkv_s8[:, :, 2 * ATTN_W:], ck, cv)
    att_s = _pad_rows(att_s[:, :DEC_SEQ].reshape(N_S, ATTN_W), N_SP)

    tails = dn_in.reshape(N_TILES, TM, CONV_CH)[:NT_P - 1, TM - HALO:]
    halo_p = jnp.concatenate([jnp.zeros((1, HALO, CONV_CH), F32), tails], axis=0)
    seq_start = (jnp.arange(NT_P) % (SEQ // TM) == 0)[:, None, None]
    halo_p = jnp.where(seq_start, 0.0, halo_p)
    q_p, k_p, v_p, bx_p, gx_p = _dn_pre(dn_in, halo_p, b_logit, a_logit, jnp.ones((N_P, 1), F32),
                                        conv_w[0], a_log[0], dt_bias[0])
    shp = (N_PROMPT_B, SEQ, DN_W)
    s0_p = jnp.zeros((N_PROMPT_B, N_HEADS // 2, LANES, LANES), F32)
    o_p, s_p = _delta(q_p.reshape(shp), k_p.reshape(shp), v_p.reshape(shp), bx_p.reshape(shp), gx_p.reshape(shp),
                      s0_p, N_PROMPT_B)

    def seq_pad(tok, state=None):
        w = tok.shape[-1]
        tok = tok.reshape(N_SAMPLE_B, DEC_SEQ, w)
        head = jnp.zeros((N_SAMPLE_B, S_TOK0, w), F32)
        if state is not None:
            head = head.at[:, S_TOK0 - (CONV_W - 1):].set(state)
        tail = jnp.zeros((N_SAMPLE_B, SEQ_S - S_TOK0 - DEC_SEQ, w), F32)
        return jnp.concatenate([head, tok, tail], axis=1).reshape(N_SAMPLE_B * SEQ_S, w)

    dn_tok_s = dn_in[N_P:N_REAL]
    dn_s = seq_pad(dn_tok_s, state_conv[0])
    rows_s = N_SAMPLE_B * SEQ_S
    valid_s = seq_pad(jnp.ones((N_S, 1), F32))
    halo_s = jnp.zeros((rows_s // TM, HALO, CONV_CH), F32)
    q_s, k_s, v_s, bx_s, gx_s = _dn_pre(dn_s, halo_s, seq_pad(b_logit[N_P:N_REAL]), seq_pad(a_logit[N_P:N_REAL]),
                                        valid_s, conv_w[0], a_log[0], dt_bias[0])
    shs = (N_SAMPLE_B, SEQ_S, DN_W)
    o_s, s_s = _delta(q_s.reshape(shs), k_s.reshape(shs), v_s.reshape(shs), bx_s.reshape(shs), gx_s.reshape(shs),
                      _state_to_bd(state_delta[0]), 4)
    dn_o_s = _pad_rows(o_s[:, S_TOK0:S_TOK0 + DEC_SEQ].reshape(N_S, DN_W), N_SP)

    x1 = _out_proj(x_p, x_s, att_p, att_s, o_p.reshape(N_P, DN_W), dn_o_s, z, dn_norm_g[0], w_out[0], ln1_g[0], ln1_b[0])

    slot_tab, gate_e, cnt = _route(x1, w_router[0], router_bias[0])
    seg, loc_off, glob_off, pad_seg, pad_off, bstart, bcount, nb, seg_lo, seg_hi = _segment_tables(cnt[:, 0, :])
    xs = _dispatch(x1, slot_tab, seg_lo, seg_hi, (seg, loc_off, glob_off, pad_seg, pad_off, nb))
    ys = _experts(xs, we_gate[0], we_up[0], we_down[0], bstart, bcount, nb)
    y_p, y_s = _combine(x1, slot_tab, gate_e, seg_lo, seg_hi, p_p, p_s, ys, seg, loc_off, glob_off,
                        ws_gate[0], ws_up[0], ws_down[0], ln2_g[0], ln2_b[0], w_ple_gate[0], w_ple_proj[0],
                        ln3_g[0], ln3_b[0])

    y_prompt = y_p.reshape(N_PROMPT_B, SEQ, D_MODEL)
    y_sample = y_s[:N_S].reshape(N_SAMPLE_B, DEC_SEQ, D_MODEL)
    heads = (N_HEADS, HEAD_DIM)
    win = lambda c0: jnp.stack([qkv[(b + 1) * SEQ - W_BUF:(b + 1) * SEQ, c0:c0 + ATTN_W] for b in range(N_PROMPT_B)])
    k_pr = win(ATTN_W).reshape(N_PROMPT_B, W_BUF, *heads)
    v_pr = win(2 * ATTN_W).reshape(N_PROMPT_B, W_BUF, *heads)
    conv_p = jnp.stack([dn_in[(b + 1) * SEQ - (CONV_W - 1):(b + 1) * SEQ] for b in range(N_PROMPT_B)])
    k_new = qkv_s[:, :, ATTN_W:2 * ATTN_W].reshape(N_SAMPLE_B, DEC_SEQ, *heads)
    v_new = qkv_s[:, :, 2 * ATTN_W:].reshape(N_SAMPLE_B, DEC_SEQ, *heads)
    k_sm = jnp.concatenate([cache_k_win[0][:, DEC_SEQ:], k_new], axis=1)
    v_sm = jnp.concatenate([cache_v_win[0][:, DEC_SEQ:], v_new], axis=1)
    conv_s = dn_tok_s.reshape(N_SAMPLE_B, DEC_SEQ, CONV_CH)[:, DEC_SEQ - (CONV_W - 1):]
    return (y_prompt, y_sample, k_pr[None], v_pr[None], conv_p[None], _state_from_bd(s_p)[None],
            k_sm[None], v_sm[None], conv_s[None], _state_from_bd(s_s)[None])
```

```python
import functools

import jax
import jax.numpy as jnp
import numpy as np
from jax import lax
from jax.experimental import pallas as pl
from jax.experimental.pallas import tpu as pltpu

F32 = jnp.float32
BF16 = jnp.bfloat16
I32 = jnp.int32

D_MODEL = 1024
N_PROMPT_B, SEQ = 2, 8192
N_SAMPLE_B, DEC_SEQ = 32, 4
W_BUF = 2048
N_HEADS = 8
HEAD_DIM = 64
ATTN_W = 512
CONV_CH = 1536
CONV_W = 4
DN_W = 512
N_EXPERTS = 64
TOP_K = 8
EXPERT_DIM = 256
PLE_DIM = 256
BRANCHES = ((128, 1), (512, 4), (2048, 16))
N_BACK = 128
ROUTED_SCALE = 2.5
LN_EPS = 1e-5
NORM_EPS = 1e-6
ALPHA = 2.0 ** 0.25

LANES = 128
HALF = 64
VMEM_LIMIT = 56 * 1024 * 1024

N_P = N_PROMPT_B * SEQ
N_S = N_SAMPLE_B * DEC_SEQ
N_REAL = N_P + N_S
N_PAD = N_P + 512
TM = 256
N_TILES = N_PAD // TM
NT_P = N_P // TM
N_SP = N_PAD - N_P

SEG_ALIGN = 16
M_T = 3072
M_CHUNK = 256
M_MAIN = 2560
BM = 512
EXPERT_SLOTS = 4
R_MAX = N_PAD * TOP_K + N_TILES * N_EXPERTS * (SEG_ALIGN - 1) + N_EXPERTS * (BM - 1)
NB_MAX = -(-R_MAX // BM)

DN_C = 64
SEQ_S = 64
S_TOK0 = 8


def _dot(a, b):
    return jnp.dot(a, b, preferred_element_type=F32)


def _dot_nt(a, b):
    return lax.dot_general(a, b, (((1,), (1,)), ((), ())), preferred_element_type=F32)


def _split2(x):
    hi = x.astype(BF16)
    lo = (x - hi.astype(F32)).astype(BF16)
    return hi, lo


def _dot_l01(m01, x):
    hi, lo = _split2(x)
    return _dot(m01, hi) + _dot(m01, lo)


def _dot_r01(x, m01):
    hi, lo = _split2(x)
    return _dot(hi, m01) + _dot(lo, m01)


def _sigmoid(x):
    return 1.0 / (1.0 + jnp.exp(-x))


def _layer_norm(x, g, b):
    mu = jnp.mean(x, axis=-1, keepdims=True)
    xc = x - mu
    var = jnp.mean(xc * xc, axis=-1, keepdims=True)
    return xc * lax.rsqrt(var + LN_EPS) * g + b


def _cparams(sem=None):
    return pltpu.CompilerParams(dimension_semantics=sem, vmem_limit_bytes=VMEM_LIMIT)


def _full(shape):
    return pl.BlockSpec(shape, lambda *_: (0,) * len(shape))


def _row2(w):
    return [pl.BlockSpec((TM, w), lambda i, *_: (jnp.minimum(i, NT_P - 1), 0)),
            pl.BlockSpec((TM, w), lambda i, *_: (jnp.maximum(i - NT_P, 0), 0))]


def _pick(p_ref, s_ref):
    return jnp.where(pl.program_id(0) < NT_P, p_ref[...], s_ref[...])


def _proj_kernel(xp_ref, xs_ref, wqkv_ref, wdn_ref, wz_ref, wg_ref, qkv_ref, dn_ref, z_ref, b_ref, a_ref):
    x = _pick(xp_ref, xs_ref)
    xh, xl = _split2(x)
    qkv_ref[...] = _dot(xh, wqkv_ref[...])
    dn_ref[...] = _dot(xh, wdn_ref[...])
    z_ref[...] = _dot(xh, wz_ref[...])
    ng = 2 * N_HEADS
    o1 = _dot(xh, wg_ref[...])
    o2 = _dot(xl, wg_ref[...])
    gates = o1[:, :ng] + o1[:, ng:2 * ng] + o2[:, :ng]
    b_ref[...] = gates[:, :N_HEADS]
    a_ref[...] = gates[:, N_HEADS:]


def _project(x_p, x_s, w_in):
    wqkv = w_in[:, :3 * ATTN_W].astype(BF16)
    wdn = w_in[:, 3 * ATTN_W:3 * ATTN_W + CONV_CH].astype(BF16)
    c0 = 3 * ATTN_W + CONV_CH
    wz = w_in[:, c0:c0 + DN_W].astype(BF16)
    wgh, wgl = _split2(w_in[:, c0 + DN_W:])
    wg = jnp.concatenate([wgh, wgl, jnp.zeros((D_MODEL, LANES - 4 * N_HEADS), BF16)], axis=1)
    row = lambda w: pl.BlockSpec((TM, w), lambda i: (i, 0))
    outs = (3 * ATTN_W, CONV_CH, DN_W, N_HEADS, N_HEADS)
    return pl.pallas_call(
        _proj_kernel,
        name="proj",
        grid=(N_TILES,),
        in_specs=_row2(D_MODEL) + [_full(wqkv.shape), _full(wdn.shape), _full(wz.shape), _full(wg.shape)],
        out_specs=[row(w) for w in outs],
        out_shape=[jax.ShapeDtypeStruct((N_PAD, w), F32) for w in outs],
        compiler_params=_cparams(("arbitrary",)),
    )(x_p, x_s, wqkv, wdn, wz, wg)


QT = 2048
ATTN_UNROLL = 8


def _attn_bias():
    qi = np.arange(N_BACK)[:, None]
    ki = np.arange(2 * N_BACK)[None, :]
    out = np.zeros((2, len(BRANCHES), N_HEADS, N_BACK, 2 * N_BACK), np.float32)
    for var, off in enumerate((N_BACK, 0)):
        dist = qi - ki + off
        valid = (dist >= 0) & (dist <= N_BACK)
        for di, (_, dil) in enumerate(BRANCHES):
            for h in range(N_HEADS):
                slope = 2.0 ** (-8.0 * (h + 1) / N_HEADS) * dil
                out[var, di, h] = np.where(valid, -slope * dist, -1e30)
    return out


def _attn_prompt_kernel(q_ref, k_ref, v_ref, bias_ref, o_ref, acc_ref, m_ref, l_ref):
    qt = pl.program_id(2)
    lane = lax.broadcasted_iota(I32, (1, LANES), 1)
    h0 = lane < HALF
    for di, (_, dil) in enumerate(BRANCHES):
        per_r = QT // (N_BACK * dil)

        def body(i, carry, di=di, dil=dil, per_r=per_r):
            def ds(start, n):
                return pl.ds(start, n) if dil == 1 else pl.ds(start, n, stride=dil)

            blocks = []
            for u in range(ATTN_UNROLL):
                blk = i * ATTN_UNROLL + u
                r = blk // per_r
                jb = blk % per_r
                qloc = r + dil * N_BACK * jb
                first = jnp.logical_and(qt == 0, jb == 0)
                kstart = jnp.where(first, r, qt * QT + qloc - N_BACK * dil)
                blocks.append((qloc, kstart, jnp.where(first, 1, 0)))
            q8 = [q_ref[ds(qloc, N_BACK), :] * (HEAD_DIM ** -0.5) for qloc, _, _ in blocks]
            kb = [k_ref[ds(kstart, 2 * N_BACK), :].astype(BF16) for _, kstart, _ in blocks]
            vv = [v_ref[ds(kstart, 2 * N_BACK), :] for _, kstart, _ in blocks]
            heads = [(u, hh) for u in range(ATTN_UNROLL) for hh in range(2)]
            keep = lambda hh: h0 if hh == 0 else jnp.logical_not(h0)
            s = [_dot_nt(jnp.where(keep(hh), q8[u], 0.0).astype(BF16), kb[u]) + bias_ref[blocks[u][2], di, hh]
                 for u, hh in heads]
            m = [jnp.max(x, axis=-1, keepdims=True) for x in s]
            p = [jnp.exp(x - mm).astype(BF16) for x, mm in zip(s, m)]
            res = [_dot(pp, jnp.where(keep(hh), vv[u], 1.0).astype(BF16)) for pp, (u, hh) in zip(p, heads)]
            for u, (qloc, _, _) in enumerate(blocks):
                rows = ds(qloc, N_BACK)
                acc_ref[di, rows, :] = jnp.where(h0, res[2 * u], res[2 * u + 1])
                l_ref[di, rows, :] = jnp.where(h0, res[2 * u + 1], res[2 * u])
                m_ref[di, rows, :] = jnp.where(h0, m[2 * u], m[2 * u + 1])
            return carry

        lax.fori_loop(0, QT // (N_BACK * ATTN_UNROLL), body, 0)

    mall = m_ref[...]
    mtot = jnp.max(mall, axis=0)
    num = jnp.zeros((QT, LANES), F32)
    den = jnp.zeros((QT, LANES), F32)
    for di in range(len(BRANCHES)):
        w = jnp.exp(mall[di] - mtot)
        num = num + w * acc_ref[di]
        den = den + w * pltpu.roll(l_ref[di], HALF, axis=1)
    o_ref[...] = num / den


def _attn_prompt(qkv):
    bias = jnp.asarray(_attn_bias())
    npair = N_HEADS // 2
    nqt = SEQ // QT
    kv_rows = SEQ
    q_spec = pl.BlockSpec((QT, LANES), lambda b, hp, t: (b * nqt + t, hp))
    k_spec = pl.BlockSpec((kv_rows, LANES), lambda b, hp, t: (b, npair + hp))
    v_spec = pl.BlockSpec((kv_rows, LANES), lambda b, hp, t: (b, 2 * npair + hp))
    bias_spec = pl.BlockSpec((2, len(BRANCHES), 2, N_BACK, 2 * N_BACK), lambda b, hp, t: (0, 0, hp, 0, 0))
    return pl.pallas_call(
        _attn_prompt_kernel,
        name="attn_prompt",
        grid=(N_PROMPT_B, npair, nqt),
        in_specs=[q_spec, k_spec, v_spec, bias_spec],
        out_specs=pl.BlockSpec((QT, LANES), lambda b, hp, t: (b * nqt + t, hp)),
        out_shape=jax.ShapeDtypeStruct((N_P, ATTN_W), F32),
        scratch_shapes=[pltpu.VMEM((len(BRANCHES), QT, LANES), F32)] * 3,
        compiler_params=_cparams(("arbitrary", "arbitrary", "arbitrary")),
    )(qkv, qkv, qkv, bias)


ROWS_S = 8


def _attn_sample_kernel(q_ref, kn_ref, vn_ref, ck_ref, cv_ref, e_ref, et_ref, slope_ref, o_ref, kall_ref, vall_ref):
    npair = N_HEADS // 2
    for hp in range(npair):
        sl = slice(hp * LANES, (hp + 1) * LANES)
        kall_ref[hp, 0:W_BUF, :] = ck_ref[0, :, sl]
        vall_ref[hp, 0:W_BUF, :] = cv_ref[0, :, sl]
        kall_ref[hp, W_BUF:W_BUF + ROWS_S, :] = kn_ref[0, :, sl]
        vall_ref[hp, W_BUF:W_BUF + ROWS_S, :] = vn_ref[0, :, sl]

    def rows(ref, start, n, dil):
        idx = pl.ds(start, n) if dil == 1 else pl.ds(start, n, stride=dil)
        return jnp.concatenate([ref[hp, idx, :] for hp in range(npair)], axis=1)

    e01 = e_ref[...]
    et01 = et_ref[...]
    slope = slope_ref[...]
    jrev = (N_BACK - lax.broadcasted_iota(I32, (N_BACK, 1), 0)).astype(F32)
    inew = lax.broadcasted_iota(I32, (ROWS_S, 1), 0)
    scale = HEAD_DIM ** -0.5
    knew = rows(kall_ref, W_BUF, ROWS_S, 1)
    vnew = rows(vall_ref, W_BUF, ROWS_S, 1).astype(BF16).astype(F32)

    def scores(kr, qt):
        ph, plo = _split2(kr * qt)
        return (_dot(ph, e01) + _dot(plo, e01)) * scale

    outs = []
    for t in range(DEC_SEQ):
        qt = q_ref[0, t:t + 1, :]
        s_self = jnp.where(inew == t, scores(knew, qt), -1e30)
        s_br = []
        v_br = []
        for _, dil in BRANCHES:
            start = W_BUF + t - N_BACK * dil
            s_br.append(scores(rows(kall_ref, start, N_BACK, dil), qt) - (slope * dil) * jrev)
            v_br.append(rows(vall_ref, start, N_BACK, dil))
        m = jnp.max(s_self, axis=0, keepdims=True)
        for s in s_br:
            m = jnp.maximum(m, jnp.max(s, axis=0, keepdims=True))
        pb = (jnp.exp(s_self - m) * float(len(BRANCHES))).astype(BF16)
        den = jnp.sum(pb.astype(F32), axis=0, keepdims=True)
        num = jnp.sum(_dot(pb, et01) * vnew, axis=0, keepdims=True)
        for s, vr in zip(s_br, v_br):
            pb = jnp.exp(s - m).astype(BF16)
            den = den + jnp.sum(pb.astype(F32), axis=0, keepdims=True)
            num = num + jnp.sum(_dot(pb, et01) * vr.astype(BF16).astype(F32), axis=0, keepdims=True)
        denx = _dot_r01(jnp.broadcast_to(den, (ROWS_S, N_HEADS)), et01)[0:1]
        outs.append(num / denx)
    pad = jnp.zeros((ROWS_S - DEC_SEQ, ATTN_W), F32)
    o_ref[0] = jnp.concatenate(outs + [pad], axis=0)


def _head_expand():
    e = np.zeros((ATTN_W, N_HEADS), np.float32)
    for h in range(N_HEADS):
        e[h * HEAD_DIM:(h + 1) * HEAD_DIM, h] = 1.0
    return e


def _attn_sample(q_s, k_s, v_s, cache_k, cache_v):
    e = _head_expand()
    e01 = jnp.asarray(e, BF16)
    et01 = jnp.asarray(e.T, BF16)
    slopes = jnp.asarray([[2.0 ** (-8.0 * (h + 1) / N_HEADS) for h in range(N_HEADS)]], F32)
    row = pl.BlockSpec((1, ROWS_S, ATTN_W), lambda b: (b, 0, 0))
    cache = pl.BlockSpec((1, W_BUF, ATTN_W), lambda b: (b, 0, 0))
    return pl.pallas_call(
        _attn_sample_kernel,
        name="attn_sample",
        grid=(N_SAMPLE_B,),
        in_specs=[row, row, row, cache, cache, _full(e01.shape), _full(et01.shape), _full(slopes.shape)],
        out_specs=row,
        out_shape=jax.ShapeDtypeStruct((N_SAMPLE_B, ROWS_S, ATTN_W), F32),
        scratch_shapes=[pltpu.VMEM((N_HEADS // 2, W_BUF + ROWS_S, LANES), F32)] * 2,
        compiler_params=_cparams(("arbitrary",)),
    )(q_s, k_s, v_s, cache_k, cache_v, e01, et01, slopes)


HALO = 8


def _dn_pre_kernel(dn_ref, halo_ref, b_ref, a_ref, valid_ref, cw_ref, alog_ref, dtb_ref, e_ref, et_ref, ltri_ref,
                   q_ref, k_ref, v_ref, bx_ref, gx_ref, buf_ref):
    ts = dn_ref.shape[0]
    buf_ref[0:HALO, :] = halo_ref[0]
    buf_ref[HALO:HALO + ts, :] = dn_ref[...]
    y = jnp.zeros((ts, CONV_CH), F32)
    for i in range(CONV_W):
        y = y + buf_ref[pl.ds(HALO - (CONV_W - 1) + i, ts), :] * cw_ref[i:i + 1, :]
    y = y * _sigmoid(y)
    valid = valid_ref[...]
    e01 = e_ref[...]
    et01 = et_ref[...]

    def l2n(t):
        ss = _dot_r01(t * t, e01)
        inv = lax.rsqrt(ss + NORM_EPS)
        return t * _dot_r01(inv, et01)

    q_ref[...] = l2n(y[:, :DN_W]) * (HEAD_DIM ** -0.5)
    k_ref[...] = l2n(y[:, DN_W:2 * DN_W]) * valid
    v_ref[...] = y[:, 2 * DN_W:] * valid
    beta = _sigmoid(b_ref[...]) * valid
    sp_in = a_ref[...] + dtb_ref[...]
    softplus = jnp.maximum(sp_in, 0.0) + jnp.log1p(jnp.exp(-jnp.abs(sp_in)))
    g = -jnp.exp(alog_ref[...]) * softplus * valid
    bx_ref[...] = _dot_r01(beta, et01)
    gx_ref[...] = _dot_l01(ltri_ref[...], _dot_r01(g, et01))


def _dn_pre(dn, halo, b_logit, a_logit, valid, conv_w, a_log, dt_bias):
    nt = halo.shape[0]
    rows = nt * TM
    e = _head_expand()
    e01 = jnp.asarray(e, BF16)
    et01 = jnp.asarray(e.T, BF16)
    ii = np.arange(TM)
    ltri = jnp.asarray(((ii[:, None] >= ii[None, :]) & (ii[:, None] // DN_C == ii[None, :] // DN_C)), BF16)
    row = lambda w: pl.BlockSpec((TM, w), lambda i: (i, 0))
    return pl.pallas_call(
        _dn_pre_kernel,
        name="dn_pre",
        grid=(nt,),
        in_specs=[row(CONV_CH), pl.BlockSpec((1, HALO, CONV_CH), lambda i: (i, 0, 0)), row(N_HEADS), row(N_HEADS),
                  row(1), _full(conv_w.shape), _full((1, N_HEADS)), _full((1, N_HEADS)),
                  _full(e01.shape), _full(et01.shape), _full(ltri.shape)],
        out_specs=[row(DN_W)] * 5,
        out_shape=[jax.ShapeDtypeStruct((rows, DN_W), F32)] * 5,
        scratch_shapes=[pltpu.VMEM((HALO + TM, CONV_CH), F32)],
        compiler_params=_cparams(("arbitrary",)),
    )(dn, halo, b_logit, a_logit, valid, conv_w, a_log.reshape(1, N_HEADS), dt_bias.reshape(1, N_HEADS),
      e01, et01, ltri)


def _delta_streams(q, k, v, bx, gcx, s_bd):
    c = DN_C
    n = len(q)
    each = range(n)
    lane = lax.broadcasted_iota(I32, (1, LANES), 1)
    h0 = lane < HALF
    m0 = h0.astype(F32)
    m1 = 1.0 - m0
    ii = lax.broadcasted_iota(I32, (2 * c, 2 * c), 0)
    jj = lax.broadcasted_iota(I32, (2 * c, 2 * c), 1)
    same = jnp.where(ii < c, 0, 1) == jnp.where(jj < c, 0, 1)
    order = jnp.where(same, ii - jj, -1)
    tril = order >= 0
    strict = order > 0
    eye = jnp.where(ii == jj, 1.0, 0.0)

    def stack(x):
        return jnp.concatenate([x * m0, x * m1], axis=0)

    def fold(x2):
        return x2[:c] + x2[c:]

    def decay_of(g):
        gsw = pltpu.roll(g, HALF, axis=1)
        row_b = jnp.concatenate([jnp.where(h0, g, gsw), jnp.where(h0, gsw, g)], axis=0)
        return jnp.where(tril, jnp.exp(jnp.where(tril, row_b - row_b.T, 0.0)), 0.0)

    eg = [jnp.exp(gcx[i]) for i in each]
    kb = [k[i] * bx[i] for i in each]
    glast = [gcx[i][c - 1:c, :] for i in each]
    decay = [decay_of(gcx[i]) for i in each]
    k2 = [stack(k[i]).astype(BF16) for i in each]
    lhs = [jnp.concatenate([stack(kb[i]), stack(q[i])], axis=0).astype(BF16) for i in each]
    aq = [_dot_nt(lhs[i], k2[i]) for i in each]
    a_low = [jnp.where(strict, aq[i][:2 * c] * decay[i], 0.0) for i in each]
    aqk = [jnp.where(tril, aq[i][2 * c:] * decay[i], 0.0).astype(BF16) for i in each]
    tinv = [eye - a_low[i] for i in each]
    apow = [a_low[i].astype(BF16) for i in each]
    for _ in range(5):
        apow = [_dot(apow[i], apow[i]).astype(BF16) for i in each]
        tinv = [tinv[i] + _dot(tinv[i].astype(BF16), apow[i]) for i in each]
    rhs = [jnp.concatenate([stack(v[i] * bx[i]), stack(kb[i] * eg[i])], axis=1).astype(BF16) for i in each]
    uw = [_dot(tinv[i].astype(BF16), rhs[i]) for i in each]
    u = [fold(uw[i][:, :LANES]) for i in each]
    w = [fold(uw[i][:, LANES:]) for i in each]
    ws = [_dot(jnp.concatenate([w[i], q[i] * eg[i]], axis=0).astype(BF16), s_bd[i].astype(BF16)) for i in each]
    v_new = [u[i] - ws[i][:c] for i in each]
    intra = [_dot(aqk[i], stack(v_new[i]).astype(BF16)) for i in each]
    kd = [(k[i] * jnp.exp(glast[i] - gcx[i])).T.astype(BF16) for i in each]
    upd = [_dot(kd[i], v_new[i].astype(BF16)) for i in each]
    o = [ws[i][c:] + fold(intra[i]) for i in each]
    s_new = [s_bd[i] * jnp.exp(glast[i]) + jnp.where(same, upd[i], 0.0) for i in each]
    return o, s_new


def _delta_kernel(q_ref, k_ref, v_ref, bx_ref, gx_ref, s0_ref, o_ref, sout_ref, s_ref):
    ci = pl.program_id(1)
    nb = q_ref.shape[0]
    npair = N_HEADS // 2

    @pl.when(ci == 0)
    def _():
        s_ref[...] = s0_ref[...]

    streams = [(b, hp) for b in range(nb) for hp in range(npair)]
    lanes = lambda hp: slice(hp * LANES, (hp + 1) * LANES)
    take = lambda ref: [ref[b, :, lanes(hp)] for b, hp in streams]
    o, s_new = _delta_streams(take(q_ref), take(k_ref), take(v_ref), take(bx_ref), take(gx_ref),
                              [s_ref[b, hp] for b, hp in streams])
    for i, (b, hp) in enumerate(streams):
        o_ref[b, :, lanes(hp)] = o[i]
        s_ref[b, hp] = s_new[i]

    @pl.when(ci == pl.num_programs(1) - 1)
    def _():
        sout_ref[...] = s_ref[...]


def _delta(q, k, v, bx, gx, s0_bd, bblk):
    nbatch, length, _ = q.shape
    npair = N_HEADS // 2
    seq = pl.BlockSpec((bblk, DN_C, DN_W), lambda g, c: (g, c, 0))
    st = pl.BlockSpec((bblk, npair, LANES, LANES), lambda g, c: (g, 0, 0, 0))
    return pl.pallas_call(
        _delta_kernel,
        name="delta",
        grid=(nbatch // bblk, length // DN_C),
        in_specs=[seq] * 5 + [st],
        out_specs=[seq, st],
        out_shape=[jax.ShapeDtypeStruct((nbatch, length, DN_W), F32),
                   jax.ShapeDtypeStruct((nbatch, npair, LANES, LANES), F32)],
        scratch_shapes=[pltpu.VMEM((bblk, npair, LANES, LANES), F32)],
        compiler_params=_cparams(("arbitrary", "arbitrary")),
    )(q, k, v, bx, gx, s0_bd)


def _state_to_bd(s):
    b = s.shape[0]
    s = s.reshape(b, N_HEADS // 2, 2, HALF, HALF)
    zero = jnp.zeros_like(s[:, :, 0])
    top = jnp.concatenate([s[:, :, 0], zero], axis=-1)
    bot = jnp.concatenate([zero, s[:, :, 1]], axis=-1)
    return jnp.concatenate([top, bot], axis=-2)


def _state_from_bd(s):
    b = s.shape[0]
    return jnp.stack([s[:, :, :HALF, :HALF], s[:, :, HALF:, HALF:]], axis=2).reshape(b, N_HEADS, HALF, HALF)


def _out_kernel(xp_ref, xs_ref, ap_ref, as_ref, dp_ref, ds_ref, z_ref, ng_ref, bavg_ref, wa_ref, wd_ref, g_ref, b_ref,
                o_ref):
    o = _pick(dp_ref, ds_ref)
    oh, ol = _split2(o * o)
    ms = _dot(oh, bavg_ref[...]) + _dot(ol, bavg_ref[...])
    z = z_ref[...]
    dn = o * lax.rsqrt(ms + NORM_EPS) * ng_ref[...] * (z * _sigmoid(z))
    mix = _dot(_pick(ap_ref, as_ref).astype(BF16), wa_ref[...]) + _dot(dn.astype(BF16), wd_ref[...])
    o_ref[...] = _layer_norm(ALPHA * _pick(xp_ref, xs_ref) + mix, g_ref[...], b_ref[...])


def _out_proj(x_p, x_s, att_p, att_s, dn_p, dn_s, z, norm_g, w_out, g, b):
    wa = w_out[:ATTN_W].astype(BF16)
    wd = w_out[ATTN_W:].astype(BF16)
    normg = jnp.tile(norm_g.reshape(1, HEAD_DIM), (1, N_HEADS))
    bavg = jnp.asarray(np.kron(np.eye(N_HEADS), np.ones((HALF, HALF))) / HALF, BF16)
    row = lambda w: pl.BlockSpec((TM, w), lambda i: (i, 0))
    return pl.pallas_call(
        _out_kernel,
        name="out_proj",
        grid=(N_TILES,),
        in_specs=_row2(D_MODEL) + _row2(ATTN_W) + _row2(DN_W) + [
            row(DN_W), _full(normg.shape), _full(bavg.shape),
            _full(wa.shape), _full(wd.shape), _full((1, D_MODEL)), _full((1, D_MODEL))],
        out_specs=row(D_MODEL),
        out_shape=jax.ShapeDtypeStruct((N_PAD, D_MODEL), F32),
        compiler_params=_cparams(("arbitrary",)),
    )(x_p, x_s, att_p, att_s, dn_p, dn_s, z, normg, bavg, wa, wd, g.reshape(1, -1), b.reshape(1, -1))


N_GROUPS = 8
GROUP = N_EXPERTS // N_GROUPS
SLOT_RADIX = 64
_ET_SPEC = pl.BlockSpec((1, 2 * N_EXPERTS, TM), lambda i, *_: (i, 0, 0))
_SEG_SPEC = pl.BlockSpec((1, 1, 2 * N_EXPERTS), lambda i, *_: (i, 0, 0))
TOPK_GROUPS = 4
NEG = -jnp.inf


def _route_kernel(x_ref, wh_ref, wl_ref, bias_ref, upper_ref, lows_ref, st_ref, ge_ref, cnt_ref):
    t = pl.program_id(0)
    xh, xl = _split2(x_ref[...])
    wh = wh_ref[...]
    logits = _dot_nt(wh, xh) + _dot_nt(wh, xl) + _dot_nt(wl_ref[...], xh)
    scores = _sigmoid(logits)
    choice = scores + bias_ref[...]
    i8 = lax.broadcasted_iota(I32, (GROUP, TM), 0)

    def first_max(vals):
        m = jnp.max(vals, axis=0, keepdims=True)
        idx = jnp.min(jnp.where(vals == m, i8, GROUP), axis=0, keepdims=True)
        return m, idx

    pieces = [choice[g * GROUP:(g + 1) * GROUP, :] for g in range(N_GROUPS)]
    gsc = jnp.zeros((N_GROUPS, TM), F32)
    for g in range(N_GROUPS):
        m1, idx1 = first_max(pieces[g])
        m2 = jnp.max(jnp.where(i8 == idx1, NEG, pieces[g]), axis=0, keepdims=True)
        gsc = jnp.where(i8 == g, m1 + m2, gsc)
    gsel = jnp.zeros((N_GROUPS, TM), F32)
    for _ in range(TOPK_GROUPS):
        _, idx = first_max(gsc)
        hit = i8 == idx
        gsel = jnp.where(hit, 1.0, gsel)
        gsc = jnp.where(hit, NEG, gsc)
    masked = [jnp.where(gsel[g:g + 1, :] > 0.5, pieces[g], NEG) for g in range(N_GROUPS)]
    eidx = [i8 + g * GROUP for g in range(N_GROUPS)]
    member = [jnp.zeros((GROUP, TM), F32) for _ in range(N_GROUPS)]
    for _ in range(TOP_K):
        mm = masked[0]
        for g in range(1, N_GROUPS):
            mm = jnp.maximum(mm, masked[g])
        m = jnp.max(mm, axis=0, keepdims=True)
        cand = jnp.where(masked[0] == m, eidx[0], N_EXPERTS)
        for g in range(1, N_GROUPS):
            cand = jnp.minimum(cand, jnp.where(masked[g] == m, eidx[g], N_EXPERTS))
        idx = jnp.min(cand, axis=0, keepdims=True)
        for g in range(N_GROUPS):
            hit = eidx[g] == idx
            member[g] = jnp.where(hit, 1.0, member[g])
            masked[g] = jnp.where(hit, NEG, masked[g])

    tok = t * TM + lax.broadcasted_iota(I32, (1, TM), 1)
    real = jnp.where(tok < N_REAL, 1.0, 0.0)
    mem = jnp.concatenate(member, axis=0) * real
    sel = mem * scores
    gate = (sel / jnp.sum(sel + (1.0 - real), axis=0, keepdims=True) * ROUTED_SCALE).astype(BF16)
    ge_ref[0] = jnp.concatenate([gate, jnp.zeros_like(gate)], axis=0)
    memb = mem.astype(BF16)
    rank = _dot(memb, upper_ref[...])
    cnt = jnp.sum(mem, axis=1, keepdims=True)
    cpad = jnp.floor((cnt + (SEG_ALIGN - 1)) * (1.0 / SEG_ALIGN))
    loc = _dot(lows_ref[...], jnp.broadcast_to(cpad, (N_EXPERTS, TM)).astype(BF16)) * SEG_ALIGN
    slot_e = jnp.where(mem > 0.5, loc + rank, float(M_T))
    slot_hi = jnp.floor(slot_e * (1.0 / SLOT_RADIX)) * SLOT_RADIX
    st_ref[0] = jnp.concatenate([slot_hi, slot_e - slot_hi + 1.0], axis=0).astype(BF16)
    ones = jnp.ones((GROUP, TM), BF16)
    cnt_ref[0] = _dot_nt(ones, memb).astype(I32)


def _route(x1, w_router, router_bias):
    wt = w_router.T
    wh, wl = _split2(wt)
    ii = np.arange(TM)
    upper = jnp.asarray(ii[:, None] < ii[None, :], BF16)
    ee = np.arange(N_EXPERTS)
    lows = jnp.asarray(ee[:, None] > ee[None, :], BF16)
    return pl.pallas_call(
        _route_kernel,
        name="route",
        grid=(N_TILES,),
        in_specs=[pl.BlockSpec((TM, D_MODEL), lambda i: (i, 0)), _full(wh.shape), _full(wl.shape),
                  _full((N_EXPERTS, 1)), _full(upper.shape), _full(lows.shape)],
        out_specs=[_ET_SPEC, _ET_SPEC, pl.BlockSpec((1, GROUP, N_EXPERTS), lambda i: (i, 0, 0))],
        out_shape=[jax.ShapeDtypeStruct((N_TILES, 2 * N_EXPERTS, TM), BF16)] * 2 + [
            jax.ShapeDtypeStruct((N_TILES, GROUP, N_EXPERTS), I32)],
        compiler_params=_cparams(("arbitrary",)),
    )(x1, wh, wl, router_bias.reshape(N_EXPERTS, 1), upper, lows)


def _segment_tables(cnt):
    seg = (cnt + SEG_ALIGN - 1) // SEG_ALIGN
    cpad = seg * SEG_ALIGN
    loc_off = jnp.cumsum(cpad, axis=1) - cpad
    rows_e = jnp.sum(cpad, axis=0)
    blocks_e = (rows_e + BM - 1) // BM
    bend = jnp.cumsum(blocks_e)
    bstart = bend - blocks_e
    glob_off = (bstart * BM)[None, :] + jnp.cumsum(cpad, axis=0) - cpad
    nb = bend[-1]
    pad_off = bstart * BM + rows_e
    pad_seg = (blocks_e * BM - rows_e) // SEG_ALIGN
    i32 = lambda a: a.reshape(-1).astype(I32)
    bounds = lambda a: jnp.tile(a.astype(F32), (1, 2)).reshape(N_TILES, 1, 2 * N_EXPERTS)
    return (i32(seg), i32(loc_off), i32(glob_off), i32(pad_seg), i32(pad_off), i32(bstart), i32(blocks_e), i32(nb),
            bounds(loc_off), bounds(loc_off + cpad))


def _rows_copy(src_of, dst_of, lo, go, rows, sem):
    lo, go, rows = (v if isinstance(v, int) else pl.multiple_of(v, SEG_ALIGN) for v in (lo, go, rows))
    return pltpu.make_async_copy(src_of(lo, go, rows), dst_of(lo, go, rows), sem)


def _segment_copies(seg_ref, loc_ref, glob_ref, t, src_of, dst_of, sem):
    def body(e, carry):
        k = seg_ref[t * N_EXPERTS + e]

        @pl.when(k > 0)
        def _():
            _rows_copy(src_of, dst_of, loc_ref[t * N_EXPERTS + e], glob_ref[t * N_EXPERTS + e], k * SEG_ALIGN, sem).start()

        return carry

    lax.fori_loop(0, N_EXPERTS, body, 0)


def _wait_rows(src_of, dst_of, rows, sem):
    @pl.when(rows > 0)
    def _():
        _rows_copy(src_of, dst_of, 0, 0, rows, sem).wait()


def _local_rows(m0, rows, lo, hi, slot_tab):
    mcol = (lax.broadcasted_iota(I32, (rows, 2 * N_EXPERTS), 0) + m0).astype(F32)
    e01 = jnp.where(mcol >= lo, jnp.where(mcol < hi, 1.0, 0.0), 0.0).astype(BF16)
    mrow = (lax.broadcasted_iota(I32, (rows, TM), 0) + (m0 + 1)).astype(F32)
    return _dot(e01, slot_tab) == mrow, e01


def _row_spans(used):
    yield None, 0, M_MAIN
    for m0 in range(M_MAIN, M_T, M_CHUNK):
        yield m0 < used, m0, M_CHUNK


def _tile_rows(seg_ref, loc_ref, t):
    last = t * N_EXPERTS + N_EXPERTS - 1
    return loc_ref[last] + seg_ref[last] * SEG_ALIGN


def _dispatch_kernel(seg_ref, loc_ref, glob_ref, pseg_ref, poff_ref, nb_ref, x_ref, st_ref, lo_ref, hi_ref,
                     xs_ref, buf_ref, sem):
    t = pl.program_id(0)
    s = t % 2
    used = _tile_rows(seg_ref, loc_ref, t)
    for live, m0, rows in _row_spans(used):
        def sort_rows(m0=m0, rows=rows):
            hit, _ = _local_rows(m0, rows, lo_ref[0], hi_ref[0], st_ref[0])
            onehot = jnp.where(hit, 1.0, 0.0).astype(BF16)
            buf_ref[s, m0:m0 + rows, :] = _dot(onehot, x_ref[...].astype(BF16)).astype(BF16)

        if live is None:
            sort_rows()
        else:
            pl.when(live)(sort_rows)

    src = lambda lo, go, rows: buf_ref.at[s, pl.ds(lo, rows), :]
    dst = lambda lo, go, rows: xs_ref.at[pl.ds(go, rows), :]
    _segment_copies(seg_ref, loc_ref, glob_ref, t, src, dst, sem.at[s])

    @pl.when(t > 0)
    def _():
        prev = lambda lo, go, rows: buf_ref.at[1 - s, pl.ds(lo, rows), :]
        _wait_rows(prev, dst, _tile_rows(seg_ref, loc_ref, t - 1), sem.at[1 - s])

    @pl.when(t == N_TILES - 1)
    def _():
        _wait_rows(src, dst, used, sem.at[s])
        buf_ref[s, 0:BM, :] = jnp.zeros((BM, D_MODEL), BF16)
        zeros = lambda lo, go, rows: buf_ref.at[s, pl.ds(0, rows), :]

        def fill(wait):
            def pad(e, carry):
                k = pseg_ref[e]

                @pl.when(k > 0)
                def _():
                    cp = _rows_copy(zeros, dst, 0, poff_ref[e], k * SEG_ALIGN, sem.at[s])
                    cp.wait() if wait else cp.start()

                return carry

            lax.fori_loop(0, N_EXPERTS, pad, 0)

            def tail(b, carry):
                cp = _rows_copy(zeros, dst, 0, b * BM, BM, sem.at[s])
                cp.wait() if wait else cp.start()
                return carry

            lax.fori_loop(nb_ref[0], NB_MAX, tail, 0)

        fill(False)
        fill(True)


def _dispatch(x1, slot_tab, seg_lo, seg_hi, tables):
    grid_spec = pltpu.PrefetchScalarGridSpec(
        num_scalar_prefetch=len(tables),
        grid=(N_TILES,),
        in_specs=[pl.BlockSpec((TM, D_MODEL), lambda i, *_: (i, 0)), _ET_SPEC, _SEG_SPEC, _SEG_SPEC],
        out_specs=pl.BlockSpec(memory_space=pl.ANY),
        scratch_shapes=[pltpu.VMEM((2, M_T, D_MODEL), BF16), pltpu.SemaphoreType.DMA((2,))],
    )
    return pl.pallas_call(
        _dispatch_kernel,
        name="dispatch",
        grid_spec=grid_spec,
        out_shape=jax.ShapeDtypeStruct((NB_MAX * BM, D_MODEL), BF16),
        compiler_params=_cparams(("arbitrary",)),
    )(*tables, x1, slot_tab, seg_lo, seg_hi)


def _expert_kernel(bstart_ref, bcount_ref, nb_ref, xs_ref, wg_ref, wu_ref, wd_ref, ys_ref,
                   xbuf, ybuf, wgu_s, wd_s, sem_in, sem_out):
    e = pl.program_id(0)
    nb = nb_ref[0]
    b0 = bstart_ref[e]
    n = bcount_ref[e]

    def rows_of(ref, g):
        return ref.at[pl.ds(pl.multiple_of(g * BM, BM), BM), :]

    def x_copy(g):
        return pltpu.make_async_copy(rows_of(xs_ref, g), xbuf.at[g % EXPERT_SLOTS], sem_in.at[g % EXPERT_SLOTS])

    def y_copy(g):
        return pltpu.make_async_copy(ybuf.at[g % EXPERT_SLOTS], rows_of(ys_ref, g), sem_out.at[g % EXPERT_SLOTS])

    @pl.when(e == 0)
    def _():
        for g in range(EXPERT_SLOTS - 1):
            @pl.when(g < nb)
            def _(g=g):
                x_copy(g).start()

    wgu_s[:, :EXPERT_DIM] = wg_ref[0].astype(BF16)
    wgu_s[:, EXPERT_DIM:] = wu_ref[0].astype(BF16)
    wd_s[...] = wd_ref[0].astype(BF16)

    def block(i, carry):
        g = b0 + i
        slot = g % EXPERT_SLOTS

        @pl.when(g + EXPERT_SLOTS - 1 < nb)
        def _():
            x_copy(g + EXPERT_SLOTS - 1).start()

        x_copy(g).wait()

        @pl.when(g >= EXPERT_SLOTS)
        def _():
            y_copy(g - EXPERT_SLOTS).wait()

        h = _dot(xbuf[slot], wgu_s[...])
        gate = h[:, :EXPERT_DIM]
        a = (gate * _sigmoid(gate) * h[:, EXPERT_DIM:]).astype(BF16)
        ybuf[slot] = _dot(a, wd_s[...]).astype(BF16)
        y_copy(g).start()
        return carry

    lax.fori_loop(0, n, block, 0)

    @pl.when(e == N_EXPERTS - 1)
    def _():
        for k in range(EXPERT_SLOTS, 0, -1):
            @pl.when(nb >= k)
            def _(k=k):
                y_copy(nb - k).wait()

        ybuf[0] = jnp.zeros((BM, D_MODEL), BF16)

        def tail(wait):
            def body(g, carry):
                cp = pltpu.make_async_copy(ybuf.at[0], rows_of(ys_ref, g), sem_out.at[0])
                cp.wait() if wait else cp.start()
                return carry

            lax.fori_loop(nb, NB_MAX, body, 0)

        tail(False)
        tail(True)


def _experts(xs, we_gate, we_up, we_down, bstart, bcount, nb):
    weight = lambda shape: pl.BlockSpec((1,) + shape, lambda e, *_: (e, 0, 0))
    grid_spec = pltpu.PrefetchScalarGridSpec(
        num_scalar_prefetch=3,
        grid=(N_EXPERTS,),
        in_specs=[pl.BlockSpec(memory_space=pl.ANY), weight((D_MODEL, EXPERT_DIM)), weight((D_MODEL, EXPERT_DIM)),
                  weight((EXPERT_DIM, D_MODEL))],
        out_specs=pl.BlockSpec(memory_space=pl.ANY),
        scratch_shapes=[pltpu.VMEM((EXPERT_SLOTS, BM, D_MODEL), BF16), pltpu.VMEM((EXPERT_SLOTS, BM, D_MODEL), BF16),
                        pltpu.VMEM((D_MODEL, 2 * EXPERT_DIM), BF16), pltpu.VMEM((EXPERT_DIM, D_MODEL), BF16),
                        pltpu.SemaphoreType.DMA((EXPERT_SLOTS,)), pltpu.SemaphoreType.DMA((EXPERT_SLOTS,))],
    )
    return pl.pallas_call(
        _expert_kernel,
        name="expert",
        grid_spec=grid_spec,
        out_shape=jax.ShapeDtypeStruct((NB_MAX * BM, D_MODEL), BF16),
        compiler_params=_cparams(("arbitrary",)),
    )(bstart, bcount, nb, xs, we_gate, we_up, we_down)


def _combine_kernel(seg_ref, loc_ref, glob_ref, x_ref, st_ref, ge_ref, lo_ref, hi_ref, pp_ref, ps_ref, ys_ref,
                    wsgu_ref, wsd_ref, g2_ref, b2_ref, wpg_ref, wpp_ref, g3_ref, b3_ref, op_ref, os_ref,
                    buf_ref, acc_ref, sem):
    t = pl.program_id(0)
    s = t % 2
    used = _tile_rows(seg_ref, loc_ref, t)
    src = lambda lo, go, rows: ys_ref.at[pl.ds(go, rows), :]
    into = lambda slot: (lambda lo, go, rows: buf_ref.at[slot, pl.ds(lo, rows), :])

    @pl.when(t == 0)
    def _():
        buf_ref[...] = jnp.zeros((2, M_T, D_MODEL), BF16)
        _segment_copies(seg_ref, loc_ref, glob_ref, t, src, into(s), sem.at[s])

    @pl.when(t + 1 < N_TILES)
    def _():
        _segment_copies(seg_ref, loc_ref, glob_ref, t + 1, src, into(1 - s), sem.at[1 - s])

    _wait_rows(src, into(s), used, sem.at[s])

    for live, m0, rows in _row_spans(used):
        def gather_rows(m0=m0, rows=rows):
            hit, e01 = _local_rows(m0, rows, lo_ref[0], hi_ref[0], st_ref[0])
            pg = jnp.where(hit, _dot(e01, ge_ref[0]), 0.0).astype(BF16)
            return lax.dot_general(pg, buf_ref[s, m0:m0 + rows, :], (((0,), (0,)), ((), ())),
                                   preferred_element_type=F32)

        if live is None:
            acc_ref[...] = gather_rows()
        else:
            @pl.when(live)
            def _(gather_rows=gather_rows):
                acc_ref[...] += gather_rows()

    x = x_ref[...]
    routed = acc_ref[...]
    xb = x.astype(BF16)
    h = _dot(xb, wsgu_ref[...])
    g = h[:, :EXPERT_DIM]
    shared = _dot((g * _sigmoid(g) * h[:, EXPERT_DIM:]).astype(BF16), wsd_ref[...])
    x2 = _layer_norm(ALPHA * x + (routed + shared), g2_ref[...], b2_ref[...])
    ple = _sigmoid(_dot(x2.astype(BF16), wpg_ref[...])) * _dot(_pick(pp_ref, ps_ref).astype(BF16), wpp_ref[...])
    y = _layer_norm(ALPHA * x2 + ple, g3_ref[...], b3_ref[...])

    @pl.when(t < NT_P)
    def _():
        op_ref[...] = y

    @pl.when(t >= NT_P)
    def _():
        os_ref[...] = y


def _combine(x1, slot_tab, gate_e, seg_lo, seg_hi, p_p, p_s, ys, seg, loc_off, glob_off, ws_gate, ws_up, ws_down,
             ln2_g, ln2_b, w_ple_gate, w_ple_proj, ln3_g, ln3_b):
    wsgu = jnp.concatenate([ws_gate, ws_up], axis=1).astype(BF16)
    wsd = ws_down.astype(BF16)
    wpg = w_ple_gate.astype(BF16)
    wpp = w_ple_proj.astype(BF16)
    row = lambda w: pl.BlockSpec((TM, w), lambda i, *_: (i, 0))
    full = lambda shape: pl.BlockSpec(shape, lambda i, *_: (0,) * len(shape))
    vec = full((1, D_MODEL))
    grid_spec = pltpu.PrefetchScalarGridSpec(
        num_scalar_prefetch=3,
        grid=(N_TILES,),
        in_specs=[row(D_MODEL), _ET_SPEC, _ET_SPEC, _SEG_SPEC, _SEG_SPEC] + _row2(PLE_DIM) + [
            pl.BlockSpec(memory_space=pl.ANY),
            full(wsgu.shape), full(wsd.shape), vec, vec, full(wpg.shape), full(wpp.shape), vec, vec],
        out_specs=_row2(D_MODEL),
        scratch_shapes=[pltpu.VMEM((2, M_T, D_MODEL), BF16), pltpu.VMEM((TM, D_MODEL), F32),
                        pltpu.SemaphoreType.DMA((2,))],
    )
    r = lambda a: a.reshape(1, -1)
    return pl.pallas_call(
        _combine_kernel,
        name="combine",
        grid_spec=grid_spec,
        out_shape=[jax.ShapeDtypeStruct((N_P, D_MODEL), F32), jax.ShapeDtypeStruct((N_SP, D_MODEL), F32)],
        compiler_params=_cparams(("arbitrary",)),
    )(seg, loc_off, glob_off, x1, slot_tab, gate_e, seg_lo, seg_hi, p_p, p_s, ys, wsgu, wsd, r(ln2_g), r(ln2_b),
      wpg, wpp, r(ln3_g), r(ln3_b))


def _pad_rows(a, rows):
    return jnp.concatenate([a, jnp.zeros((rows - a.shape[0],) + a.shape[1:], a.dtype)], axis=0)


def kernel(x_prompt, x_sample, cache_k_win, cache_v_win, state_conv, state_delta, p_prompt, p_sample, w_in, conv_w, a_log, dt_bias, dn_norm_g, w_out, ln1_g, ln1_b, w_router, router_bias, we_gate, we_up, we_down, ws_gate, ws_up, ws_down, ln2_g, ln2_b, w_ple_gate, w_ple_proj, ln3_g, ln3_b):
    x_p = x_prompt.reshape(N_P, D_MODEL)
    x_s = _pad_rows(x_sample.reshape(N_S, D_MODEL), N_SP)
    p_p = p_prompt[0].reshape(N_P, PLE_DIM)
    p_s = _pad_rows(p_sample[0].reshape(N_S, PLE_DIM), N_SP)

    qkv, dn_in, z, b_logit, a_logit = _project(x_p, x_s, w_in[0])

    att_p = _attn_prompt(qkv)
    qkv_s = qkv[N_P:N_REAL].reshape(N_SAMPLE_B, DEC_SEQ, 3 * ATTN_W)
    qkv_s8 = jnp.pad(qkv_s, ((0, 0), (0, ROWS_S - DEC_SEQ), (0, 0)))
    ck = cache_k_win[0].reshape(N_SAMPLE_B, W_BUF, ATTN_W)
    cv = cache_v_win[0].reshape(N_SAMPLE_B, W_BUF, ATTN_W)
    att_s = _attn_sample(qkv_s8[:, :, :ATTN_W], qkv_s8[:, :, ATTN_W:2 * ATTN_W], qkv_s8[:, :, 2 * ATTN_W:], ck, cv)
    att_s = _pad_rows(att_s[:, :DEC_SEQ].reshape(N_S, ATTN_W), N_SP)

    tails = dn_in.reshape(N_TILES, TM, CONV_CH)[:NT_P - 1, TM - HALO:]
    halo_p = jnp.concatenate([jnp.zeros((1, HALO, CONV_CH), F32), tails], axis=0)
    seq_start = (jnp.arange(NT_P) % (SEQ // TM) == 0)[:, None, None]
    halo_p = jnp.where(seq_start, 0.0, halo_p)
    q_p, k_p, v_p, bx_p, gx_p = _dn_pre(dn_in, halo_p, b_logit, a_logit, jnp.ones((N_P, 1), F32),
                                        conv_w[0], a_log[0], dt_bias[0])
    shp = (N_PROMPT_B, SEQ, DN_W)
    s0_p = jnp.zeros((N_PROMPT_B, N_HEADS // 2, LANES, LANES), F32)
    o_p, s_p = _delta(q_p.reshape(shp), k_p.reshape(shp), v_p.reshape(shp), bx_p.reshape(shp), gx_p.reshape(shp),
                      s0_p, N_PROMPT_B)

    def seq_pad(tok, state=None):
        w = tok.shape[-1]
        tok = tok.reshape(N_SAMPLE_B, DEC_SEQ, w)
        head = jnp.zeros((N_SAMPLE_B, S_TOK0, w), F32)
        if state is not None:
            head = head.at[:, S_TOK0 - (CONV_W - 1):].set(state)
        tail = jnp.zeros((N_SAMPLE_B, SEQ_S - S_TOK0 - DEC_SEQ, w), F32)
        return jnp.concatenate([head, tok, tail], axis=1).reshape(N_SAMPLE_B * SEQ_S, w)

    dn_tok_s = dn_in[N_P:N_REAL]
    dn_s = seq_pad(dn_tok_s, state_conv[0])
    rows_s = N_SAMPLE_B * SEQ_S
    valid_s = seq_pad(jnp.ones((N_S, 1), F32))
    halo_s = jnp.zeros((rows_s // TM, HALO, CONV_CH), F32)
    q_s, k_s, v_s, bx_s, gx_s = _dn_pre(dn_s, halo_s, seq_pad(b_logit[N_P:N_REAL]), seq_pad(a_logit[N_P:N_REAL]),
                                        valid_s, conv_w[0], a_log[0], dt_bias[0])
    shs = (N_SAMPLE_B, SEQ_S, DN_W)
    o_s, s_s = _delta(q_s.reshape(shs), k_s.reshape(shs), v_s.reshape(shs), bx_s.reshape(shs), gx_s.reshape(shs),
                      _state_to_bd(state_delta[0]), 4)
    dn_o_s = _pad_rows(o_s[:, S_TOK0:S_TOK0 + DEC_SEQ].reshape(N_S, DN_W), N_SP)

    x1 = _out_proj(x_p, x_s, att_p, att_s, o_p.reshape(N_P, DN_W), dn_o_s, z, dn_norm_g[0], w_out[0], ln1_g[0], ln1_b[0])

    slot_tab, gate_e, cnt = _route(x1, w_router[0], router_bias[0])
    seg, loc_off, glob_off, pad_seg, pad_off, bstart, bcount, nb, seg_lo, seg_hi = _segment_tables(cnt[:, 0, :])
    xs = _dispatch(x1, slot_tab, seg_lo, seg_hi, (seg, loc_off, glob_off, pad_seg, pad_off, nb))
    ys = _experts(xs, we_gate[0], we_up[0], we_down[0], bstart, bcount, nb)
    y_p, y_s = _combine(x1, slot_tab, gate_e, seg_lo, seg_hi, p_p, p_s, ys, seg, loc_off, glob_off,
                        ws_gate[0], ws_up[0], ws_down[0], ln2_g[0], ln2_b[0], w_ple_gate[0], w_ple_proj[0],
                        ln3_g[0], ln3_b[0])

    y_prompt = y_p.reshape(N_PROMPT_B, SEQ, D_MODEL)
    y_sample = y_s[:N_S].reshape(N_SAMPLE_B, DEC_SEQ, D_MODEL)
    heads = (N_HEADS, HEAD_DIM)
    win = lambda c0: jnp.stack([qkv[(b + 1) * SEQ - W_BUF:(b + 1) * SEQ, c0:c0 + ATTN_W] for b in range(N_PROMPT_B)])
    k_pr = win(ATTN_W).reshape(N_PROMPT_B, W_BUF, *heads)
    v_pr = win(2 * ATTN_W).reshape(N_PROMPT_B, W_BUF, *heads)
    conv_p = jnp.stack([dn_in[(b + 1) * SEQ - (CONV_W - 1):(b + 1) * SEQ] for b in range(N_PROMPT_B)])
    k_new = qkv_s[:, :, ATTN_W:2 * ATTN_W].reshape(N_SAMPLE_B, DEC_SEQ, *heads)
    v_new = qkv_s[:, :, 2 * ATTN_W:].reshape(N_SAMPLE_B, DEC_SEQ, *heads)
    k_sm = jnp.concatenate([cache_k_win[0][:, DEC_SEQ:], k_new], axis=1)
    v_sm = jnp.concatenate([cache_v_win[0][:, DEC_SEQ:], v_new], axis=1)
    conv_s = dn_tok_s.reshape(N_SAMPLE_B, DEC_SEQ, CONV_CH)[:, DEC_SEQ - (CONV_W - 1):]
    return (y_prompt, y_sample, k_pr[None], v_pr[None], conv_p[None], _state_from_bd(s_p)[None],
            k_sm[None], v_sm[None], conv_s[None], _state_from_bd(s_s)[None])
```

```python
import functools

import jax
import jax.numpy as jnp
import numpy as np
from jax import lax
from jax.experimental import pallas as pl
from jax.experimental.pallas import tpu as pltpu

F32 = jnp.float32
BF16 = jnp.bfloat16
I32 = jnp.int32

D_MODEL = 1024
N_PROMPT_B, SEQ = 2, 8192
N_SAMPLE_B, DEC_SEQ = 32, 4
W_BUF = 2048
N_HEADS = 8
HEAD_DIM = 64
ATTN_W = 512
CONV_CH = 1536
CONV_W = 4
DN_W = 512
N_EXPERTS = 64
TOP_K = 8
EXPERT_DIM = 256
PLE_DIM = 256
BRANCHES = ((128, 1), (512, 4), (2048, 16))
N_BACK = 128
ROUTED_SCALE = 2.5
LN_EPS = 1e-5
NORM_EPS = 1e-6
ALPHA = 2.0 ** 0.25

LANES = 128
HALF = 64
VMEM_LIMIT = 56 * 1024 * 1024

N_P = N_PROMPT_B * SEQ
N_S = N_SAMPLE_B * DEC_SEQ
N_REAL = N_P + N_S
N_PAD = N_P + 512
TM = 256
N_TILES = N_PAD // TM
NT_P = N_P // TM
N_SP = N_PAD - N_P

SEG_ALIGN = 16
M_T = 3072
M_CHUNK = 256
M_MAIN = 2560
BM = 512
EXPERT_SLOTS = 4
R_MAX = N_PAD * TOP_K + N_TILES * N_EXPERTS * (SEG_ALIGN - 1) + N_EXPERTS * (BM - 1)
NB_MAX = -(-R_MAX // BM)

DN_C = 64
SEQ_S = 64
S_TOK0 = 8


def _dot(a, b):
    return jnp.dot(a, b, preferred_element_type=F32)


def _dot_nt(a, b):
    return lax.dot_general(a, b, (((1,), (1,)), ((), ())), preferred_element_type=F32)


def _split2(x):
    hi = x.astype(BF16)
    lo = (x - hi.astype(F32)).astype(BF16)
    return hi, lo


def _dot_l01(m01, x):
    hi, lo = _split2(x)
    return _dot(m01, hi) + _dot(m01, lo)


def _dot_r01(x, m01):
    hi, lo = _split2(x)
    return _dot(hi, m01) + _dot(lo, m01)


def _sigmoid(x):
    return 1.0 / (1.0 + jnp.exp(-x))


def _layer_norm(x, g, b):
    mu = jnp.mean(x, axis=-1, keepdims=True)
    xc = x - mu
    var = jnp.mean(xc * xc, axis=-1, keepdims=True)
    return xc * lax.rsqrt(var + LN_EPS) * g + b


def _cparams(sem=None):
    return pltpu.CompilerParams(dimension_semantics=sem, vmem_limit_bytes=VMEM_LIMIT)


def _full(shape):
    return pl.BlockSpec(shape, lambda *_: (0,) * len(shape))


def _row2(w):
    return [pl.BlockSpec((TM, w), lambda i, *_: (jnp.minimum(i, NT_P - 1), 0)),
            pl.BlockSpec((TM, w), lambda i, *_: (jnp.maximum(i - NT_P, 0), 0))]


def _pick(p_ref, s_ref):
    return jnp.where(pl.program_id(0) < NT_P, p_ref[...], s_ref[...])


def _proj_kernel(xp_ref, xs_ref, wqkv_ref, wdn_ref, wz_ref, wg_ref, qkv_ref, dn_ref, z_ref, b_ref, a_ref):
    x = _pick(xp_ref, xs_ref)
    xh, xl = _split2(x)
    qkv_ref[...] = _dot(xh, wqkv_ref[...])
    dn_ref[...] = _dot(xh, wdn_ref[...])
    z_ref[...] = _dot(xh, wz_ref[...])
    ng = 2 * N_HEADS
    o1 = _dot(xh, wg_ref[...])
    o2 = _dot(xl, wg_ref[...])
    gates = o1[:, :ng] + o1[:, ng:2 * ng] + o2[:, :ng]
    b_ref[...] = gates[:, :N_HEADS]
    a_ref[...] = gates[:, N_HEADS:]


def _project(x_p, x_s, w_in):
    wqkv = w_in[:, :3 * ATTN_W].astype(BF16)
    wdn = w_in[:, 3 * ATTN_W:3 * ATTN_W + CONV_CH].astype(BF16)
    c0 = 3 * ATTN_W + CONV_CH
    wz = w_in[:, c0:c0 + DN_W].astype(BF16)
    wgh, wgl = _split2(w_in[:, c0 + DN_W:])
    wg = jnp.concatenate([wgh, wgl, jnp.zeros((D_MODEL, LANES - 4 * N_HEADS), BF16)], axis=1)
    row = lambda w: pl.BlockSpec((TM, w), lambda i: (i, 0))
    outs = (3 * ATTN_W, CONV_CH, DN_W, N_HEADS, N_HEADS)
    return pl.pallas_call(
        _proj_kernel,
        name="proj",
        grid=(N_TILES,),
        in_specs=_row2(D_MODEL) + [_full(wqkv.shape), _full(wdn.shape), _full(wz.shape), _full(wg.shape)],
        out_specs=[row(w) for w in outs],
        out_shape=[jax.ShapeDtypeStruct((N_PAD, w), F32) for w in outs],
        compiler_params=_cparams(("arbitrary",)),
    )(x_p, x_s, wqkv, wdn, wz, wg)


QT = 2048
ATTN_UNROLL = 8


def _attn_bias():
    qi = np.arange(N_BACK)[:, None]
    ki = np.arange(2 * N_BACK)[None, :]
    out = np.zeros((2, len(BRANCHES), N_HEADS, N_BACK, 2 * N_BACK), np.float32)
    for var, off in enumerate((N_BACK, 0)):
        dist = qi - ki + off
        valid = (dist >= 0) & (dist <= N_BACK)
        for di, (_, dil) in enumerate(BRANCHES):
            for h in range(N_HEADS):
                slope = 2.0 ** (-8.0 * (h + 1) / N_HEADS) * dil
                out[var, di, h] = np.where(valid, -slope * dist, -1e30)
    return out


def _attn_prompt_kernel(q_ref, k_ref, v_ref, bias_ref, o_ref, acc_ref, m_ref, l_ref):
    qt = pl.program_id(2)
    lane = lax.broadcasted_iota(I32, (1, LANES), 1)
    h0 = lane < HALF
    for di, (_, dil) in enumerate(BRANCHES):
        per_r = QT // (N_BACK * dil)

        def body(i, carry, di=di, dil=dil, per_r=per_r):
            def ds(start, n):
                return pl.ds(start, n) if dil == 1 else pl.ds(start, n, stride=dil)

            blocks = []
            for u in range(ATTN_UNROLL):
                blk = i * ATTN_UNROLL + u
                r = blk // per_r
                jb = blk % per_r
                qloc = r + dil * N_BACK * jb
                first = jnp.logical_and(qt == 0, jb == 0)
                kstart = jnp.where(first, r, qt * QT + qloc - N_BACK * dil)
                blocks.append((qloc, kstart, jnp.where(first, 1, 0)))
            q8 = [q_ref[ds(qloc, N_BACK), :] * (HEAD_DIM ** -0.5) for qloc, _, _ in blocks]
            kb = [k_ref[ds(kstart, 2 * N_BACK), :].astype(BF16) for _, kstart, _ in blocks]
            vv = [v_ref[ds(kstart, 2 * N_BACK), :] for _, kstart, _ in blocks]
            heads = [(u, hh) for u in range(ATTN_UNROLL) for hh in range(2)]
            keep = lambda hh: h0 if hh == 0 else jnp.logical_not(h0)
            s = [_dot_nt(jnp.where(keep(hh), q8[u], 0.0).astype(BF16), kb[u]) + bias_ref[blocks[u][2], di, hh]
                 for u, hh in heads]
            m = [jnp.max(x, axis=-1, keepdims=True) for x in s]
            p = [jnp.exp(x - mm).astype(BF16) for x, mm in zip(s, m)]
            res = [_dot(pp, jnp.where(keep(hh), vv[u], 1.0).astype(BF16)) for pp, (u, hh) in zip(p, heads)]
            for u, (qloc, _, _) in enumerate(blocks):
                rows = ds(qloc, N_BACK)
                acc_ref[di, rows, :] = jnp.where(h0, res[2 * u], res[2 * u + 1])
                l_ref[di, rows, :] = jnp.where(h0, res[2 * u + 1], res[2 * u])
                m_ref[di, rows, :] = jnp.where(h0, m[2 * u], m[2 * u + 1])
            return carry

        lax.fori_loop(0, QT // (N_BACK * ATTN_UNROLL), body, 0)

    mall = m_ref[...]
    mtot = jnp.max(mall, axis=0)
    num = jnp.zeros((QT, LANES), F32)
    den = jnp.zeros((QT, LANES), F32)
    for di in range(len(BRANCHES)):
        w = jnp.exp(mall[di] - mtot)
        num = num + w * acc_ref[di]
        den = den + w * pltpu.roll(l_ref[di], HALF, axis=1)
    o_ref[...] = num / den


def _attn_prompt(qkv):
    bias = jnp.asarray(_attn_bias())
    npair = N_HEADS // 2
    nqt = SEQ // QT
    kv_rows = SEQ
    q_spec = pl.BlockSpec((QT, LANES), lambda b, hp, t: (b * nqt + t, hp))
    k_spec = pl.BlockSpec((kv_rows, LANES), lambda b, hp, t: (b, npair + hp))
    v_spec = pl.BlockSpec((kv_rows, LANES), lambda b, hp, t: (b, 2 * npair + hp))
    bias_spec = pl.BlockSpec((2, len(BRANCHES), 2, N_BACK, 2 * N_BACK), lambda b, hp, t: (0, 0, hp, 0, 0))
    return pl.pallas_call(
        _attn_prompt_kernel,
        name="attn_prompt",
        grid=(N_PROMPT_B, npair, nqt),
        in_specs=[q_spec, k_spec, v_spec, bias_spec],
        out_specs=pl.BlockSpec((QT, LANES), lambda b, hp, t: (b * nqt + t, hp)),
        out_shape=jax.ShapeDtypeStruct((N_P, ATTN_W), F32),
        scratch_shapes=[pltpu.VMEM((len(BRANCHES), QT, LANES), F32)] * 3,
        compiler_params=_cparams(("arbitrary", "arbitrary", "arbitrary")),
    )(qkv, qkv, qkv, bias)


ROWS_S = 8
NEAR = BRANCHES[-2][0]
FAR_PERIOD = BRANCHES[-1][1]
FAR_KEEP = DEC_SEQ
FAR_ROWS = (W_BUF - NEAR) // FAR_PERIOD * FAR_KEEP


def _attn_sample_kernel(q_ref, kn_ref, vn_ref, nk_ref, nv_ref, fk_ref, fv_ref, e_ref, et_ref, slope_ref, o_ref,
                        knear, vnear, kfar, vfar):
    npair = N_HEADS // 2
    for hp in range(npair):
        sl = slice(hp * LANES, (hp + 1) * LANES)
        knear[hp, 0:NEAR, :] = nk_ref[0, :, sl]
        vnear[hp, 0:NEAR, :] = nv_ref[0, :, sl]
        knear[hp, NEAR:NEAR + ROWS_S, :] = kn_ref[0, :, sl]
        vnear[hp, NEAR:NEAR + ROWS_S, :] = vn_ref[0, :, sl]
        kfar[hp] = fk_ref[0, :, sl]
        vfar[hp] = fv_ref[0, :, sl]

    def take(ref, start, n, stride):
        idx = pl.ds(start, n) if stride == 1 else pl.ds(start, n, stride=stride)
        return jnp.concatenate([ref[hp, idx, :] for hp in range(npair)], axis=1)

    def rows(near, far, t, window, dil):
        first = W_BUF + t - window
        n_far = max(0, -(-(W_BUF - NEAR - first) // dil))
        parts = []
        if n_far:
            assert dil % FAR_PERIOD == 0 and first % FAR_PERIOD < FAR_KEEP
            far_row = first // FAR_PERIOD * FAR_KEEP + first % FAR_PERIOD
            parts.append(take(far, far_row, n_far, dil // FAR_PERIOD * FAR_KEEP))
        parts.append(take(near, first + n_far * dil - (W_BUF - NEAR), N_BACK - n_far, dil))
        return parts[0] if len(parts) == 1 else jnp.concatenate(parts, axis=0)

    e01 = e_ref[...]
    et01 = et_ref[...]
    slope = slope_ref[...]
    jrev = (N_BACK - lax.broadcasted_iota(I32, (N_BACK, 1), 0)).astype(F32)
    inew = lax.broadcasted_iota(I32, (ROWS_S, 1), 0)
    scale = HEAD_DIM ** -0.5
    knew = take(knear, NEAR, ROWS_S, 1)
    vnew = take(vnear, NEAR, ROWS_S, 1).astype(BF16).astype(F32)

    def scores(kr, qt):
        ph, plo = _split2(kr * qt)
        return (_dot(ph, e01) + _dot(plo, e01)) * scale

    outs = []
    for t in range(DEC_SEQ):
        qt = q_ref[0, t:t + 1, :]
        s_self = jnp.where(inew == t, scores(knew, qt), -1e30)
        s_br = []
        v_br = []
        for window, dil in BRANCHES:
            s_br.append(scores(rows(knear, kfar, t, window, dil), qt) - (slope * dil) * jrev)
            v_br.append(rows(vnear, vfar, t, window, dil))
        m = jnp.max(s_self, axis=0, keepdims=True)
        for s in s_br:
            m = jnp.maximum(m, jnp.max(s, axis=0, keepdims=True))
        pb = (jnp.exp(s_self - m) * float(len(BRANCHES))).astype(BF16)
        den = jnp.sum(pb.astype(F32), axis=0, keepdims=True)
        num = jnp.sum(_dot(pb, et01) * vnew, axis=0, keepdims=True)
        for s, vr in zip(s_br, v_br):
            pb = jnp.exp(s - m).astype(BF16)
            den = den + jnp.sum(pb.astype(F32), axis=0, keepdims=True)
            num = num + jnp.sum(_dot(pb, et01) * vr.astype(BF16).astype(F32), axis=0, keepdims=True)
        denx = _dot_r01(jnp.broadcast_to(den, (ROWS_S, N_HEADS)), et01)[0:1]
        outs.append(num / denx)
    pad = jnp.zeros((ROWS_S - DEC_SEQ, ATTN_W), F32)
    o_ref[0] = jnp.concatenate(outs + [pad], axis=0)


def _head_expand():
    e = np.zeros((ATTN_W, N_HEADS), np.float32)
    for h in range(N_HEADS):
        e[h * HEAD_DIM:(h + 1) * HEAD_DIM, h] = 1.0
    return e


def _window_rows(cache):
    nb = cache.shape[0]
    near = cache[:, W_BUF - NEAR:].reshape(nb, NEAR, ATTN_W)
    far = cache[:, :W_BUF - NEAR].reshape(nb, (W_BUF - NEAR) // FAR_PERIOD, FAR_PERIOD, N_HEADS, HEAD_DIM)
    return near, far[:, :, :FAR_KEEP].reshape(nb, FAR_ROWS, ATTN_W)


def _attn_sample(q_s, k_s, v_s, cache_k, cache_v):
    e = _head_expand()
    e01 = jnp.asarray(e, BF16)
    et01 = jnp.asarray(e.T, BF16)
    slopes = jnp.asarray([[2.0 ** (-8.0 * (h + 1) / N_HEADS) for h in range(N_HEADS)]], F32)
    near_k, far_k = _window_rows(cache_k)
    near_v, far_v = _window_rows(cache_v)
    row = pl.BlockSpec((1, ROWS_S, ATTN_W), lambda b: (b, 0, 0))
    near = pl.BlockSpec((1, NEAR, ATTN_W), lambda b: (b, 0, 0))
    far = pl.BlockSpec((1, FAR_ROWS, ATTN_W), lambda b: (b, 0, 0))
    pairs = N_HEADS // 2
    return pl.pallas_call(
        _attn_sample_kernel,
        name="attn_sample",
        grid=(N_SAMPLE_B,),
        in_specs=[row, row, row, near, near, far, far, _full(e01.shape), _full(et01.shape), _full(slopes.shape)],
        out_specs=row,
        out_shape=jax.ShapeDtypeStruct((N_SAMPLE_B, ROWS_S, ATTN_W), F32),
        scratch_shapes=[pltpu.VMEM((pairs, NEAR + ROWS_S, LANES), F32)] * 2 + [pltpu.VMEM((pairs, FAR_ROWS, LANES), F32)] * 2,
        compiler_params=_cparams(("arbitrary",)),
    )(q_s, k_s, v_s, near_k, near_v, far_k, far_v, e01, et01, slopes)


HALO = 8


def _dn_pre_kernel(dn_ref, halo_ref, b_ref, a_ref, valid_ref, cw_ref, alog_ref, dtb_ref, e_ref, et_ref, ltri_ref,
                   q_ref, k_ref, v_ref, bx_ref, gx_ref, buf_ref):
    ts = dn_ref.shape[0]
    buf_ref[0:HALO, :] = halo_ref[0]
    buf_ref[HALO:HALO + ts, :] = dn_ref[...]
    y = jnp.zeros((ts, CONV_CH), F32)
    for i in range(CONV_W):
        y = y + buf_ref[pl.ds(HALO - (CONV_W - 1) + i, ts), :] * cw_ref[i:i + 1, :]
    y = y * _sigmoid(y)
    valid = valid_ref[...]
    e01 = e_ref[...]
    et01 = et_ref[...]

    def l2n(t):
        ss = _dot_r01(t * t, e01)
        inv = lax.rsqrt(ss + NORM_EPS)
        return t * _dot_r01(inv, et01)

    q_ref[...] = l2n(y[:, :DN_W]) * (HEAD_DIM ** -0.5)
    k_ref[...] = l2n(y[:, DN_W:2 * DN_W]) * valid
    v_ref[...] = y[:, 2 * DN_W:] * valid
    beta = _sigmoid(b_ref[...]) * valid
    sp_in = a_ref[...] + dtb_ref[...]
    softplus = jnp.maximum(sp_in, 0.0) + jnp.log1p(jnp.exp(-jnp.abs(sp_in)))
    g = -jnp.exp(alog_ref[...]) * softplus * valid
    bx_ref[...] = _dot_r01(beta, et01)
    gx_ref[...] = _dot_l01(ltri_ref[...], _dot_r01(g, et01))


def _dn_pre(dn, halo, b_logit, a_logit, valid, conv_w, a_log, dt_bias):
    nt = halo.shape[0]
    rows = nt * TM
    e = _head_expand()
    e01 = jnp.asarray(e, BF16)
    et01 = jnp.asarray(e.T, BF16)
    ii = np.arange(TM)
    ltri = jnp.asarray(((ii[:, None] >= ii[None, :]) & (ii[:, None] // DN_C == ii[None, :] // DN_C)), BF16)
    row = lambda w: pl.BlockSpec((TM, w), lambda i: (i, 0))
    return pl.pallas_call(
        _dn_pre_kernel,
        name="dn_pre",
        grid=(nt,),
        in_specs=[row(CONV_CH), pl.BlockSpec((1, HALO, CONV_CH), lambda i: (i, 0, 0)), row(N_HEADS), row(N_HEADS),
                  row(1), _full(conv_w.shape), _full((1, N_HEADS)), _full((1, N_HEADS)),
                  _full(e01.shape), _full(et01.shape), _full(ltri.shape)],
        out_specs=[row(DN_W)] * 5,
        out_shape=[jax.ShapeDtypeStruct((rows, DN_W), F32)] * 5,
        scratch_shapes=[pltpu.VMEM((HALO + TM, CONV_CH), F32)],
        compiler_params=_cparams(("arbitrary",)),
    )(dn, halo, b_logit, a_logit, valid, conv_w, a_log.reshape(1, N_HEADS), dt_bias.reshape(1, N_HEADS),
      e01, et01, ltri)


def _delta_streams(q, k, v, bx, gcx, s_bd):
    c = DN_C
    n = len(q)
    each = range(n)
    lane = lax.broadcasted_iota(I32, (1, LANES), 1)
    h0 = lane < HALF
    m0 = h0.astype(F32)
    m1 = 1.0 - m0
    ii = lax.broadcasted_iota(I32, (2 * c, 2 * c), 0)
    jj = lax.broadcasted_iota(I32, (2 * c, 2 * c), 1)
    same = jnp.where(ii < c, 0, 1) == jnp.where(jj < c, 0, 1)
    order = jnp.where(same, ii - jj, -1)
    tril = order >= 0
    strict = order > 0
    eye = jnp.where(ii == jj, 1.0, 0.0)

    def stack(x):
        return jnp.concatenate([x * m0, x * m1], axis=0)

    def fold(x2):
        return x2[:c] + x2[c:]

    def decay_of(g):
        gsw = pltpu.roll(g, HALF, axis=1)
        row_b = jnp.concatenate([jnp.where(h0, g, gsw), jnp.where(h0, gsw, g)], axis=0)
        return jnp.where(tril, jnp.exp(jnp.where(tril, row_b - row_b.T, 0.0)), 0.0)

    eg = [jnp.exp(gcx[i]) for i in each]
    kb = [k[i] * bx[i] for i in each]
    glast = [gcx[i][c - 1:c, :] for i in each]
    decay = [decay_of(gcx[i]) for i in each]
    k2 = [stack(k[i]).astype(BF16) for i in each]
    lhs = [jnp.concatenate([stack(kb[i]), stack(q[i])], axis=0).astype(BF16) for i in each]
    aq = [_dot_nt(lhs[i], k2[i]) for i in each]
    a_low = [jnp.where(strict, aq[i][:2 * c] * decay[i], 0.0) for i in each]
    aqk = [jnp.where(tril, aq[i][2 * c:] * decay[i], 0.0).astype(BF16) for i in each]
    tinv = [eye - a_low[i] for i in each]
    apow = [a_low[i].astype(BF16) for i in each]
    for _ in range(5):
        apow = [_dot(apow[i], apow[i]).astype(BF16) for i in each]
        tinv = [tinv[i] + _dot(tinv[i].astype(BF16), apow[i]) for i in each]
    rhs = [jnp.concatenate([stack(v[i] * bx[i]), stack(kb[i] * eg[i])], axis=1).astype(BF16) for i in each]
    uw = [_dot(tinv[i].astype(BF16), rhs[i]) for i in each]
    u = [fold(uw[i][:, :LANES]) for i in each]
    w = [fold(uw[i][:, LANES:]) for i in each]
    ws = [_dot(jnp.concatenate([w[i], q[i] * eg[i]], axis=0).astype(BF16), s_bd[i].astype(BF16)) for i in each]
    v_new = [u[i] - ws[i][:c] for i in each]
    intra = [_dot(aqk[i], stack(v_new[i]).astype(BF16)) for i in each]
    kd = [(k[i] * jnp.exp(glast[i] - gcx[i])).T.astype(BF16) for i in each]
    upd = [_dot(kd[i], v_new[i].astype(BF16)) for i in each]
    o = [ws[i][c:] + fold(intra[i]) for i in each]
    s_new = [s_bd[i] * jnp.exp(glast[i]) + jnp.where(same, upd[i], 0.0) for i in each]
    return o, s_new


def _delta_kernel(q_ref, k_ref, v_ref, bx_ref, gx_ref, s0_ref, o_ref, sout_ref, s_ref):
    ci = pl.program_id(1)
    nb = q_ref.shape[0]
    npair = N_HEADS // 2

    @pl.when(ci == 0)
    def _():
        s_ref[...] = s0_ref[...]

    streams = [(b, hp) for b in range(nb) for hp in range(npair)]
    lanes = lambda hp: slice(hp * LANES, (hp + 1) * LANES)
    take = lambda ref: [ref[b, :, lanes(hp)] for b, hp in streams]
    o, s_new = _delta_streams(take(q_ref), take(k_ref), take(v_ref), take(bx_ref), take(gx_ref),
                              [s_ref[b, hp] for b, hp in streams])
    for i, (b, hp) in enumerate(streams):
        o_ref[b, :, lanes(hp)] = o[i]
        s_ref[b, hp] = s_new[i]

    @pl.when(ci == pl.num_programs(1) - 1)
    def _():
        sout_ref[...] = s_ref[...]


def _delta(q, k, v, bx, gx, s0_bd, bblk):
    nbatch, length, _ = q.shape
    npair = N_HEADS // 2
    seq = pl.BlockSpec((bblk, DN_C, DN_W), lambda g, c: (g, c, 0))
    st = pl.BlockSpec((bblk, npair, LANES, LANES), lambda g, c: (g, 0, 0, 0))
    return pl.pallas_call(
        _delta_kernel,
        name="delta",
        grid=(nbatch // bblk, length // DN_C),
        in_specs=[seq] * 5 + [st],
        out_specs=[seq, st],
        out_shape=[jax.ShapeDtypeStruct((nbatch, length, DN_W), F32),
                   jax.ShapeDtypeStruct((nbatch, npair, LANES, LANES), F32)],
        scratch_shapes=[pltpu.VMEM((bblk, npair, LANES, LANES), F32)],
        compiler_params=_cparams(("arbitrary", "arbitrary")),
    )(q, k, v, bx, gx, s0_bd)


def _state_to_bd(s):
    b = s.shape[0]
    s = s.reshape(b, N_HEADS // 2, 2, HALF, HALF)
    zero = jnp.zeros_like(s[:, :, 0])
    top = jnp.concatenate([s[:, :, 0], zero], axis=-1)
    bot = jnp.concatenate([zero, s[:, :, 1]], axis=-1)
    return jnp.concatenate([top, bot], axis=-2)


def _state_from_bd(s):
    b = s.shape[0]
    return jnp.stack([s[:, :, :HALF, :HALF], s[:, :, HALF:, HALF:]], axis=2).reshape(b, N_HEADS, HALF, HALF)


def _out_kernel(xp_ref, xs_ref, ap_ref, as_ref, dp_ref, ds_ref, z_ref, ng_ref, bavg_ref, wa_ref, wd_ref, g_ref, b_ref,
                o_ref):
    o = _pick(dp_ref, ds_ref)
    oh, ol = _split2(o * o)
    ms = _dot(oh, bavg_ref[...]) + _dot(ol, bavg_ref[...])
    z = z_ref[...]
    dn = o * lax.rsqrt(ms + NORM_EPS) * ng_ref[...] * (z * _sigmoid(z))
    mix = _dot(_pick(ap_ref, as_ref).astype(BF16), wa_ref[...]) + _dot(dn.astype(BF16), wd_ref[...])
    o_ref[...] = _layer_norm(ALPHA * _pick(xp_ref, xs_ref) + mix, g_ref[...], b_ref[...])


def _out_proj(x_p, x_s, att_p, att_s, dn_p, dn_s, z, norm_g, w_out, g, b):
    wa = w_out[:ATTN_W].astype(BF16)
    wd = w_out[ATTN_W:].astype(BF16)
    normg = jnp.tile(norm_g.reshape(1, HEAD_DIM), (1, N_HEADS))
    bavg = jnp.asarray(np.kron(np.eye(N_HEADS), np.ones((HALF, HALF))) / HALF, BF16)
    row = lambda w: pl.BlockSpec((TM, w), lambda i: (i, 0))
    return pl.pallas_call(
        _out_kernel,
        name="out_proj",
        grid=(N_TILES,),
        in_specs=_row2(D_MODEL) + _row2(ATTN_W) + _row2(DN_W) + [
            row(DN_W), _full(normg.shape), _full(bavg.shape),
            _full(wa.shape), _full(wd.shape), _full((1, D_MODEL)), _full((1, D_MODEL))],
        out_specs=row(D_MODEL),
        out_shape=jax.ShapeDtypeStruct((N_PAD, D_MODEL), F32),
        compiler_params=_cparams(("arbitrary",)),
    )(x_p, x_s, att_p, att_s, dn_p, dn_s, z, normg, bavg, wa, wd, g.reshape(1, -1), b.reshape(1, -1))


N_GROUPS = 8
GROUP = N_EXPERTS // N_GROUPS
SLOT_RADIX = 64
_ET_SPEC = pl.BlockSpec((1, 2 * N_EXPERTS, TM), lambda i, *_: (i, 0, 0))
_SEG_SPEC = pl.BlockSpec((1, 1, 2 * N_EXPERTS), lambda i, *_: (i, 0, 0))
TOPK_GROUPS = 4
NEG = -jnp.inf


def _route_kernel(x_ref, wh_ref, wl_ref, bias_ref, upper_ref, lows_ref, st_ref, ge_ref, cnt_ref):
    t = pl.program_id(0)
    xh, xl = _split2(x_ref[...])
    wh = wh_ref[...]
    logits = _dot_nt(wh, xh) + _dot_nt(wh, xl) + _dot_nt(wl_ref[...], xh)
    scores = _sigmoid(logits)
    choice = scores + bias_ref[...]
    i8 = lax.broadcasted_iota(I32, (GROUP, TM), 0)

    def first_max(vals):
        m = jnp.max(vals, axis=0, keepdims=True)
        idx = jnp.min(jnp.where(vals == m, i8, GROUP), axis=0, keepdims=True)
        return m, idx

    pieces = [choice[g * GROUP:(g + 1) * GROUP, :] for g in range(N_GROUPS)]
    gsc = jnp.zeros((N_GROUPS, TM), F32)
    for g in range(N_GROUPS):
        m1, idx1 = first_max(pieces[g])
        m2 = jnp.max(jnp.where(i8 == idx1, NEG, pieces[g]), axis=0, keepdims=True)
        gsc = jnp.where(i8 == g, m1 + m2, gsc)
    gsel = jnp.zeros((N_GROUPS, TM), F32)
    for _ in range(TOPK_GROUPS):
        _, idx = first_max(gsc)
        hit = i8 == idx
        gsel = jnp.where(hit, 1.0, gsel)
        gsc = jnp.where(hit, NEG, gsc)
    masked = [jnp.where(gsel[g:g + 1, :] > 0.5, pieces[g], NEG) for g in range(N_GROUPS)]
    eidx = [i8 + g * GROUP for g in range(N_GROUPS)]
    member = [jnp.zeros((GROUP, TM), F32) for _ in range(N_GROUPS)]
    for _ in range(TOP_K):
        mm = masked[0]
        for g in range(1, N_GROUPS):
            mm = jnp.maximum(mm, masked[g])
        m = jnp.max(mm, axis=0, keepdims=True)
        cand = jnp.where(masked[0] == m, eidx[0], N_EXPERTS)
        for g in range(1, N_GROUPS):
            cand = jnp.minimum(cand, jnp.where(masked[g] == m, eidx[g], N_EXPERTS))
        idx = jnp.min(cand, axis=0, keepdims=True)
        for g in range(N_GROUPS):
            hit = eidx[g] == idx
            member[g] = jnp.where(hit, 1.0, member[g])
            masked[g] = jnp.where(hit, NEG, masked[g])

    tok = t * TM + lax.broadcasted_iota(I32, (1, TM), 1)
    real = jnp.where(tok < N_REAL, 1.0, 0.0)
    mem = jnp.concatenate(member, axis=0) * real
    sel = mem * scores
    gate = (sel / jnp.sum(sel + (1.0 - real), axis=0, keepdims=True) * ROUTED_SCALE).astype(BF16)
    ge_ref[0] = jnp.concatenate([gate, jnp.zeros_like(gate)], axis=0)
    memb = mem.astype(BF16)
    rank = _dot(memb, upper_ref[...])
    cnt = jnp.sum(mem, axis=1, keepdims=True)
    cpad = jnp.floor((cnt + (SEG_ALIGN - 1)) * (1.0 / SEG_ALIGN))
    loc = _dot(lows_ref[...], jnp.broadcast_to(cpad, (N_EXPERTS, TM)).astype(BF16)) * SEG_ALIGN
    slot_e = jnp.where(mem > 0.5, loc + rank, float(M_T))
    slot_hi = jnp.floor(slot_e * (1.0 / SLOT_RADIX)) * SLOT_RADIX
    st_ref[0] = jnp.concatenate([slot_hi, slot_e - slot_hi + 1.0], axis=0).astype(BF16)
    ones = jnp.ones((GROUP, TM), BF16)
    cnt_ref[0] = _dot_nt(ones, memb).astype(I32)


def _route(x1, w_router, router_bias):
    wt = w_router.T
    wh, wl = _split2(wt)
    ii = np.arange(TM)
    upper = jnp.asarray(ii[:, None] < ii[None, :], BF16)
    ee = np.arange(N_EXPERTS)
    lows = jnp.asarray(ee[:, None] > ee[None, :], BF16)
    return pl.pallas_call(
        _route_kernel,
        name="route",
        grid=(N_TILES,),
        in_specs=[pl.BlockSpec((TM, D_MODEL), lambda i: (i, 0)), _full(wh.shape), _full(wl.shape),
                  _full((N_EXPERTS, 1)), _full(upper.shape), _full(lows.shape)],
        out_specs=[_ET_SPEC, _ET_SPEC, pl.BlockSpec((1, GROUP, N_EXPERTS), lambda i: (i, 0, 0))],
        out_shape=[jax.ShapeDtypeStruct((N_TILES, 2 * N_EXPERTS, TM), BF16)] * 2 + [
            jax.ShapeDtypeStruct((N_TILES, GROUP, N_EXPERTS), I32)],
        compiler_params=_cparams(("arbitrary",)),
    )(x1, wh, wl, router_bias.reshape(N_EXPERTS, 1), upper, lows)


def _segment_tables(cnt):
    seg = (cnt + SEG_ALIGN - 1) // SEG_ALIGN
    cpad = seg * SEG_ALIGN
    loc_off = jnp.cumsum(cpad, axis=1) - cpad
    rows_e = jnp.sum(cpad, axis=0)
    blocks_e = (rows_e + BM - 1) // BM
    bend = jnp.cumsum(blocks_e)
    bstart = bend - blocks_e
    glob_off = (bstart * BM)[None, :] + jnp.cumsum(cpad, axis=0) - cpad
    nb = bend[-1]
    pad_off = bstart * BM + rows_e
    pad_seg = (blocks_e * BM - rows_e) // SEG_ALIGN
    i32 = lambda a: a.reshape(-1).astype(I32)
    bounds = lambda a: jnp.tile(a.astype(F32), (1, 2)).reshape(N_TILES, 1, 2 * N_EXPERTS)
    return (i32(seg), i32(loc_off), i32(glob_off), i32(pad_seg), i32(pad_off), i32(bstart), i32(blocks_e), i32(nb),
            bounds(loc_off), bounds(loc_off + cpad))


def _rows_copy(src_of, dst_of, lo, go, rows, sem):
    lo, go, rows = (v if isinstance(v, int) else pl.multiple_of(v, SEG_ALIGN) for v in (lo, go, rows))
    return pltpu.make_async_copy(src_of(lo, go, rows), dst_of(lo, go, rows), sem)


def _segment_copies(seg_ref, loc_ref, glob_ref, t, src_of, dst_of, sem):
    def body(e, carry):
        k = seg_ref[t * N_EXPERTS + e]

        @pl.when(k > 0)
        def _():
            _rows_copy(src_of, dst_of, loc_ref[t * N_EXPERTS + e], glob_ref[t * N_EXPERTS + e], k * SEG_ALIGN, sem).start()

        return carry

    lax.fori_loop(0, N_EXPERTS, body, 0)


def _wait_rows(src_of, dst_of, rows, sem):
    @pl.when(rows > 0)
    def _():
        _rows_copy(src_of, dst_of, 0, 0, rows, sem).wait()


def _local_rows(m0, rows, lo, hi, slot_tab):
    mcol = (lax.broadcasted_iota(I32, (rows, 2 * N_EXPERTS), 0) + m0).astype(F32)
    e01 = jnp.where(mcol >= lo, jnp.where(mcol < hi, 1.0, 0.0), 0.0).astype(BF16)
    mrow = (lax.broadcasted_iota(I32, (rows, TM), 0) + (m0 + 1)).astype(F32)
    return _dot(e01, slot_tab) == mrow, e01


def _row_spans(used):
    yield None, 0, M_MAIN
    for m0 in range(M_MAIN, M_T, M_CHUNK):
        yield m0 < used, m0, M_CHUNK


def _tile_rows(seg_ref, loc_ref, t):
    last = t * N_EXPERTS + N_EXPERTS - 1
    return loc_ref[last] + seg_ref[last] * SEG_ALIGN


def _dispatch_kernel(seg_ref, loc_ref, glob_ref, pseg_ref, poff_ref, nb_ref, x_ref, st_ref, lo_ref, hi_ref,
                     xs_ref, buf_ref, sem):
    t = pl.program_id(0)
    s = t % 2
    used = _tile_rows(seg_ref, loc_ref, t)
    for live, m0, rows in _row_spans(used):
        def sort_rows(m0=m0, rows=rows):
            hit, _ = _local_rows(m0, rows, lo_ref[0], hi_ref[0], st_ref[0])
            onehot = jnp.where(hit, 1.0, 0.0).astype(BF16)
            buf_ref[s, m0:m0 + rows, :] = _dot(onehot, x_ref[...].astype(BF16)).astype(BF16)

        if live is None:
            sort_rows()
        else:
            pl.when(live)(sort_rows)

    src = lambda lo, go, rows: buf_ref.at[s, pl.ds(lo, rows), :]
    dst = lambda lo, go, rows: xs_ref.at[pl.ds(go, rows), :]
    _segment_copies(seg_ref, loc_ref, glob_ref, t, src, dst, sem.at[s])

    @pl.when(t > 0)
    def _():
        prev = lambda lo, go, rows: buf_ref.at[1 - s, pl.ds(lo, rows), :]
        _wait_rows(prev, dst, _tile_rows(seg_ref, loc_ref, t - 1), sem.at[1 - s])

    @pl.when(t == N_TILES - 1)
    def _():
        _wait_rows(src, dst, used, sem.at[s])
        buf_ref[s, 0:BM, :] = jnp.zeros((BM, D_MODEL), BF16)
        zeros = lambda lo, go, rows: buf_ref.at[s, pl.ds(0, rows), :]

        def fill(wait):
            def pad(e, carry):
                k = pseg_ref[e]

                @pl.when(k > 0)
                def _():
                    cp = _rows_copy(zeros, dst, 0, poff_ref[e], k * SEG_ALIGN, sem.at[s])
                    cp.wait() if wait else cp.start()

                return carry

            lax.fori_loop(0, N_EXPERTS, pad, 0)

            def tail(b, carry):
                cp = _rows_copy(zeros, dst, 0, b * BM, BM, sem.at[s])
                cp.wait() if wait else cp.start()
                return carry

            lax.fori_loop(nb_ref[0], NB_MAX, tail, 0)

        fill(False)
        fill(True)


def _dispatch(x1, slot_tab, seg_lo, seg_hi, tables):
    grid_spec = pltpu.PrefetchScalarGridSpec(
        num_scalar_prefetch=len(tables),
        grid=(N_TILES,),
        in_specs=[pl.BlockSpec((TM, D_MODEL), lambda i, *_: (i, 0)), _ET_SPEC, _SEG_SPEC, _SEG_SPEC],
        out_specs=pl.BlockSpec(memory_space=pl.ANY),
        scratch_shapes=[pltpu.VMEM((2, M_T, D_MODEL), BF16), pltpu.SemaphoreType.DMA((2,))],
    )
    return pl.pallas_call(
        _dispatch_kernel,
        name="dispatch",
        grid_spec=grid_spec,
        out_shape=jax.ShapeDtypeStruct((NB_MAX * BM, D_MODEL), BF16),
        compiler_params=_cparams(("arbitrary",)),
    )(*tables, x1, slot_tab, seg_lo, seg_hi)


def _expert_kernel(bstart_ref, bcount_ref, nb_ref, xs_ref, wg_ref, wu_ref, wd_ref, ys_ref,
                   xbuf, ybuf, wgu_s, wd_s, sem_in, sem_out):
    e = pl.program_id(0)
    nb = nb_ref[0]
    b0 = bstart_ref[e]
    n = bcount_ref[e]

    def rows_of(ref, g):
        return ref.at[pl.ds(pl.multiple_of(g * BM, BM), BM), :]

    def x_copy(g):
        return pltpu.make_async_copy(rows_of(xs_ref, g), xbuf.at[g % EXPERT_SLOTS], sem_in.at[g % EXPERT_SLOTS])

    def y_copy(g):
        return pltpu.make_async_copy(ybuf.at[g % EXPERT_SLOTS], rows_of(ys_ref, g), sem_out.at[g % EXPERT_SLOTS])

    @pl.when(e == 0)
    def _():
        for g in range(EXPERT_SLOTS - 1):
            @pl.when(g < nb)
            def _(g=g):
                x_copy(g).start()

    wgu_s[:, :EXPERT_DIM] = wg_ref[0].astype(BF16)
    wgu_s[:, EXPERT_DIM:] = wu_ref[0].astype(BF16)
    wd_s[...] = wd_ref[0].astype(BF16)

    def block(i, carry):
        g = b0 + i
        slot = g % EXPERT_SLOTS

        @pl.when(g + EXPERT_SLOTS - 1 < nb)
        def _():
            x_copy(g + EXPERT_SLOTS - 1).start()

        x_copy(g).wait()

        @pl.when(g >= EXPERT_SLOTS)
        def _():
            y_copy(g - EXPERT_SLOTS).wait()

        h = _dot(xbuf[slot], wgu_s[...])
        gate = h[:, :EXPERT_DIM]
        a = (gate * _sigmoid(gate) * h[:, EXPERT_DIM:]).astype(BF16)
        ybuf[slot] = _dot(a, wd_s[...]).astype(BF16)
        y_copy(g).start()
        return carry

    lax.fori_loop(0, n, block, 0)

    @pl.when(e == N_EXPERTS - 1)
    def _():
        for k in range(EXPERT_SLOTS, 0, -1):
            @pl.when(nb >= k)
            def _(k=k):
                y_copy(nb - k).wait()

        ybuf[0] = jnp.zeros((BM, D_MODEL), BF16)

        def tail(wait):
            def body(g, carry):
                cp = pltpu.make_async_copy(ybuf.at[0], rows_of(ys_ref, g), sem_out.at[0])
                cp.wait() if wait else cp.start()
                return carry

            lax.fori_loop(nb, NB_MAX, body, 0)

        tail(False)
        tail(True)


def _experts(xs, we_gate, we_up, we_down, bstart, bcount, nb):
    weight = lambda shape: pl.BlockSpec((1,) + shape, lambda e, *_: (e, 0, 0))
    grid_spec = pltpu.PrefetchScalarGridSpec(
        num_scalar_prefetch=3,
        grid=(N_EXPERTS,),
        in_specs=[pl.BlockSpec(memory_space=pl.ANY), weight((D_MODEL, EXPERT_DIM)), weight((D_MODEL, EXPERT_DIM)),
                  weight((EXPERT_DIM, D_MODEL))],
        out_specs=pl.BlockSpec(memory_space=pl.ANY),
        scratch_shapes=[pltpu.VMEM((EXPERT_SLOTS, BM, D_MODEL), BF16), pltpu.VMEM((EXPERT_SLOTS, BM, D_MODEL), BF16),
                        pltpu.VMEM((D_MODEL, 2 * EXPERT_DIM), BF16), pltpu.VMEM((EXPERT_DIM, D_MODEL), BF16),
                        pltpu.SemaphoreType.DMA((EXPERT_SLOTS,)), pltpu.SemaphoreType.DMA((EXPERT_SLOTS,))],
    )
    return pl.pallas_call(
        _expert_kernel,
        name="expert",
        grid_spec=grid_spec,
        out_shape=jax.ShapeDtypeStruct((NB_MAX * BM, D_MODEL), BF16),
        compiler_params=_cparams(("arbitrary",)),
    )(bstart, bcount, nb, xs, we_gate, we_up, we_down)


def _combine_kernel(seg_ref, loc_ref, glob_ref, x_ref, st_ref, ge_ref, lo_ref, hi_ref, pp_ref, ps_ref, ys_ref,
                    wsgu_ref, wsd_ref, g2_ref, b2_ref, wpg_ref, wpp_ref, g3_ref, b3_ref, op_ref, os_ref,
                    buf_ref, acc_ref, sem):
    t = pl.program_id(0)
    s = t % 2
    used = _tile_rows(seg_ref, loc_ref, t)
    src = lambda lo, go, rows: ys_ref.at[pl.ds(go, rows), :]
    into = lambda slot: (lambda lo, go, rows: buf_ref.at[slot, pl.ds(lo, rows), :])

    @pl.when(t == 0)
    def _():
        buf_ref[...] = jnp.zeros((2, M_T, D_MODEL), BF16)
        _segment_copies(seg_ref, loc_ref, glob_ref, t, src, into(s), sem.at[s])

    @pl.when(t + 1 < N_TILES)
    def _():
        _segment_copies(seg_ref, loc_ref, glob_ref, t + 1, src, into(1 - s), sem.at[1 - s])

    _wait_rows(src, into(s), used, sem.at[s])

    for live, m0, rows in _row_spans(used):
        def gather_rows(m0=m0, rows=rows):
            hit, e01 = _local_rows(m0, rows, lo_ref[0], hi_ref[0], st_ref[0])
            pg = jnp.where(hit, _dot(e01, ge_ref[0]), 0.0).astype(BF16)
            return lax.dot_general(pg, buf_ref[s, m0:m0 + rows, :], (((0,), (0,)), ((), ())),
                                   preferred_element_type=F32)

        if live is None:
            acc_ref[...] = gather_rows()
        else:
            @pl.when(live)
            def _(gather_rows=gather_rows):
                acc_ref[...] += gather_rows()

    x = x_ref[...]
    routed = acc_ref[...]
    xb = x.astype(BF16)
    h = _dot(xb, wsgu_ref[...])
    g = h[:, :EXPERT_DIM]
    shared = _dot((g * _sigmoid(g) * h[:, EXPERT_DIM:]).astype(BF16), wsd_ref[...])
    x2 = _layer_norm(ALPHA * x + (routed + shared), g2_ref[...], b2_ref[...])
    ple = _sigmoid(_dot(x2.astype(BF16), wpg_ref[...])) * _dot(_pick(pp_ref, ps_ref).astype(BF16), wpp_ref[...])
    y = _layer_norm(ALPHA * x2 + ple, g3_ref[...], b3_ref[...])

    @pl.when(t < NT_P)
    def _():
        op_ref[...] = y

    @pl.when(t >= NT_P)
    def _():
        os_ref[...] = y


def _combine(x1, slot_tab, gate_e, seg_lo, seg_hi, p_p, p_s, ys, seg, loc_off, glob_off, ws_gate, ws_up, ws_down,
             ln2_g, ln2_b, w_ple_gate, w_ple_proj, ln3_g, ln3_b):
    wsgu = jnp.concatenate([ws_gate, ws_up], axis=1).astype(BF16)
    wsd = ws_down.astype(BF16)
    wpg = w_ple_gate.astype(BF16)
    wpp = w_ple_proj.astype(BF16)
    row = lambda w: pl.BlockSpec((TM, w), lambda i, *_: (i, 0))
    full = lambda shape: pl.BlockSpec(shape, lambda i, *_: (0,) * len(shape))
    vec = full((1, D_MODEL))
    grid_spec = pltpu.PrefetchScalarGridSpec(
        num_scalar_prefetch=3,
        grid=(N_TILES,),
        in_specs=[row(D_MODEL), _ET_SPEC, _ET_SPEC, _SEG_SPEC, _SEG_SPEC] + _row2(PLE_DIM) + [
            pl.BlockSpec(memory_space=pl.ANY),
            full(wsgu.shape), full(wsd.shape), vec, vec, full(wpg.shape), full(wpp.shape), vec, vec],
        out_specs=_row2(D_MODEL),
        scratch_shapes=[pltpu.VMEM((2, M_T, D_MODEL), BF16), pltpu.VMEM((TM, D_MODEL), F32),
                        pltpu.SemaphoreType.DMA((2,))],
    )
    r = lambda a: a.reshape(1, -1)
    return pl.pallas_call(
        _combine_kernel,
        name="combine",
        grid_spec=grid_spec,
        out_shape=[jax.ShapeDtypeStruct((N_P, D_MODEL), F32), jax.ShapeDtypeStruct((N_SP, D_MODEL), F32)],
        compiler_params=_cparams(("arbitrary",)),
    )(seg, loc_off, glob_off, x1, slot_tab, gate_e, seg_lo, seg_hi, p_p, p_s, ys, wsgu, wsd, r(ln2_g), r(ln2_b),
      wpg, wpp, r(ln3_g), r(ln3_b))


def _pad_rows(a, rows):
    return jnp.concatenate([a, jnp.zeros((rows - a.shape[0],) + a.shape[1:], a.dtype)], axis=0)


def kernel(x_prompt, x_sample, cache_k_win, cache_v_win, state_conv, state_delta, p_prompt, p_sample, w_in, conv_w, a_log, dt_bias, dn_norm_g, w_out, ln1_g, ln1_b, w_router, router_bias, we_gate, we_up, we_down, ws_gate, ws_up, ws_down, ln2_g, ln2_b, w_ple_gate, w_ple_proj, ln3_g, ln3_b):
    x_p = x_prompt.reshape(N_P, D_MODEL)
    x_s = _pad_rows(x_sample.reshape(N_S, D_MODEL), N_SP)
    p_p = p_prompt[0].reshape(N_P, PLE_DIM)
    p_s = _pad_rows(p_sample[0].reshape(N_S, PLE_DIM), N_SP)

    qkv, dn_in, z, b_logit, a_logit = _project(x_p, x_s, w_in[0])

    att_p = _attn_prompt(qkv)
    qkv_s = qkv[N_P:N_REAL].reshape(N_SAMPLE_B, DEC_SEQ, 3 * ATTN_W)
    qkv_s8 = jnp.pad(qkv_s, ((0, 0), (0, ROWS_S - DEC_SEQ), (0, 0)))
    att_s = _attn_sample(qkv_s8[:, :, :ATTN_W], qkv_s8[:, :, ATTN_W:2 * ATTN_W], qkv_s8[:, :, 2 * ATTN_W:],
                         cache_k_win[0], cache_v_win[0])
    att_s = _pad_rows(att_s[:, :DEC_SEQ].reshape(N_S, ATTN_W), N_SP)

    tails = dn_in.reshape(N_TILES, TM, CONV_CH)[:NT_P - 1, TM - HALO:]
    halo_p = jnp.concatenate([jnp.zeros((1, HALO, CONV_CH), F32), tails], axis=0)
    seq_start = (jnp.arange(NT_P) % (SEQ // TM) == 0)[:, None, None]
    halo_p = jnp.where(seq_start, 0.0, halo_p)
    q_p, k_p, v_p, bx_p, gx_p = _dn_pre(dn_in, halo_p, b_logit, a_logit, jnp.ones((N_P, 1), F32),
                                        conv_w[0], a_log[0], dt_bias[0])
    shp = (N_PROMPT_B, SEQ, DN_W)
    s0_p = jnp.zeros((N_PROMPT_B, N_HEADS // 2, LANES, LANES), F32)
    o_p, s_p = _delta(q_p.reshape(shp), k_p.reshape(shp), v_p.reshape(shp), bx_p.reshape(shp), gx_p.reshape(shp),
                      s0_p, N_PROMPT_B)

    def seq_pad(tok, state=None):
        w = tok.shape[-1]
        tok = tok.reshape(N_SAMPLE_B, DEC_SEQ, w)
        head = jnp.zeros((N_SAMPLE_B, S_TOK0, w), F32)
        if state is not None:
            head = head.at[:, S_TOK0 - (CONV_W - 1):].set(state)
        tail = jnp.zeros((N_SAMPLE_B, SEQ_S - S_TOK0 - DEC_SEQ, w), F32)
        return jnp.concatenate([head, tok, tail], axis=1).reshape(N_SAMPLE_B * SEQ_S, w)

    dn_tok_s = dn_in[N_P:N_REAL]
    dn_s = seq_pad(dn_tok_s, state_conv[0])
    rows_s = N_SAMPLE_B * SEQ_S
    valid_s = seq_pad(jnp.ones((N_S, 1), F32))
    halo_s = jnp.zeros((rows_s // TM, HALO, CONV_CH), F32)
    q_s, k_s, v_s, bx_s, gx_s = _dn_pre(dn_s, halo_s, seq_pad(b_logit[N_P:N_REAL]), seq_pad(a_logit[N_P:N_REAL]),
                                        valid_s, conv_w[0], a_log[0], dt_bias[0])
    shs = (N_SAMPLE_B, SEQ_S, DN_W)
    o_s, s_s = _delta(q_s.reshape(shs), k_s.reshape(shs), v_s.reshape(shs), bx_s.reshape(shs), gx_s.reshape(shs),
                      _state_to_bd(state_delta[0]), 4)
    dn_o_s = _pad_rows(o_s[:, S_TOK0:S_TOK0 + DEC_SEQ].reshape(N_S, DN_W), N_SP)

    x1 = _out_proj(x_p, x_s, att_p, att_s, o_p.reshape(N_P, DN_W), dn_o_s, z, dn_norm_g[0], w_out[0], ln1_g[0], ln1_b[0])

    slot_tab, gate_e, cnt = _route(x1, w_router[0], router_bias[0])
    seg, loc_off, glob_off, pad_seg, pad_off, bstart, bcount, nb, seg_lo, seg_hi = _segment_tables(cnt[:, 0, :])
    xs = _dispatch(x1, slot_tab, seg_lo, seg_hi, (seg, loc_off, glob_off, pad_seg, pad_off, nb))
    ys = _experts(xs, we_gate[0], we_up[0], we_down[0], bstart, bcount, nb)
    y_p, y_s = _combine(x1, slot_tab, gate_e, seg_lo, seg_hi, p_p, p_s, ys, seg, loc_off, glob_off,
                        ws_gate[0], ws_up[0], ws_down[0], ln2_g[0], ln2_b[0], w_ple_gate[0], w_ple_proj[0],
                        ln3_g[0], ln3_b[0])

    y_prompt = y_p.reshape(N_PROMPT_B, SEQ, D_MODEL)
    y_sample = y_s[:N_S].reshape(N_SAMPLE_B, DEC_SEQ, D_MODEL)
    heads = (N_HEADS, HEAD_DIM)
    win = lambda c0: jnp.stack([qkv[(b + 1) * SEQ - W_BUF:(b + 1) * SEQ, c0:c0 + ATTN_W] for b in range(N_PROMPT_B)])
    k_pr = win(ATTN_W).reshape(N_PROMPT_B, W_BUF, *heads)
    v_pr = win(2 * ATTN_W).reshape(N_PROMPT_B, W_BUF, *heads)
    conv_p = jnp.stack([dn_in[(b + 1) * SEQ - (CONV_W - 1):(b + 1) * SEQ] for b in range(N_PROMPT_B)])
    k_new = qkv_s[:, :, ATTN_W:2 * ATTN_W].reshape(N_SAMPLE_B, DEC_SEQ, *heads)
    v_new = qkv_s[:, :, 2 * ATTN_W:].reshape(N_SAMPLE_B, DEC_SEQ, *heads)
    k_sm = jnp.concatenate([cache_k_win[0][:, DEC_SEQ:], k_new], axis=1)
    v_sm = jnp.concatenate([cache_v_win[0][:, DEC_SEQ:], v_new], axis=1)
    conv_s = dn_tok_s.reshape(N_SAMPLE_B, DEC_SEQ, CONV_CH)[:, DEC_SEQ - (CONV_W - 1):]
    return (y_prompt, y_sample, k_pr[None], v_pr[None], conv_p[None], _state_from_bd(s_p)[None],
            k_sm[None], v_sm[None], conv_s[None], _state_from_bd(s_s)[None])
```

```python
import functools

import jax
import jax.numpy as jnp
import numpy as np
from jax import lax
from jax.experimental import pallas as pl
from jax.experimental.pallas import tpu as pltpu

F32 = jnp.float32
BF16 = jnp.bfloat16
I32 = jnp.int32

D_MODEL = 1024
N_PROMPT_B, SEQ = 2, 8192
N_SAMPLE_B, DEC_SEQ = 32, 4
W_BUF = 2048
N_HEADS = 8
HEAD_DIM = 64
ATTN_W = 512
CONV_CH = 1536
CONV_W = 4
DN_W = 512
N_EXPERTS = 64
TOP_K = 8
EXPERT_DIM = 256
PLE_DIM = 256
BRANCHES = ((128, 1), (512, 4), (2048, 16))
N_BACK = 128
ROUTED_SCALE = 2.5
LN_EPS = 1e-5
NORM_EPS = 1e-6
ALPHA = 2.0 ** 0.25

LANES = 128
HALF = 64
VMEM_LIMIT = 56 * 1024 * 1024

N_P = N_PROMPT_B * SEQ
N_S = N_SAMPLE_B * DEC_SEQ
N_REAL = N_P + N_S
N_PAD = N_P + 512
TM = 256
N_TILES = N_PAD // TM
NT_P = N_P // TM
N_SP = N_PAD - N_P

SEG_ALIGN = 16
M_T = 3072
M_CHUNK = 256
M_MAIN = 2560
BM = 512
EXPERT_SLOTS = 4
R_MAX = N_PAD * TOP_K + N_TILES * N_EXPERTS * (SEG_ALIGN - 1) + N_EXPERTS * (BM - 1)
NB_MAX = -(-R_MAX // BM)

DN_C = 64
SEQ_S = 64
S_TOK0 = 8


def _dot(a, b):
    return jnp.dot(a, b, preferred_element_type=F32)


def _dot_nt(a, b):
    return lax.dot_general(a, b, (((1,), (1,)), ((), ())), preferred_element_type=F32)


def _split2(x):
    hi = x.astype(BF16)
    lo = (x - hi.astype(F32)).astype(BF16)
    return hi, lo


def _dot_l01(m01, x):
    hi, lo = _split2(x)
    return _dot(m01, hi) + _dot(m01, lo)


def _dot_r01(x, m01):
    hi, lo = _split2(x)
    return _dot(hi, m01) + _dot(lo, m01)


def _sigmoid(x):
    return 1.0 / (1.0 + jnp.exp(-x))


def _layer_norm(x, g, b):
    mu = jnp.mean(x, axis=-1, keepdims=True)
    xc = x - mu
    var = jnp.mean(xc * xc, axis=-1, keepdims=True)
    return xc * lax.rsqrt(var + LN_EPS) * g + b


def _cparams(sem=None):
    return pltpu.CompilerParams(dimension_semantics=sem, vmem_limit_bytes=VMEM_LIMIT)


def _full(shape):
    return pl.BlockSpec(shape, lambda *_: (0,) * len(shape))


def _row2(w):
    return [pl.BlockSpec((TM, w), lambda i, *_: (jnp.minimum(i, NT_P - 1), 0)),
            pl.BlockSpec((TM, w), lambda i, *_: (jnp.maximum(i - NT_P, 0), 0))]


def _pick(p_ref, s_ref):
    return jnp.where(pl.program_id(0) < NT_P, p_ref[...], s_ref[...])


def _proj_kernel(xp_ref, xs_ref, wqkv_ref, wdn_ref, wz_ref, wg_ref, qkv_ref, dn_ref, z_ref, b_ref, a_ref):
    x = _pick(xp_ref, xs_ref)
    xh, xl = _split2(x)
    qkv_ref[...] = _dot(xh, wqkv_ref[...])
    dn_ref[...] = _dot(xh, wdn_ref[...])
    z_ref[...] = _dot(xh, wz_ref[...])
    ng = 2 * N_HEADS
    o1 = _dot(xh, wg_ref[...])
    o2 = _dot(xl, wg_ref[...])
    gates = o1[:, :ng] + o1[:, ng:2 * ng] + o2[:, :ng]
    b_ref[...] = gates[:, :N_HEADS]
    a_ref[...] = gates[:, N_HEADS:]


def _project(x_p, x_s, w_in):
    wqkv = w_in[:, :3 * ATTN_W].astype(BF16)
    wdn = w_in[:, 3 * ATTN_W:3 * ATTN_W + CONV_CH].astype(BF16)
    c0 = 3 * ATTN_W + CONV_CH
    wz = w_in[:, c0:c0 + DN_W].astype(BF16)
    wgh, wgl = _split2(w_in[:, c0 + DN_W:])
    wg = jnp.concatenate([wgh, wgl, jnp.zeros((D_MODEL, LANES - 4 * N_HEADS), BF16)], axis=1)
    row = lambda w: pl.BlockSpec((TM, w), lambda i: (i, 0))
    outs = (3 * ATTN_W, CONV_CH, DN_W, N_HEADS, N_HEADS)
    return pl.pallas_call(
        _proj_kernel,
        name="proj",
        grid=(N_TILES,),
        in_specs=_row2(D_MODEL) + [_full(wqkv.shape), _full(wdn.shape), _full(wz.shape), _full(wg.shape)],
        out_specs=[row(w) for w in outs],
        out_shape=[jax.ShapeDtypeStruct((N_PAD, w), F32) for w in outs],
        compiler_params=_cparams(("arbitrary",)),
    )(x_p, x_s, wqkv, wdn, wz, wg)


QT = 2048
ATTN_UNROLL = 8


def _attn_bias():
    qi = np.arange(N_BACK)[:, None]
    ki = np.arange(2 * N_BACK)[None, :]
    out = np.zeros((2, len(BRANCHES), N_HEADS, N_BACK, 2 * N_BACK), np.float32)
    for var, off in enumerate((N_BACK, 0)):
        dist = qi - ki + off
        valid = (dist >= 0) & (dist <= N_BACK)
        for di, (_, dil) in enumerate(BRANCHES):
            for h in range(N_HEADS):
                slope = 2.0 ** (-8.0 * (h + 1) / N_HEADS) * dil
                out[var, di, h] = np.where(valid, -slope * dist, -1e30)
    return out


def _attn_prompt_kernel(q_ref, k_ref, v_ref, bias_ref, o_ref, acc_ref, m_ref, l_ref):
    qt = pl.program_id(2)
    lane = lax.broadcasted_iota(I32, (1, LANES), 1)
    h0 = lane < HALF
    for di, (_, dil) in enumerate(BRANCHES):
        per_r = QT // (N_BACK * dil)

        def body(i, carry, di=di, dil=dil, per_r=per_r):
            def ds(start, n):
                return pl.ds(start, n) if dil == 1 else pl.ds(start, n, stride=dil)

            blocks = []
            for u in range(ATTN_UNROLL):
                blk = i * ATTN_UNROLL + u
                r = blk // per_r
                jb = blk % per_r
                qloc = r + dil * N_BACK * jb
                first = jnp.logical_and(qt == 0, jb == 0)
                kstart = jnp.where(first, r, qt * QT + qloc - N_BACK * dil)
                blocks.append((qloc, kstart, jnp.where(first, 1, 0)))
            q8 = [q_ref[ds(qloc, N_BACK), :] * (HEAD_DIM ** -0.5) for qloc, _, _ in blocks]
            kb = [k_ref[ds(kstart, 2 * N_BACK), :].astype(BF16) for _, kstart, _ in blocks]
            vv = [v_ref[ds(kstart, 2 * N_BACK), :] for _, kstart, _ in blocks]
            heads = [(u, hh) for u in range(ATTN_UNROLL) for hh in range(2)]
            keep = lambda hh: h0 if hh == 0 else jnp.logical_not(h0)
            s = [_dot_nt(jnp.where(keep(hh), q8[u], 0.0).astype(BF16), kb[u]) + bias_ref[blocks[u][2], di, hh]
                 for u, hh in heads]
            m = [jnp.max(x, axis=-1, keepdims=True) for x in s]
            p = [jnp.exp(x - mm).astype(BF16) for x, mm in zip(s, m)]
            res = [_dot(pp, jnp.where(keep(hh), vv[u], 1.0).astype(BF16)) for pp, (u, hh) in zip(p, heads)]
            for u, (qloc, _, _) in enumerate(blocks):
                rows = ds(qloc, N_BACK)
                acc_ref[di, rows, :] = jnp.where(h0, res[2 * u], res[2 * u + 1])
                l_ref[di, rows, :] = jnp.where(h0, res[2 * u + 1], res[2 * u])
                m_ref[di, rows, :] = jnp.where(h0, m[2 * u], m[2 * u + 1])
            return carry

        lax.fori_loop(0, QT // (N_BACK * ATTN_UNROLL), body, 0)

    mall = m_ref[...]
    mtot = jnp.max(mall, axis=0)
    num = jnp.zeros((QT, LANES), F32)
    den = jnp.zeros((QT, LANES), F32)
    for di in range(len(BRANCHES)):
        w = jnp.exp(mall[di] - mtot)
        num = num + w * acc_ref[di]
        den = den + w * pltpu.roll(l_ref[di], HALF, axis=1)
    o_ref[...] = num / den


def _attn_prompt(qkv):
    bias = jnp.asarray(_attn_bias())
    npair = N_HEADS // 2
    nqt = SEQ // QT
    kv_rows = SEQ
    q_spec = pl.BlockSpec((QT, LANES), lambda b, hp, t: (b * nqt + t, hp))
    k_spec = pl.BlockSpec((kv_rows, LANES), lambda b, hp, t: (b, npair + hp))
    v_spec = pl.BlockSpec((kv_rows, LANES), lambda b, hp, t: (b, 2 * npair + hp))
    bias_spec = pl.BlockSpec((2, len(BRANCHES), 2, N_BACK, 2 * N_BACK), lambda b, hp, t: (0, 0, hp, 0, 0))
    return pl.pallas_call(
        _attn_prompt_kernel,
        name="attn_prompt",
        grid=(N_PROMPT_B, npair, nqt),
        in_specs=[q_spec, k_spec, v_spec, bias_spec],
        out_specs=pl.BlockSpec((QT, LANES), lambda b, hp, t: (b * nqt + t, hp)),
        out_shape=jax.ShapeDtypeStruct((N_P, ATTN_W), F32),
        scratch_shapes=[pltpu.VMEM((len(BRANCHES), QT, LANES), F32)] * 3,
        compiler_params=_cparams(("arbitrary", "arbitrary", "arbitrary")),
    )(qkv, qkv, qkv, bias)


ROWS_S = 8
NEAR = BRANCHES[-2][0]
FAR_PERIOD = BRANCHES[-1][1]
FAR_KEEP = 8
FAR_ROWS = (W_BUF - NEAR) // FAR_PERIOD * FAR_KEEP


def _attn_sample_kernel(q_ref, kn_ref, vn_ref, nk_ref, nv_ref, fk_ref, fv_ref, e_ref, et_ref, slope_ref, o_ref,
                        knear, vnear, kfar, vfar):
    npair = N_HEADS // 2
    for hp in range(npair):
        sl = slice(hp * LANES, (hp + 1) * LANES)
        knear[hp, 0:NEAR, :] = nk_ref[0, :, sl]
        vnear[hp, 0:NEAR, :] = nv_ref[0, :, sl]
        knear[hp, NEAR:NEAR + ROWS_S, :] = kn_ref[0, :, sl]
        vnear[hp, NEAR:NEAR + ROWS_S, :] = vn_ref[0, :, sl]
        kfar[hp] = fk_ref[0, :, :, sl].reshape(FAR_ROWS, LANES)
        vfar[hp] = fv_ref[0, :, :, sl].reshape(FAR_ROWS, LANES)

    def take(ref, start, n, stride):
        idx = pl.ds(start, n) if stride == 1 else pl.ds(start, n, stride=stride)
        return jnp.concatenate([ref[hp, idx, :] for hp in range(npair)], axis=1)

    def rows(near, far, t, window, dil):
        first = W_BUF + t - window
        n_far = max(0, -(-(W_BUF - NEAR - first) // dil))
        parts = []
        if n_far:
            assert dil % FAR_PERIOD == 0 and first % FAR_PERIOD < FAR_KEEP
            far_row = first // FAR_PERIOD * FAR_KEEP + first % FAR_PERIOD
            parts.append(take(far, far_row, n_far, dil // FAR_PERIOD * FAR_KEEP))
        parts.append(take(near, first + n_far * dil - (W_BUF - NEAR), N_BACK - n_far, dil))
        return parts[0] if len(parts) == 1 else jnp.concatenate(parts, axis=0)

    e01 = e_ref[...]
    et01 = et_ref[...]
    slope = slope_ref[...]
    jrev = (N_BACK - lax.broadcasted_iota(I32, (N_BACK, 1), 0)).astype(F32)
    inew = lax.broadcasted_iota(I32, (ROWS_S, 1), 0)
    scale = HEAD_DIM ** -0.5
    knew = take(knear, NEAR, ROWS_S, 1)
    vnew = take(vnear, NEAR, ROWS_S, 1).astype(BF16).astype(F32)

    def scores(kr, qt):
        ph, plo = _split2(kr * qt)
        return (_dot(ph, e01) + _dot(plo, e01)) * scale

    outs = []
    for t in range(DEC_SEQ):
        qt = q_ref[0, t:t + 1, :]
        s_self = jnp.where(inew == t, scores(knew, qt), -1e30)
        s_br = []
        v_br = []
        for window, dil in BRANCHES:
            s_br.append(scores(rows(knear, kfar, t, window, dil), qt) - (slope * dil) * jrev)
            v_br.append(rows(vnear, vfar, t, window, dil))
        m = jnp.max(s_self, axis=0, keepdims=True)
        for s in s_br:
            m = jnp.maximum(m, jnp.max(s, axis=0, keepdims=True))
        pb = (jnp.exp(s_self - m) * float(len(BRANCHES))).astype(BF16)
        den = jnp.sum(pb.astype(F32), axis=0, keepdims=True)
        num = jnp.sum(_dot(pb, et01) * vnew, axis=0, keepdims=True)
        for s, vr in zip(s_br, v_br):
            pb = jnp.exp(s - m).astype(BF16)
            den = den + jnp.sum(pb.astype(F32), axis=0, keepdims=True)
            num = num + jnp.sum(_dot(pb, et01) * vr.astype(BF16).astype(F32), axis=0, keepdims=True)
        denx = _dot_r01(jnp.broadcast_to(den, (ROWS_S, N_HEADS)), et01)[0:1]
        outs.append(num / denx)
    pad = jnp.zeros((ROWS_S - DEC_SEQ, ATTN_W), F32)
    o_ref[0] = jnp.concatenate(outs + [pad], axis=0)


def _head_expand():
    e = np.zeros((ATTN_W, N_HEADS), np.float32)
    for h in range(N_HEADS):
        e[h * HEAD_DIM:(h + 1) * HEAD_DIM, h] = 1.0
    return e


def _attn_sample(q_s, k_s, v_s, cache_k, cache_v):
    e = _head_expand()
    e01 = jnp.asarray(e, BF16)
    et01 = jnp.asarray(e.T, BF16)
    slopes = jnp.asarray([[2.0 ** (-8.0 * (h + 1) / N_HEADS) for h in range(N_HEADS)]], F32)
    periods = lambda c: c.reshape(N_SAMPLE_B, W_BUF // FAR_PERIOD, FAR_PERIOD, ATTN_W)
    row = pl.BlockSpec((1, ROWS_S, ATTN_W), lambda b: (b, 0, 0))
    near = pl.BlockSpec((1, NEAR, ATTN_W), lambda b: (b, W_BUF // NEAR - 1, 0))
    far = pl.BlockSpec((1, (W_BUF - NEAR) // FAR_PERIOD, FAR_KEEP, ATTN_W), lambda b: (b, 0, 0, 0))
    pairs = N_HEADS // 2
    return pl.pallas_call(
        _attn_sample_kernel,
        name="attn_sample",
        grid=(N_SAMPLE_B,),
        in_specs=[row, row, row, near, near, far, far, _full(e01.shape), _full(et01.shape), _full(slopes.shape)],
        out_specs=row,
        out_shape=jax.ShapeDtypeStruct((N_SAMPLE_B, ROWS_S, ATTN_W), F32),
        scratch_shapes=[pltpu.VMEM((pairs, NEAR + ROWS_S, LANES), F32)] * 2 + [pltpu.VMEM((pairs, FAR_ROWS, LANES), F32)] * 2,
        compiler_params=_cparams(("arbitrary",)),
    )(q_s, k_s, v_s, cache_k, cache_v, periods(cache_k), periods(cache_v), e01, et01, slopes)


HALO = 8


def _dn_pre_kernel(dn_ref, halo_ref, b_ref, a_ref, valid_ref, cw_ref, alog_ref, dtb_ref, e_ref, et_ref, ltri_ref,
                   q_ref, k_ref, v_ref, bx_ref, gx_ref, buf_ref):
    ts = dn_ref.shape[0]
    buf_ref[0:HALO, :] = halo_ref[0]
    buf_ref[HALO:HALO + ts, :] = dn_ref[...]
    y = jnp.zeros((ts, CONV_CH), F32)
    for i in range(CONV_W):
        y = y + buf_ref[pl.ds(HALO - (CONV_W - 1) + i, ts), :] * cw_ref[i:i + 1, :]
    y = y * _sigmoid(y)
    valid = valid_ref[...]
    e01 = e_ref[...]
    et01 = et_ref[...]

    def l2n(t):
        ss = _dot_r01(t * t, e01)
        inv = lax.rsqrt(ss + NORM_EPS)
        return t * _dot_r01(inv, et01)

    q_ref[...] = l2n(y[:, :DN_W]) * (HEAD_DIM ** -0.5)
    k_ref[...] = l2n(y[:, DN_W:2 * DN_W]) * valid
    v_ref[...] = y[:, 2 * DN_W:] * valid
    beta = _sigmoid(b_ref[...]) * valid
    sp_in = a_ref[...] + dtb_ref[...]
    softplus = jnp.maximum(sp_in, 0.0) + jnp.log1p(jnp.exp(-jnp.abs(sp_in)))
    g = -jnp.exp(alog_ref[...]) * softplus * valid
    bx_ref[...] = _dot_r01(beta, et01)
    gx_ref[...] = _dot_l01(ltri_ref[...], _dot_r01(g, et01))


def _dn_pre(dn, halo, b_logit, a_logit, valid, conv_w, a_log, dt_bias):
    nt = halo.shape[0]
    rows = nt * TM
    e = _head_expand()
    e01 = jnp.asarray(e, BF16)
    et01 = jnp.asarray(e.T, BF16)
    ii = np.arange(TM)
    ltri = jnp.asarray(((ii[:, None] >= ii[None, :]) & (ii[:, None] // DN_C == ii[None, :] // DN_C)), BF16)
    row = lambda w: pl.BlockSpec((TM, w), lambda i: (i, 0))
    return pl.pallas_call(
        _dn_pre_kernel,
        name="dn_pre",
        grid=(nt,),
        in_specs=[row(CONV_CH), pl.BlockSpec((1, HALO, CONV_CH), lambda i: (i, 0, 0)), row(N_HEADS), row(N_HEADS),
                  row(1), _full(conv_w.shape), _full((1, N_HEADS)), _full((1, N_HEADS)),
                  _full(e01.shape), _full(et01.shape), _full(ltri.shape)],
        out_specs=[row(DN_W)] * 5,
        out_shape=[jax.ShapeDtypeStruct((rows, DN_W), F32)] * 5,
        scratch_shapes=[pltpu.VMEM((HALO + TM, CONV_CH), F32)],
        compiler_params=_cparams(("arbitrary",)),
    )(dn, halo, b_logit, a_logit, valid, conv_w, a_log.reshape(1, N_HEADS), dt_bias.reshape(1, N_HEADS),
      e01, et01, ltri)


def _delta_streams(q, k, v, bx, gcx, s_bd):
    c = DN_C
    n = len(q)
    each = range(n)
    lane = lax.broadcasted_iota(I32, (1, LANES), 1)
    h0 = lane < HALF
    m0 = h0.astype(F32)
    m1 = 1.0 - m0
    ii = lax.broadcasted_iota(I32, (2 * c, 2 * c), 0)
    jj = lax.broadcasted_iota(I32, (2 * c, 2 * c), 1)
    same = jnp.where(ii < c, 0, 1) == jnp.where(jj < c, 0, 1)
    order = jnp.where(same, ii - jj, -1)
    tril = order >= 0
    strict = order > 0
    eye = jnp.where(ii == jj, 1.0, 0.0)

    def stack(x):
        return jnp.concatenate([x * m0, x * m1], axis=0)

    def fold(x2):
        return x2[:c] + x2[c:]

    def decay_of(g):
        gsw = pltpu.roll(g, HALF, axis=1)
        row_b = jnp.concatenate([jnp.where(h0, g, gsw), jnp.where(h0, gsw, g)], axis=0)
        return jnp.where(tril, jnp.exp(jnp.where(tril, row_b - row_b.T, 0.0)), 0.0)

    eg = [jnp.exp(gcx[i]) for i in each]
    kb = [k[i] * bx[i] for i in each]
    glast = [gcx[i][c - 1:c, :] for i in each]
    decay = [decay_of(gcx[i]) for i in each]
    k2 = [stack(k[i]).astype(BF16) for i in each]
    lhs = [jnp.concatenate([stack(kb[i]), stack(q[i])], axis=0).astype(BF16) for i in each]
    aq = [_dot_nt(lhs[i], k2[i]) for i in each]
    a_low = [jnp.where(strict, aq[i][:2 * c] * decay[i], 0.0) for i in each]
    aqk = [jnp.where(tril, aq[i][2 * c:] * decay[i], 0.0).astype(BF16) for i in each]
    tinv = [eye - a_low[i] for i in each]
    apow = [a_low[i].astype(BF16) for i in each]
    for _ in range(5):
        apow = [_dot(apow[i], apow[i]).astype(BF16) for i in each]
        tinv = [tinv[i] + _dot(tinv[i].astype(BF16), apow[i]) for i in each]
    rhs = [jnp.concatenate([stack(v[i] * bx[i]), stack(kb[i] * eg[i])], axis=1).astype(BF16) for i in each]
    uw = [_dot(tinv[i].astype(BF16), rhs[i]) for i in each]
    u = [fold(uw[i][:, :LANES]) for i in each]
    w = [fold(uw[i][:, LANES:]) for i in each]
    ws = [_dot(jnp.concatenate([w[i], q[i] * eg[i]], axis=0).astype(BF16), s_bd[i].astype(BF16)) for i in each]
    v_new = [u[i] - ws[i][:c] for i in each]
    intra = [_dot(aqk[i], stack(v_new[i]).astype(BF16)) for i in each]
    kd = [(k[i] * jnp.exp(glast[i] - gcx[i])).T.astype(BF16) for i in each]
    upd = [_dot(kd[i], v_new[i].astype(BF16)) for i in each]
    o = [ws[i][c:] + fold(intra[i]) for i in each]
    s_new = [s_bd[i] * jnp.exp(glast[i]) + jnp.where(same, upd[i], 0.0) for i in each]
    return o, s_new


def _delta_kernel(q_ref, k_ref, v_ref, bx_ref, gx_ref, s0_ref, o_ref, sout_ref, s_ref):
    ci = pl.program_id(1)
    nb = q_ref.shape[0]
    npair = N_HEADS // 2

    @pl.when(ci == 0)
    def _():
        s_ref[...] = s0_ref[...]

    streams = [(b, hp) for b in range(nb) for hp in range(npair)]
    lanes = lambda hp: slice(hp * LANES, (hp + 1) * LANES)
    take = lambda ref: [ref[b, :, lanes(hp)] for b, hp in streams]
    o, s_new = _delta_streams(take(q_ref), take(k_ref), take(v_ref), take(bx_ref), take(gx_ref),
                              [s_ref[b, hp] for b, hp in streams])
    for i, (b, hp) in enumerate(streams):
        o_ref[b, :, lanes(hp)] = o[i]
        s_ref[b, hp] = s_new[i]

    @pl.when(ci == pl.num_programs(1) - 1)
    def _():
        sout_ref[...] = s_ref[...]


def _delta(q, k, v, bx, gx, s0_bd, bblk):
    nbatch, length, _ = q.shape
    npair = N_HEADS // 2
    seq = pl.BlockSpec((bblk, DN_C, DN_W), lambda g, c: (g, c, 0))
    st = pl.BlockSpec((bblk, npair, LANES, LANES), lambda g, c: (g, 0, 0, 0))
    return pl.pallas_call(
        _delta_kernel,
        name="delta",
        grid=(nbatch // bblk, length // DN_C),
        in_specs=[seq] * 5 + [st],
        out_specs=[seq, st],
        out_shape=[jax.ShapeDtypeStruct((nbatch, length, DN_W), F32),
                   jax.ShapeDtypeStruct((nbatch, npair, LANES, LANES), F32)],
        scratch_shapes=[pltpu.VMEM((bblk, npair, LANES, LANES), F32)],
        compiler_params=_cparams(("arbitrary", "arbitrary")),
    )(q, k, v, bx, gx, s0_bd)


def _state_to_bd(s):
    b = s.shape[0]
    s = s.reshape(b, N_HEADS // 2, 2, HALF, HALF)
    zero = jnp.zeros_like(s[:, :, 0])
    top = jnp.concatenate([s[:, :, 0], zero], axis=-1)
    bot = jnp.concatenate([zero, s[:, :, 1]], axis=-1)
    return jnp.concatenate([top, bot], axis=-2)


def _state_from_bd(s):
    b = s.shape[0]
    return jnp.stack([s[:, :, :HALF, :HALF], s[:, :, HALF:, HALF:]], axis=2).reshape(b, N_HEADS, HALF, HALF)


def _out_kernel(xp_ref, xs_ref, ap_ref, as_ref, dp_ref, ds_ref, z_ref, ng_ref, bavg_ref, wa_ref, wd_ref, g_ref, b_ref,
                o_ref):
    o = _pick(dp_ref, ds_ref)
    oh, ol = _split2(o * o)
    ms = _dot(oh, bavg_ref[...]) + _dot(ol, bavg_ref[...])
    z = z_ref[...]
    dn = o * lax.rsqrt(ms + NORM_EPS) * ng_ref[...] * (z * _sigmoid(z))
    mix = _dot(_pick(ap_ref, as_ref).astype(BF16), wa_ref[...]) + _dot(dn.astype(BF16), wd_ref[...])
    o_ref[...] = _layer_norm(ALPHA * _pick(xp_ref, xs_ref) + mix, g_ref[...], b_ref[...])


def _out_proj(x_p, x_s, att_p, att_s, dn_p, dn_s, z, norm_g, w_out, g, b):
    wa = w_out[:ATTN_W].astype(BF16)
    wd = w_out[ATTN_W:].astype(BF16)
    normg = jnp.tile(norm_g.reshape(1, HEAD_DIM), (1, N_HEADS))
    bavg = jnp.asarray(np.kron(np.eye(N_HEADS), np.ones((HALF, HALF))) / HALF, BF16)
    row = lambda w: pl.BlockSpec((TM, w), lambda i: (i, 0))
    return pl.pallas_call(
        _out_kernel,
        name="out_proj",
        grid=(N_TILES,),
        in_specs=_row2(D_MODEL) + _row2(ATTN_W) + _row2(DN_W) + [
            row(DN_W), _full(normg.shape), _full(bavg.shape),
            _full(wa.shape), _full(wd.shape), _full((1, D_MODEL)), _full((1, D_MODEL))],
        out_specs=row(D_MODEL),
        out_shape=jax.ShapeDtypeStruct((N_PAD, D_MODEL), F32),
        compiler_params=_cparams(("arbitrary",)),
    )(x_p, x_s, att_p, att_s, dn_p, dn_s, z, normg, bavg, wa, wd, g.reshape(1, -1), b.reshape(1, -1))


N_GROUPS = 8
GROUP = N_EXPERTS // N_GROUPS
SLOT_RADIX = 64
_ET_SPEC = pl.BlockSpec((1, 2 * N_EXPERTS, TM), lambda i, *_: (i, 0, 0))
_SEG_SPEC = pl.BlockSpec((1, 1, 2 * N_EXPERTS), lambda i, *_: (i, 0, 0))
TOPK_GROUPS = 4
NEG = -jnp.inf


def _route_kernel(x_ref, wh_ref, wl_ref, bias_ref, upper_ref, lows_ref, st_ref, ge_ref, cnt_ref):
    t = pl.program_id(0)
    xh, xl = _split2(x_ref[...])
    wh = wh_ref[...]
    logits = _dot_nt(wh, xh) + _dot_nt(wh, xl) + _dot_nt(wl_ref[...], xh)
    scores = _sigmoid(logits)
    choice = scores + bias_ref[...]
    i8 = lax.broadcasted_iota(I32, (GROUP, TM), 0)

    def first_max(vals):
        m = jnp.max(vals, axis=0, keepdims=True)
        idx = jnp.min(jnp.where(vals == m, i8, GROUP), axis=0, keepdims=True)
        return m, idx

    pieces = [choice[g * GROUP:(g + 1) * GROUP, :] for g in range(N_GROUPS)]
    gsc = jnp.zeros((N_GROUPS, TM), F32)
    for g in range(N_GROUPS):
        m1, idx1 = first_max(pieces[g])
        m2 = jnp.max(jnp.where(i8 == idx1, NEG, pieces[g]), axis=0, keepdims=True)
        gsc = jnp.where(i8 == g, m1 + m2, gsc)
    gsel = jnp.zeros((N_GROUPS, TM), F32)
    for _ in range(TOPK_GROUPS):
        _, idx = first_max(gsc)
        hit = i8 == idx
        gsel = jnp.where(hit, 1.0, gsel)
        gsc = jnp.where(hit, NEG, gsc)
    masked = [jnp.where(gsel[g:g + 1, :] > 0.5, pieces[g], NEG) for g in range(N_GROUPS)]
    eidx = [i8 + g * GROUP for g in range(N_GROUPS)]
    member = [jnp.zeros((GROUP, TM), F32) for _ in range(N_GROUPS)]
    for _ in range(TOP_K):
        mm = masked[0]
        for g in range(1, N_GROUPS):
            mm = jnp.maximum(mm, masked[g])
        m = jnp.max(mm, axis=0, keepdims=True)
        cand = jnp.where(masked[0] == m, eidx[0], N_EXPERTS)
        for g in range(1, N_GROUPS):
            cand = jnp.minimum(cand, jnp.where(masked[g] == m, eidx[g], N_EXPERTS))
        idx = jnp.min(cand, axis=0, keepdims=True)
        for g in range(N_GROUPS):
            hit = eidx[g] == idx
            member[g] = jnp.where(hit, 1.0, member[g])
            masked[g] = jnp.where(hit, NEG, masked[g])

    tok = t * TM + lax.broadcasted_iota(I32, (1, TM), 1)
    real = jnp.where(tok < N_REAL, 1.0, 0.0)
    mem = jnp.concatenate(member, axis=0) * real
    sel = mem * scores
    gate = (sel / jnp.sum(sel + (1.0 - real), axis=0, keepdims=True) * ROUTED_SCALE).astype(BF16)
    ge_ref[0] = jnp.concatenate([gate, jnp.zeros_like(gate)], axis=0)
    memb = mem.astype(BF16)
    rank = _dot(memb, upper_ref[...])
    cnt = jnp.sum(mem, axis=1, keepdims=True)
    cpad = jnp.floor((cnt + (SEG_ALIGN - 1)) * (1.0 / SEG_ALIGN))
    loc = _dot(lows_ref[...], jnp.broadcast_to(cpad, (N_EXPERTS, TM)).astype(BF16)) * SEG_ALIGN
    slot_e = jnp.where(mem > 0.5, loc + rank, float(M_T))
    slot_hi = jnp.floor(slot_e * (1.0 / SLOT_RADIX)) * SLOT_RADIX
    st_ref[0] = jnp.concatenate([slot_hi, slot_e - slot_hi + 1.0], axis=0).astype(BF16)
    ones = jnp.ones((GROUP, TM), BF16)
    cnt_ref[0] = _dot_nt(ones, memb).astype(I32)


def _route(x1, w_router, router_bias):
    wt = w_router.T
    wh, wl = _split2(wt)
    ii = np.arange(TM)
    upper = jnp.asarray(ii[:, None] < ii[None, :], BF16)
    ee = np.arange(N_EXPERTS)
    lows = jnp.asarray(ee[:, None] > ee[None, :], BF16)
    return pl.pallas_call(
        _route_kernel,
        name="route",
        grid=(N_TILES,),
        in_specs=[pl.BlockSpec((TM, D_MODEL), lambda i: (i, 0)), _full(wh.shape), _full(wl.shape),
                  _full((N_EXPERTS, 1)), _full(upper.shape), _full(lows.shape)],
        out_specs=[_ET_SPEC, _ET_SPEC, pl.BlockSpec((1, GROUP, N_EXPERTS), lambda i: (i, 0, 0))],
        out_shape=[jax.ShapeDtypeStruct((N_TILES, 2 * N_EXPERTS, TM), BF16)] * 2 + [
            jax.ShapeDtypeStruct((N_TILES, GROUP, N_EXPERTS), I32)],
        compiler_params=_cparams(("arbitrary",)),
    )(x1, wh, wl, router_bias.reshape(N_EXPERTS, 1), upper, lows)


def _segment_tables(cnt):
    seg = (cnt + SEG_ALIGN - 1) // SEG_ALIGN
    cpad = seg * SEG_ALIGN
    loc_off = jnp.cumsum(cpad, axis=1) - cpad
    rows_e = jnp.sum(cpad, axis=0)
    blocks_e = (rows_e + BM - 1) // BM
    bend = jnp.cumsum(blocks_e)
    bstart = bend - blocks_e
    glob_off = (bstart * BM)[None, :] + jnp.cumsum(cpad, axis=0) - cpad
    nb = bend[-1]
    pad_off = bstart * BM + rows_e
    pad_seg = (blocks_e * BM - rows_e) // SEG_ALIGN
    i32 = lambda a: a.reshape(-1).astype(I32)
    bounds = lambda a: jnp.tile(a.astype(F32), (1, 2)).reshape(N_TILES, 1, 2 * N_EXPERTS)
    return (i32(seg), i32(loc_off), i32(glob_off), i32(pad_seg), i32(pad_off), i32(bstart), i32(blocks_e), i32(nb),
            bounds(loc_off), bounds(loc_off + cpad))


def _rows_copy(src_of, dst_of, lo, go, rows, sem):
    lo, go, rows = (v if isinstance(v, int) else pl.multiple_of(v, SEG_ALIGN) for v in (lo, go, rows))
    return pltpu.make_async_copy(src_of(lo, go, rows), dst_of(lo, go, rows), sem)


def _segment_copies(seg_ref, loc_ref, glob_ref, t, src_of, dst_of, sem):
    def body(e, carry):
        k = seg_ref[t * N_EXPERTS + e]

        @pl.when(k > 0)
        def _():
            _rows_copy(src_of, dst_of, loc_ref[t * N_EXPERTS + e], glob_ref[t * N_EXPERTS + e], k * SEG_ALIGN, sem).start()

        return carry

    lax.fori_loop(0, N_EXPERTS, body, 0)


def _wait_rows(src_of, dst_of, rows, sem):
    @pl.when(rows > 0)
    def _():
        _rows_copy(src_of, dst_of, 0, 0, rows, sem).wait()


def _local_rows(m0, rows, lo, hi, slot_tab):
    mcol = (lax.broadcasted_iota(I32, (rows, 2 * N_EXPERTS), 0) + m0).astype(F32)
    e01 = jnp.where(mcol >= lo, jnp.where(mcol < hi, 1.0, 0.0), 0.0).astype(BF16)
    mrow = (lax.broadcasted_iota(I32, (rows, TM), 0) + (m0 + 1)).astype(F32)
    return _dot(e01, slot_tab) == mrow, e01


def _row_spans(used):
    yield None, 0, M_MAIN
    for m0 in range(M_MAIN, M_T, M_CHUNK):
        yield m0 < used, m0, M_CHUNK


def _tile_rows(seg_ref, loc_ref, t):
    last = t * N_EXPERTS + N_EXPERTS - 1
    return loc_ref[last] + seg_ref[last] * SEG_ALIGN


def _dispatch_kernel(seg_ref, loc_ref, glob_ref, pseg_ref, poff_ref, nb_ref, x_ref, st_ref, lo_ref, hi_ref,
                     xs_ref, buf_ref, sem):
    t = pl.program_id(0)
    s = t % 2
    used = _tile_rows(seg_ref, loc_ref, t)
    for live, m0, rows in _row_spans(used):
        def sort_rows(m0=m0, rows=rows):
            hit, _ = _local_rows(m0, rows, lo_ref[0], hi_ref[0], st_ref[0])
            onehot = jnp.where(hit, 1.0, 0.0).astype(BF16)
            buf_ref[s, m0:m0 + rows, :] = _dot(onehot, x_ref[...].astype(BF16)).astype(BF16)

        if live is None:
            sort_rows()
        else:
            pl.when(live)(sort_rows)

    src = lambda lo, go, rows: buf_ref.at[s, pl.ds(lo, rows), :]
    dst = lambda lo, go, rows: xs_ref.at[pl.ds(go, rows), :]
    _segment_copies(seg_ref, loc_ref, glob_ref, t, src, dst, sem.at[s])

    @pl.when(t > 0)
    def _():
        prev = lambda lo, go, rows: buf_ref.at[1 - s, pl.ds(lo, rows), :]
        _wait_rows(prev, dst, _tile_rows(seg_ref, loc_ref, t - 1), sem.at[1 - s])

    @pl.when(t == N_TILES - 1)
    def _():
        _wait_rows(src, dst, used, sem.at[s])
        buf_ref[s, 0:BM, :] = jnp.zeros((BM, D_MODEL), BF16)
        zeros = lambda lo, go, rows: buf_ref.at[s, pl.ds(0, rows), :]

        def fill(wait):
            def pad(e, carry):
                k = pseg_ref[e]

                @pl.when(k > 0)
                def _():
                    cp = _rows_copy(zeros, dst, 0, poff_ref[e], k * SEG_ALIGN, sem.at[s])
                    cp.wait() if wait else cp.start()

                return carry

            lax.fori_loop(0, N_EXPERTS, pad, 0)

            def tail(b, carry):
                cp = _rows_copy(zeros, dst, 0, b * BM, BM, sem.at[s])
                cp.wait() if wait else cp.start()
                return carry

            lax.fori_loop(nb_ref[0], NB_MAX, tail, 0)

        fill(False)
        fill(True)


def _dispatch(x1, slot_tab, seg_lo, seg_hi, tables):
    grid_spec = pltpu.PrefetchScalarGridSpec(
        num_scalar_prefetch=len(tables),
        grid=(N_TILES,),
        in_specs=[pl.BlockSpec((TM, D_MODEL), lambda i, *_: (i, 0)), _ET_SPEC, _SEG_SPEC, _SEG_SPEC],
        out_specs=pl.BlockSpec(memory_space=pl.ANY),
        scratch_shapes=[pltpu.VMEM((2, M_T, D_MODEL), BF16), pltpu.SemaphoreType.DMA((2,))],
    )
    return pl.pallas_call(
        _dispatch_kernel,
        name="dispatch",
        grid_spec=grid_spec,
        out_shape=jax.ShapeDtypeStruct((NB_MAX * BM, D_MODEL), BF16),
        compiler_params=_cparams(("arbitrary",)),
    )(*tables, x1, slot_tab, seg_lo, seg_hi)


def _expert_kernel(bstart_ref, bcount_ref, nb_ref, xs_ref, wg_ref, wu_ref, wd_ref, ys_ref,
                   xbuf, ybuf, wgu_s, wd_s, sem_in, sem_out):
    e = pl.program_id(0)
    nb = nb_ref[0]
    b0 = bstart_ref[e]
    n = bcount_ref[e]

    def rows_of(ref, g):
        return ref.at[pl.ds(pl.multiple_of(g * BM, BM), BM), :]

    def x_copy(g):
        return pltpu.make_async_copy(rows_of(xs_ref, g), xbuf.at[g % EXPERT_SLOTS], sem_in.at[g % EXPERT_SLOTS])

    def y_copy(g):
        return pltpu.make_async_copy(ybuf.at[g % EXPERT_SLOTS], rows_of(ys_ref, g), sem_out.at[g % EXPERT_SLOTS])

    @pl.when(e == 0)
    def _():
        for g in range(EXPERT_SLOTS - 1):
            @pl.when(g < nb)
            def _(g=g):
                x_copy(g).start()

    wgu_s[:, :EXPERT_DIM] = wg_ref[0].astype(BF16)
    wgu_s[:, EXPERT_DIM:] = wu_ref[0].astype(BF16)
    wd_s[...] = wd_ref[0].astype(BF16)

    def block(i, carry):
        g = b0 + i
        slot = g % EXPERT_SLOTS

        @pl.when(g + EXPERT_SLOTS - 1 < nb)
        def _():
            x_copy(g + EXPERT_SLOTS - 1).start()

        x_copy(g).wait()

        @pl.when(g >= EXPERT_SLOTS)
        def _():
            y_copy(g - EXPERT_SLOTS).wait()

        h = _dot(xbuf[slot], wgu_s[...])
        gate = h[:, :EXPERT_DIM]
        a = (gate * _sigmoid(gate) * h[:, EXPERT_DIM:]).astype(BF16)
        ybuf[slot] = _dot(a, wd_s[...]).astype(BF16)
        y_copy(g).start()
        return carry

    lax.fori_loop(0, n, block, 0)

    @pl.when(e == N_EXPERTS - 1)
    def _():
        for k in range(EXPERT_SLOTS, 0, -1):
            @pl.when(nb >= k)
            def _(k=k):
                y_copy(nb - k).wait()

        ybuf[0] = jnp.zeros((BM, D_MODEL), BF16)

        def tail(wait):
            def body(g, carry):
                cp = pltpu.make_async_copy(ybuf.at[0], rows_of(ys_ref, g), sem_out.at[0])
                cp.wait() if wait else cp.start()
                return carry

            lax.fori_loop(nb, NB_MAX, body, 0)

        tail(False)
        tail(True)


def _experts(xs, we_gate, we_up, we_down, bstart, bcount, nb):
    weight = lambda shape: pl.BlockSpec((1,) + shape, lambda e, *_: (e, 0, 0))
    grid_spec = pltpu.PrefetchScalarGridSpec(
        num_scalar_prefetch=3,
        grid=(N_EXPERTS,),
        in_specs=[pl.BlockSpec(memory_space=pl.ANY), weight((D_MODEL, EXPERT_DIM)), weight((D_MODEL, EXPERT_DIM)),
                  weight((EXPERT_DIM, D_MODEL))],
        out_specs=pl.BlockSpec(memory_space=pl.ANY),
        scratch_shapes=[pltpu.VMEM((EXPERT_SLOTS, BM, D_MODEL), BF16), pltpu.VMEM((EXPERT_SLOTS, BM, D_MODEL), BF16),
                        pltpu.VMEM((D_MODEL, 2 * EXPERT_DIM), BF16), pltpu.VMEM((EXPERT_DIM, D_MODEL), BF16),
                        pltpu.SemaphoreType.DMA((EXPERT_SLOTS,)), pltpu.SemaphoreType.DMA((EXPERT_SLOTS,))],
    )
    return pl.pallas_call(
        _expert_kernel,
        name="expert",
        grid_spec=grid_spec,
        out_shape=jax.ShapeDtypeStruct((NB_MAX * BM, D_MODEL), BF16),
        compiler_params=_cparams(("arbitrary",)),
    )(bstart, bcount, nb, xs, we_gate, we_up, we_down)


def _combine_kernel(seg_ref, loc_ref, glob_ref, x_ref, st_ref, ge_ref, lo_ref, hi_ref, pp_ref, ps_ref, ys_ref,
                    wsgu_ref, wsd_ref, g2_ref, b2_ref, wpg_ref, wpp_ref, g3_ref, b3_ref, op_ref, os_ref,
                    buf_ref, acc_ref, sem):
    t = pl.program_id(0)
    s = t % 2
    used = _tile_rows(seg_ref, loc_ref, t)
    src = lambda lo, go, rows: ys_ref.at[pl.ds(go, rows), :]
    into = lambda slot: (lambda lo, go, rows: buf_ref.at[slot, pl.ds(lo, rows), :])

    @pl.when(t == 0)
    def _():
        buf_ref[...] = jnp.zeros((2, M_T, D_MODEL), BF16)
        _segment_copies(seg_ref, loc_ref, glob_ref, t, src, into(s), sem.at[s])

    @pl.when(t + 1 < N_TILES)
    def _():
        _segment_copies(seg_ref, loc_ref, glob_ref, t + 1, src, into(1 - s), sem.at[1 - s])

    _wait_rows(src, into(s), used, sem.at[s])

    for live, m0, rows in _row_spans(used):
        def gather_rows(m0=m0, rows=rows):
            hit, e01 = _local_rows(m0, rows, lo_ref[0], hi_ref[0], st_ref[0])
            pg = jnp.where(hit, _dot(e01, ge_ref[0]), 0.0).astype(BF16)
            return lax.dot_general(pg, buf_ref[s, m0:m0 + rows, :], (((0,), (0,)), ((), ())),
                                   preferred_element_type=F32)

        if live is None:
            acc_ref[...] = gather_rows()
        else:
            @pl.when(live)
            def _(gather_rows=gather_rows):
                acc_ref[...] += gather_rows()

    x = x_ref[...]
    routed = acc_ref[...]
    xb = x.astype(BF16)
    h = _dot(xb, wsgu_ref[...])
    g = h[:, :EXPERT_DIM]
    shared = _dot((g * _sigmoid(g) * h[:, EXPERT_DIM:]).astype(BF16), wsd_ref[...])
    x2 = _layer_norm(ALPHA * x + (routed + shared), g2_ref[...], b2_ref[...])
    ple = _sigmoid(_dot(x2.astype(BF16), wpg_ref[...])) * _dot(_pick(pp_ref, ps_ref).astype(BF16), wpp_ref[...])
    y = _layer_norm(ALPHA * x2 + ple, g3_ref[...], b3_ref[...])

    @pl.when(t < NT_P)
    def _():
        op_ref[...] = y

    @pl.when(t >= NT_P)
    def _():
        os_ref[...] = y


def _combine(x1, slot_tab, gate_e, seg_lo, seg_hi, p_p, p_s, ys, seg, loc_off, glob_off, ws_gate, ws_up, ws_down,
             ln2_g, ln2_b, w_ple_gate, w_ple_proj, ln3_g, ln3_b):
    wsgu = jnp.concatenate([ws_gate, ws_up], axis=1).astype(BF16)
    wsd = ws_down.astype(BF16)
    wpg = w_ple_gate.astype(BF16)
    wpp = w_ple_proj.astype(BF16)
    row = lambda w: pl.BlockSpec((TM, w), lambda i, *_: (i, 0))
    full = lambda shape: pl.BlockSpec(shape, lambda i, *_: (0,) * len(shape))
    vec = full((1, D_MODEL))
    grid_spec = pltpu.PrefetchScalarGridSpec(
        num_scalar_prefetch=3,
        grid=(N_TILES,),
        in_specs=[row(D_MODEL), _ET_SPEC, _ET_SPEC, _SEG_SPEC, _SEG_SPEC] + _row2(PLE_DIM) + [
            pl.BlockSpec(memory_space=pl.ANY),
            full(wsgu.shape), full(wsd.shape), vec, vec, full(wpg.shape), full(wpp.shape), vec, vec],
        out_specs=_row2(D_MODEL),
        scratch_shapes=[pltpu.VMEM((2, M_T, D_MODEL), BF16), pltpu.VMEM((TM, D_MODEL), F32),
                        pltpu.SemaphoreType.DMA((2,))],
    )
    r = lambda a: a.reshape(1, -1)
    return pl.pallas_call(
        _combine_kernel,
        name="combine",
        grid_spec=grid_spec,
        out_shape=[jax.ShapeDtypeStruct((N_P, D_MODEL), F32), jax.ShapeDtypeStruct((N_SP, D_MODEL), F32)],
        compiler_params=_cparams(("arbitrary",)),
    )(seg, loc_off, glob_off, x1, slot_tab, gate_e, seg_lo, seg_hi, p_p, p_s, ys, wsgu, wsd, r(ln2_g), r(ln2_b),
      wpg, wpp, r(ln3_g), r(ln3_b))


def _pad_rows(a, rows):
    return jnp.concatenate([a, jnp.zeros((rows - a.shape[0],) + a.shape[1:], a.dtype)], axis=0)


def kernel(x_prompt, x_sample, cache_k_win, cache_v_win, state_conv, state_delta, p_prompt, p_sample, w_in, conv_w, a_log, dt_bias, dn_norm_g, w_out, ln1_g, ln1_b, w_router, router_bias, we_gate, we_up, we_down, ws_gate, ws_up, ws_down, ln2_g, ln2_b, w_ple_gate, w_ple_proj, ln3_g, ln3_b):
    x_p = x_prompt.reshape(N_P, D_MODEL)
    x_s = _pad_rows(x_sample.reshape(N_S, D_MODEL), N_SP)
    p_p = p_prompt[0].reshape(N_P, PLE_DIM)
    p_s = _pad_rows(p_sample[0].reshape(N_S, PLE_DIM), N_SP)

    qkv, dn_in, z, b_logit, a_logit = _project(x_p, x_s, w_in[0])

    att_p = _attn_prompt(qkv)
    qkv_s = qkv[N_P:N_REAL].reshape(N_SAMPLE_B, DEC_SEQ, 3 * ATTN_W)
    qkv_s8 = jnp.pad(qkv_s, ((0, 0), (0, ROWS_S - DEC_SEQ), (0, 0)))
    ck = cache_k_win[0].reshape(N_SAMPLE_B, W_BUF, ATTN_W)
    cv = cache_v_win[0].reshape(N_SAMPLE_B, W_BUF, ATTN_W)
    att_s = _attn_sample(qkv_s8[:, :, :ATTN_W], qkv_s8[:, :, ATTN_W:2 * ATTN_W], qkv_s8[:, :, 2 * ATTN_W:], ck, cv)
    att_s = _pad_rows(att_s[:, :DEC_SEQ].reshape(N_S, ATTN_W), N_SP)

    tails = dn_in.reshape(N_TILES, TM, CONV_CH)[:NT_P - 1, TM - HALO:]
    halo_p = jnp.concatenate([jnp.zeros((1, HALO, CONV_CH), F32), tails], axis=0)
    seq_start = (jnp.arange(NT_P) % (SEQ // TM) == 0)[:, None, None]
    halo_p = jnp.where(seq_start, 0.0, halo_p)
    q_p, k_p, v_p, bx_p, gx_p = _dn_pre(dn_in, halo_p, b_logit, a_logit, jnp.ones((N_P, 1), F32),
                                        conv_w[0], a_log[0], dt_bias[0])
    shp = (N_PROMPT_B, SEQ, DN_W)
    s0_p = jnp.zeros((N_PROMPT_B, N_HEADS // 2, LANES, LANES), F32)
    o_p, s_p = _delta(q_p.reshape(shp), k_p.reshape(shp), v_p.reshape(shp), bx_p.reshape(shp), gx_p.reshape(shp),
                      s0_p, N_PROMPT_B)

    def seq_pad(tok, state=None):
        w = tok.shape[-1]
        tok = tok.reshape(N_SAMPLE_B, DEC_SEQ, w)
        head = jnp.zeros((N_SAMPLE_B, S_TOK0, w), F32)
        if state is not None:
            head = head.at[:, S_TOK0 - (CONV_W - 1):].set(state)
        tail = jnp.zeros((N_SAMPLE_B, SEQ_S - S_TOK0 - DEC_SEQ, w), F32)
        return jnp.concatenate([head, tok, tail], axis=1).reshape(N_SAMPLE_B * SEQ_S, w)

    dn_tok_s = dn_in[N_P:N_REAL]
    dn_s = seq_pad(dn_tok_s, state_conv[0])
    rows_s = N_SAMPLE_B * SEQ_S
    valid_s = seq_pad(jnp.ones((N_S, 1), F32))
    halo_s = jnp.zeros((rows_s // TM, HALO, CONV_CH), F32)
    q_s, k_s, v_s, bx_s, gx_s = _dn_pre(dn_s, halo_s, seq_pad(b_logit[N_P:N_REAL]), seq_pad(a_logit[N_P:N_REAL]),
                                        valid_s, conv_w[0], a_log[0], dt_bias[0])
    shs = (N_SAMPLE_B, SEQ_S, DN_W)
    o_s, s_s = _delta(q_s.reshape(shs), k_s.reshape(shs), v_s.reshape(shs), bx_s.reshape(shs), gx_s.reshape(shs),
                      _state_to_bd(state_delta[0]), 4)
    dn_o_s = _pad_rows(o_s[:, S_TOK0:S_TOK0 + DEC_SEQ].reshape(N_S, DN_W), N_SP)

    x1 = _out_proj(x_p, x_s, att_p, att_s, o_p.reshape(N_P, DN_W), dn_o_s, z, dn_norm_g[0], w_out[0], ln1_g[0], ln1_b[0])

    slot_tab, gate_e, cnt = _route(x1, w_router[0], router_bias[0])
    seg, loc_off, glob_off, pad_seg, pad_off, bstart, bcount, nb, seg_lo, seg_hi = _segment_tables(cnt[:, 0, :])
    xs = _dispatch(x1, slot_tab, seg_lo, seg_hi, (seg, loc_off, glob_off, pad_seg, pad_off, nb))
    ys = _experts(xs, we_gate[0], we_up[0], we_down[0], bstart, bcount, nb)
    y_p, y_s = _combine(x1, slot_tab, gate_e, seg_lo, seg_hi, p_p, p_s, ys, seg, loc_off, glob_off,
                        ws_gate[0], ws_up[0], ws_down[0], ln2_g[0], ln2_b[0], w_ple_gate[0], w_ple_proj[0],
                        ln3_g[0], ln3_b[0])

    y_prompt = y_p.reshape(N_PROMPT_B, SEQ, D_MODEL)
    y_sample = y_s[:N_S].reshape(N_SAMPLE_B, DEC_SEQ, D_MODEL)
    heads = (N_HEADS, HEAD_DIM)
    win = lambda c0: jnp.stack([qkv[(b + 1) * SEQ - W_BUF:(b + 1) * SEQ, c0:c0 + ATTN_W] for b in range(N_PROMPT_B)])
    k_pr = win(ATTN_W).reshape(N_PROMPT_B, W_BUF, *heads)
    v_pr = win(2 * ATTN_W).reshape(N_PROMPT_B, W_BUF, *heads)
    conv_p = jnp.stack([dn_in[(b + 1) * SEQ - (CONV_W - 1):(b + 1) * SEQ] for b in range(N_PROMPT_B)])
    k_new = qkv_s[:, :, ATTN_W:2 * ATTN_W].reshape(N_SAMPLE_B, DEC_SEQ, *heads)
    v_new = qkv_s[:, :, 2 * ATTN_W:].reshape(N_SAMPLE_B, DEC_SEQ, *heads)
    k_sm = jnp.concatenate([cache_k_win[0][:, DEC_SEQ:], k_new], axis=1)
    v_sm = jnp.concatenate([cache_v_win[0][:, DEC_SEQ:], v_new], axis=1)
    conv_s = dn_tok_s.reshape(N_SAMPLE_B, DEC_SEQ, CONV_CH)[:, DEC_SEQ - (CONV_W - 1):]
    return (y_prompt, y_sample, k_pr[None], v_pr[None], conv_p[None], _state_from_bd(s_p)[None],
            k_sm[None], v_sm[None], conv_s[None], _state_from_bd(s_s)[None])
```

```python
import functools

import jax
import jax.numpy as jnp
import numpy as np
from jax import lax
from jax.experimental import pallas as pl
from jax.experimental.pallas import tpu as pltpu

F32 = jnp.float32
BF16 = jnp.bfloat16
I32 = jnp.int32

D_MODEL = 1024
N_PROMPT_B, SEQ = 2, 8192
N_SAMPLE_B, DEC_SEQ = 32, 4
W_BUF = 2048
N_HEADS = 8
HEAD_DIM = 64
ATTN_W = 512
CONV_CH = 1536
CONV_W = 4
DN_W = 512
N_EXPERTS = 64
TOP_K = 8
EXPERT_DIM = 256
PLE_DIM = 256
BRANCHES = ((128, 1), (512, 4), (2048, 16))
N_BACK = 128
ROUTED_SCALE = 2.5
LN_EPS = 1e-5
NORM_EPS = 1e-6
ALPHA = 2.0 ** 0.25

LANES = 128
HALF = 64
VMEM_LIMIT = 56 * 1024 * 1024

N_P = N_PROMPT_B * SEQ
N_S = N_SAMPLE_B * DEC_SEQ
N_REAL = N_P + N_S
N_PAD = N_P + 512
TM = 256
N_TILES = N_PAD // TM
NT_P = N_P // TM
N_SP = N_PAD - N_P

SEG_ALIGN = 16
M_T = 3072
M_CHUNK = 256
M_MAIN = 2560
BM = 512
EXPERT_SLOTS = 4
R_MAX = N_PAD * TOP_K + N_TILES * N_EXPERTS * SEG_ALIGN + N_EXPERTS * (BM - 1)
NB_MAX = -(-R_MAX // BM)

DN_C = 64
SEQ_S = 64
S_TOK0 = 8


def _dot(a, b):
    return jnp.dot(a, b, preferred_element_type=F32)


def _dot_nt(a, b):
    return lax.dot_general(a, b, (((1,), (1,)), ((), ())), preferred_element_type=F32)


def _split2(x):
    hi = x.astype(BF16)
    lo = (x - hi.astype(F32)).astype(BF16)
    return hi, lo


def _dot_l01(m01, x):
    hi, lo = _split2(x)
    return _dot(m01, hi) + _dot(m01, lo)


def _dot_r01(x, m01):
    hi, lo = _split2(x)
    return _dot(hi, m01) + _dot(lo, m01)


def _sigmoid(x):
    return 1.0 / (1.0 + jnp.exp(-x))


def _layer_norm(x, g, b):
    mu = jnp.mean(x, axis=-1, keepdims=True)
    xc = x - mu
    var = jnp.mean(xc * xc, axis=-1, keepdims=True)
    return xc * lax.rsqrt(var + LN_EPS) * g + b


def _cparams(sem=None):
    return pltpu.CompilerParams(dimension_semantics=sem, vmem_limit_bytes=VMEM_LIMIT)


def _full(shape):
    return pl.BlockSpec(shape, lambda *_: (0,) * len(shape))


def _row2(w):
    return [pl.BlockSpec((TM, w), lambda i, *_: (jnp.minimum(i, NT_P - 1), 0)),
            pl.BlockSpec((TM, w), lambda i, *_: (jnp.maximum(i - NT_P, 0), 0))]


def _pick(p_ref, s_ref):
    return jnp.where(pl.program_id(0) < NT_P, p_ref[...], s_ref[...])


def _proj_kernel(xp_ref, xs_ref, wqkv_ref, wdn_ref, wz_ref, wg_ref, qkv_ref, dn_ref, z_ref, b_ref, a_ref):
    x = _pick(xp_ref, xs_ref)
    xh, xl = _split2(x)
    qkv_ref[...] = _dot(xh, wqkv_ref[...])
    dn_ref[...] = _dot(xh, wdn_ref[...])
    z_ref[...] = _dot(xh, wz_ref[...])
    ng = 2 * N_HEADS
    o1 = _dot(xh, wg_ref[...])
    o2 = _dot(xl, wg_ref[...])
    gates = o1[:, :ng] + o1[:, ng:2 * ng] + o2[:, :ng]
    b_ref[...] = gates[:, :N_HEADS]
    a_ref[...] = gates[:, N_HEADS:]


def _project(x_p, x_s, w_in):
    wqkv = w_in[:, :3 * ATTN_W].astype(BF16)
    wdn = w_in[:, 3 * ATTN_W:3 * ATTN_W + CONV_CH].astype(BF16)
    c0 = 3 * ATTN_W + CONV_CH
    wz = w_in[:, c0:c0 + DN_W].astype(BF16)
    wgh, wgl = _split2(w_in[:, c0 + DN_W:])
    wg = jnp.concatenate([wgh, wgl, jnp.zeros((D_MODEL, LANES - 4 * N_HEADS), BF16)], axis=1)
    row = lambda w: pl.BlockSpec((TM, w), lambda i: (i, 0))
    outs = (3 * ATTN_W, CONV_CH, DN_W, N_HEADS, N_HEADS)
    return pl.pallas_call(
        _proj_kernel,
        name="proj",
        grid=(N_TILES,),
        in_specs=_row2(D_MODEL) + [_full(wqkv.shape), _full(wdn.shape), _full(wz.shape), _full(wg.shape)],
        out_specs=[row(w) for w in outs],
        out_shape=[jax.ShapeDtypeStruct((N_PAD, w), F32) for w in outs],
        compiler_params=_cparams(("arbitrary",)),
    )(x_p, x_s, wqkv, wdn, wz, wg)


QT = 2048
ATTN_UNROLL = 8


def _attn_bias():
    qi = np.arange(N_BACK)[:, None]
    ki = np.arange(2 * N_BACK)[None, :]
    out = np.zeros((2, len(BRANCHES), N_HEADS, N_BACK, 2 * N_BACK), np.float32)
    for var, off in enumerate((N_BACK, 0)):
        dist = qi - ki + off
        valid = (dist >= 0) & (dist <= N_BACK)
        for di, (_, dil) in enumerate(BRANCHES):
            for h in range(N_HEADS):
                slope = 2.0 ** (-8.0 * (h + 1) / N_HEADS) * dil
                out[var, di, h] = np.where(valid, -slope * dist, -1e30)
    return out


def _attn_prompt_kernel(q_ref, k_ref, v_ref, bias_ref, o_ref, acc_ref, m_ref, l_ref):
    qt = pl.program_id(2)
    lane = lax.broadcasted_iota(I32, (1, LANES), 1)
    h0 = lane < HALF
    for di, (_, dil) in enumerate(BRANCHES):
        per_r = QT // (N_BACK * dil)

        def body(i, carry, di=di, dil=dil, per_r=per_r):
            def ds(start, n):
                return pl.ds(start, n) if dil == 1 else pl.ds(start, n, stride=dil)

            blocks = []
            for u in range(ATTN_UNROLL):
                blk = i * ATTN_UNROLL + u
                r = blk // per_r
                jb = blk % per_r
                qloc = r + dil * N_BACK * jb
                first = jnp.logical_and(qt == 0, jb == 0)
                kstart = jnp.where(first, r, qt * QT + qloc - N_BACK * dil)
                blocks.append((qloc, kstart, jnp.where(first, 1, 0)))
            q8 = [q_ref[ds(qloc, N_BACK), :] * (HEAD_DIM ** -0.5) for qloc, _, _ in blocks]
            kb = [k_ref[ds(kstart, 2 * N_BACK), :].astype(BF16) for _, kstart, _ in blocks]
            vv = [v_ref[ds(kstart, 2 * N_BACK), :] for _, kstart, _ in blocks]
            heads = [(u, hh) for u in range(ATTN_UNROLL) for hh in range(2)]
            keep = lambda hh: h0 if hh == 0 else jnp.logical_not(h0)
            s = [_dot_nt(jnp.where(keep(hh), q8[u], 0.0).astype(BF16), kb[u]) + bias_ref[blocks[u][2], di, hh]
                 for u, hh in heads]
            m = [jnp.max(x, axis=-1, keepdims=True) for x in s]
            p = [jnp.exp(x - mm).astype(BF16) for x, mm in zip(s, m)]
            res = [_dot(pp, jnp.where(keep(hh), vv[u], 1.0).astype(BF16)) for pp, (u, hh) in zip(p, heads)]
            for u, (qloc, _, _) in enumerate(blocks):
                rows = ds(qloc, N_BACK)
                acc_ref[di, rows, :] = jnp.where(h0, res[2 * u], res[2 * u + 1])
                l_ref[di, rows, :] = jnp.where(h0, res[2 * u + 1], res[2 * u])
                m_ref[di, rows, :] = jnp.where(h0, m[2 * u], m[2 * u + 1])
            return carry

        lax.fori_loop(0, QT // (N_BACK * ATTN_UNROLL), body, 0)

    mall = m_ref[...]
    mtot = jnp.max(mall, axis=0)
    num = jnp.zeros((QT, LANES), F32)
    den = jnp.zeros((QT, LANES), F32)
    for di in range(len(BRANCHES)):
        w = jnp.exp(mall[di] - mtot)
        num = num + w * acc_ref[di]
        den = den + w * pltpu.roll(l_ref[di], HALF, axis=1)
    o_ref[...] = num / den


def _attn_prompt(qkv):
    bias = jnp.asarray(_attn_bias())
    npair = N_HEADS // 2
    nqt = SEQ // QT
    kv_rows = SEQ
    q_spec = pl.BlockSpec((QT, LANES), lambda b, hp, t: (b * nqt + t, hp))
    k_spec = pl.BlockSpec((kv_rows, LANES), lambda b, hp, t: (b, npair + hp))
    v_spec = pl.BlockSpec((kv_rows, LANES), lambda b, hp, t: (b, 2 * npair + hp))
    bias_spec = pl.BlockSpec((2, len(BRANCHES), 2, N_BACK, 2 * N_BACK), lambda b, hp, t: (0, 0, hp, 0, 0))
    return pl.pallas_call(
        _attn_prompt_kernel,
        name="attn_prompt",
        grid=(N_PROMPT_B, npair, nqt),
        in_specs=[q_spec, k_spec, v_spec, bias_spec],
        out_specs=pl.BlockSpec((QT, LANES), lambda b, hp, t: (b * nqt + t, hp)),
        out_shape=jax.ShapeDtypeStruct((N_P, ATTN_W), F32),
        scratch_shapes=[pltpu.VMEM((len(BRANCHES), QT, LANES), F32)] * 3,
        compiler_params=_cparams(("arbitrary", "arbitrary", "arbitrary")),
    )(qkv, qkv, qkv, bias)


ROWS_S = 8


def _attn_sample_kernel(q_ref, kn_ref, vn_ref, ck_ref, cv_ref, e_ref, et_ref, slope_ref, o_ref, kall_ref, vall_ref):
    npair = N_HEADS // 2
    for hp in range(npair):
        sl = slice(hp * LANES, (hp + 1) * LANES)
        kall_ref[hp, 0:W_BUF, :] = ck_ref[0, :, sl]
        vall_ref[hp, 0:W_BUF, :] = cv_ref[0, :, sl]
        kall_ref[hp, W_BUF:W_BUF + ROWS_S, :] = kn_ref[0, :, sl]
        vall_ref[hp, W_BUF:W_BUF + ROWS_S, :] = vn_ref[0, :, sl]

    def rows(ref, start, n, dil):
        idx = pl.ds(start, n) if dil == 1 else pl.ds(start, n, stride=dil)
        return jnp.concatenate([ref[hp, idx, :] for hp in range(npair)], axis=1)

    e01 = e_ref[...]
    et01 = et_ref[...]
    slope = slope_ref[...]
    jrev = (N_BACK - lax.broadcasted_iota(I32, (N_BACK, 1), 0)).astype(F32)
    inew = lax.broadcasted_iota(I32, (ROWS_S, 1), 0)
    scale = HEAD_DIM ** -0.5
    knew = rows(kall_ref, W_BUF, ROWS_S, 1)
    vnew = rows(vall_ref, W_BUF, ROWS_S, 1).astype(BF16).astype(F32)

    def scores(kr, qt):
        ph, plo = _split2(kr * qt)
        return (_dot(ph, e01) + _dot(plo, e01)) * scale

    outs = []
    for t in range(DEC_SEQ):
        qt = q_ref[0, t:t + 1, :]
        s_self = jnp.where(inew == t, scores(knew, qt), -1e30)
        s_br = []
        v_br = []
        for _, dil in BRANCHES:
            start = W_BUF + t - N_BACK * dil
            s_br.append(scores(rows(kall_ref, start, N_BACK, dil), qt) - (slope * dil) * jrev)
            v_br.append(rows(vall_ref, start, N_BACK, dil))
        m = jnp.max(s_self, axis=0, keepdims=True)
        for s in s_br:
            m = jnp.maximum(m, jnp.max(s, axis=0, keepdims=True))
        pb = (jnp.exp(s_self - m) * float(len(BRANCHES))).astype(BF16)
        den = jnp.sum(pb.astype(F32), axis=0, keepdims=True)
        num = jnp.sum(_dot(pb, et01) * vnew, axis=0, keepdims=True)
        for s, vr in zip(s_br, v_br):
            pb = jnp.exp(s - m).astype(BF16)
            den = den + jnp.sum(pb.astype(F32), axis=0, keepdims=True)
            num = num + jnp.sum(_dot(pb, et01) * vr.astype(BF16).astype(F32), axis=0, keepdims=True)
        denx = _dot_r01(jnp.broadcast_to(den, (ROWS_S, N_HEADS)), et01)[0:1]
        outs.append(num / denx)
    pad = jnp.zeros((ROWS_S - DEC_SEQ, ATTN_W), F32)
    o_ref[0] = jnp.concatenate(outs + [pad], axis=0)


def _head_expand():
    e = np.zeros((ATTN_W, N_HEADS), np.float32)
    for h in range(N_HEADS):
        e[h * HEAD_DIM:(h + 1) * HEAD_DIM, h] = 1.0
    return e


def _attn_sample(q_s, k_s, v_s, cache_k, cache_v):
    e = _head_expand()
    e01 = jnp.asarray(e, BF16)
    et01 = jnp.asarray(e.T, BF16)
    slopes = jnp.asarray([[2.0 ** (-8.0 * (h + 1) / N_HEADS) for h in range(N_HEADS)]], F32)
    row = pl.BlockSpec((1, ROWS_S, ATTN_W), lambda b: (b, 0, 0))
    cache = pl.BlockSpec((1, W_BUF, ATTN_W), lambda b: (b, 0, 0))
    return pl.pallas_call(
        _attn_sample_kernel,
        name="attn_sample",
        grid=(N_SAMPLE_B,),
        in_specs=[row, row, row, cache, cache, _full(e01.shape), _full(et01.shape), _full(slopes.shape)],
        out_specs=row,
        out_shape=jax.ShapeDtypeStruct((N_SAMPLE_B, ROWS_S, ATTN_W), F32),
        scratch_shapes=[pltpu.VMEM((N_HEADS // 2, W_BUF + ROWS_S, LANES), F32)] * 2,
        compiler_params=_cparams(("arbitrary",)),
    )(q_s, k_s, v_s, cache_k, cache_v, e01, et01, slopes)


HALO = 8


def _dn_pre_kernel(dn_ref, halo_ref, b_ref, a_ref, valid_ref, cw_ref, alog_ref, dtb_ref, e_ref, et_ref, ltri_ref,
                   q_ref, k_ref, v_ref, bx_ref, gx_ref, buf_ref):
    ts = dn_ref.shape[0]
    buf_ref[0:HALO, :] = halo_ref[0]
    buf_ref[HALO:HALO + ts, :] = dn_ref[...]
    y = jnp.zeros((ts, CONV_CH), F32)
    for i in range(CONV_W):
        y = y + buf_ref[pl.ds(HALO - (CONV_W - 1) + i, ts), :] * cw_ref[i:i + 1, :]
    y = y * _sigmoid(y)
    valid = valid_ref[...]
    e01 = e_ref[...]
    et01 = et_ref[...]

    def l2n(t):
        ss = _dot_r01(t * t, e01)
        inv = lax.rsqrt(ss + NORM_EPS)
        return t * _dot_r01(inv, et01)

    q_ref[...] = l2n(y[:, :DN_W]) * (HEAD_DIM ** -0.5)
    k_ref[...] = l2n(y[:, DN_W:2 * DN_W]) * valid
    v_ref[...] = y[:, 2 * DN_W:] * valid
    beta = _sigmoid(b_ref[...]) * valid
    sp_in = a_ref[...] + dtb_ref[...]
    softplus = jnp.maximum(sp_in, 0.0) + jnp.log1p(jnp.exp(-jnp.abs(sp_in)))
    g = -jnp.exp(alog_ref[...]) * softplus * valid
    bx_ref[...] = _dot_r01(beta, et01)
    gx_ref[...] = _dot_l01(ltri_ref[...], _dot_r01(g, et01))


def _dn_pre(dn, halo, b_logit, a_logit, valid, conv_w, a_log, dt_bias):
    nt = halo.shape[0]
    rows = nt * TM
    e = _head_expand()
    e01 = jnp.asarray(e, BF16)
    et01 = jnp.asarray(e.T, BF16)
    ii = np.arange(TM)
    ltri = jnp.asarray(((ii[:, None] >= ii[None, :]) & (ii[:, None] // DN_C == ii[None, :] // DN_C)), BF16)
    row = lambda w: pl.BlockSpec((TM, w), lambda i: (i, 0))
    return pl.pallas_call(
        _dn_pre_kernel,
        name="dn_pre",
        grid=(nt,),
        in_specs=[row(CONV_CH), pl.BlockSpec((1, HALO, CONV_CH), lambda i: (i, 0, 0)), row(N_HEADS), row(N_HEADS),
                  row(1), _full(conv_w.shape), _full((1, N_HEADS)), _full((1, N_HEADS)),
                  _full(e01.shape), _full(et01.shape), _full(ltri.shape)],
        out_specs=[row(DN_W)] * 5,
        out_shape=[jax.ShapeDtypeStruct((rows, DN_W), F32)] * 5,
        scratch_shapes=[pltpu.VMEM((HALO + TM, CONV_CH), F32)],
        compiler_params=_cparams(("arbitrary",)),
    )(dn, halo, b_logit, a_logit, valid, conv_w, a_log.reshape(1, N_HEADS), dt_bias.reshape(1, N_HEADS),
      e01, et01, ltri)


def _delta_streams(q, k, v, bx, gcx, s_bd):
    c = DN_C
    n = len(q)
    each = range(n)
    lane = lax.broadcasted_iota(I32, (1, LANES), 1)
    h0 = lane < HALF
    m0 = h0.astype(F32)
    m1 = 1.0 - m0
    ii = lax.broadcasted_iota(I32, (2 * c, 2 * c), 0)
    jj = lax.broadcasted_iota(I32, (2 * c, 2 * c), 1)
    same = jnp.where(ii < c, 0, 1) == jnp.where(jj < c, 0, 1)
    order = jnp.where(same, ii - jj, -1)
    tril = order >= 0
    strict = order > 0
    eye = jnp.where(ii == jj, 1.0, 0.0)

    def stack(x):
        return jnp.concatenate([x * m0, x * m1], axis=0)

    def fold(x2):
        return x2[:c] + x2[c:]

    def decay_of(g):
        gsw = pltpu.roll(g, HALF, axis=1)
        row_b = jnp.concatenate([jnp.where(h0, g, gsw), jnp.where(h0, gsw, g)], axis=0)
        return jnp.where(tril, jnp.exp(jnp.where(tril, row_b - row_b.T, 0.0)), 0.0)

    eg = [jnp.exp(gcx[i]) for i in each]
    kb = [k[i] * bx[i] for i in each]
    glast = [gcx[i][c - 1:c, :] for i in each]
    decay = [decay_of(gcx[i]) for i in each]
    k2 = [stack(k[i]).astype(BF16) for i in each]
    lhs = [jnp.concatenate([stack(kb[i]), stack(q[i])], axis=0).astype(BF16) for i in each]
    aq = [_dot_nt(lhs[i], k2[i]) for i in each]
    a_low = [jnp.where(strict, aq[i][:2 * c] * decay[i], 0.0) for i in each]
    aqk = [jnp.where(tril, aq[i][2 * c:] * decay[i], 0.0).astype(BF16) for i in each]
    tinv = [eye - a_low[i] for i in each]
    apow = [a_low[i].astype(BF16) for i in each]
    for _ in range(5):
        apow = [_dot(apow[i], apow[i]).astype(BF16) for i in each]
        tinv = [tinv[i] + _dot(tinv[i].astype(BF16), apow[i]) for i in each]
    rhs = [jnp.concatenate([stack(v[i] * bx[i]), stack(kb[i] * eg[i])], axis=1).astype(BF16) for i in each]
    uw = [_dot(tinv[i].astype(BF16), rhs[i]) for i in each]
    u = [fold(uw[i][:, :LANES]) for i in each]
    w = [fold(uw[i][:, LANES:]) for i in each]
    wq = [jnp.concatenate([w[i], q[i] * eg[i]], axis=0).astype(BF16) for i in each]
    kd = [(k[i] * jnp.exp(glast[i] - gcx[i])).T.astype(BF16) for i in each]
    gt = [jnp.exp(glast[i]) for i in each]
    ns = len(s_bd)
    state = list(s_bd)
    o = [None] * n
    for sub in range(n // ns):
        js = [sub * ns + i for i in range(ns)]
        ws = [_dot(wq[j], state[i].astype(BF16)) for i, j in enumerate(js)]
        v_new = [u[j] - ws[i][:c] for i, j in enumerate(js)]
        intra = [_dot(aqk[j], stack(v_new[i]).astype(BF16)) for i, j in enumerate(js)]
        upd = [_dot(kd[j], v_new[i].astype(BF16)) for i, j in enumerate(js)]
        for i, j in enumerate(js):
            o[j] = ws[i][c:] + fold(intra[i])
            state[i] = state[i] * gt[j] + jnp.where(same, upd[i], 0.0)
    return o, state


def _delta_kernel(q_ref, k_ref, v_ref, bx_ref, gx_ref, s0_ref, o_ref, sout_ref, s_ref):
    ci = pl.program_id(1)
    nb = q_ref.shape[0]
    npair = N_HEADS // 2

    @pl.when(ci == 0)
    def _():
        s_ref[...] = s0_ref[...]

    streams = [(b, hp) for b in range(nb) for hp in range(npair)]
    chunks = [(slice(sub * DN_C, (sub + 1) * DN_C), b, slice(hp * LANES, (hp + 1) * LANES))
              for sub in range(q_ref.shape[1] // DN_C) for b, hp in streams]
    take = lambda ref: [ref[b, rows, lanes] for rows, b, lanes in chunks]
    o, s_new = _delta_streams(take(q_ref), take(k_ref), take(v_ref), take(bx_ref), take(gx_ref),
                              [s_ref[b, hp] for b, hp in streams])
    for (rows, b, lanes), oc in zip(chunks, o):
        o_ref[b, rows, lanes] = oc
    for (b, hp), sn in zip(streams, s_new):
        s_ref[b, hp] = sn

    @pl.when(ci == pl.num_programs(1) - 1)
    def _():
        sout_ref[...] = s_ref[...]


def _delta(q, k, v, bx, gx, s0_bd, bblk, nsub):
    nbatch, length, _ = q.shape
    npair = N_HEADS // 2
    seq = pl.BlockSpec((bblk, nsub * DN_C, DN_W), lambda g, c: (g, c, 0))
    st = pl.BlockSpec((bblk, npair, LANES, LANES), lambda g, c: (g, 0, 0, 0))
    return pl.pallas_call(
        _delta_kernel,
        name="delta",
        grid=(nbatch // bblk, length // (nsub * DN_C)),
        in_specs=[seq] * 5 + [st],
        out_specs=[seq, st],
        out_shape=[jax.ShapeDtypeStruct((nbatch, length, DN_W), F32),
                   jax.ShapeDtypeStruct((nbatch, npair, LANES, LANES), F32)],
        scratch_shapes=[pltpu.VMEM((bblk, npair, LANES, LANES), F32)],
        compiler_params=_cparams(("arbitrary", "arbitrary")),
    )(q, k, v, bx, gx, s0_bd)


def _state_to_bd(s):
    b = s.shape[0]
    s = s.reshape(b, N_HEADS // 2, 2, HALF, HALF)
    zero = jnp.zeros_like(s[:, :, 0])
    top = jnp.concatenate([s[:, :, 0], zero], axis=-1)
    bot = jnp.concatenate([zero, s[:, :, 1]], axis=-1)
    return jnp.concatenate([top, bot], axis=-2)


def _state_from_bd(s):
    b = s.shape[0]
    return jnp.stack([s[:, :, :HALF, :HALF], s[:, :, HALF:, HALF:]], axis=2).reshape(b, N_HEADS, HALF, HALF)


def _out_kernel(xp_ref, xs_ref, ap_ref, as_ref, dp_ref, ds_ref, z_ref, ng_ref, bavg_ref, wa_ref, wd_ref, g_ref, b_ref,
                o_ref):
    o = _pick(dp_ref, ds_ref)
    oh, ol = _split2(o * o)
    ms = _dot(oh, bavg_ref[...]) + _dot(ol, bavg_ref[...])
    z = z_ref[...]
    dn = o * lax.rsqrt(ms + NORM_EPS) * ng_ref[...] * (z * _sigmoid(z))
    mix = _dot(_pick(ap_ref, as_ref).astype(BF16), wa_ref[...]) + _dot(dn.astype(BF16), wd_ref[...])
    o_ref[...] = _layer_norm(ALPHA * _pick(xp_ref, xs_ref) + mix, g_ref[...], b_ref[...])


def _out_proj(x_p, x_s, att_p, att_s, dn_p, dn_s, z, norm_g, w_out, g, b):
    wa = w_out[:ATTN_W].astype(BF16)
    wd = w_out[ATTN_W:].astype(BF16)
    normg = jnp.tile(norm_g.reshape(1, HEAD_DIM), (1, N_HEADS))
    bavg = jnp.asarray(np.kron(np.eye(N_HEADS), np.ones((HALF, HALF))) / HALF, BF16)
    row = lambda w: pl.BlockSpec((TM, w), lambda i: (i, 0))
    return pl.pallas_call(
        _out_kernel,
        name="out_proj",
        grid=(N_TILES,),
        in_specs=_row2(D_MODEL) + _row2(ATTN_W) + _row2(DN_W) + [
            row(DN_W), _full(normg.shape), _full(bavg.shape),
            _full(wa.shape), _full(wd.shape), _full((1, D_MODEL)), _full((1, D_MODEL))],
        out_specs=row(D_MODEL),
        out_shape=jax.ShapeDtypeStruct((N_PAD, D_MODEL), F32),
        compiler_params=_cparams(("arbitrary",)),
    )(x_p, x_s, att_p, att_s, dn_p, dn_s, z, normg, bavg, wa, wd, g.reshape(1, -1), b.reshape(1, -1))


N_GROUPS = 8
GROUP = N_EXPERTS // N_GROUPS
SLOT_RADIX = 64
_ET_SPEC = pl.BlockSpec((1, 2 * N_EXPERTS, TM), lambda i, *_: (i, 0, 0))
_SEG_SPEC = pl.BlockSpec((1, 1, 2 * N_EXPERTS), lambda i, *_: (i, 0, 0))
TOPK_GROUPS = 4
NEG = -jnp.inf


def _route_kernel(x_ref, wh_ref, wl_ref, bias_ref, upper_ref, lows_ref, st_ref, ge_ref, cnt_ref):
    t = pl.program_id(0)
    xh, xl = _split2(x_ref[...])
    wh = wh_ref[...]
    logits = _dot_nt(wh, xh) + _dot_nt(wh, xl) + _dot_nt(wl_ref[...], xh)
    scores = _sigmoid(logits)
    choice = scores + bias_ref[...]
    i8 = lax.broadcasted_iota(I32, (GROUP, TM), 0)

    def first_max(vals):
        m = jnp.max(vals, axis=0, keepdims=True)
        idx = jnp.min(jnp.where(vals == m, i8, GROUP), axis=0, keepdims=True)
        return m, idx

    pieces = [choice[g * GROUP:(g + 1) * GROUP, :] for g in range(N_GROUPS)]
    gsc = jnp.zeros((N_GROUPS, TM), F32)
    for g in range(N_GROUPS):
        m1, idx1 = first_max(pieces[g])
        m2 = jnp.max(jnp.where(i8 == idx1, NEG, pieces[g]), axis=0, keepdims=True)
        gsc = jnp.where(i8 == g, m1 + m2, gsc)
    gsel = jnp.zeros((N_GROUPS, TM), F32)
    for _ in range(TOPK_GROUPS):
        _, idx = first_max(gsc)
        hit = i8 == idx
        gsel = jnp.where(hit, 1.0, gsel)
        gsc = jnp.where(hit, NEG, gsc)
    masked = [jnp.where(gsel[g:g + 1, :] > 0.5, pieces[g], NEG) for g in range(N_GROUPS)]
    eidx = [i8 + g * GROUP for g in range(N_GROUPS)]
    member = [jnp.zeros((GROUP, TM), F32) for _ in range(N_GROUPS)]
    for _ in range(TOP_K):
        mm = masked[0]
        for g in range(1, N_GROUPS):
            mm = jnp.maximum(mm, masked[g])
        m = jnp.max(mm, axis=0, keepdims=True)
        cand = jnp.where(masked[0] == m, eidx[0], N_EXPERTS)
        for g in range(1, N_GROUPS):
            cand = jnp.minimum(cand, jnp.where(masked[g] == m, eidx[g], N_EXPERTS))
        idx = jnp.min(cand, axis=0, keepdims=True)
        for g in range(N_GROUPS):
            hit = eidx[g] == idx
            member[g] = jnp.where(hit, 1.0, member[g])
            masked[g] = jnp.where(hit, NEG, masked[g])

    tok = t * TM + lax.broadcasted_iota(I32, (1, TM), 1)
    real = jnp.where(tok < N_REAL, 1.0, 0.0)
    mem = jnp.concatenate(member, axis=0) * real
    sel = mem * scores
    gate = (sel / jnp.sum(sel + (1.0 - real), axis=0, keepdims=True) * ROUTED_SCALE).astype(BF16)
    ge_ref[0] = jnp.concatenate([gate, jnp.zeros_like(gate)], axis=0)
    memb = mem.astype(BF16)
    rank = _dot(memb, upper_ref[...])
    cnt = jnp.sum(mem, axis=1, keepdims=True)
    cpad = jnp.maximum(jnp.floor((cnt + (SEG_ALIGN - 1)) * (1.0 / SEG_ALIGN)), 1.0)
    loc = _dot(lows_ref[...], jnp.broadcast_to(cpad, (N_EXPERTS, TM)).astype(BF16)) * SEG_ALIGN
    slot_e = jnp.where(mem > 0.5, loc + rank, float(M_T))
    slot_hi = jnp.floor(slot_e * (1.0 / SLOT_RADIX)) * SLOT_RADIX
    st_ref[0] = jnp.concatenate([slot_hi, slot_e - slot_hi + 1.0], axis=0).astype(BF16)
    ones = jnp.ones((GROUP, TM), BF16)
    cnt_ref[0] = _dot_nt(ones, memb).astype(I32)


def _route(x1, w_router, router_bias):
    wt = w_router.T
    wh, wl = _split2(wt)
    ii = np.arange(TM)
    upper = jnp.asarray(ii[:, None] < ii[None, :], BF16)
    ee = np.arange(N_EXPERTS)
    lows = jnp.asarray(ee[:, None] > ee[None, :], BF16)
    return pl.pallas_call(
        _route_kernel,
        name="route",
        grid=(N_TILES,),
        in_specs=[pl.BlockSpec((TM, D_MODEL), lambda i: (i, 0)), _full(wh.shape), _full(wl.shape),
                  _full((N_EXPERTS, 1)), _full(upper.shape), _full(lows.shape)],
        out_specs=[_ET_SPEC, _ET_SPEC, pl.BlockSpec((1, GROUP, N_EXPERTS), lambda i: (i, 0, 0))],
        out_shape=[jax.ShapeDtypeStruct((N_TILES, 2 * N_EXPERTS, TM), BF16)] * 2 + [
            jax.ShapeDtypeStruct((N_TILES, GROUP, N_EXPERTS), I32)],
        compiler_params=_cparams(("arbitrary",)),
    )(x1, wh, wl, router_bias.reshape(N_EXPERTS, 1), upper, lows)


def _segment_tables(cnt):
    seg = jnp.maximum((cnt + SEG_ALIGN - 1) // SEG_ALIGN, 1)
    cpad = seg * SEG_ALIGN
    loc_off = jnp.cumsum(cpad, axis=1) - cpad
    rows_e = jnp.sum(cpad, axis=0)
    blocks_e = (rows_e + BM - 1) // BM
    bend = jnp.cumsum(blocks_e)
    bstart = bend - blocks_e
    glob_off = (bstart * BM)[None, :] + jnp.cumsum(cpad, axis=0) - cpad
    nb = bend[-1]
    pad_off = bstart * BM + rows_e
    pad_seg = (blocks_e * BM - rows_e) // SEG_ALIGN
    i32 = lambda a: a.reshape(-1).astype(I32)
    bounds = lambda a: jnp.tile(a.astype(F32), (1, 2)).reshape(N_TILES, 1, 2 * N_EXPERTS)
    return (i32(seg), i32(loc_off), i32(glob_off), i32(pad_seg), i32(pad_off), i32(bstart), i32(blocks_e), i32(nb),
            bounds(loc_off), bounds(loc_off + cpad))


def _rows_copy(src_of, dst_of, lo, go, rows, sem):
    lo, go, rows = (v if isinstance(v, int) else pl.multiple_of(v, SEG_ALIGN) for v in (lo, go, rows))
    return pltpu.make_async_copy(src_of(lo, go, rows), dst_of(lo, go, rows), sem)


def _segment_copies(seg_ref, loc_ref, glob_ref, t, src_of, dst_of, sem):
    for e in range(N_EXPERTS):
        i = t * N_EXPERTS + e
        _rows_copy(src_of, dst_of, loc_ref[i], glob_ref[i], seg_ref[i] * SEG_ALIGN, sem).start()


def _wait_rows(src_of, dst_of, rows, sem):
    _rows_copy(src_of, dst_of, 0, 0, rows, sem).wait()


def _local_rows(m0, rows, lo, hi, slot_tab):
    mcol = (lax.broadcasted_iota(I32, (rows, 2 * N_EXPERTS), 0) + m0).astype(F32)
    e01 = jnp.where(mcol >= lo, jnp.where(mcol < hi, 1.0, 0.0), 0.0).astype(BF16)
    mrow = (lax.broadcasted_iota(I32, (rows, TM), 0) + (m0 + 1)).astype(F32)
    return _dot(e01, slot_tab) == mrow, e01


def _row_spans(used):
    yield None, 0, M_MAIN
    for m0 in range(M_MAIN, M_T, M_CHUNK):
        yield m0 < used, m0, M_CHUNK


def _tile_rows(seg_ref, loc_ref, t):
    last = t * N_EXPERTS + N_EXPERTS - 1
    return loc_ref[last] + seg_ref[last] * SEG_ALIGN


def _dispatch_kernel(seg_ref, loc_ref, glob_ref, pseg_ref, poff_ref, nb_ref, x_ref, st_ref, lo_ref, hi_ref,
                     xs_ref, buf_ref, sem):
    t = pl.program_id(0)
    s = t % 2
    used = _tile_rows(seg_ref, loc_ref, t)
    for live, m0, rows in _row_spans(used):
        def sort_rows(m0=m0, rows=rows):
            hit, _ = _local_rows(m0, rows, lo_ref[0], hi_ref[0], st_ref[0])
            onehot = jnp.where(hit, 1.0, 0.0).astype(BF16)
            buf_ref[s, m0:m0 + rows, :] = _dot(onehot, x_ref[...].astype(BF16)).astype(BF16)

        if live is None:
            sort_rows()
        else:
            pl.when(live)(sort_rows)

    src = lambda lo, go, rows: buf_ref.at[s, pl.ds(lo, rows), :]
    dst = lambda lo, go, rows: xs_ref.at[pl.ds(go, rows), :]
    _segment_copies(seg_ref, loc_ref, glob_ref, t, src, dst, sem.at[s])

    @pl.when(t > 0)
    def _():
        prev = lambda lo, go, rows: buf_ref.at[1 - s, pl.ds(lo, rows), :]
        _wait_rows(prev, dst, _tile_rows(seg_ref, loc_ref, t - 1), sem.at[1 - s])

    @pl.when(t == N_TILES - 1)
    def _():
        _wait_rows(src, dst, used, sem.at[s])
        buf_ref[s, 0:BM, :] = jnp.zeros((BM, D_MODEL), BF16)
        zeros = lambda lo, go, rows: buf_ref.at[s, pl.ds(0, rows), :]

        def fill(wait):
            def pad(e, carry):
                k = pseg_ref[e]

                @pl.when(k > 0)
                def _():
                    cp = _rows_copy(zeros, dst, 0, poff_ref[e], k * SEG_ALIGN, sem.at[s])
                    cp.wait() if wait else cp.start()

                return carry

            lax.fori_loop(0, N_EXPERTS, pad, 0)

            def tail(b, carry):
                cp = _rows_copy(zeros, dst, 0, b * BM, BM, sem.at[s])
                cp.wait() if wait else cp.start()
                return carry

            lax.fori_loop(nb_ref[0], NB_MAX, tail, 0)

        fill(False)
        fill(True)


def _dispatch(x1, slot_tab, seg_lo, seg_hi, tables):
    grid_spec = pltpu.PrefetchScalarGridSpec(
        num_scalar_prefetch=len(tables),
        grid=(N_TILES,),
        in_specs=[pl.BlockSpec((TM, D_MODEL), lambda i, *_: (i, 0)), _ET_SPEC, _SEG_SPEC, _SEG_SPEC],
        out_specs=pl.BlockSpec(memory_space=pl.ANY),
        scratch_shapes=[pltpu.VMEM((2, M_T, D_MODEL), BF16), pltpu.SemaphoreType.DMA((2,))],
    )
    return pl.pallas_call(
        _dispatch_kernel,
        name="dispatch",
        grid_spec=grid_spec,
        out_shape=jax.ShapeDtypeStruct((NB_MAX * BM, D_MODEL), BF16),
        compiler_params=_cparams(("arbitrary",)),
    )(*tables, x1, slot_tab, seg_lo, seg_hi)


def _expert_kernel(bstart_ref, bcount_ref, nb_ref, xs_ref, wg_ref, wu_ref, wd_ref, ys_ref,
                   xbuf, ybuf, wgu_s, wd_s, sem_in, sem_out):
    e = pl.program_id(0)
    nb = nb_ref[0]
    b0 = bstart_ref[e]
    n = bcount_ref[e]

    def rows_of(ref, g):
        return ref.at[pl.ds(pl.multiple_of(g * BM, BM), BM), :]

    def x_copy(g):
        return pltpu.make_async_copy(rows_of(xs_ref, g), xbuf.at[g % EXPERT_SLOTS], sem_in.at[g % EXPERT_SLOTS])

    def y_copy(g):
        return pltpu.make_async_copy(ybuf.at[g % EXPERT_SLOTS], rows_of(ys_ref, g), sem_out.at[g % EXPERT_SLOTS])

    @pl.when(e == 0)
    def _():
        for g in range(EXPERT_SLOTS - 1):
            @pl.when(g < nb)
            def _(g=g):
                x_copy(g).start()

    wgu_s[:, :EXPERT_DIM] = wg_ref[0].astype(BF16)
    wgu_s[:, EXPERT_DIM:] = wu_ref[0].astype(BF16)
    wd_s[...] = wd_ref[0].astype(BF16)

    def block(i, carry):
        g = b0 + i
        slot = g % EXPERT_SLOTS

        @pl.when(g + EXPERT_SLOTS - 1 < nb)
        def _():
            x_copy(g + EXPERT_SLOTS - 1).start()

        x_copy(g).wait()

        @pl.when(g >= EXPERT_SLOTS)
        def _():
            y_copy(g - EXPERT_SLOTS).wait()

        h = _dot(xbuf[slot], wgu_s[...])
        gate = h[:, :EXPERT_DIM]
        a = (gate * _sigmoid(gate) * h[:, EXPERT_DIM:]).astype(BF16)
        ybuf[slot] = _dot(a, wd_s[...]).astype(BF16)
        y_copy(g).start()
        return carry

    lax.fori_loop(0, n, block, 0)

    @pl.when(e == N_EXPERTS - 1)
    def _():
        for k in range(EXPERT_SLOTS, 0, -1):
            @pl.when(nb >= k)
            def _(k=k):
                y_copy(nb - k).wait()

        ybuf[0] = jnp.zeros((BM, D_MODEL), BF16)

        def tail(wait):
            def body(g, carry):
                cp = pltpu.make_async_copy(ybuf.at[0], rows_of(ys_ref, g), sem_out.at[0])
                cp.wait() if wait else cp.start()
                return carry

            lax.fori_loop(nb, NB_MAX, body, 0)

        tail(False)
        tail(True)


def _experts(xs, we_gate, we_up, we_down, bstart, bcount, nb):
    weight = lambda shape: pl.BlockSpec((1,) + shape, lambda e, *_: (e, 0, 0))
    grid_spec = pltpu.PrefetchScalarGridSpec(
        num_scalar_prefetch=3,
        grid=(N_EXPERTS,),
        in_specs=[pl.BlockSpec(memory_space=pl.ANY), weight((D_MODEL, EXPERT_DIM)), weight((D_MODEL, EXPERT_DIM)),
                  weight((EXPERT_DIM, D_MODEL))],
        out_specs=pl.BlockSpec(memory_space=pl.ANY),
        scratch_shapes=[pltpu.VMEM((EXPERT_SLOTS, BM, D_MODEL), BF16), pltpu.VMEM((EXPERT_SLOTS, BM, D_MODEL), BF16),
                        pltpu.VMEM((D_MODEL, 2 * EXPERT_DIM), BF16), pltpu.VMEM((EXPERT_DIM, D_MODEL), BF16),
                        pltpu.SemaphoreType.DMA((EXPERT_SLOTS,)), pltpu.SemaphoreType.DMA((EXPERT_SLOTS,))],
    )
    return pl.pallas_call(
        _expert_kernel,
        name="expert",
        grid_spec=grid_spec,
        out_shape=jax.ShapeDtypeStruct((NB_MAX * BM, D_MODEL), BF16),
        compiler_params=_cparams(("arbitrary",)),
    )(bstart, bcount, nb, xs, we_gate, we_up, we_down)


def _combine_kernel(seg_ref, loc_ref, glob_ref, x_ref, st_ref, ge_ref, lo_ref, hi_ref, pp_ref, ps_ref, ys_ref,
                    wsgu_ref, wsd_ref, g2_ref, b2_ref, wpg_ref, wpp_ref, g3_ref, b3_ref, op_ref, os_ref,
                    buf_ref, acc_ref, sem):
    t = pl.program_id(0)
    s = t % 2
    used = _tile_rows(seg_ref, loc_ref, t)
    src = lambda lo, go, rows: ys_ref.at[pl.ds(go, rows), :]
    into = lambda slot: (lambda lo, go, rows: buf_ref.at[slot, pl.ds(lo, rows), :])

    @pl.when(t == 0)
    def _():
        buf_ref[...] = jnp.zeros((2, M_T, D_MODEL), BF16)
        _segment_copies(seg_ref, loc_ref, glob_ref, t, src, into(s), sem.at[s])

    @pl.when(t + 1 < N_TILES)
    def _():
        _segment_copies(seg_ref, loc_ref, glob_ref, t + 1, src, into(1 - s), sem.at[1 - s])

    _wait_rows(src, into(s), used, sem.at[s])

    for live, m0, rows in _row_spans(used):
        def gather_rows(m0=m0, rows=rows):
            hit, e01 = _local_rows(m0, rows, lo_ref[0], hi_ref[0], st_ref[0])
            pg = jnp.where(hit, _dot(e01, ge_ref[0]), 0.0).astype(BF16)
            return lax.dot_general(pg, buf_ref[s, m0:m0 + rows, :], (((0,), (0,)), ((), ())),
                                   preferred_element_type=F32)

        if live is None:
            acc_ref[...] = gather_rows()
        else:
            @pl.when(live)
            def _(gather_rows=gather_rows):
                acc_ref[...] += gather_rows()

    x = x_ref[...]
    routed = acc_ref[...]
    xb = x.astype(BF16)
    h = _dot(xb, wsgu_ref[...])
    g = h[:, :EXPERT_DIM]
    shared = _dot((g * _sigmoid(g) * h[:, EXPERT_DIM:]).astype(BF16), wsd_ref[...])
    x2 = _layer_norm(ALPHA * x + (routed + shared), g2_ref[...], b2_ref[...])
    ple = _sigmoid(_dot(x2.astype(BF16), wpg_ref[...])) * _dot(_pick(pp_ref, ps_ref).astype(BF16), wpp_ref[...])
    y = _layer_norm(ALPHA * x2 + ple, g3_ref[...], b3_ref[...])

    @pl.when(t < NT_P)
    def _():
        op_ref[...] = y

    @pl.when(t >= NT_P)
    def _():
        os_ref[...] = y


def _combine(x1, slot_tab, gate_e, seg_lo, seg_hi, p_p, p_s, ys, seg, loc_off, glob_off, ws_gate, ws_up, ws_down,
             ln2_g, ln2_b, w_ple_gate, w_ple_proj, ln3_g, ln3_b):
    wsgu = jnp.concatenate([ws_gate, ws_up], axis=1).astype(BF16)
    wsd = ws_down.astype(BF16)
    wpg = w_ple_gate.astype(BF16)
    wpp = w_ple_proj.astype(BF16)
    row = lambda w: pl.BlockSpec((TM, w), lambda i, *_: (i, 0))
    full = lambda shape: pl.BlockSpec(shape, lambda i, *_: (0,) * len(shape))
    vec = full((1, D_MODEL))
    grid_spec = pltpu.PrefetchScalarGridSpec(
        num_scalar_prefetch=3,
        grid=(N_TILES,),
        in_specs=[row(D_MODEL), _ET_SPEC, _ET_SPEC, _SEG_SPEC, _SEG_SPEC] + _row2(PLE_DIM) + [
            pl.BlockSpec(memory_space=pl.ANY),
            full(wsgu.shape), full(wsd.shape), vec, vec, full(wpg.shape), full(wpp.shape), vec, vec],
        out_specs=_row2(D_MODEL),
        scratch_shapes=[pltpu.VMEM((2, M_T, D_MODEL), BF16), pltpu.VMEM((TM, D_MODEL), F32),
                        pltpu.SemaphoreType.DMA((2,))],
    )
    r = lambda a: a.reshape(1, -1)
    return pl.pallas_call(
        _combine_kernel,
        name="combine",
        grid_spec=grid_spec,
        out_shape=[jax.ShapeDtypeStruct((N_P, D_MODEL), F32), jax.ShapeDtypeStruct((N_SP, D_MODEL), F32)],
        compiler_params=_cparams(("arbitrary",)),
    )(seg, loc_off, glob_off, x1, slot_tab, gate_e, seg_lo, seg_hi, p_p, p_s, ys, wsgu, wsd, r(ln2_g), r(ln2_b),
      wpg, wpp, r(ln3_g), r(ln3_b))


def _pad_rows(a, rows):
    return jnp.concatenate([a, jnp.zeros((rows - a.shape[0],) + a.shape[1:], a.dtype)], axis=0)


def kernel(x_prompt, x_sample, cache_k_win, cache_v_win, state_conv, state_delta, p_prompt, p_sample, w_in, conv_w, a_log, dt_bias, dn_norm_g, w_out, ln1_g, ln1_b, w_router, router_bias, we_gate, we_up, we_down, ws_gate, ws_up, ws_down, ln2_g, ln2_b, w_ple_gate, w_ple_proj, ln3_g, ln3_b):
    x_p = x_prompt.reshape(N_P, D_MODEL)
    x_s = _pad_rows(x_sample.reshape(N_S, D_MODEL), N_SP)
    p_p = p_prompt[0].reshape(N_P, PLE_DIM)
    p_s = _pad_rows(p_sample[0].reshape(N_S, PLE_DIM), N_SP)

    qkv, dn_in, z, b_logit, a_logit = _project(x_p, x_s, w_in[0])

    att_p = _attn_prompt(qkv)
    qkv_s = qkv[N_P:N_REAL].reshape(N_SAMPLE_B, DEC_SEQ, 3 * ATTN_W)
    qkv_s8 = jnp.pad(qkv_s, ((0, 0), (0, ROWS_S - DEC_SEQ), (0, 0)))
    ck = cache_k_win[0].reshape(N_SAMPLE_B, W_BUF, ATTN_W)
    cv = cache_v_win[0].reshape(N_SAMPLE_B, W_BUF, ATTN_W)
    att_s = _attn_sample(qkv_s8[:, :, :ATTN_W], qkv_s8[:, :, ATTN_W:2 * ATTN_W], qkv_s8[:, :, 2 * ATTN_W:], ck, cv)
    att_s = _pad_rows(att_s[:, :DEC_SEQ].reshape(N_S, ATTN_W), N_SP)

    tails = dn_in.reshape(N_TILES, TM, CONV_CH)[:NT_P - 1, TM - HALO:]
    halo_p = jnp.concatenate([jnp.zeros((1, HALO, CONV_CH), F32), tails], axis=0)
    seq_start = (jnp.arange(NT_P) % (SEQ // TM) == 0)[:, None, None]
    halo_p = jnp.where(seq_start, 0.0, halo_p)
    q_p, k_p, v_p, bx_p, gx_p = _dn_pre(dn_in, halo_p, b_logit, a_logit, jnp.ones((N_P, 1), F32),
                                        conv_w[0], a_log[0], dt_bias[0])
    shp = (N_PROMPT_B, SEQ, DN_W)
    s0_p = jnp.zeros((N_PROMPT_B, N_HEADS // 2, LANES, LANES), F32)
    o_p, s_p = _delta(q_p.reshape(shp), k_p.reshape(shp), v_p.reshape(shp), bx_p.reshape(shp), gx_p.reshape(shp),
                      s0_p, N_PROMPT_B, 4)

    def seq_pad(tok, state=None):
        w = tok.shape[-1]
        tok = tok.reshape(N_SAMPLE_B, DEC_SEQ, w)
        head = jnp.zeros((N_SAMPLE_B, S_TOK0, w), F32)
        if state is not None:
            head = head.at[:, S_TOK0 - (CONV_W - 1):].set(state)
        tail = jnp.zeros((N_SAMPLE_B, SEQ_S - S_TOK0 - DEC_SEQ, w), F32)
        return jnp.concatenate([head, tok, tail], axis=1).reshape(N_SAMPLE_B * SEQ_S, w)

    dn_tok_s = dn_in[N_P:N_REAL]
    dn_s = seq_pad(dn_tok_s, state_conv[0])
    rows_s = N_SAMPLE_B * SEQ_S
    valid_s = seq_pad(jnp.ones((N_S, 1), F32))
    halo_s = jnp.zeros((rows_s // TM, HALO, CONV_CH), F32)
    q_s, k_s, v_s, bx_s, gx_s = _dn_pre(dn_s, halo_s, seq_pad(b_logit[N_P:N_REAL]), seq_pad(a_logit[N_P:N_REAL]),
                                        valid_s, conv_w[0], a_log[0], dt_bias[0])
    shs = (N_SAMPLE_B, SEQ_S, DN_W)
    o_s, s_s = _delta(q_s.reshape(shs), k_s.reshape(shs), v_s.reshape(shs), bx_s.reshape(shs), gx_s.reshape(shs),
                      _state_to_bd(state_delta[0]), 4, 1)
    dn_o_s = _pad_rows(o_s[:, S_TOK0:S_TOK0 + DEC_SEQ].reshape(N_S, DN_W), N_SP)

    x1 = _out_proj(x_p, x_s, att_p, att_s, o_p.reshape(N_P, DN_W), dn_o_s, z, dn_norm_g[0], w_out[0], ln1_g[0], ln1_b[0])

    slot_tab, gate_e, cnt = _route(x1, w_router[0], router_bias[0])
    seg, loc_off, glob_off, pad_seg, pad_off, bstart, bcount, nb, seg_lo, seg_hi = _segment_tables(cnt[:, 0, :])
    xs = _dispatch(x1, slot_tab, seg_lo, seg_hi, (seg, loc_off, glob_off, pad_seg, pad_off, nb))
    ys = _experts(xs, we_gate[0], we_up[0], we_down[0], bstart, bcount, nb)
    y_p, y_s = _combine(x1, slot_tab, gate_e, seg_lo, seg_hi, p_p, p_s, ys, seg, loc_off, glob_off,
                        ws_gate[0], ws_up[0], ws_down[0], ln2_g[0], ln2_b[0], w_ple_gate[0], w_ple_proj[0],
                        ln3_g[0], ln3_b[0])

    y_prompt = y_p.reshape(N_PROMPT_B, SEQ, D_MODEL)
    y_sample = y_s[:N_S].reshape(N_SAMPLE_B, DEC_SEQ, D_MODEL)
    heads = (N_HEADS, HEAD_DIM)
    win = lambda c0: jnp.stack([qkv[(b + 1) * SEQ - W_BUF:(b + 1) * SEQ, c0:c0 + ATTN_W] for b in range(N_PROMPT_B)])
    k_pr = win(ATTN_W).reshape(N_PROMPT_B, W_BUF, *heads)
    v_pr = win(2 * ATTN_W).reshape(N_PROMPT_B, W_BUF, *heads)
    conv_p = jnp.stack([dn_in[(b + 1) * SEQ - (CONV_W - 1):(b + 1) * SEQ] for b in range(N_PROMPT_B)])
    k_new = qkv_s[:, :, ATTN_W:2 * ATTN_W].reshape(N_SAMPLE_B, DEC_SEQ, *heads)
    v_new = qkv_s[:, :, 2 * ATTN_W:].reshape(N_SAMPLE_B, DEC_SEQ, *heads)
    k_sm = jnp.concatenate([cache_k_win[0][:, DEC_SEQ:], k_new], axis=1)
    v_sm = jnp.concatenate([cache_v_win[0][:, DEC_SEQ:], v_new], axis=1)
    conv_s = dn_tok_s.reshape(N_SAMPLE_B, DEC_SEQ, CONV_CH)[:, DEC_SEQ - (CONV_W - 1):]
    return (y_prompt, y_sample, k_pr[None], v_pr[None], conv_p[None], _state_from_bd(s_p)[None],
            k_sm[None], v_sm[None], conv_s[None], _state_from_bd(s_s)[None])
```

```python
import functools

import jax
import jax.numpy as jnp
import numpy as np
from jax import lax
from jax.experimental import pallas as pl
from jax.experimental.pallas import tpu as pltpu

F32 = jnp.float32
BF16 = jnp.bfloat16
I32 = jnp.int32

D_MODEL = 1024
N_PROMPT_B, SEQ = 2, 8192
N_SAMPLE_B, DEC_SEQ = 32, 4
W_BUF = 2048
N_HEADS = 8
HEAD_DIM = 64
ATTN_W = 512
CONV_CH = 1536
CONV_W = 4
DN_W = 512
N_EXPERTS = 64
TOP_K = 8
EXPERT_DIM = 256
PLE_DIM = 256
BRANCHES = ((128, 1), (512, 4), (2048, 16))
N_BACK = 128
ROUTED_SCALE = 2.5
LN_EPS = 1e-5
NORM_EPS = 1e-6
ALPHA = 2.0 ** 0.25

LANES = 128
HALF = 64
VMEM_LIMIT = 56 * 1024 * 1024

N_P = N_PROMPT_B * SEQ
N_S = N_SAMPLE_B * DEC_SEQ
N_REAL = N_P + N_S
N_PAD = N_P + 512
TM = 256
TB = 512
N_TILES = N_PAD // TM
NT_P = N_P // TM
N_SP = N_PAD - N_P

SEG_ALIGN = 16
M_T = 3072
M_CHUNK = 256
M_MAIN = 2560
BM = 512
EXPERT_SLOTS = 4
R_MAX = N_PAD * TOP_K + N_TILES * N_EXPERTS * SEG_ALIGN + N_EXPERTS * (BM - 1)
NB_MAX = -(-R_MAX // BM)

DN_C = 64
SEQ_S = 64
S_TOK0 = 8


def _dot(a, b):
    return jnp.dot(a, b, preferred_element_type=F32)


def _dot_nt(a, b):
    return lax.dot_general(a, b, (((1,), (1,)), ((), ())), preferred_element_type=F32)


def _split2(x):
    hi = x.astype(BF16)
    lo = (x - hi.astype(F32)).astype(BF16)
    return hi, lo


def _dot_l01(m01, x):
    hi, lo = _split2(x)
    return _dot(m01, hi) + _dot(m01, lo)


def _dot_r01(x, m01):
    hi, lo = _split2(x)
    return _dot(hi, m01) + _dot(lo, m01)


def _sigmoid(x):
    return 1.0 / (1.0 + jnp.exp(-x))


def _layer_norm(x, g, b):
    mu = jnp.mean(x, axis=-1, keepdims=True)
    xc = x - mu
    var = jnp.mean(xc * xc, axis=-1, keepdims=True)
    return xc * lax.rsqrt(var + LN_EPS) * g + b


def _cparams(sem=None):
    return pltpu.CompilerParams(dimension_semantics=sem, vmem_limit_bytes=VMEM_LIMIT)


def _full(shape):
    return pl.BlockSpec(shape, lambda *_: (0,) * len(shape))


def _row2(w, tile=TM):
    ntp = N_P // tile
    return [pl.BlockSpec((tile, w), lambda i, *_: (jnp.minimum(i, ntp - 1), 0)),
            pl.BlockSpec((tile, w), lambda i, *_: (jnp.maximum(i - ntp, 0), 0))]


def _pick(p_ref, s_ref):
    return jnp.where(pl.program_id(0) < N_P // p_ref.shape[0], p_ref[...], s_ref[...])


def _proj_kernel(xp_ref, xs_ref, wqkv_ref, wdn_ref, wz_ref, wg_ref, qkv_ref, dn_ref, z_ref, b_ref, a_ref):
    x = _pick(xp_ref, xs_ref)
    xh, xl = _split2(x)
    qkv_ref[...] = _dot(xh, wqkv_ref[...])
    dn_ref[...] = _dot(xh, wdn_ref[...])
    z_ref[...] = _dot(xh, wz_ref[...])
    ng = 2 * N_HEADS
    o1 = _dot(xh, wg_ref[...])
    o2 = _dot(xl, wg_ref[...])
    gates = o1[:, :ng] + o1[:, ng:2 * ng] + o2[:, :ng]
    b_ref[...] = gates[:, :N_HEADS]
    a_ref[...] = gates[:, N_HEADS:]


def _project(x_p, x_s, w_in):
    wqkv = w_in[:, :3 * ATTN_W].astype(BF16)
    wdn = w_in[:, 3 * ATTN_W:3 * ATTN_W + CONV_CH].astype(BF16)
    c0 = 3 * ATTN_W + CONV_CH
    wz = w_in[:, c0:c0 + DN_W].astype(BF16)
    wgh, wgl = _split2(w_in[:, c0 + DN_W:])
    wg = jnp.concatenate([wgh, wgl, jnp.zeros((D_MODEL, LANES - 4 * N_HEADS), BF16)], axis=1)
    row = lambda w: pl.BlockSpec((TB, w), lambda i: (i, 0))
    outs = (3 * ATTN_W, CONV_CH, DN_W, N_HEADS, N_HEADS)
    return pl.pallas_call(
        _proj_kernel,
        name="proj",
        grid=(N_PAD // TB,),
        in_specs=_row2(D_MODEL, TB) + [_full(wqkv.shape), _full(wdn.shape), _full(wz.shape), _full(wg.shape)],
        out_specs=[row(w) for w in outs],
        out_shape=[jax.ShapeDtypeStruct((N_PAD, w), F32) for w in outs],
        compiler_params=_cparams(("arbitrary",)),
    )(x_p, x_s, wqkv, wdn, wz, wg)


QT = 2048
ATTN_UNROLL = 8


def _attn_bias():
    qi = np.arange(N_BACK)[:, None]
    ki = np.arange(2 * N_BACK)[None, :]
    out = np.zeros((2, len(BRANCHES), N_HEADS, N_BACK, 2 * N_BACK), np.float32)
    for var, off in enumerate((N_BACK, 0)):
        dist = qi - ki + off
        valid = (dist >= 0) & (dist <= N_BACK)
        for di, (_, dil) in enumerate(BRANCHES):
            for h in range(N_HEADS):
                slope = 2.0 ** (-8.0 * (h + 1) / N_HEADS) * dil
                out[var, di, h] = np.where(valid, -slope * dist, -1e30)
    return out


def _attn_prompt_kernel(q_ref, k_ref, v_ref, bias_ref, o_ref, acc_ref, m_ref, l_ref):
    qt = pl.program_id(2)
    lane = lax.broadcasted_iota(I32, (1, LANES), 1)
    h0 = lane < HALF
    for di, (_, dil) in enumerate(BRANCHES):
        per_r = QT // (N_BACK * dil)

        def body(i, carry, di=di, dil=dil, per_r=per_r):
            def ds(start, n):
                return pl.ds(start, n) if dil == 1 else pl.ds(start, n, stride=dil)

            blocks = []
            for u in range(ATTN_UNROLL):
                blk = i * ATTN_UNROLL + u
                r = blk // per_r
                jb = blk % per_r
                qloc = r + dil * N_BACK * jb
                first = jnp.logical_and(qt == 0, jb == 0)
                kstart = jnp.where(first, r, qt * QT + qloc - N_BACK * dil)
                blocks.append((qloc, kstart, jnp.where(first, 1, 0)))
            q8 = [q_ref[ds(qloc, N_BACK), :] * (HEAD_DIM ** -0.5) for qloc, _, _ in blocks]
            kb = [k_ref[ds(kstart, 2 * N_BACK), :].astype(BF16) for _, kstart, _ in blocks]
            vv = [v_ref[ds(kstart, 2 * N_BACK), :] for _, kstart, _ in blocks]
            heads = [(u, hh) for u in range(ATTN_UNROLL) for hh in range(2)]
            keep = lambda hh: h0 if hh == 0 else jnp.logical_not(h0)
            s = [_dot_nt(jnp.where(keep(hh), q8[u], 0.0).astype(BF16), kb[u]) + bias_ref[blocks[u][2], di, hh]
                 for u, hh in heads]
            m = [jnp.max(x, axis=-1, keepdims=True) for x in s]
            p = [jnp.exp(x - mm).astype(BF16) for x, mm in zip(s, m)]
            res = [_dot(pp, jnp.where(keep(hh), vv[u], 1.0).astype(BF16)) for pp, (u, hh) in zip(p, heads)]
            for u, (qloc, _, _) in enumerate(blocks):
                rows = ds(qloc, N_BACK)
                acc_ref[di, rows, :] = jnp.where(h0, res[2 * u], res[2 * u + 1])
                l_ref[di, rows, :] = jnp.where(h0, res[2 * u + 1], res[2 * u])
                m_ref[di, rows, :] = jnp.where(h0, m[2 * u], m[2 * u + 1])
            return carry

        lax.fori_loop(0, QT // (N_BACK * ATTN_UNROLL), body, 0)

    mall = m_ref[...]
    mtot = jnp.max(mall, axis=0)
    num = jnp.zeros((QT, LANES), F32)
    den = jnp.zeros((QT, LANES), F32)
    for di in range(len(BRANCHES)):
        w = jnp.exp(mall[di] - mtot)
        num = num + w * acc_ref[di]
        den = den + w * pltpu.roll(l_ref[di], HALF, axis=1)
    o_ref[...] = num / den


def _attn_prompt(qkv):
    bias = jnp.asarray(_attn_bias())
    npair = N_HEADS // 2
    nqt = SEQ // QT
    kv_rows = SEQ
    q_spec = pl.BlockSpec((QT, LANES), lambda b, hp, t: (b * nqt + t, hp))
    k_spec = pl.BlockSpec((kv_rows, LANES), lambda b, hp, t: (b, npair + hp))
    v_spec = pl.BlockSpec((kv_rows, LANES), lambda b, hp, t: (b, 2 * npair + hp))
    bias_spec = pl.BlockSpec((2, len(BRANCHES), 2, N_BACK, 2 * N_BACK), lambda b, hp, t: (0, 0, hp, 0, 0))
    return pl.pallas_call(
        _attn_prompt_kernel,
        name="attn_prompt",
        grid=(N_PROMPT_B, npair, nqt),
        in_specs=[q_spec, k_spec, v_spec, bias_spec],
        out_specs=pl.BlockSpec((QT, LANES), lambda b, hp, t: (b * nqt + t, hp)),
        out_shape=jax.ShapeDtypeStruct((N_P, ATTN_W), F32),
        scratch_shapes=[pltpu.VMEM((len(BRANCHES), QT, LANES), F32)] * 3,
        compiler_params=_cparams(("arbitrary", "arbitrary", "arbitrary")),
    )(qkv, qkv, qkv, bias)


ROWS_S = 8


def _attn_sample_kernel(q_ref, kn_ref, vn_ref, ck_ref, cv_ref, e_ref, et_ref, slope_ref, o_ref, kall_ref, vall_ref):
    npair = N_HEADS // 2
    for hp in range(npair):
        sl = slice(hp * LANES, (hp + 1) * LANES)
        kall_ref[hp, 0:W_BUF, :] = ck_ref[0, :, sl]
        vall_ref[hp, 0:W_BUF, :] = cv_ref[0, :, sl]
        kall_ref[hp, W_BUF:W_BUF + ROWS_S, :] = kn_ref[0, :, sl]
        vall_ref[hp, W_BUF:W_BUF + ROWS_S, :] = vn_ref[0, :, sl]

    def rows(ref, start, n, dil):
        idx = pl.ds(start, n) if dil == 1 else pl.ds(start, n, stride=dil)
        return jnp.concatenate([ref[hp, idx, :] for hp in range(npair)], axis=1)

    e01 = e_ref[...]
    et01 = et_ref[...]
    slope = slope_ref[...]
    jrev = (N_BACK - lax.broadcasted_iota(I32, (N_BACK, 1), 0)).astype(F32)
    inew = lax.broadcasted_iota(I32, (ROWS_S, 1), 0)
    scale = HEAD_DIM ** -0.5
    knew = rows(kall_ref, W_BUF, ROWS_S, 1)
    vnew = rows(vall_ref, W_BUF, ROWS_S, 1).astype(BF16).astype(F32)

    def scores(kr, qt):
        ph, plo = _split2(kr * qt)
        return (_dot(ph, e01) + _dot(plo, e01)) * scale

    outs = []
    for t in range(DEC_SEQ):
        qt = q_ref[0, t:t + 1, :]
        s_self = jnp.where(inew == t, scores(knew, qt), -1e30)
        s_br = []
        v_br = []
        for _, dil in BRANCHES:
            start = W_BUF + t - N_BACK * dil
            s_br.append(scores(rows(kall_ref, start, N_BACK, dil), qt) - (slope * dil) * jrev)
            v_br.append(rows(vall_ref, start, N_BACK, dil))
        m = jnp.max(s_self, axis=0, keepdims=True)
        for s in s_br:
            m = jnp.maximum(m, jnp.max(s, axis=0, keepdims=True))
        pb = (jnp.exp(s_self - m) * float(len(BRANCHES))).astype(BF16)
        den = jnp.sum(pb.astype(F32), axis=0, keepdims=True)
        num = jnp.sum(_dot(pb, et01) * vnew, axis=0, keepdims=True)
        for s, vr in zip(s_br, v_br):
            pb = jnp.exp(s - m).astype(BF16)
            den = den + jnp.sum(pb.astype(F32), axis=0, keepdims=True)
            num = num + jnp.sum(_dot(pb, et01) * vr.astype(BF16).astype(F32), axis=0, keepdims=True)
        denx = _dot_r01(jnp.broadcast_to(den, (ROWS_S, N_HEADS)), et01)[0:1]
        outs.append(num / denx)
    pad = jnp.zeros((ROWS_S - DEC_SEQ, ATTN_W), F32)
    o_ref[0] = jnp.concatenate(outs + [pad], axis=0)


def _head_expand():
    e = np.zeros((ATTN_W, N_HEADS), np.float32)
    for h in range(N_HEADS):
        e[h * HEAD_DIM:(h + 1) * HEAD_DIM, h] = 1.0
    return e


def _attn_sample(q_s, k_s, v_s, cache_k, cache_v):
    e = _head_expand()
    e01 = jnp.asarray(e, BF16)
    et01 = jnp.asarray(e.T, BF16)
    slopes = jnp.asarray([[2.0 ** (-8.0 * (h + 1) / N_HEADS) for h in range(N_HEADS)]], F32)
    row = pl.BlockSpec((1, ROWS_S, ATTN_W), lambda b: (b, 0, 0))
    cache = pl.BlockSpec((1, W_BUF, ATTN_W), lambda b: (b, 0, 0))
    return pl.pallas_call(
        _attn_sample_kernel,
        name="attn_sample",
        grid=(N_SAMPLE_B,),
        in_specs=[row, row, row, cache, cache, _full(e01.shape), _full(et01.shape), _full(slopes.shape)],
        out_specs=row,
        out_shape=jax.ShapeDtypeStruct((N_SAMPLE_B, ROWS_S, ATTN_W), F32),
        scratch_shapes=[pltpu.VMEM((N_HEADS // 2, W_BUF + ROWS_S, LANES), F32)] * 2,
        compiler_params=_cparams(("arbitrary",)),
    )(q_s, k_s, v_s, cache_k, cache_v, e01, et01, slopes)


HALO = 8


def _dn_pre_kernel(dn_ref, halo_ref, b_ref, a_ref, valid_ref, cw_ref, alog_ref, dtb_ref, e_ref, et_ref, ltri_ref,
                   q_ref, k_ref, v_ref, bx_ref, gx_ref, buf_ref):
    ts = dn_ref.shape[0]
    buf_ref[0:HALO, :] = halo_ref[0]
    buf_ref[HALO:HALO + ts, :] = dn_ref[...]
    y = jnp.zeros((ts, CONV_CH), F32)
    for i in range(CONV_W):
        y = y + buf_ref[pl.ds(HALO - (CONV_W - 1) + i, ts), :] * cw_ref[i:i + 1, :]
    y = y * _sigmoid(y)
    valid = valid_ref[...]
    e01 = e_ref[...]
    et01 = et_ref[...]

    def l2n(t):
        ss = _dot_r01(t * t, e01)
        inv = lax.rsqrt(ss + NORM_EPS)
        return t * _dot_r01(inv, et01)

    q_ref[...] = l2n(y[:, :DN_W]) * (HEAD_DIM ** -0.5)
    k_ref[...] = l2n(y[:, DN_W:2 * DN_W]) * valid
    v_ref[...] = y[:, 2 * DN_W:] * valid
    beta = _sigmoid(b_ref[...]) * valid
    sp_in = a_ref[...] + dtb_ref[...]
    softplus = jnp.maximum(sp_in, 0.0) + jnp.log1p(jnp.exp(-jnp.abs(sp_in)))
    g = -jnp.exp(alog_ref[...]) * softplus * valid
    bx_ref[...] = _dot_r01(beta, et01)
    gx_ref[...] = _dot_l01(ltri_ref[...], _dot_r01(g, et01))


def _dn_pre(dn, halo, b_logit, a_logit, valid, conv_w, a_log, dt_bias):
    nt = halo.shape[0]
    rows = nt * TM
    e = _head_expand()
    e01 = jnp.asarray(e, BF16)
    et01 = jnp.asarray(e.T, BF16)
    ii = np.arange(TM)
    ltri = jnp.asarray(((ii[:, None] >= ii[None, :]) & (ii[:, None] // DN_C == ii[None, :] // DN_C)), BF16)
    row = lambda w: pl.BlockSpec((TM, w), lambda i: (i, 0))
    return pl.pallas_call(
        _dn_pre_kernel,
        name="dn_pre",
        grid=(nt,),
        in_specs=[row(CONV_CH), pl.BlockSpec((1, HALO, CONV_CH), lambda i: (i, 0, 0)), row(N_HEADS), row(N_HEADS),
                  row(1), _full(conv_w.shape), _full((1, N_HEADS)), _full((1, N_HEADS)),
                  _full(e01.shape), _full(et01.shape), _full(ltri.shape)],
        out_specs=[row(DN_W)] * 5,
        out_shape=[jax.ShapeDtypeStruct((rows, DN_W), F32)] * 5,
        scratch_shapes=[pltpu.VMEM((HALO + TM, CONV_CH), F32)],
        compiler_params=_cparams(("arbitrary",)),
    )(dn, halo, b_logit, a_logit, valid, conv_w, a_log.reshape(1, N_HEADS), dt_bias.reshape(1, N_HEADS),
      e01, et01, ltri)


def _delta_streams(q, k, v, bx, gcx, s_bd):
    c = DN_C
    n = len(q)
    each = range(n)
    lane = lax.broadcasted_iota(I32, (1, LANES), 1)
    h0 = lane < HALF
    m0 = h0.astype(F32)
    m1 = 1.0 - m0
    ii = lax.broadcasted_iota(I32, (2 * c, 2 * c), 0)
    jj = lax.broadcasted_iota(I32, (2 * c, 2 * c), 1)
    same = jnp.where(ii < c, 0, 1) == jnp.where(jj < c, 0, 1)
    order = jnp.where(same, ii - jj, -1)
    tril = order >= 0
    strict = order > 0
    eye = jnp.where(ii == jj, 1.0, 0.0)

    def stack(x):
        return jnp.concatenate([x * m0, x * m1], axis=0)

    def fold(x2):
        return x2[:c] + x2[c:]

    def decay_of(g):
        gsw = pltpu.roll(g, HALF, axis=1)
        row_b = jnp.concatenate([jnp.where(h0, g, gsw), jnp.where(h0, gsw, g)], axis=0)
        return jnp.where(tril, jnp.exp(jnp.where(tril, row_b - row_b.T, 0.0)), 0.0)

    eg = [jnp.exp(gcx[i]) for i in each]
    kb = [k[i] * bx[i] for i in each]
    glast = [gcx[i][c - 1:c, :] for i in each]
    decay = [decay_of(gcx[i]) for i in each]
    k2 = [stack(k[i]).astype(BF16) for i in each]
    lhs = [jnp.concatenate([stack(kb[i]), stack(q[i])], axis=0).astype(BF16) for i in each]
    aq = [_dot_nt(lhs[i], k2[i]) for i in each]
    a_low = [jnp.where(strict, aq[i][:2 * c] * decay[i], 0.0) for i in each]
    aqk = [jnp.where(tril, aq[i][2 * c:] * decay[i], 0.0).astype(BF16) for i in each]
    tinv = [eye - a_low[i] for i in each]
    apow = [a_low[i].astype(BF16) for i in each]
    for _ in range(5):
        apow = [_dot(apow[i], apow[i]).astype(BF16) for i in each]
        tinv = [tinv[i] + _dot(tinv[i].astype(BF16), apow[i]) for i in each]
    rhs = [jnp.concatenate([stack(v[i] * bx[i]), stack(kb[i] * eg[i])], axis=1).astype(BF16) for i in each]
    uw = [_dot(tinv[i].astype(BF16), rhs[i]) for i in each]
    u = [fold(uw[i][:, :LANES]) for i in each]
    w = [fold(uw[i][:, LANES:]) for i in each]
    wq = [jnp.concatenate([w[i], q[i] * eg[i]], axis=0).astype(BF16) for i in each]
    kd = [(k[i] * jnp.exp(glast[i] - gcx[i])).T.astype(BF16) for i in each]
    gt = [jnp.exp(glast[i]) for i in each]
    ns = len(s_bd)
    state = list(s_bd)
    o = [None] * n
    for sub in range(n // ns):
        js = [sub * ns + i for i in range(ns)]
        ws = [_dot(wq[j], state[i].astype(BF16)) for i, j in enumerate(js)]
        v_new = [u[j] - ws[i][:c] for i, j in enumerate(js)]
        intra = [_dot(aqk[j], stack(v_new[i]).astype(BF16)) for i, j in enumerate(js)]
        upd = [_dot(kd[j], v_new[i].astype(BF16)) for i, j in enumerate(js)]
        for i, j in enumerate(js):
            o[j] = ws[i][c:] + fold(intra[i])
            state[i] = state[i] * gt[j] + jnp.where(same, upd[i], 0.0)
    return o, state


def _delta_kernel(q_ref, k_ref, v_ref, bx_ref, gx_ref, s0_ref, o_ref, sout_ref, s_ref):
    ci = pl.program_id(1)
    nb = q_ref.shape[0]
    npair = N_HEADS // 2

    @pl.when(ci == 0)
    def _():
        s_ref[...] = s0_ref[...]

    streams = [(b, hp) for b in range(nb) for hp in range(npair)]
    chunks = [(slice(sub * DN_C, (sub + 1) * DN_C), b, slice(hp * LANES, (hp + 1) * LANES))
              for sub in range(q_ref.shape[1] // DN_C) for b, hp in streams]
    take = lambda ref: [ref[b, rows, lanes] for rows, b, lanes in chunks]
    o, s_new = _delta_streams(take(q_ref), take(k_ref), take(v_ref), take(bx_ref), take(gx_ref),
                              [s_ref[b, hp] for b, hp in streams])
    for (rows, b, lanes), oc in zip(chunks, o):
        o_ref[b, rows, lanes] = oc
    for (b, hp), sn in zip(streams, s_new):
        s_ref[b, hp] = sn

    @pl.when(ci == pl.num_programs(1) - 1)
    def _():
        sout_ref[...] = s_ref[...]


def _delta(q, k, v, bx, gx, s0_bd, bblk, nsub):
    nbatch, length, _ = q.shape
    npair = N_HEADS // 2
    seq = pl.BlockSpec((bblk, nsub * DN_C, DN_W), lambda g, c: (g, c, 0))
    st = pl.BlockSpec((bblk, npair, LANES, LANES), lambda g, c: (g, 0, 0, 0))
    return pl.pallas_call(
        _delta_kernel,
        name="delta",
        grid=(nbatch // bblk, length // (nsub * DN_C)),
        in_specs=[seq] * 5 + [st],
        out_specs=[seq, st],
        out_shape=[jax.ShapeDtypeStruct((nbatch, length, DN_W), F32),
                   jax.ShapeDtypeStruct((nbatch, npair, LANES, LANES), F32)],
        scratch_shapes=[pltpu.VMEM((bblk, npair, LANES, LANES), F32)],
        compiler_params=_cparams(("arbitrary", "arbitrary")),
    )(q, k, v, bx, gx, s0_bd)


def _state_to_bd(s):
    b = s.shape[0]
    s = s.reshape(b, N_HEADS // 2, 2, HALF, HALF)
    zero = jnp.zeros_like(s[:, :, 0])
    top = jnp.concatenate([s[:, :, 0], zero], axis=-1)
    bot = jnp.concatenate([zero, s[:, :, 1]], axis=-1)
    return jnp.concatenate([top, bot], axis=-2)


def _state_from_bd(s):
    b = s.shape[0]
    return jnp.stack([s[:, :, :HALF, :HALF], s[:, :, HALF:, HALF:]], axis=2).reshape(b, N_HEADS, HALF, HALF)


def _out_kernel(xp_ref, xs_ref, ap_ref, as_ref, dp_ref, ds_ref, z_ref, ng_ref, bavg_ref, wa_ref, wd_ref, g_ref, b_ref,
                o_ref):
    o = _pick(dp_ref, ds_ref)
    oh, ol = _split2(o * o)
    ms = _dot(oh, bavg_ref[...]) + _dot(ol, bavg_ref[...])
    z = z_ref[...]
    dn = o * lax.rsqrt(ms + NORM_EPS) * ng_ref[...] * (z * _sigmoid(z))
    mix = _dot(_pick(ap_ref, as_ref).astype(BF16), wa_ref[...]) + _dot(dn.astype(BF16), wd_ref[...])
    o_ref[...] = _layer_norm(ALPHA * _pick(xp_ref, xs_ref) + mix, g_ref[...], b_ref[...])


def _out_proj(x_p, x_s, att_p, att_s, dn_p, dn_s, z, norm_g, w_out, g, b):
    wa = w_out[:ATTN_W].astype(BF16)
    wd = w_out[ATTN_W:].astype(BF16)
    normg = jnp.tile(norm_g.reshape(1, HEAD_DIM), (1, N_HEADS))
    bavg = jnp.asarray(np.kron(np.eye(N_HEADS), np.ones((HALF, HALF))) / HALF, BF16)
    row = lambda w: pl.BlockSpec((TB, w), lambda i: (i, 0))
    return pl.pallas_call(
        _out_kernel,
        name="out_proj",
        grid=(N_PAD // TB,),
        in_specs=_row2(D_MODEL, TB) + _row2(ATTN_W, TB) + _row2(DN_W, TB) + [
            row(DN_W), _full(normg.shape), _full(bavg.shape),
            _full(wa.shape), _full(wd.shape), _full((1, D_MODEL)), _full((1, D_MODEL))],
        out_specs=row(D_MODEL),
        out_shape=jax.ShapeDtypeStruct((N_PAD, D_MODEL), F32),
        compiler_params=_cparams(("arbitrary",)),
    )(x_p, x_s, att_p, att_s, dn_p, dn_s, z, normg, bavg, wa, wd, g.reshape(1, -1), b.reshape(1, -1))


N_GROUPS = 8
GROUP = N_EXPERTS // N_GROUPS
SLOT_RADIX = 64
_ET_SPEC = pl.BlockSpec((1, 2 * N_EXPERTS, TM), lambda i, *_: (i, 0, 0))
_SEG_SPEC = pl.BlockSpec((1, 1, 2 * N_EXPERTS), lambda i, *_: (i, 0, 0))
TOPK_GROUPS = 4
NEG = -jnp.inf


def _route_kernel(x_ref, wh_ref, wl_ref, bias_ref, upper_ref, lows_ref, st_ref, ge_ref, cnt_ref):
    t = pl.program_id(0)
    xh, xl = _split2(x_ref[...])
    wh = wh_ref[...]
    logits = _dot_nt(wh, xh) + _dot_nt(wh, xl) + _dot_nt(wl_ref[...], xh)
    scores = _sigmoid(logits)
    choice = scores + bias_ref[...]
    i8 = lax.broadcasted_iota(I32, (GROUP, TM), 0)

    def first_max(vals):
        m = jnp.max(vals, axis=0, keepdims=True)
        idx = jnp.min(jnp.where(vals == m, i8, GROUP), axis=0, keepdims=True)
        return m, idx

    pieces = [choice[g * GROUP:(g + 1) * GROUP, :] for g in range(N_GROUPS)]
    gsc = jnp.zeros((N_GROUPS, TM), F32)
    for g in range(N_GROUPS):
        m1, idx1 = first_max(pieces[g])
        m2 = jnp.max(jnp.where(i8 == idx1, NEG, pieces[g]), axis=0, keepdims=True)
        gsc = jnp.where(i8 == g, m1 + m2, gsc)
    gsel = jnp.zeros((N_GROUPS, TM), F32)
    for _ in range(TOPK_GROUPS):
        _, idx = first_max(gsc)
        hit = i8 == idx
        gsel = jnp.where(hit, 1.0, gsel)
        gsc = jnp.where(hit, NEG, gsc)
    masked = [jnp.where(gsel[g:g + 1, :] > 0.5, pieces[g], NEG) for g in range(N_GROUPS)]
    eidx = [i8 + g * GROUP for g in range(N_GROUPS)]
    member = [jnp.zeros((GROUP, TM), F32) for _ in range(N_GROUPS)]
    for _ in range(TOP_K):
        mm = masked[0]
        for g in range(1, N_GROUPS):
            mm = jnp.maximum(mm, masked[g])
        m = jnp.max(mm, axis=0, keepdims=True)
        cand = jnp.where(masked[0] == m, eidx[0], N_EXPERTS)
        for g in range(1, N_GROUPS):
            cand = jnp.minimum(cand, jnp.where(masked[g] == m, eidx[g], N_EXPERTS))
        idx = jnp.min(cand, axis=0, keepdims=True)
        for g in range(N_GROUPS):
            hit = eidx[g] == idx
            member[g] = jnp.where(hit, 1.0, member[g])
            masked[g] = jnp.where(hit, NEG, masked[g])

    tok = t * TM + lax.broadcasted_iota(I32, (1, TM), 1)
    real = jnp.where(tok < N_REAL, 1.0, 0.0)
    mem = jnp.concatenate(member, axis=0) * real
    sel = mem * scores
    gate = (sel / jnp.sum(sel + (1.0 - real), axis=0, keepdims=True) * ROUTED_SCALE).astype(BF16)
    ge_ref[0] = jnp.concatenate([gate, jnp.zeros_like(gate)], axis=0)
    memb = mem.astype(BF16)
    rank = _dot(memb, upper_ref[...])
    cnt = jnp.sum(mem, axis=1, keepdims=True)
    cpad = jnp.maximum(jnp.floor((cnt + (SEG_ALIGN - 1)) * (1.0 / SEG_ALIGN)), 1.0)
    loc = _dot(lows_ref[...], jnp.broadcast_to(cpad, (N_EXPERTS, TM)).astype(BF16)) * SEG_ALIGN
    slot_e = jnp.where(mem > 0.5, loc + rank, float(M_T))
    slot_hi = jnp.floor(slot_e * (1.0 / SLOT_RADIX)) * SLOT_RADIX
    st_ref[0] = jnp.concatenate([slot_hi, slot_e - slot_hi + 1.0], axis=0).astype(BF16)
    ones = jnp.ones((GROUP, TM), BF16)
    cnt_ref[0] = _dot_nt(ones, memb).astype(I32)


def _route(x1, w_router, router_bias):
    wt = w_router.T
    wh, wl = _split2(wt)
    ii = np.arange(TM)
    upper = jnp.asarray(ii[:, None] < ii[None, :], BF16)
    ee = np.arange(N_EXPERTS)
    lows = jnp.asarray(ee[:, None] > ee[None, :], BF16)
    return pl.pallas_call(
        _route_kernel,
        name="route",
        grid=(N_TILES,),
        in_specs=[pl.BlockSpec((TM, D_MODEL), lambda i: (i, 0)), _full(wh.shape), _full(wl.shape),
                  _full((N_EXPERTS, 1)), _full(upper.shape), _full(lows.shape)],
        out_specs=[_ET_SPEC, _ET_SPEC, pl.BlockSpec((1, GROUP, N_EXPERTS), lambda i: (i, 0, 0))],
        out_shape=[jax.ShapeDtypeStruct((N_TILES, 2 * N_EXPERTS, TM), BF16)] * 2 + [
            jax.ShapeDtypeStruct((N_TILES, GROUP, N_EXPERTS), I32)],
        compiler_params=_cparams(("arbitrary",)),
    )(x1, wh, wl, router_bias.reshape(N_EXPERTS, 1), upper, lows)


def _segment_tables(cnt):
    seg = jnp.maximum((cnt + SEG_ALIGN - 1) // SEG_ALIGN, 1)
    cpad = seg * SEG_ALIGN
    loc_off = jnp.cumsum(cpad, axis=1) - cpad
    rows_e = jnp.sum(cpad, axis=0)
    blocks_e = (rows_e + BM - 1) // BM
    bend = jnp.cumsum(blocks_e)
    bstart = bend - blocks_e
    glob_off = (bstart * BM)[None, :] + jnp.cumsum(cpad, axis=0) - cpad
    nb = bend[-1]
    pad_off = bstart * BM + rows_e
    pad_seg = (blocks_e * BM - rows_e) // SEG_ALIGN
    i32 = lambda a: a.reshape(-1).astype(I32)
    bounds = lambda a: jnp.tile(a.astype(F32), (1, 2)).reshape(N_TILES, 1, 2 * N_EXPERTS)
    return (i32(seg), i32(loc_off), i32(glob_off), i32(pad_seg), i32(pad_off), i32(bstart), i32(blocks_e), i32(nb),
            bounds(loc_off), bounds(loc_off + cpad))


def _rows_copy(src_of, dst_of, lo, go, rows, sem):
    lo, go, rows = (v if isinstance(v, int) else pl.multiple_of(v, SEG_ALIGN) for v in (lo, go, rows))
    return pltpu.make_async_copy(src_of(lo, go, rows), dst_of(lo, go, rows), sem)


def _segment_copies(seg_ref, loc_ref, glob_ref, t, src_of, dst_of, sem):
    for e in range(N_EXPERTS):
        i = t * N_EXPERTS + e
        _rows_copy(src_of, dst_of, loc_ref[i], glob_ref[i], seg_ref[i] * SEG_ALIGN, sem).start()


def _wait_rows(src_of, dst_of, rows, sem):
    _rows_copy(src_of, dst_of, 0, 0, rows, sem).wait()


def _local_rows(m0, rows, lo, hi, slot_tab):
    mcol = (lax.broadcasted_iota(I32, (rows, 2 * N_EXPERTS), 0) + m0).astype(F32)
    e01 = jnp.where(mcol >= lo, jnp.where(mcol < hi, 1.0, 0.0), 0.0).astype(BF16)
    mrow = (lax.broadcasted_iota(I32, (rows, TM), 0) + (m0 + 1)).astype(F32)
    return _dot(e01, slot_tab) == mrow, e01


def _row_spans(used):
    yield None, 0, M_MAIN
    for m0 in range(M_MAIN, M_T, M_CHUNK):
        yield m0 < used, m0, M_CHUNK


def _tile_rows(seg_ref, loc_ref, t):
    last = t * N_EXPERTS + N_EXPERTS - 1
    return loc_ref[last] + seg_ref[last] * SEG_ALIGN


def _dispatch_kernel(seg_ref, loc_ref, glob_ref, pseg_ref, poff_ref, nb_ref, x_ref, st_ref, lo_ref, hi_ref,
                     xs_ref, buf_ref, sem):
    t = pl.program_id(0)
    s = t % 2
    used = _tile_rows(seg_ref, loc_ref, t)
    for live, m0, rows in _row_spans(used):
        def sort_rows(m0=m0, rows=rows):
            hit, _ = _local_rows(m0, rows, lo_ref[0], hi_ref[0], st_ref[0])
            onehot = jnp.where(hit, 1.0, 0.0).astype(BF16)
            buf_ref[s, m0:m0 + rows, :] = _dot(onehot, x_ref[...].astype(BF16)).astype(BF16)

        if live is None:
            sort_rows()
        else:
            pl.when(live)(sort_rows)

    src = lambda lo, go, rows: buf_ref.at[s, pl.ds(lo, rows), :]
    dst = lambda lo, go, rows: xs_ref.at[pl.ds(go, rows), :]
    _segment_copies(seg_ref, loc_ref, glob_ref, t, src, dst, sem.at[s])

    @pl.when(t > 0)
    def _():
        prev = lambda lo, go, rows: buf_ref.at[1 - s, pl.ds(lo, rows), :]
        _wait_rows(prev, dst, _tile_rows(seg_ref, loc_ref, t - 1), sem.at[1 - s])

    @pl.when(t == N_TILES - 1)
    def _():
        _wait_rows(src, dst, used, sem.at[s])
        buf_ref[s, 0:BM, :] = jnp.zeros((BM, D_MODEL), BF16)
        zeros = lambda lo, go, rows: buf_ref.at[s, pl.ds(0, rows), :]

        def fill(wait):
            def pad(e, carry):
                k = pseg_ref[e]

                @pl.when(k > 0)
                def _():
                    cp = _rows_copy(zeros, dst, 0, poff_ref[e], k * SEG_ALIGN, sem.at[s])
                    cp.wait() if wait else cp.start()

                return carry

            lax.fori_loop(0, N_EXPERTS, pad, 0)

            def tail(b, carry):
                cp = _rows_copy(zeros, dst, 0, b * BM, BM, sem.at[s])
                cp.wait() if wait else cp.start()
                return carry

            lax.fori_loop(nb_ref[0], NB_MAX, tail, 0)

        fill(False)
        fill(True)


def _dispatch(x1, slot_tab, seg_lo, seg_hi, tables):
    grid_spec = pltpu.PrefetchScalarGridSpec(
        num_scalar_prefetch=len(tables),
        grid=(N_TILES,),
        in_specs=[pl.BlockSpec((TM, D_MODEL), lambda i, *_: (i, 0)), _ET_SPEC, _SEG_SPEC, _SEG_SPEC],
        out_specs=pl.BlockSpec(memory_space=pl.ANY),
        scratch_shapes=[pltpu.VMEM((2, M_T, D_MODEL), BF16), pltpu.SemaphoreType.DMA((2,))],
    )
    return pl.pallas_call(
        _dispatch_kernel,
        name="dispatch",
        grid_spec=grid_spec,
        out_shape=jax.ShapeDtypeStruct((NB_MAX * BM, D_MODEL), BF16),
        compiler_params=_cparams(("arbitrary",)),
    )(*tables, x1, slot_tab, seg_lo, seg_hi)


def _expert_kernel(bstart_ref, bcount_ref, nb_ref, xs_ref, wg_ref, wu_ref, wd_ref, ys_ref,
                   xbuf, ybuf, wgu_s, wd_s, sem_in, sem_out):
    e = pl.program_id(0)
    nb = nb_ref[0]
    b0 = bstart_ref[e]
    n = bcount_ref[e]

    def rows_of(ref, g):
        return ref.at[pl.ds(pl.multiple_of(g * BM, BM), BM), :]

    def x_copy(g):
        return pltpu.make_async_copy(rows_of(xs_ref, g), xbuf.at[g % EXPERT_SLOTS], sem_in.at[g % EXPERT_SLOTS])

    def y_copy(g):
        return pltpu.make_async_copy(ybuf.at[g % EXPERT_SLOTS], rows_of(ys_ref, g), sem_out.at[g % EXPERT_SLOTS])

    @pl.when(e == 0)
    def _():
        for g in range(EXPERT_SLOTS - 1):
            @pl.when(g < nb)
            def _(g=g):
                x_copy(g).start()

    wgu_s[:, :EXPERT_DIM] = wg_ref[0].astype(BF16)
    wgu_s[:, EXPERT_DIM:] = wu_ref[0].astype(BF16)
    wd_s[...] = wd_ref[0].astype(BF16)

    def block(i, carry):
        g = b0 + i
        slot = g % EXPERT_SLOTS

        @pl.when(g + EXPERT_SLOTS - 1 < nb)
        def _():
            x_copy(g + EXPERT_SLOTS - 1).start()

        x_copy(g).wait()

        @pl.when(g >= EXPERT_SLOTS)
        def _():
            y_copy(g - EXPERT_SLOTS).wait()

        h = _dot(xbuf[slot], wgu_s[...])
        gate = h[:, :EXPERT_DIM]
        a = (gate * _sigmoid(gate) * h[:, EXPERT_DIM:]).astype(BF16)
        ybuf[slot] = _dot(a, wd_s[...]).astype(BF16)
        y_copy(g).start()
        return carry

    lax.fori_loop(0, n, block, 0)

    @pl.when(e == N_EXPERTS - 1)
    def _():
        for k in range(EXPERT_SLOTS, 0, -1):
            @pl.when(nb >= k)
            def _(k=k):
                y_copy(nb - k).wait()

        ybuf[0] = jnp.zeros((BM, D_MODEL), BF16)

        def tail(wait):
            def body(g, carry):
                cp = pltpu.make_async_copy(ybuf.at[0], rows_of(ys_ref, g), sem_out.at[0])
                cp.wait() if wait else cp.start()
                return carry

            lax.fori_loop(nb, NB_MAX, body, 0)

        tail(False)
        tail(True)


def _experts(xs, we_gate, we_up, we_down, bstart, bcount, nb):
    weight = lambda shape: pl.BlockSpec((1,) + shape, lambda e, *_: (e, 0, 0))
    grid_spec = pltpu.PrefetchScalarGridSpec(
        num_scalar_prefetch=3,
        grid=(N_EXPERTS,),
        in_specs=[pl.BlockSpec(memory_space=pl.ANY), weight((D_MODEL, EXPERT_DIM)), weight((D_MODEL, EXPERT_DIM)),
                  weight((EXPERT_DIM, D_MODEL))],
        out_specs=pl.BlockSpec(memory_space=pl.ANY),
        scratch_shapes=[pltpu.VMEM((EXPERT_SLOTS, BM, D_MODEL), BF16), pltpu.VMEM((EXPERT_SLOTS, BM, D_MODEL), BF16),
                        pltpu.VMEM((D_MODEL, 2 * EXPERT_DIM), BF16), pltpu.VMEM((EXPERT_DIM, D_MODEL), BF16),
                        pltpu.SemaphoreType.DMA((EXPERT_SLOTS,)), pltpu.SemaphoreType.DMA((EXPERT_SLOTS,))],
    )
    return pl.pallas_call(
        _expert_kernel,
        name="expert",
        grid_spec=grid_spec,
        out_shape=jax.ShapeDtypeStruct((NB_MAX * BM, D_MODEL), BF16),
        compiler_params=_cparams(("arbitrary",)),
    )(bstart, bcount, nb, xs, we_gate, we_up, we_down)


def _combine_kernel(seg_ref, loc_ref, glob_ref, x_ref, st_ref, ge_ref, lo_ref, hi_ref, pp_ref, ps_ref, ys_ref,
                    wsgu_ref, wsd_ref, g2_ref, b2_ref, wpg_ref, wpp_ref, g3_ref, b3_ref, op_ref, os_ref,
                    buf_ref, acc_ref, sem):
    t = pl.program_id(0)
    s = t % 2
    used = _tile_rows(seg_ref, loc_ref, t)
    src = lambda lo, go, rows: ys_ref.at[pl.ds(go, rows), :]
    into = lambda slot: (lambda lo, go, rows: buf_ref.at[slot, pl.ds(lo, rows), :])

    @pl.when(t == 0)
    def _():
        buf_ref[...] = jnp.zeros((2, M_T, D_MODEL), BF16)
        _segment_copies(seg_ref, loc_ref, glob_ref, t, src, into(s), sem.at[s])

    @pl.when(t + 1 < N_TILES)
    def _():
        _segment_copies(seg_ref, loc_ref, glob_ref, t + 1, src, into(1 - s), sem.at[1 - s])

    _wait_rows(src, into(s), used, sem.at[s])

    for live, m0, rows in _row_spans(used):
        def gather_rows(m0=m0, rows=rows):
            hit, e01 = _local_rows(m0, rows, lo_ref[0], hi_ref[0], st_ref[0])
            pg = jnp.where(hit, _dot(e01, ge_ref[0]), 0.0).astype(BF16)
            return lax.dot_general(pg, buf_ref[s, m0:m0 + rows, :], (((0,), (0,)), ((), ())),
                                   preferred_element_type=F32)

        if live is None:
            acc_ref[...] = gather_rows()
        else:
            @pl.when(live)
            def _(gather_rows=gather_rows):
                acc_ref[...] += gather_rows()

    x = x_ref[...]
    routed = acc_ref[...]
    xb = x.astype(BF16)
    h = _dot(xb, wsgu_ref[...])
    g = h[:, :EXPERT_DIM]
    shared = _dot((g * _sigmoid(g) * h[:, EXPERT_DIM:]).astype(BF16), wsd_ref[...])
    x2 = _layer_norm(ALPHA * x + (routed + shared), g2_ref[...], b2_ref[...])
    ple = _sigmoid(_dot(x2.astype(BF16), wpg_ref[...])) * _dot(_pick(pp_ref, ps_ref).astype(BF16), wpp_ref[...])
    y = _layer_norm(ALPHA * x2 + ple, g3_ref[...], b3_ref[...])

    @pl.when(t < NT_P)
    def _():
        op_ref[...] = y

    @pl.when(t >= NT_P)
    def _():
        os_ref[...] = y


def _combine(x1, slot_tab, gate_e, seg_lo, seg_hi, p_p, p_s, ys, seg, loc_off, glob_off, ws_gate, ws_up, ws_down,
             ln2_g, ln2_b, w_ple_gate, w_ple_proj, ln3_g, ln3_b):
    wsgu = jnp.concatenate([ws_gate, ws_up], axis=1).astype(BF16)
    wsd = ws_down.astype(BF16)
    wpg = w_ple_gate.astype(BF16)
    wpp = w_ple_proj.astype(BF16)
    row = lambda w: pl.BlockSpec((TM, w), lambda i, *_: (i, 0))
    full = lambda shape: pl.BlockSpec(shape, lambda i, *_: (0,) * len(shape))
    vec = full((1, D_MODEL))
    grid_spec = pltpu.PrefetchScalarGridSpec(
        num_scalar_prefetch=3,
        grid=(N_TILES,),
        in_specs=[row(D_MODEL), _ET_SPEC, _ET_SPEC, _SEG_SPEC, _SEG_SPEC] + _row2(PLE_DIM) + [
            pl.BlockSpec(memory_space=pl.ANY),
            full(wsgu.shape), full(wsd.shape), vec, vec, full(wpg.shape), full(wpp.shape), vec, vec],
        out_specs=_row2(D_MODEL),
        scratch_shapes=[pltpu.VMEM((2, M_T, D_MODEL), BF16), pltpu.VMEM((TM, D_MODEL), F32),
                        pltpu.SemaphoreType.DMA((2,))],
    )
    r = lambda a: a.reshape(1, -1)
    return pl.pallas_call(
        _combine_kernel,
        name="combine",
        grid_spec=grid_spec,
        out_shape=[jax.ShapeDtypeStruct((N_P, D_MODEL), F32), jax.ShapeDtypeStruct((N_SP, D_MODEL), F32)],
        compiler_params=_cparams(("arbitrary",)),
    )(seg, loc_off, glob_off, x1, slot_tab, gate_e, seg_lo, seg_hi, p_p, p_s, ys, wsgu, wsd, r(ln2_g), r(ln2_b),
      wpg, wpp, r(ln3_g), r(ln3_b))


def _window_kernel(ck_ref, cv_ref, kn_ref, vn_ref, ok_ref, ov_ref, sem):
    keep = W_BUF - DEC_SEQ
    copies = []
    for i, (cache, new, out) in enumerate(((ck_ref, kn_ref, ok_ref), (cv_ref, vn_ref, ov_ref))):
        copies.append(pltpu.make_async_copy(cache.at[:, :, pl.ds(DEC_SEQ, keep)], out.at[:, :, pl.ds(0, keep)],
                                            sem.at[2 * i]))
        copies.append(pltpu.make_async_copy(new, out.at[:, :, pl.ds(keep, DEC_SEQ)], sem.at[2 * i + 1]))
    for cp in copies:
        cp.start()
    for cp in copies:
        cp.wait()


def _window_update(cache_k, cache_v, k_new, v_new):
    any_spec = pl.BlockSpec(memory_space=pl.ANY)
    return pl.pallas_call(
        _window_kernel,
        name="window_update",
        in_specs=[any_spec] * 4,
        out_specs=[any_spec] * 2,
        out_shape=[jax.ShapeDtypeStruct(cache_k.shape, cache_k.dtype)] * 2,
        scratch_shapes=[pltpu.SemaphoreType.DMA((4,))],
    )(cache_k, cache_v, k_new, v_new)


def _pad_rows(a, rows):
    return jnp.concatenate([a, jnp.zeros((rows - a.shape[0],) + a.shape[1:], a.dtype)], axis=0)


def kernel(x_prompt, x_sample, cache_k_win, cache_v_win, state_conv, state_delta, p_prompt, p_sample, w_in, conv_w, a_log, dt_bias, dn_norm_g, w_out, ln1_g, ln1_b, w_router, router_bias, we_gate, we_up, we_down, ws_gate, ws_up, ws_down, ln2_g, ln2_b, w_ple_gate, w_ple_proj, ln3_g, ln3_b):
    x_p = x_prompt.reshape(N_P, D_MODEL)
    x_s = _pad_rows(x_sample.reshape(N_S, D_MODEL), N_SP)
    p_p = p_prompt[0].reshape(N_P, PLE_DIM)
    p_s = _pad_rows(p_sample[0].reshape(N_S, PLE_DIM), N_SP)

    qkv, dn_in, z, b_logit, a_logit = _project(x_p, x_s, w_in[0])

    att_p = _attn_prompt(qkv)
    qkv_s = qkv[N_P:N_REAL].reshape(N_SAMPLE_B, DEC_SEQ, 3 * ATTN_W)
    qkv_s8 = jnp.pad(qkv_s, ((0, 0), (0, ROWS_S - DEC_SEQ), (0, 0)))
    ck = cache_k_win[0].reshape(N_SAMPLE_B, W_BUF, ATTN_W)
    cv = cache_v_win[0].reshape(N_SAMPLE_B, W_BUF, ATTN_W)
    att_s = _attn_sample(qkv_s8[:, :, :ATTN_W], qkv_s8[:, :, ATTN_W:2 * ATTN_W], qkv_s8[:, :, 2 * ATTN_W:], ck, cv)
    att_s = _pad_rows(att_s[:, :DEC_SEQ].reshape(N_S, ATTN_W), N_SP)

    tails = dn_in.reshape(N_TILES, TM, CONV_CH)[:NT_P - 1, TM - HALO:]
    halo_p = jnp.concatenate([jnp.zeros((1, HALO, CONV_CH), F32), tails], axis=0)
    seq_start = (jnp.arange(NT_P) % (SEQ // TM) == 0)[:, None, None]
    halo_p = jnp.where(seq_start, 0.0, halo_p)
    q_p, k_p, v_p, bx_p, gx_p = _dn_pre(dn_in, halo_p, b_logit, a_logit, jnp.ones((N_P, 1), F32),
                                        conv_w[0], a_log[0], dt_bias[0])
    shp = (N_PROMPT_B, SEQ, DN_W)
    s0_p = jnp.zeros((N_PROMPT_B, N_HEADS // 2, LANES, LANES), F32)
    o_p, s_p = _delta(q_p.reshape(shp), k_p.reshape(shp), v_p.reshape(shp), bx_p.reshape(shp), gx_p.reshape(shp),
                      s0_p, N_PROMPT_B, 4)

    def seq_pad(tok, state=None):
        w = tok.shape[-1]
        tok = tok.reshape(N_SAMPLE_B, DEC_SEQ, w)
        head = jnp.zeros((N_SAMPLE_B, S_TOK0, w), F32)
        if state is not None:
            head = head.at[:, S_TOK0 - (CONV_W - 1):].set(state)
        tail = jnp.zeros((N_SAMPLE_B, SEQ_S - S_TOK0 - DEC_SEQ, w), F32)
        return jnp.concatenate([head, tok, tail], axis=1).reshape(N_SAMPLE_B * SEQ_S, w)

    dn_tok_s = dn_in[N_P:N_REAL]
    dn_s = seq_pad(dn_tok_s, state_conv[0])
    rows_s = N_SAMPLE_B * SEQ_S
    valid_s = seq_pad(jnp.ones((N_S, 1), F32))
    halo_s = jnp.zeros((rows_s // TM, HALO, CONV_CH), F32)
    q_s, k_s, v_s, bx_s, gx_s = _dn_pre(dn_s, halo_s, seq_pad(b_logit[N_P:N_REAL]), seq_pad(a_logit[N_P:N_REAL]),
                                        valid_s, conv_w[0], a_log[0], dt_bias[0])
    shs = (N_SAMPLE_B, SEQ_S, DN_W)
    o_s, s_s = _delta(q_s.reshape(shs), k_s.reshape(shs), v_s.reshape(shs), bx_s.reshape(shs), gx_s.reshape(shs),
                      _state_to_bd(state_delta[0]), 4, 1)
    dn_o_s = _pad_rows(o_s[:, S_TOK0:S_TOK0 + DEC_SEQ].reshape(N_S, DN_W), N_SP)

    x1 = _out_proj(x_p, x_s, att_p, att_s, o_p.reshape(N_P, DN_W), dn_o_s, z, dn_norm_g[0], w_out[0], ln1_g[0], ln1_b[0])

    slot_tab, gate_e, cnt = _route(x1, w_router[0], router_bias[0])
    seg, loc_off, glob_off, pad_seg, pad_off, bstart, bcount, nb, seg_lo, seg_hi = _segment_tables(cnt[:, 0, :])
    xs = _dispatch(x1, slot_tab, seg_lo, seg_hi, (seg, loc_off, glob_off, pad_seg, pad_off, nb))
    ys = _experts(xs, we_gate[0], we_up[0], we_down[0], bstart, bcount, nb)
    y_p, y_s = _combine(x1, slot_tab, gate_e, seg_lo, seg_hi, p_p, p_s, ys, seg, loc_off, glob_off,
                        ws_gate[0], ws_up[0], ws_down[0], ln2_g[0], ln2_b[0], w_ple_gate[0], w_ple_proj[0],
                        ln3_g[0], ln3_b[0])

    y_prompt = y_p.reshape(N_PROMPT_B, SEQ, D_MODEL)
    y_sample = y_s[:N_S].reshape(N_SAMPLE_B, DEC_SEQ, D_MODEL)
    heads = (N_HEADS, HEAD_DIM)
    win = lambda c0: jnp.stack([qkv[(b + 1) * SEQ - W_BUF:(b + 1) * SEQ, c0:c0 + ATTN_W] for b in range(N_PROMPT_B)])
    k_pr = win(ATTN_W).reshape(N_PROMPT_B, W_BUF, *heads)
    v_pr = win(2 * ATTN_W).reshape(N_PROMPT_B, W_BUF, *heads)
    conv_p = jnp.stack([dn_in[(b + 1) * SEQ - (CONV_W - 1):(b + 1) * SEQ] for b in range(N_PROMPT_B)])
    k_new = qkv_s[:, :, ATTN_W:2 * ATTN_W].reshape(N_SAMPLE_B, DEC_SEQ, *heads)
    v_new = qkv_s[:, :, 2 * ATTN_W:].reshape(N_SAMPLE_B, DEC_SEQ, *heads)
    k_sm, v_sm = _window_update(cache_k_win, cache_v_win, k_new[None], v_new[None])
    conv_s = dn_tok_s.reshape(N_SAMPLE_B, DEC_SEQ, CONV_CH)[:, DEC_SEQ - (CONV_W - 1):]
    return (y_prompt, y_sample, k_pr[None], v_pr[None], conv_p[None], _state_from_bd(s_p)[None],
            k_sm, v_sm, conv_s[None], _state_from_bd(s_s)[None])
```

```python
import functools

import jax
import jax.numpy as jnp
import numpy as np
from jax import lax
from jax.experimental import pallas as pl
from jax.experimental.pallas import tpu as pltpu

F32 = jnp.float32
BF16 = jnp.bfloat16
I32 = jnp.int32

D_MODEL = 1024
N_PROMPT_B, SEQ = 2, 8192
N_SAMPLE_B, DEC_SEQ = 32, 4
W_BUF = 2048
N_HEADS = 8
HEAD_DIM = 64
ATTN_W = 512
CONV_CH = 1536
CONV_W = 4
DN_W = 512
N_EXPERTS = 64
TOP_K = 8
EXPERT_DIM = 256
PLE_DIM = 256
BRANCHES = ((128, 1), (512, 4), (2048, 16))
N_BACK = 128
ROUTED_SCALE = 2.5
LN_EPS = 1e-5
NORM_EPS = 1e-6
ALPHA = 2.0 ** 0.25

LANES = 128
HALF = 64
VMEM_LIMIT = 56 * 1024 * 1024

N_P = N_PROMPT_B * SEQ
N_S = N_SAMPLE_B * DEC_SEQ
N_REAL = N_P + N_S
N_PAD = N_P + 512
TM = 256
TB = 512
N_TILES = N_PAD // TM
NT_P = N_P // TM
N_SP = N_PAD - N_P

SEG_ALIGN = 16
M_T = 3072
M_CHUNK = 256
M_MAIN = 2560
BM = 512
EXPERT_SLOTS = 4
R_MAX = N_PAD * TOP_K + N_TILES * N_EXPERTS * SEG_ALIGN + N_EXPERTS * (BM - 1)
NB_MAX = -(-R_MAX // BM)

DN_C = 64
SEQ_S = 64
S_TOK0 = 8


def _dot(a, b):
    return jnp.dot(a, b, preferred_element_type=F32)


def _dot_nt(a, b):
    return lax.dot_general(a, b, (((1,), (1,)), ((), ())), preferred_element_type=F32)


def _split2(x):
    hi = x.astype(BF16)
    lo = (x - hi.astype(F32)).astype(BF16)
    return hi, lo


def _dot_l01(m01, x):
    hi, lo = _split2(x)
    return _dot(m01, hi) + _dot(m01, lo)


def _dot_r01(x, m01):
    hi, lo = _split2(x)
    return _dot(hi, m01) + _dot(lo, m01)


def _sigmoid(x):
    return 1.0 / (1.0 + jnp.exp(-x))


def _layer_norm(x, g, b):
    mu = jnp.mean(x, axis=-1, keepdims=True)
    xc = x - mu
    var = jnp.mean(xc * xc, axis=-1, keepdims=True)
    return xc * lax.rsqrt(var + LN_EPS) * g + b


def _cparams(sem=None):
    return pltpu.CompilerParams(dimension_semantics=sem, vmem_limit_bytes=VMEM_LIMIT)


def _full(shape):
    return pl.BlockSpec(shape, lambda *_: (0,) * len(shape))


def _row2(w, tile=TM):
    ntp = N_P // tile
    return [pl.BlockSpec((tile, w), lambda i, *_: (jnp.minimum(i, ntp - 1), 0)),
            pl.BlockSpec((tile, w), lambda i, *_: (jnp.maximum(i - ntp, 0), 0))]


def _pick(p_ref, s_ref):
    return jnp.where(pl.program_id(0) < N_P // p_ref.shape[0], p_ref[...], s_ref[...])


def _proj_kernel(xp_ref, xs_ref, wqkv_ref, wdn_ref, wz_ref, wg_ref, qkv_ref, dn_ref, z_ref, b_ref, a_ref):
    x = _pick(xp_ref, xs_ref)
    xh, xl = _split2(x)
    qkv_ref[...] = _dot(xh, wqkv_ref[...])
    dn_ref[...] = _dot(xh, wdn_ref[...])
    z_ref[...] = _dot(xh, wz_ref[...])
    ng = 2 * N_HEADS
    o1 = _dot(xh, wg_ref[...])
    o2 = _dot(xl, wg_ref[...])
    gates = o1[:, :ng] + o1[:, ng:2 * ng] + o2[:, :ng]
    b_ref[...] = gates[:, :N_HEADS]
    a_ref[...] = gates[:, N_HEADS:]


def _project(x_p, x_s, w_in):
    wqkv = w_in[:, :3 * ATTN_W].astype(BF16)
    wdn = w_in[:, 3 * ATTN_W:3 * ATTN_W + CONV_CH].astype(BF16)
    c0 = 3 * ATTN_W + CONV_CH
    wz = w_in[:, c0:c0 + DN_W].astype(BF16)
    wgh, wgl = _split2(w_in[:, c0 + DN_W:])
    wg = jnp.concatenate([wgh, wgl, jnp.zeros((D_MODEL, LANES - 4 * N_HEADS), BF16)], axis=1)
    row = lambda w: pl.BlockSpec((TB, w), lambda i: (i, 0))
    outs = (3 * ATTN_W, CONV_CH, DN_W, N_HEADS, N_HEADS)
    return pl.pallas_call(
        _proj_kernel,
        name="proj",
        grid=(N_PAD // TB,),
        in_specs=_row2(D_MODEL, TB) + [_full(wqkv.shape), _full(wdn.shape), _full(wz.shape), _full(wg.shape)],
        out_specs=[row(w) for w in outs],
        out_shape=[jax.ShapeDtypeStruct((N_PAD, w), F32) for w in outs],
        compiler_params=_cparams(("arbitrary",)),
    )(x_p, x_s, wqkv, wdn, wz, wg)


QT = 2048
ATTN_UNROLL = 8


def _attn_bias():
    qi = np.arange(N_BACK)[:, None]
    ki = np.arange(2 * N_BACK)[None, :]
    out = np.zeros((2, len(BRANCHES), N_HEADS, N_BACK, 2 * N_BACK), np.float32)
    for var, off in enumerate((N_BACK, 0)):
        dist = qi - ki + off
        valid = (dist >= 0) & (dist <= N_BACK)
        for di, (_, dil) in enumerate(BRANCHES):
            for h in range(N_HEADS):
                slope = 2.0 ** (-8.0 * (h + 1) / N_HEADS) * dil
                out[var, di, h] = np.where(valid, -slope * dist, -1e30)
    return out


def _attn_prompt_kernel(q_ref, k_ref, v_ref, bias_ref, o_ref, acc_ref, m_ref, l_ref):
    qt = pl.program_id(2)
    lane = lax.broadcasted_iota(I32, (1, LANES), 1)
    h0 = lane < HALF
    for di, (_, dil) in enumerate(BRANCHES):
        per_r = QT // (N_BACK * dil)

        def body(i, carry, di=di, dil=dil, per_r=per_r):
            def ds(start, n):
                return pl.ds(start, n) if dil == 1 else pl.ds(start, n, stride=dil)

            blocks = []
            for u in range(ATTN_UNROLL):
                blk = i * ATTN_UNROLL + u
                r = blk // per_r
                jb = blk % per_r
                qloc = r + dil * N_BACK * jb
                first = jnp.logical_and(qt == 0, jb == 0)
                kstart = jnp.where(first, r, qt * QT + qloc - N_BACK * dil)
                blocks.append((qloc, kstart, jnp.where(first, 1, 0)))
            q8 = [q_ref[ds(qloc, N_BACK), :] * (HEAD_DIM ** -0.5) for qloc, _, _ in blocks]
            kb = [k_ref[ds(kstart, 2 * N_BACK), :].astype(BF16) for _, kstart, _ in blocks]
            vv = [v_ref[ds(kstart, 2 * N_BACK), :] for _, kstart, _ in blocks]
            heads = [(u, hh) for u in range(ATTN_UNROLL) for hh in range(2)]
            keep = lambda hh: h0 if hh == 0 else jnp.logical_not(h0)
            s = [_dot_nt(jnp.where(keep(hh), q8[u], 0.0).astype(BF16), kb[u]) + bias_ref[blocks[u][2], di, hh]
                 for u, hh in heads]
            m = [jnp.max(x, axis=-1, keepdims=True) for x in s]
            p = [jnp.exp(x - mm).astype(BF16) for x, mm in zip(s, m)]
            res = [_dot(pp, jnp.where(keep(hh), vv[u], 1.0).astype(BF16)) for pp, (u, hh) in zip(p, heads)]
            for u, (qloc, _, _) in enumerate(blocks):
                rows = ds(qloc, N_BACK)
                acc_ref[di, rows, :] = jnp.where(h0, res[2 * u], res[2 * u + 1])
                l_ref[di, rows, :] = jnp.where(h0, res[2 * u + 1], res[2 * u])
                m_ref[di, rows, :] = jnp.where(h0, m[2 * u], m[2 * u + 1])
            return carry

        lax.fori_loop(0, QT // (N_BACK * ATTN_UNROLL), body, 0)

    mall = m_ref[...]
    mtot = jnp.max(mall, axis=0)
    num = jnp.zeros((QT, LANES), F32)
    den = jnp.zeros((QT, LANES), F32)
    for di in range(len(BRANCHES)):
        w = jnp.exp(mall[di] - mtot)
        num = num + w * acc_ref[di]
        den = den + w * pltpu.roll(l_ref[di], HALF, axis=1)
    o_ref[...] = num / den


def _attn_prompt(qkv):
    bias = jnp.asarray(_attn_bias())
    npair = N_HEADS // 2
    nqt = SEQ // QT
    kv_rows = SEQ
    q_spec = pl.BlockSpec((QT, LANES), lambda b, hp, t: (b * nqt + t, hp))
    k_spec = pl.BlockSpec((kv_rows, LANES), lambda b, hp, t: (b, npair + hp))
    v_spec = pl.BlockSpec((kv_rows, LANES), lambda b, hp, t: (b, 2 * npair + hp))
    bias_spec = pl.BlockSpec((2, len(BRANCHES), 2, N_BACK, 2 * N_BACK), lambda b, hp, t: (0, 0, hp, 0, 0))
    return pl.pallas_call(
        _attn_prompt_kernel,
        name="attn_prompt",
        grid=(N_PROMPT_B, npair, nqt),
        in_specs=[q_spec, k_spec, v_spec, bias_spec],
        out_specs=pl.BlockSpec((QT, LANES), lambda b, hp, t: (b * nqt + t, hp)),
        out_shape=jax.ShapeDtypeStruct((N_P, ATTN_W), F32),
        scratch_shapes=[pltpu.VMEM((len(BRANCHES), QT, LANES), F32)] * 3,
        compiler_params=_cparams(("arbitrary", "arbitrary", "arbitrary")),
    )(qkv, qkv, qkv, bias)


ROWS_S = 8


def _attn_sample_kernel(q_ref, kn_ref, vn_ref, ck_ref, cv_ref, e_ref, et_ref, slope_ref, o_ref, kall_ref, vall_ref):
    npair = N_HEADS // 2
    for hp in range(npair):
        sl = slice(hp * LANES, (hp + 1) * LANES)
        kall_ref[hp, 0:W_BUF, :] = ck_ref[0, :, sl]
        vall_ref[hp, 0:W_BUF, :] = cv_ref[0, :, sl]
        kall_ref[hp, W_BUF:W_BUF + ROWS_S, :] = kn_ref[0, :, sl]
        vall_ref[hp, W_BUF:W_BUF + ROWS_S, :] = vn_ref[0, :, sl]

    def rows(ref, start, n, dil):
        idx = pl.ds(start, n) if dil == 1 else pl.ds(start, n, stride=dil)
        return jnp.concatenate([ref[hp, idx, :] for hp in range(npair)], axis=1)

    e01 = e_ref[...]
    et01 = et_ref[...]
    slope = slope_ref[...]
    jrev = (N_BACK - lax.broadcasted_iota(I32, (N_BACK, 1), 0)).astype(F32)
    inew = lax.broadcasted_iota(I32, (ROWS_S, 1), 0)
    scale = HEAD_DIM ** -0.5
    knew = rows(kall_ref, W_BUF, ROWS_S, 1)
    vnew = rows(vall_ref, W_BUF, ROWS_S, 1).astype(BF16).astype(F32)

    def scores(kr, qt):
        ph, plo = _split2(kr * qt)
        return (_dot(ph, e01) + _dot(plo, e01)) * scale

    outs = []
    for t in range(DEC_SEQ):
        qt = q_ref[0, t:t + 1, :]
        s_self = jnp.where(inew == t, scores(knew, qt), -1e30)
        s_br = []
        v_br = []
        for _, dil in BRANCHES:
            start = W_BUF + t - N_BACK * dil
            s_br.append(scores(rows(kall_ref, start, N_BACK, dil), qt) - (slope * dil) * jrev)
            v_br.append(rows(vall_ref, start, N_BACK, dil))
        m = jnp.max(s_self, axis=0, keepdims=True)
        for s in s_br:
            m = jnp.maximum(m, jnp.max(s, axis=0, keepdims=True))
        pb = (jnp.exp(s_self - m) * float(len(BRANCHES))).astype(BF16)
        den = jnp.sum(pb.astype(F32), axis=0, keepdims=True)
        num = jnp.sum(_dot(pb, et01) * vnew, axis=0, keepdims=True)
        for s, vr in zip(s_br, v_br):
            pb = jnp.exp(s - m).astype(BF16)
            den = den + jnp.sum(pb.astype(F32), axis=0, keepdims=True)
            num = num + jnp.sum(_dot(pb, et01) * vr.astype(BF16).astype(F32), axis=0, keepdims=True)
        denx = _dot_r01(jnp.broadcast_to(den, (ROWS_S, N_HEADS)), et01)[0:1]
        outs.append(num / denx)
    pad = jnp.zeros((ROWS_S - DEC_SEQ, ATTN_W), F32)
    o_ref[0] = jnp.concatenate(outs + [pad], axis=0)


def _head_expand():
    e = np.zeros((ATTN_W, N_HEADS), np.float32)
    for h in range(N_HEADS):
        e[h * HEAD_DIM:(h + 1) * HEAD_DIM, h] = 1.0
    return e


def _attn_sample(q_s, k_s, v_s, cache_k, cache_v):
    e = _head_expand()
    e01 = jnp.asarray(e, BF16)
    et01 = jnp.asarray(e.T, BF16)
    slopes = jnp.asarray([[2.0 ** (-8.0 * (h + 1) / N_HEADS) for h in range(N_HEADS)]], F32)
    row = pl.BlockSpec((1, ROWS_S, ATTN_W), lambda b: (b, 0, 0))
    cache = pl.BlockSpec((1, W_BUF, ATTN_W), lambda b: (b, 0, 0))
    return pl.pallas_call(
        _attn_sample_kernel,
        name="attn_sample",
        grid=(N_SAMPLE_B,),
        in_specs=[row, row, row, cache, cache, _full(e01.shape), _full(et01.shape), _full(slopes.shape)],
        out_specs=row,
        out_shape=jax.ShapeDtypeStruct((N_SAMPLE_B, ROWS_S, ATTN_W), F32),
        scratch_shapes=[pltpu.VMEM((N_HEADS // 2, W_BUF + ROWS_S, LANES), F32)] * 2,
        compiler_params=_cparams(("arbitrary",)),
    )(q_s, k_s, v_s, cache_k, cache_v, e01, et01, slopes)


HALO = 8


def _dn_pre_kernel(dn_ref, halo_ref, b_ref, a_ref, valid_ref, cw_ref, alog_ref, dtb_ref, e_ref, et_ref, ltri_ref,
                   q_ref, k_ref, v_ref, bx_ref, gx_ref, buf_ref):
    ts = dn_ref.shape[0]
    buf_ref[0:HALO, :] = halo_ref[0]
    buf_ref[HALO:HALO + ts, :] = dn_ref[...]
    y = jnp.zeros((ts, CONV_CH), F32)
    for i in range(CONV_W):
        y = y + buf_ref[pl.ds(HALO - (CONV_W - 1) + i, ts), :] * cw_ref[i:i + 1, :]
    y = y * _sigmoid(y)
    valid = valid_ref[...]
    e01 = e_ref[...]
    et01 = et_ref[...]

    def l2n(t):
        ss = _dot_r01(t * t, e01)
        inv = lax.rsqrt(ss + NORM_EPS)
        return t * _dot_r01(inv, et01)

    q_ref[...] = l2n(y[:, :DN_W]) * (HEAD_DIM ** -0.5)
    k_ref[...] = l2n(y[:, DN_W:2 * DN_W]) * valid
    v_ref[...] = y[:, 2 * DN_W:] * valid
    beta = _sigmoid(b_ref[...]) * valid
    sp_in = a_ref[...] + dtb_ref[...]
    softplus = jnp.maximum(sp_in, 0.0) + jnp.log1p(jnp.exp(-jnp.abs(sp_in)))
    g = -jnp.exp(alog_ref[...]) * softplus * valid
    bx_ref[...] = _dot_r01(beta, et01)
    gx_ref[...] = _dot_l01(ltri_ref[...], _dot_r01(g, et01))


def _dn_pre(dn, halo, b_logit, a_logit, valid, conv_w, a_log, dt_bias):
    nt = halo.shape[0]
    rows = nt * TM
    e = _head_expand()
    e01 = jnp.asarray(e, BF16)
    et01 = jnp.asarray(e.T, BF16)
    ii = np.arange(TM)
    ltri = jnp.asarray(((ii[:, None] >= ii[None, :]) & (ii[:, None] // DN_C == ii[None, :] // DN_C)), BF16)
    row = lambda w: pl.BlockSpec((TM, w), lambda i: (i, 0))
    return pl.pallas_call(
        _dn_pre_kernel,
        name="dn_pre",
        grid=(nt,),
        in_specs=[row(CONV_CH), pl.BlockSpec((1, HALO, CONV_CH), lambda i: (i, 0, 0)), row(N_HEADS), row(N_HEADS),
                  row(1), _full(conv_w.shape), _full((1, N_HEADS)), _full((1, N_HEADS)),
                  _full(e01.shape), _full(et01.shape), _full(ltri.shape)],
        out_specs=[row(DN_W)] * 5,
        out_shape=[jax.ShapeDtypeStruct((rows, DN_W), F32)] * 5,
        scratch_shapes=[pltpu.VMEM((HALO + TM, CONV_CH), F32)],
        compiler_params=_cparams(("arbitrary",)),
    )(dn, halo, b_logit, a_logit, valid, conv_w, a_log.reshape(1, N_HEADS), dt_bias.reshape(1, N_HEADS),
      e01, et01, ltri)


def _delta_streams(q, k, v, bx, gcx, s_bd):
    c = DN_C
    n = len(q)
    each = range(n)
    lane = lax.broadcasted_iota(I32, (1, LANES), 1)
    h0 = lane < HALF
    m0 = h0.astype(F32)
    m1 = 1.0 - m0
    ii = lax.broadcasted_iota(I32, (2 * c, 2 * c), 0)
    jj = lax.broadcasted_iota(I32, (2 * c, 2 * c), 1)
    same = jnp.where(ii < c, 0, 1) == jnp.where(jj < c, 0, 1)
    order = jnp.where(same, ii - jj, -1)
    tril = order >= 0
    strict = order > 0
    eye = jnp.where(ii == jj, 1.0, 0.0)

    def stack(x):
        return jnp.concatenate([x * m0, x * m1], axis=0)

    def fold(x2):
        return x2[:c] + x2[c:]

    def decay_of(g):
        gsw = pltpu.roll(g, HALF, axis=1)
        row_b = jnp.concatenate([jnp.where(h0, g, gsw), jnp.where(h0, gsw, g)], axis=0)
        return jnp.where(tril, jnp.exp(jnp.where(tril, row_b - row_b.T, 0.0)), 0.0)

    eg = [jnp.exp(gcx[i]) for i in each]
    kb = [k[i] * bx[i] for i in each]
    glast = [gcx[i][c - 1:c, :] for i in each]
    decay = [decay_of(gcx[i]) for i in each]
    k2 = [stack(k[i]).astype(BF16) for i in each]
    lhs = [jnp.concatenate([stack(kb[i]), stack(q[i])], axis=0).astype(BF16) for i in each]
    aq = [_dot_nt(lhs[i], k2[i]) for i in each]
    a_low = [jnp.where(strict, aq[i][:2 * c] * decay[i], 0.0) for i in each]
    aqk = [jnp.where(tril, aq[i][2 * c:] * decay[i], 0.0).astype(BF16) for i in each]
    tinv = [eye - a_low[i] for i in each]
    apow = [a_low[i].astype(BF16) for i in each]
    for _ in range(5):
        apow = [_dot(apow[i], apow[i]).astype(BF16) for i in each]
        tinv = [tinv[i] + _dot(tinv[i].astype(BF16), apow[i]) for i in each]
    rhs = [jnp.concatenate([stack(v[i] * bx[i]), stack(kb[i] * eg[i])], axis=1).astype(BF16) for i in each]
    uw = [_dot(tinv[i].astype(BF16), rhs[i]) for i in each]
    u = [fold(uw[i][:, :LANES]) for i in each]
    w = [fold(uw[i][:, LANES:]) for i in each]
    wq = [jnp.concatenate([w[i], q[i] * eg[i]], axis=0).astype(BF16) for i in each]
    kd = [(k[i] * jnp.exp(glast[i] - gcx[i])).T.astype(BF16) for i in each]
    gt = [jnp.exp(glast[i]) for i in each]
    ns = len(s_bd)
    state = list(s_bd)
    o = [None] * n
    for sub in range(n // ns):
        js = [sub * ns + i for i in range(ns)]
        ws = [_dot(wq[j], state[i].astype(BF16)) for i, j in enumerate(js)]
        v_new = [u[j] - ws[i][:c] for i, j in enumerate(js)]
        intra = [_dot(aqk[j], stack(v_new[i]).astype(BF16)) for i, j in enumerate(js)]
        upd = [_dot(kd[j], v_new[i].astype(BF16)) for i, j in enumerate(js)]
        for i, j in enumerate(js):
            o[j] = ws[i][c:] + fold(intra[i])
            state[i] = state[i] * gt[j] + jnp.where(same, upd[i], 0.0)
    return o, state


def _delta_kernel(q_ref, k_ref, v_ref, bx_ref, gx_ref, s0_ref, o_ref, sout_ref, s_ref):
    ci = pl.program_id(1)
    nb = q_ref.shape[0]
    npair = N_HEADS // 2

    @pl.when(ci == 0)
    def _():
        s_ref[...] = s0_ref[...]

    streams = [(b, hp) for b in range(nb) for hp in range(npair)]
    chunks = [(slice(sub * DN_C, (sub + 1) * DN_C), b, slice(hp * LANES, (hp + 1) * LANES))
              for sub in range(q_ref.shape[1] // DN_C) for b, hp in streams]
    take = lambda ref: [ref[b, rows, lanes] for rows, b, lanes in chunks]
    o, s_new = _delta_streams(take(q_ref), take(k_ref), take(v_ref), take(bx_ref), take(gx_ref),
                              [s_ref[b, hp] for b, hp in streams])
    for (rows, b, lanes), oc in zip(chunks, o):
        o_ref[b, rows, lanes] = oc
    for (b, hp), sn in zip(streams, s_new):
        s_ref[b, hp] = sn

    @pl.when(ci == pl.num_programs(1) - 1)
    def _():
        sout_ref[...] = s_ref[...]


def _delta(q, k, v, bx, gx, s0_bd, bblk, nsub):
    nbatch, length, _ = q.shape
    npair = N_HEADS // 2
    seq = pl.BlockSpec((bblk, nsub * DN_C, DN_W), lambda g, c: (g, c, 0))
    st = pl.BlockSpec((bblk, npair, LANES, LANES), lambda g, c: (g, 0, 0, 0))
    return pl.pallas_call(
        _delta_kernel,
        name="delta",
        grid=(nbatch // bblk, length // (nsub * DN_C)),
        in_specs=[seq] * 5 + [st],
        out_specs=[seq, st],
        out_shape=[jax.ShapeDtypeStruct((nbatch, length, DN_W), F32),
                   jax.ShapeDtypeStruct((nbatch, npair, LANES, LANES), F32)],
        scratch_shapes=[pltpu.VMEM((bblk, npair, LANES, LANES), F32)],
        compiler_params=_cparams(("arbitrary", "arbitrary")),
    )(q, k, v, bx, gx, s0_bd)


def _state_to_bd(s):
    b = s.shape[0]
    s = s.reshape(b, N_HEADS // 2, 2, HALF, HALF)
    zero = jnp.zeros_like(s[:, :, 0])
    top = jnp.concatenate([s[:, :, 0], zero], axis=-1)
    bot = jnp.concatenate([zero, s[:, :, 1]], axis=-1)
    return jnp.concatenate([top, bot], axis=-2)


def _state_from_bd(s):
    b = s.shape[0]
    return jnp.stack([s[:, :, :HALF, :HALF], s[:, :, HALF:, HALF:]], axis=2).reshape(b, N_HEADS, HALF, HALF)


def _out_kernel(xp_ref, xs_ref, ap_ref, as_ref, dp_ref, ds_ref, z_ref, ng_ref, bavg_ref, wa_ref, wd_ref, g_ref, b_ref,
                o_ref):
    o = _pick(dp_ref, ds_ref)
    oh, ol = _split2(o * o)
    ms = _dot(oh, bavg_ref[...]) + _dot(ol, bavg_ref[...])
    z = z_ref[...]
    dn = o * lax.rsqrt(ms + NORM_EPS) * ng_ref[...] * (z * _sigmoid(z))
    mix = _dot(_pick(ap_ref, as_ref).astype(BF16), wa_ref[...]) + _dot(dn.astype(BF16), wd_ref[...])
    o_ref[...] = _layer_norm(ALPHA * _pick(xp_ref, xs_ref) + mix, g_ref[...], b_ref[...])


def _out_proj(x_p, x_s, att_p, att_s, dn_p, dn_s, z, norm_g, w_out, g, b):
    wa = w_out[:ATTN_W].astype(BF16)
    wd = w_out[ATTN_W:].astype(BF16)
    normg = jnp.tile(norm_g.reshape(1, HEAD_DIM), (1, N_HEADS))
    bavg = jnp.asarray(np.kron(np.eye(N_HEADS), np.ones((HALF, HALF))) / HALF, BF16)
    row = lambda w: pl.BlockSpec((TB, w), lambda i: (i, 0))
    return pl.pallas_call(
        _out_kernel,
        name="out_proj",
        grid=(N_PAD // TB,),
        in_specs=_row2(D_MODEL, TB) + _row2(ATTN_W, TB) + _row2(DN_W, TB) + [
            row(DN_W), _full(normg.shape), _full(bavg.shape),
            _full(wa.shape), _full(wd.shape), _full((1, D_MODEL)), _full((1, D_MODEL))],
        out_specs=row(D_MODEL),
        out_shape=jax.ShapeDtypeStruct((N_PAD, D_MODEL), F32),
        compiler_params=_cparams(("arbitrary",)),
    )(x_p, x_s, att_p, att_s, dn_p, dn_s, z, normg, bavg, wa, wd, g.reshape(1, -1), b.reshape(1, -1))


N_GROUPS = 8
GROUP = N_EXPERTS // N_GROUPS
SLOT_RADIX = 64
_ET_SPEC = pl.BlockSpec((1, 2 * N_EXPERTS, TM), lambda i, *_: (i, 0, 0))
_SEG_SPEC = pl.BlockSpec((1, 1, 2 * N_EXPERTS), lambda i, *_: (i, 0, 0))
TOPK_GROUPS = 4
NEG = -jnp.inf


def _route_kernel(x_ref, wh_ref, wl_ref, bias_ref, upper_ref, lows_ref, st_ref, ge_ref, cnt_ref):
    t = pl.program_id(0)
    xh, xl = _split2(x_ref[...])
    wh = wh_ref[...]
    logits = _dot_nt(wh, xh) + _dot_nt(wh, xl) + _dot_nt(wl_ref[...], xh)
    scores = _sigmoid(logits)
    choice = scores + bias_ref[...]
    i8 = lax.broadcasted_iota(I32, (GROUP, TM), 0)

    def first_max(vals):
        m = jnp.max(vals, axis=0, keepdims=True)
        idx = jnp.min(jnp.where(vals == m, i8, GROUP), axis=0, keepdims=True)
        return m, idx

    pieces = [choice[g * GROUP:(g + 1) * GROUP, :] for g in range(N_GROUPS)]
    gsc = jnp.zeros((N_GROUPS, TM), F32)
    for g in range(N_GROUPS):
        m1, idx1 = first_max(pieces[g])
        m2 = jnp.max(jnp.where(i8 == idx1, NEG, pieces[g]), axis=0, keepdims=True)
        gsc = jnp.where(i8 == g, m1 + m2, gsc)
    gsel = jnp.zeros((N_GROUPS, TM), F32)
    for _ in range(TOPK_GROUPS):
        _, idx = first_max(gsc)
        hit = i8 == idx
        gsel = jnp.where(hit, 1.0, gsel)
        gsc = jnp.where(hit, NEG, gsc)
    masked = [jnp.where(gsel[g:g + 1, :] > 0.5, pieces[g], NEG) for g in range(N_GROUPS)]
    eidx = [i8 + g * GROUP for g in range(N_GROUPS)]
    member = [jnp.zeros((GROUP, TM), F32) for _ in range(N_GROUPS)]
    for _ in range(TOP_K):
        mm = masked[0]
        for g in range(1, N_GROUPS):
            mm = jnp.maximum(mm, masked[g])
        m = jnp.max(mm, axis=0, keepdims=True)
        cand = jnp.where(masked[0] == m, eidx[0], N_EXPERTS)
        for g in range(1, N_GROUPS):
            cand = jnp.minimum(cand, jnp.where(masked[g] == m, eidx[g], N_EXPERTS))
        idx = jnp.min(cand, axis=0, keepdims=True)
        for g in range(N_GROUPS):
            hit = eidx[g] == idx
            member[g] = jnp.where(hit, 1.0, member[g])
            masked[g] = jnp.where(hit, NEG, masked[g])

    tok = t * TM + lax.broadcasted_iota(I32, (1, TM), 1)
    real = jnp.where(tok < N_REAL, 1.0, 0.0)
    mem = jnp.concatenate(member, axis=0) * real
    sel = mem * scores
    gate = (sel / jnp.sum(sel + (1.0 - real), axis=0, keepdims=True) * ROUTED_SCALE).astype(BF16)
    ge_ref[0] = jnp.concatenate([gate, jnp.zeros_like(gate)], axis=0)
    memb = mem.astype(BF16)
    rank = _dot(memb, upper_ref[...])
    cnt = jnp.sum(mem, axis=1, keepdims=True)
    cpad = jnp.maximum(jnp.floor((cnt + (SEG_ALIGN - 1)) * (1.0 / SEG_ALIGN)), 1.0)
    loc = _dot(lows_ref[...], jnp.broadcast_to(cpad, (N_EXPERTS, TM)).astype(BF16)) * SEG_ALIGN
    slot_e = jnp.where(mem > 0.5, loc + rank, float(M_T))
    slot_hi = jnp.floor(slot_e * (1.0 / SLOT_RADIX)) * SLOT_RADIX
    st_ref[0] = jnp.concatenate([slot_hi, slot_e - slot_hi + 1.0], axis=0).astype(BF16)
    ones = jnp.ones((GROUP, TM), BF16)
    cnt_ref[0] = _dot_nt(ones, memb).astype(I32)


def _route(x1, w_router, router_bias):
    wt = w_router.T
    wh, wl = _split2(wt)
    ii = np.arange(TM)
    upper = jnp.asarray(ii[:, None] < ii[None, :], BF16)
    ee = np.arange(N_EXPERTS)
    lows = jnp.asarray(ee[:, None] > ee[None, :], BF16)
    return pl.pallas_call(
        _route_kernel,
        name="route",
        grid=(N_TILES,),
        in_specs=[pl.BlockSpec((TM, D_MODEL), lambda i: (i, 0)), _full(wh.shape), _full(wl.shape),
                  _full((N_EXPERTS, 1)), _full(upper.shape), _full(lows.shape)],
        out_specs=[_ET_SPEC, _ET_SPEC, pl.BlockSpec((1, GROUP, N_EXPERTS), lambda i: (i, 0, 0))],
        out_shape=[jax.ShapeDtypeStruct((N_TILES, 2 * N_EXPERTS, TM), BF16)] * 2 + [
            jax.ShapeDtypeStruct((N_TILES, GROUP, N_EXPERTS), I32)],
        compiler_params=_cparams(("arbitrary",)),
    )(x1, wh, wl, router_bias.reshape(N_EXPERTS, 1), upper, lows)


def _segment_tables(cnt):
    seg = jnp.maximum((cnt + SEG_ALIGN - 1) // SEG_ALIGN, 1)
    cpad = seg * SEG_ALIGN
    loc_off = jnp.cumsum(cpad, axis=1) - cpad
    rows_e = jnp.sum(cpad, axis=0)
    blocks_e = (rows_e + BM - 1) // BM
    bend = jnp.cumsum(blocks_e)
    bstart = bend - blocks_e
    glob_off = (bstart * BM)[None, :] + jnp.cumsum(cpad, axis=0) - cpad
    nb = bend[-1]
    pad_off = bstart * BM + rows_e
    pad_seg = (blocks_e * BM - rows_e) // SEG_ALIGN
    i32 = lambda a: a.reshape(-1).astype(I32)
    bounds = lambda a: jnp.tile(a.astype(F32), (1, 2)).reshape(N_TILES, 1, 2 * N_EXPERTS)
    return (i32(seg), i32(loc_off), i32(glob_off), i32(pad_seg), i32(pad_off), i32(bstart), i32(blocks_e), i32(nb),
            bounds(loc_off), bounds(loc_off + cpad))


def _rows_copy(src_of, dst_of, lo, go, rows, sem):
    lo, go, rows = (v if isinstance(v, int) else pl.multiple_of(v, SEG_ALIGN) for v in (lo, go, rows))
    return pltpu.make_async_copy(src_of(lo, go, rows), dst_of(lo, go, rows), sem)


def _segment_copies(seg_ref, loc_ref, glob_ref, t, src_of, dst_of, sem):
    for e in range(N_EXPERTS):
        i = t * N_EXPERTS + e
        _rows_copy(src_of, dst_of, loc_ref[i], glob_ref[i], seg_ref[i] * SEG_ALIGN, sem).start()


def _wait_rows(src_of, dst_of, rows, sem):
    _rows_copy(src_of, dst_of, 0, 0, rows, sem).wait()


def _local_rows(m0, rows, lo, hi, slot_tab):
    mcol = (lax.broadcasted_iota(I32, (rows, 2 * N_EXPERTS), 0) + m0).astype(F32)
    e01 = jnp.where(mcol >= lo, jnp.where(mcol < hi, 1.0, 0.0), 0.0).astype(BF16)
    mrow = (lax.broadcasted_iota(I32, (rows, TM), 0) + (m0 + 1)).astype(F32)
    return _dot(e01, slot_tab) == mrow, e01


def _row_spans(used):
    yield None, 0, M_MAIN
    for m0 in range(M_MAIN, M_T, M_CHUNK):
        yield m0 < used, m0, M_CHUNK


def _tile_rows(seg_ref, loc_ref, t):
    last = t * N_EXPERTS + N_EXPERTS - 1
    return loc_ref[last] + seg_ref[last] * SEG_ALIGN


def _dispatch_kernel(seg_ref, loc_ref, glob_ref, pseg_ref, poff_ref, nb_ref, x_ref, st_ref, lo_ref, hi_ref,
                     xs_ref, buf_ref, sem):
    t = pl.program_id(0)
    s = t % 2
    used = _tile_rows(seg_ref, loc_ref, t)
    for live, m0, rows in _row_spans(used):
        def sort_rows(m0=m0, rows=rows):
            hit, _ = _local_rows(m0, rows, lo_ref[0], hi_ref[0], st_ref[0])
            onehot = jnp.where(hit, 1.0, 0.0).astype(BF16)
            buf_ref[s, m0:m0 + rows, :] = _dot(onehot, x_ref[...].astype(BF16)).astype(BF16)

        if live is None:
            sort_rows()
        else:
            pl.when(live)(sort_rows)

    src = lambda lo, go, rows: buf_ref.at[s, pl.ds(lo, rows), :]
    dst = lambda lo, go, rows: xs_ref.at[pl.ds(go, rows), :]
    _segment_copies(seg_ref, loc_ref, glob_ref, t, src, dst, sem.at[s])

    @pl.when(t > 0)
    def _():
        prev = lambda lo, go, rows: buf_ref.at[1 - s, pl.ds(lo, rows), :]
        _wait_rows(prev, dst, _tile_rows(seg_ref, loc_ref, t - 1), sem.at[1 - s])

    @pl.when(t == N_TILES - 1)
    def _():
        _wait_rows(src, dst, used, sem.at[s])
        buf_ref[s, 0:BM, :] = jnp.zeros((BM, D_MODEL), BF16)
        zeros = lambda lo, go, rows: buf_ref.at[s, pl.ds(0, rows), :]

        def fill(wait):
            def pad(e, carry):
                k = pseg_ref[e]

                @pl.when(k > 0)
                def _():
                    cp = _rows_copy(zeros, dst, 0, poff_ref[e], k * SEG_ALIGN, sem.at[s])
                    cp.wait() if wait else cp.start()

                return carry

            lax.fori_loop(0, N_EXPERTS, pad, 0)

            def tail(b, carry):
                cp = _rows_copy(zeros, dst, 0, b * BM, BM, sem.at[s])
                cp.wait() if wait else cp.start()
                return carry

            lax.fori_loop(nb_ref[0], NB_MAX, tail, 0)

        fill(False)
        fill(True)


def _dispatch(x1, slot_tab, seg_lo, seg_hi, tables):
    grid_spec = pltpu.PrefetchScalarGridSpec(
        num_scalar_prefetch=len(tables),
        grid=(N_TILES,),
        in_specs=[pl.BlockSpec((TM, D_MODEL), lambda i, *_: (i, 0)), _ET_SPEC, _SEG_SPEC, _SEG_SPEC],
        out_specs=pl.BlockSpec(memory_space=pl.ANY),
        scratch_shapes=[pltpu.VMEM((2, M_T, D_MODEL), BF16), pltpu.SemaphoreType.DMA((2,))],
    )
    return pl.pallas_call(
        _dispatch_kernel,
        name="dispatch",
        grid_spec=grid_spec,
        out_shape=jax.ShapeDtypeStruct((NB_MAX * BM, D_MODEL), BF16),
        compiler_params=_cparams(("arbitrary",)),
    )(*tables, x1, slot_tab, seg_lo, seg_hi)


def _expert_kernel(bstart_ref, bcount_ref, nb_ref, xs_ref, wg_ref, wu_ref, wd_ref, ys_ref,
                   xbuf, ybuf, wgu_s, wd_s, sem_in, sem_out):
    e = pl.program_id(0)
    nb = nb_ref[0]
    b0 = bstart_ref[e]
    n = bcount_ref[e]

    def rows_of(ref, g):
        return ref.at[pl.ds(pl.multiple_of(g * BM, BM), BM), :]

    def x_copy(g):
        return pltpu.make_async_copy(rows_of(xs_ref, g), xbuf.at[g % EXPERT_SLOTS], sem_in.at[g % EXPERT_SLOTS])

    def y_copy(g):
        return pltpu.make_async_copy(ybuf.at[g % EXPERT_SLOTS], rows_of(ys_ref, g), sem_out.at[g % EXPERT_SLOTS])

    @pl.when(e == 0)
    def _():
        for g in range(EXPERT_SLOTS - 1):
            @pl.when(g < nb)
            def _(g=g):
                x_copy(g).start()

    wgu_s[:, :EXPERT_DIM] = wg_ref[0].astype(BF16)
    wgu_s[:, EXPERT_DIM:] = wu_ref[0].astype(BF16)
    wd_s[...] = wd_ref[0].astype(BF16)

    def block(i, carry):
        g = b0 + i
        slot = g % EXPERT_SLOTS

        @pl.when(g + EXPERT_SLOTS - 1 < nb)
        def _():
            x_copy(g + EXPERT_SLOTS - 1).start()

        x_copy(g).wait()

        @pl.when(g >= EXPERT_SLOTS)
        def _():
            y_copy(g - EXPERT_SLOTS).wait()

        h = _dot(xbuf[slot], wgu_s[...])
        gate = h[:, :EXPERT_DIM]
        a = (gate * _sigmoid(gate) * h[:, EXPERT_DIM:]).astype(BF16)
        ybuf[slot] = _dot(a, wd_s[...]).astype(BF16)
        y_copy(g).start()
        return carry

    lax.fori_loop(0, n, block, 0)

    @pl.when(e == N_EXPERTS - 1)
    def _():
        for k in range(EXPERT_SLOTS, 0, -1):
            @pl.when(nb >= k)
            def _(k=k):
                y_copy(nb - k).wait()

        ybuf[0] = jnp.zeros((BM, D_MODEL), BF16)

        def tail(wait):
            def body(g, carry):
                cp = pltpu.make_async_copy(ybuf.at[0], rows_of(ys_ref, g), sem_out.at[0])
                cp.wait() if wait else cp.start()
                return carry

            lax.fori_loop(nb, NB_MAX, body, 0)

        tail(False)
        tail(True)


def _experts(xs, we_gate, we_up, we_down, bstart, bcount, nb):
    weight = lambda shape: pl.BlockSpec((1,) + shape, lambda e, *_: (e, 0, 0))
    grid_spec = pltpu.PrefetchScalarGridSpec(
        num_scalar_prefetch=3,
        grid=(N_EXPERTS,),
        in_specs=[pl.BlockSpec(memory_space=pl.ANY), weight((D_MODEL, EXPERT_DIM)), weight((D_MODEL, EXPERT_DIM)),
                  weight((EXPERT_DIM, D_MODEL))],
        out_specs=pl.BlockSpec(memory_space=pl.ANY),
        scratch_shapes=[pltpu.VMEM((EXPERT_SLOTS, BM, D_MODEL), BF16), pltpu.VMEM((EXPERT_SLOTS, BM, D_MODEL), BF16),
                        pltpu.VMEM((D_MODEL, 2 * EXPERT_DIM), BF16), pltpu.VMEM((EXPERT_DIM, D_MODEL), BF16),
                        pltpu.SemaphoreType.DMA((EXPERT_SLOTS,)), pltpu.SemaphoreType.DMA((EXPERT_SLOTS,))],
    )
    return pl.pallas_call(
        _expert_kernel,
        name="expert",
        grid_spec=grid_spec,
        out_shape=jax.ShapeDtypeStruct((NB_MAX * BM, D_MODEL), BF16),
        compiler_params=_cparams(("arbitrary",)),
    )(bstart, bcount, nb, xs, we_gate, we_up, we_down)


def _combine_kernel(seg_ref, loc_ref, glob_ref, x_ref, st_ref, ge_ref, lo_ref, hi_ref, pp_ref, ps_ref, ys_ref,
                    wsgu_ref, wsd_ref, g2_ref, b2_ref, wpg_ref, wpp_ref, g3_ref, b3_ref, op_ref, os_ref,
                    buf_ref, acc_ref, sem):
    t = pl.program_id(0)
    s = t % 2
    used = _tile_rows(seg_ref, loc_ref, t)
    src = lambda lo, go, rows: ys_ref.at[pl.ds(go, rows), :]
    into = lambda slot: (lambda lo, go, rows: buf_ref.at[slot, pl.ds(lo, rows), :])

    @pl.when(t == 0)
    def _():
        buf_ref[...] = jnp.zeros((2, M_T, D_MODEL), BF16)
        _segment_copies(seg_ref, loc_ref, glob_ref, t, src, into(s), sem.at[s])

    @pl.when(t + 1 < N_TILES)
    def _():
        _segment_copies(seg_ref, loc_ref, glob_ref, t + 1, src, into(1 - s), sem.at[1 - s])

    _wait_rows(src, into(s), used, sem.at[s])

    for live, m0, rows in _row_spans(used):
        def gather_rows(m0=m0, rows=rows):
            hit, e01 = _local_rows(m0, rows, lo_ref[0], hi_ref[0], st_ref[0])
            pg = jnp.where(hit, _dot(e01, ge_ref[0]), 0.0).astype(BF16)
            return lax.dot_general(pg, buf_ref[s, m0:m0 + rows, :], (((0,), (0,)), ((), ())),
                                   preferred_element_type=F32)

        if live is None:
            acc_ref[...] = gather_rows()
        else:
            @pl.when(live)
            def _(gather_rows=gather_rows):
                acc_ref[...] += gather_rows()

    x = x_ref[...]
    routed = acc_ref[...]
    xb = x.astype(BF16)
    h = _dot(xb, wsgu_ref[...])
    g = h[:, :EXPERT_DIM]
    shared = _dot((g * _sigmoid(g) * h[:, EXPERT_DIM:]).astype(BF16), wsd_ref[...])
    x2 = _layer_norm(ALPHA * x + (routed + shared), g2_ref[...], b2_ref[...])
    ple = _sigmoid(_dot(x2.astype(BF16), wpg_ref[...])) * _dot(_pick(pp_ref, ps_ref).astype(BF16), wpp_ref[...])
    y = _layer_norm(ALPHA * x2 + ple, g3_ref[...], b3_ref[...])

    @pl.when(t < NT_P)
    def _():
        op_ref[...] = y

    @pl.when(t >= NT_P)
    def _():
        os_ref[...] = y


def _combine(x1, slot_tab, gate_e, seg_lo, seg_hi, p_p, p_s, ys, seg, loc_off, glob_off, ws_gate, ws_up, ws_down,
             ln2_g, ln2_b, w_ple_gate, w_ple_proj, ln3_g, ln3_b):
    wsgu = jnp.concatenate([ws_gate, ws_up], axis=1).astype(BF16)
    wsd = ws_down.astype(BF16)
    wpg = w_ple_gate.astype(BF16)
    wpp = w_ple_proj.astype(BF16)
    row = lambda w: pl.BlockSpec((TM, w), lambda i, *_: (i, 0))
    full = lambda shape: pl.BlockSpec(shape, lambda i, *_: (0,) * len(shape))
    vec = full((1, D_MODEL))
    grid_spec = pltpu.PrefetchScalarGridSpec(
        num_scalar_prefetch=3,
        grid=(N_TILES,),
        in_specs=[row(D_MODEL), _ET_SPEC, _ET_SPEC, _SEG_SPEC, _SEG_SPEC] + _row2(PLE_DIM) + [
            pl.BlockSpec(memory_space=pl.ANY),
            full(wsgu.shape), full(wsd.shape), vec, vec, full(wpg.shape), full(wpp.shape), vec, vec],
        out_specs=_row2(D_MODEL),
        scratch_shapes=[pltpu.VMEM((2, M_T, D_MODEL), BF16), pltpu.VMEM((TM, D_MODEL), F32),
                        pltpu.SemaphoreType.DMA((2,))],
    )
    r = lambda a: a.reshape(1, -1)
    return pl.pallas_call(
        _combine_kernel,
        name="combine",
        grid_spec=grid_spec,
        out_shape=[jax.ShapeDtypeStruct((N_P, D_MODEL), F32), jax.ShapeDtypeStruct((N_SP, D_MODEL), F32)],
        compiler_params=_cparams(("arbitrary",)),
    )(seg, loc_off, glob_off, x1, slot_tab, gate_e, seg_lo, seg_hi, p_p, p_s, ys, wsgu, wsd, r(ln2_g), r(ln2_b),
      wpg, wpp, r(ln3_g), r(ln3_b))


def _pad_rows(a, rows):
    return jnp.concatenate([a, jnp.zeros((rows - a.shape[0],) + a.shape[1:], a.dtype)], axis=0)


def kernel(x_prompt, x_sample, cache_k_win, cache_v_win, state_conv, state_delta, p_prompt, p_sample, w_in, conv_w, a_log, dt_bias, dn_norm_g, w_out, ln1_g, ln1_b, w_router, router_bias, we_gate, we_up, we_down, ws_gate, ws_up, ws_down, ln2_g, ln2_b, w_ple_gate, w_ple_proj, ln3_g, ln3_b):
    x_p = x_prompt.reshape(N_P, D_MODEL)
    x_s = _pad_rows(x_sample.reshape(N_S, D_MODEL), N_SP)
    p_p = p_prompt[0].reshape(N_P, PLE_DIM)
    p_s = _pad_rows(p_sample[0].reshape(N_S, PLE_DIM), N_SP)

    qkv, dn_in, z, b_logit, a_logit = _project(x_p, x_s, w_in[0])

    att_p = _attn_prompt(qkv)
    qkv_s = qkv[N_P:N_REAL].reshape(N_SAMPLE_B, DEC_SEQ, 3 * ATTN_W)
    qkv_s8 = jnp.pad(qkv_s, ((0, 0), (0, ROWS_S - DEC_SEQ), (0, 0)))
    ck = cache_k_win[0].reshape(N_SAMPLE_B, W_BUF, ATTN_W)
    cv = cache_v_win[0].reshape(N_SAMPLE_B, W_BUF, ATTN_W)
    att_s = _attn_sample(qkv_s8[:, :, :ATTN_W], qkv_s8[:, :, ATTN_W:2 * ATTN_W], qkv_s8[:, :, 2 * ATTN_W:], ck, cv)
    att_s = _pad_rows(att_s[:, :DEC_SEQ].reshape(N_S, ATTN_W), N_SP)

    tails = dn_in.reshape(N_TILES, TM, CONV_CH)[:NT_P - 1, TM - HALO:]
    halo_p = jnp.concatenate([jnp.zeros((1, HALO, CONV_CH), F32), tails], axis=0)
    seq_start = (jnp.arange(NT_P) % (SEQ // TM) == 0)[:, None, None]
    halo_p = jnp.where(seq_start, 0.0, halo_p)
    q_p, k_p, v_p, bx_p, gx_p = _dn_pre(dn_in, halo_p, b_logit, a_logit, jnp.ones((N_P, 1), F32),
                                        conv_w[0], a_log[0], dt_bias[0])
    shp = (N_PROMPT_B, SEQ, DN_W)
    s0_p = jnp.zeros((N_PROMPT_B, N_HEADS // 2, LANES, LANES), F32)
    o_p, s_p = _delta(q_p.reshape(shp), k_p.reshape(shp), v_p.reshape(shp), bx_p.reshape(shp), gx_p.reshape(shp),
                      s0_p, N_PROMPT_B, 4)

    def seq_pad(tok, state=None):
        w = tok.shape[-1]
        tok = tok.reshape(N_SAMPLE_B, DEC_SEQ, w)
        head = jnp.zeros((N_SAMPLE_B, S_TOK0, w), F32)
        if state is not None:
            head = head.at[:, S_TOK0 - (CONV_W - 1):].set(state)
        tail = jnp.zeros((N_SAMPLE_B, SEQ_S - S_TOK0 - DEC_SEQ, w), F32)
        return jnp.concatenate([head, tok, tail], axis=1).reshape(N_SAMPLE_B * SEQ_S, w)

    dn_tok_s = dn_in[N_P:N_REAL]
    dn_s = seq_pad(dn_tok_s, state_conv[0])
    rows_s = N_SAMPLE_B * SEQ_S
    valid_s = seq_pad(jnp.ones((N_S, 1), F32))
    halo_s = jnp.zeros((rows_s // TM, HALO, CONV_CH), F32)
    q_s, k_s, v_s, bx_s, gx_s = _dn_pre(dn_s, halo_s, seq_pad(b_logit[N_P:N_REAL]), seq_pad(a_logit[N_P:N_REAL]),
                                        valid_s, conv_w[0], a_log[0], dt_bias[0])
    shs = (N_SAMPLE_B, SEQ_S, DN_W)
    o_s, s_s = _delta(q_s.reshape(shs), k_s.reshape(shs), v_s.reshape(shs), bx_s.reshape(shs), gx_s.reshape(shs),
                      _state_to_bd(state_delta[0]), 4, 1)
    dn_o_s = _pad_rows(o_s[:, S_TOK0:S_TOK0 + DEC_SEQ].reshape(N_S, DN_W), N_SP)

    x1 = _out_proj(x_p, x_s, att_p, att_s, o_p.reshape(N_P, DN_W), dn_o_s, z, dn_norm_g[0], w_out[0], ln1_g[0], ln1_b[0])

    slot_tab, gate_e, cnt = _route(x1, w_router[0], router_bias[0])
    seg, loc_off, glob_off, pad_seg, pad_off, bstart, bcount, nb, seg_lo, seg_hi = _segment_tables(cnt[:, 0, :])
    xs = _dispatch(x1, slot_tab, seg_lo, seg_hi, (seg, loc_off, glob_off, pad_seg, pad_off, nb))
    ys = _experts(xs, we_gate[0], we_up[0], we_down[0], bstart, bcount, nb)
    y_p, y_s = _combine(x1, slot_tab, gate_e, seg_lo, seg_hi, p_p, p_s, ys, seg, loc_off, glob_off,
                        ws_gate[0], ws_up[0], ws_down[0], ln2_g[0], ln2_b[0], w_ple_gate[0], w_ple_proj[0],
                        ln3_g[0], ln3_b[0])

    y_prompt = y_p.reshape(N_PROMPT_B, SEQ, D_MODEL)
    y_sample = y_s[:N_S].reshape(N_SAMPLE_B, DEC_SEQ, D_MODEL)
    heads = (N_HEADS, HEAD_DIM)
    win = lambda c0: jnp.stack([qkv[(b + 1) * SEQ - W_BUF:(b + 1) * SEQ, c0:c0 + ATTN_W] for b in range(N_PROMPT_B)])
    k_pr = win(ATTN_W).reshape(N_PROMPT_B, W_BUF, *heads)
    v_pr = win(2 * ATTN_W).reshape(N_PROMPT_B, W_BUF, *heads)
    conv_p = jnp.stack([dn_in[(b + 1) * SEQ - (CONV_W - 1):(b + 1) * SEQ] for b in range(N_PROMPT_B)])
    k_new = qkv_s[:, :, ATTN_W:2 * ATTN_W].reshape(N_SAMPLE_B, DEC_SEQ, *heads)
    v_new = qkv_s[:, :, 2 * ATTN_W:].reshape(N_SAMPLE_B, DEC_SEQ, *heads)
    k_sm = jnp.concatenate([cache_k_win[0][:, DEC_SEQ:], k_new], axis=1)
    v_sm = jnp.concatenate([cache_v_win[0][:, DEC_SEQ:], v_new], axis=1)
    conv_s = dn_tok_s.reshape(N_SAMPLE_B, DEC_SEQ, CONV_CH)[:, DEC_SEQ - (CONV_W - 1):]
    return (y_prompt, y_sample, k_pr[None], v_pr[None], conv_p[None], _state_from_bd(s_p)[None],
            k_sm[None], v_sm[None], conv_s[None], _state_from_bd(s_s)[None])
```

```python
import functools

import jax
import jax.numpy as jnp
import numpy as np
from jax import lax
from jax.experimental import pallas as pl
from jax.experimental.pallas import tpu as pltpu

F32 = jnp.float32
BF16 = jnp.bfloat16
I32 = jnp.int32

D_MODEL = 1024
N_PROMPT_B, SEQ = 2, 8192
N_SAMPLE_B, DEC_SEQ = 32, 4
W_BUF = 2048
N_HEADS = 8
HEAD_DIM = 64
ATTN_W = 512
CONV_CH = 1536
CONV_W = 4
DN_W = 512
N_EXPERTS = 64
TOP_K = 8
EXPERT_DIM = 256
PLE_DIM = 256
BRANCHES = ((128, 1), (512, 4), (2048, 16))
N_BACK = 128
ROUTED_SCALE = 2.5
LN_EPS = 1e-5
NORM_EPS = 1e-6
ALPHA = 2.0 ** 0.25

LANES = 128
HALF = 64
VMEM_LIMIT = 56 * 1024 * 1024

N_P = N_PROMPT_B * SEQ
N_S = N_SAMPLE_B * DEC_SEQ
N_REAL = N_P + N_S
N_PAD = N_P + 512
TM = 256
TB = 512
N_TILES = N_PAD // TM
NT_P = N_P // TM
N_SP = N_PAD - N_P

SEG_ALIGN = 16
M_T = 3072
M_CHUNK = 256
M_MAIN = 2560
BM = 512
EXPERT_SLOTS = 4
R_MAX = N_PAD * TOP_K + N_TILES * N_EXPERTS * SEG_ALIGN + N_EXPERTS * (BM - 1)
NB_MAX = -(-R_MAX // BM)

DN_C = 64
SEQ_S = 64
S_TOK0 = 8


def _dot(a, b):
    return jnp.dot(a, b, preferred_element_type=F32)


def _dot_nt(a, b):
    return lax.dot_general(a, b, (((1,), (1,)), ((), ())), preferred_element_type=F32)


def _split2(x):
    hi = x.astype(BF16)
    lo = (x - hi.astype(F32)).astype(BF16)
    return hi, lo


def _dot_l01(m01, x):
    hi, lo = _split2(x)
    return _dot(m01, hi) + _dot(m01, lo)


def _dot_r01(x, m01):
    hi, lo = _split2(x)
    return _dot(hi, m01) + _dot(lo, m01)


def _sigmoid(x):
    return 1.0 / (1.0 + jnp.exp(-x))


def _layer_norm(x, g, b):
    mu = jnp.mean(x, axis=-1, keepdims=True)
    xc = x - mu
    var = jnp.mean(xc * xc, axis=-1, keepdims=True)
    return xc * lax.rsqrt(var + LN_EPS) * g + b


def _cparams(sem=None):
    return pltpu.CompilerParams(dimension_semantics=sem, vmem_limit_bytes=VMEM_LIMIT)


def _full(shape):
    return pl.BlockSpec(shape, lambda *_: (0,) * len(shape))


def _row2(w, tile=TM):
    ntp = N_P // tile
    return [pl.BlockSpec((tile, w), lambda i, *_: (jnp.minimum(i, ntp - 1), 0)),
            pl.BlockSpec((tile, w), lambda i, *_: (jnp.maximum(i - ntp, 0), 0))]


def _pick(p_ref, s_ref):
    return jnp.where(pl.program_id(0) < N_P // p_ref.shape[0], p_ref[...], s_ref[...])


def _proj_kernel(xp_ref, xs_ref, wqkv_ref, wdn_ref, wz_ref, wg_ref, qkv_ref, dn_ref, z_ref, b_ref, a_ref):
    x = _pick(xp_ref, xs_ref)
    xh, xl = _split2(x)
    qkv_ref[...] = _dot(xh, wqkv_ref[...])
    dn_ref[...] = _dot(xh, wdn_ref[...])
    z_ref[...] = _dot(xh, wz_ref[...])
    ng = 2 * N_HEADS
    o1 = _dot(xh, wg_ref[...])
    o2 = _dot(xl, wg_ref[...])
    gates = o1[:, :ng] + o1[:, ng:2 * ng] + o2[:, :ng]
    b_ref[...] = gates[:, :N_HEADS]
    a_ref[...] = gates[:, N_HEADS:]


def _project(x_p, x_s, w_in):
    wqkv = w_in[:, :3 * ATTN_W].astype(BF16)
    wdn = w_in[:, 3 * ATTN_W:3 * ATTN_W + CONV_CH].astype(BF16)
    c0 = 3 * ATTN_W + CONV_CH
    wz = w_in[:, c0:c0 + DN_W].astype(BF16)
    wgh, wgl = _split2(w_in[:, c0 + DN_W:])
    wg = jnp.concatenate([wgh, wgl, jnp.zeros((D_MODEL, LANES - 4 * N_HEADS), BF16)], axis=1)
    row = lambda w: pl.BlockSpec((TB, w), lambda i: (i, 0))
    outs = (3 * ATTN_W, CONV_CH, DN_W, N_HEADS, N_HEADS)
    return pl.pallas_call(
        _proj_kernel,
        name="proj",
        grid=(N_PAD // TB,),
        in_specs=_row2(D_MODEL, TB) + [_full(wqkv.shape), _full(wdn.shape), _full(wz.shape), _full(wg.shape)],
        out_specs=[row(w) for w in outs],
        out_shape=[jax.ShapeDtypeStruct((N_PAD, w), F32) for w in outs],
        compiler_params=_cparams(("arbitrary",)),
    )(x_p, x_s, wqkv, wdn, wz, wg)


QT = 2048
ATTN_UNROLL = 8


def _attn_bias():
    qi = np.arange(N_BACK)[:, None]
    ki = np.arange(2 * N_BACK)[None, :]
    out = np.zeros((2, len(BRANCHES), N_HEADS, N_BACK, 2 * N_BACK), np.float32)
    for var, off in enumerate((N_BACK, 0)):
        dist = qi - ki + off
        valid = (dist >= 0) & (dist <= N_BACK)
        for di, (_, dil) in enumerate(BRANCHES):
            for h in range(N_HEADS):
                slope = 2.0 ** (-8.0 * (h + 1) / N_HEADS) * dil
                out[var, di, h] = np.where(valid, -slope * dist, -1e30)
    return out


def _attn_prompt_kernel(q_ref, k_ref, v_ref, bias_ref, o_ref, acc_ref, m_ref, l_ref):
    qt = pl.program_id(2)
    lane = lax.broadcasted_iota(I32, (1, LANES), 1)
    h0 = lane < HALF
    for di, (_, dil) in enumerate(BRANCHES):
        per_r = QT // (N_BACK * dil)

        def body(i, carry, di=di, dil=dil, per_r=per_r):
            def ds(start, n):
                return pl.ds(start, n) if dil == 1 else pl.ds(start, n, stride=dil)

            blocks = []
            for u in range(ATTN_UNROLL):
                blk = i * ATTN_UNROLL + u
                r = blk // per_r
                jb = blk % per_r
                qloc = r + dil * N_BACK * jb
                first = jnp.logical_and(qt == 0, jb == 0)
                kstart = jnp.where(first, r, qt * QT + qloc - N_BACK * dil)
                blocks.append((qloc, kstart, jnp.where(first, 1, 0)))
            q8 = [q_ref[ds(qloc, N_BACK), :] * (HEAD_DIM ** -0.5) for qloc, _, _ in blocks]
            kb = [k_ref[ds(kstart, 2 * N_BACK), :].astype(BF16) for _, kstart, _ in blocks]
            vv = [v_ref[ds(kstart, 2 * N_BACK), :] for _, kstart, _ in blocks]
            heads = [(u, hh) for u in range(ATTN_UNROLL) for hh in range(2)]
            keep = lambda hh: h0 if hh == 0 else jnp.logical_not(h0)
            s = [_dot_nt(jnp.where(keep(hh), q8[u], 0.0).astype(BF16), kb[u]) + bias_ref[blocks[u][2], di, hh]
                 for u, hh in heads]
            m = [jnp.max(x, axis=-1, keepdims=True) for x in s]
            p = [jnp.exp(x - mm).astype(BF16) for x, mm in zip(s, m)]
            res = [_dot(pp, jnp.where(keep(hh), vv[u], 1.0).astype(BF16)) for pp, (u, hh) in zip(p, heads)]
            for u, (qloc, _, _) in enumerate(blocks):
                rows = ds(qloc, N_BACK)
                acc_ref[di, rows, :] = jnp.where(h0, res[2 * u], res[2 * u + 1])
                l_ref[di, rows, :] = jnp.where(h0, res[2 * u + 1], res[2 * u])
                m_ref[di, rows, :] = jnp.where(h0, m[2 * u], m[2 * u + 1])
            return carry

        lax.fori_loop(0, QT // (N_BACK * ATTN_UNROLL), body, 0)

    mall = m_ref[...]
    mtot = jnp.max(mall, axis=0)
    num = jnp.zeros((QT, LANES), F32)
    den = jnp.zeros((QT, LANES), F32)
    for di in range(len(BRANCHES)):
        w = jnp.exp(mall[di] - mtot)
        num = num + w * acc_ref[di]
        den = den + w * pltpu.roll(l_ref[di], HALF, axis=1)
    o_ref[...] = num / den


def _attn_prompt(qkv):
    bias = jnp.asarray(_attn_bias())
    npair = N_HEADS // 2
    nqt = SEQ // QT
    kv_rows = SEQ
    q_spec = pl.BlockSpec((QT, LANES), lambda b, hp, t: (b * nqt + t, hp))
    k_spec = pl.BlockSpec((kv_rows, LANES), lambda b, hp, t: (b, npair + hp))
    v_spec = pl.BlockSpec((kv_rows, LANES), lambda b, hp, t: (b, 2 * npair + hp))
    bias_spec = pl.BlockSpec((2, len(BRANCHES), 2, N_BACK, 2 * N_BACK), lambda b, hp, t: (0, 0, hp, 0, 0))
    return pl.pallas_call(
        _attn_prompt_kernel,
        name="attn_prompt",
        grid=(N_PROMPT_B, npair, nqt),
        in_specs=[q_spec, k_spec, v_spec, bias_spec],
        out_specs=pl.BlockSpec((QT, LANES), lambda b, hp, t: (b * nqt + t, hp)),
        out_shape=jax.ShapeDtypeStruct((N_P, ATTN_W), F32),
        scratch_shapes=[pltpu.VMEM((len(BRANCHES), QT, LANES), F32)] * 3,
        compiler_params=_cparams(("arbitrary", "arbitrary", "arbitrary")),
    )(qkv, qkv, qkv, bias)


ROWS_S = 8


def _attn_sample_kernel(q_ref, kn_ref, vn_ref, ck_ref, cv_ref, e_ref, et_ref, slope_ref, o_ref, kall_ref, vall_ref):
    npair = N_HEADS // 2
    for hp in range(npair):
        sl = slice(hp * LANES, (hp + 1) * LANES)
        kall_ref[hp, 0:W_BUF, :] = ck_ref[0, :, sl]
        vall_ref[hp, 0:W_BUF, :] = cv_ref[0, :, sl]
        kall_ref[hp, W_BUF:W_BUF + ROWS_S, :] = kn_ref[0, :, sl]
        vall_ref[hp, W_BUF:W_BUF + ROWS_S, :] = vn_ref[0, :, sl]

    def rows(ref, start, n, dil):
        idx = pl.ds(start, n) if dil == 1 else pl.ds(start, n, stride=dil)
        return jnp.concatenate([ref[hp, idx, :] for hp in range(npair)], axis=1)

    e01 = e_ref[...]
    et01 = et_ref[...]
    slope = slope_ref[...]
    jrev = (N_BACK - lax.broadcasted_iota(I32, (N_BACK, 1), 0)).astype(F32)
    inew = lax.broadcasted_iota(I32, (ROWS_S, 1), 0)
    scale = HEAD_DIM ** -0.5
    knew = rows(kall_ref, W_BUF, ROWS_S, 1)
    vnew = rows(vall_ref, W_BUF, ROWS_S, 1).astype(BF16).astype(F32)

    def scores(kr, qt):
        ph, plo = _split2(kr * qt)
        return (_dot(ph, e01) + _dot(plo, e01)) * scale

    outs = []
    for t in range(DEC_SEQ):
        qt = q_ref[0, t:t + 1, :]
        s_self = jnp.where(inew == t, scores(knew, qt), -1e30)
        s_br = []
        v_br = []
        for _, dil in BRANCHES:
            start = W_BUF + t - N_BACK * dil
            s_br.append(scores(rows(kall_ref, start, N_BACK, dil), qt) - (slope * dil) * jrev)
            v_br.append(rows(vall_ref, start, N_BACK, dil))
        m = jnp.max(s_self, axis=0, keepdims=True)
        for s in s_br:
            m = jnp.maximum(m, jnp.max(s, axis=0, keepdims=True))
        pb = (jnp.exp(s_self - m) * float(len(BRANCHES))).astype(BF16)
        den = jnp.sum(pb.astype(F32), axis=0, keepdims=True)
        num = jnp.sum(_dot(pb, et01) * vnew, axis=0, keepdims=True)
        for s, vr in zip(s_br, v_br):
            pb = jnp.exp(s - m).astype(BF16)
            den = den + jnp.sum(pb.astype(F32), axis=0, keepdims=True)
            num = num + jnp.sum(_dot(pb, et01) * vr.astype(BF16).astype(F32), axis=0, keepdims=True)
        denx = _dot_r01(jnp.broadcast_to(den, (ROWS_S, N_HEADS)), et01)[0:1]
        outs.append(num / denx)
    pad = jnp.zeros((ROWS_S - DEC_SEQ, ATTN_W), F32)
    o_ref[0] = jnp.concatenate(outs + [pad], axis=0)


def _head_expand():
    e = np.zeros((ATTN_W, N_HEADS), np.float32)
    for h in range(N_HEADS):
        e[h * HEAD_DIM:(h + 1) * HEAD_DIM, h] = 1.0
    return e


def _attn_sample(q_s, k_s, v_s, cache_k, cache_v):
    e = _head_expand()
    e01 = jnp.asarray(e, BF16)
    et01 = jnp.asarray(e.T, BF16)
    slopes = jnp.asarray([[2.0 ** (-8.0 * (h + 1) / N_HEADS) for h in range(N_HEADS)]], F32)
    row = pl.BlockSpec((1, ROWS_S, ATTN_W), lambda b: (b, 0, 0))
    cache = pl.BlockSpec((1, W_BUF, ATTN_W), lambda b: (b, 0, 0))
    return pl.pallas_call(
        _attn_sample_kernel,
        name="attn_sample",
        grid=(N_SAMPLE_B,),
        in_specs=[row, row, row, cache, cache, _full(e01.shape), _full(et01.shape), _full(slopes.shape)],
        out_specs=row,
        out_shape=jax.ShapeDtypeStruct((N_SAMPLE_B, ROWS_S, ATTN_W), F32),
        scratch_shapes=[pltpu.VMEM((N_HEADS // 2, W_BUF + ROWS_S, LANES), F32)] * 2,
        compiler_params=_cparams(("arbitrary",)),
    )(q_s, k_s, v_s, cache_k, cache_v, e01, et01, slopes)


HALO = 8


def _dn_pre_kernel(dn_ref, halo_ref, b_ref, a_ref, valid_ref, cw_ref, alog_ref, dtb_ref, e_ref, et_ref, ltri_ref,
                   q_ref, k_ref, v_ref, bx_ref, gx_ref, buf_ref):
    ts = dn_ref.shape[0]
    buf_ref[0:HALO, :] = halo_ref[0]
    buf_ref[HALO:HALO + ts, :] = dn_ref[...]
    y = jnp.zeros((ts, CONV_CH), F32)
    for i in range(CONV_W):
        y = y + buf_ref[pl.ds(HALO - (CONV_W - 1) + i, ts), :] * cw_ref[i:i + 1, :]
    y = y * _sigmoid(y)
    valid = valid_ref[...]
    e01 = e_ref[...]
    et01 = et_ref[...]

    def l2n(t):
        ss = _dot_r01(t * t, e01)
        inv = lax.rsqrt(ss + NORM_EPS)
        return t * _dot_r01(inv, et01)

    q_ref[...] = l2n(y[:, :DN_W]) * (HEAD_DIM ** -0.5)
    k_ref[...] = l2n(y[:, DN_W:2 * DN_W]) * valid
    v_ref[...] = y[:, 2 * DN_W:] * valid
    beta = _sigmoid(b_ref[...]) * valid
    sp_in = a_ref[...] + dtb_ref[...]
    softplus = jnp.maximum(sp_in, 0.0) + jnp.log1p(jnp.exp(-jnp.abs(sp_in)))
    g = -jnp.exp(alog_ref[...]) * softplus * valid
    bx_ref[...] = _dot_r01(beta, et01)
    gx_ref[...] = _dot_l01(ltri_ref[...], _dot_r01(g, et01))


def _dn_pre(dn, halo, b_logit, a_logit, valid, conv_w, a_log, dt_bias):
    nt = halo.shape[0]
    rows = nt * TM
    e = _head_expand()
    e01 = jnp.asarray(e, BF16)
    et01 = jnp.asarray(e.T, BF16)
    ii = np.arange(TM)
    ltri = jnp.asarray(((ii[:, None] >= ii[None, :]) & (ii[:, None] // DN_C == ii[None, :] // DN_C)), BF16)
    row = lambda w: pl.BlockSpec((TM, w), lambda i: (i, 0))
    return pl.pallas_call(
        _dn_pre_kernel,
        name="dn_pre",
        grid=(nt,),
        in_specs=[row(CONV_CH), pl.BlockSpec((1, HALO, CONV_CH), lambda i: (i, 0, 0)), row(N_HEADS), row(N_HEADS),
                  row(1), _full(conv_w.shape), _full((1, N_HEADS)), _full((1, N_HEADS)),
                  _full(e01.shape), _full(et01.shape), _full(ltri.shape)],
        out_specs=[row(DN_W)] * 5,
        out_shape=[jax.ShapeDtypeStruct((rows, DN_W), F32)] * 5,
        scratch_shapes=[pltpu.VMEM((HALO + TM, CONV_CH), F32)],
        compiler_params=_cparams(("arbitrary",)),
    )(dn, halo, b_logit, a_logit, valid, conv_w, a_log.reshape(1, N_HEADS), dt_bias.reshape(1, N_HEADS),
      e01, et01, ltri)


def _delta_streams(q, k, v, bx, gcx, s_bd):
    c = DN_C
    n = len(q)
    each = range(n)
    lane = lax.broadcasted_iota(I32, (1, LANES), 1)
    h0 = lane < HALF
    m0 = h0.astype(F32)
    m1 = 1.0 - m0
    ii = lax.broadcasted_iota(I32, (2 * c, 2 * c), 0)
    jj = lax.broadcasted_iota(I32, (2 * c, 2 * c), 1)
    same = jnp.where(ii < c, 0, 1) == jnp.where(jj < c, 0, 1)
    order = jnp.where(same, ii - jj, -1)
    tril = order >= 0
    strict = order > 0
    eye = jnp.where(ii == jj, 1.0, 0.0)

    def stack(x):
        return jnp.concatenate([x * m0, x * m1], axis=0)

    def fold(x2):
        return x2[:c] + x2[c:]

    def decay_of(g):
        gsw = pltpu.roll(g, HALF, axis=1)
        row_b = jnp.concatenate([jnp.where(h0, g, gsw), jnp.where(h0, gsw, g)], axis=0)
        return jnp.where(tril, jnp.exp(jnp.where(tril, row_b - row_b.T, 0.0)), 0.0)

    eg = [jnp.exp(gcx[i]) for i in each]
    kb = [k[i] * bx[i] for i in each]
    glast = [gcx[i][c - 1:c, :] for i in each]
    decay = [decay_of(gcx[i]) for i in each]
    k2 = [stack(k[i]).astype(BF16) for i in each]
    lhs = [jnp.concatenate([stack(kb[i]), stack(q[i])], axis=0).astype(BF16) for i in each]
    aq = [_dot_nt(lhs[i], k2[i]) for i in each]
    a_low = [jnp.where(strict, aq[i][:2 * c] * decay[i], 0.0) for i in each]
    aqk = [jnp.where(tril, aq[i][2 * c:] * decay[i], 0.0).astype(BF16) for i in each]
    tinv = [eye - a_low[i] for i in each]
    apow = [a_low[i].astype(BF16) for i in each]
    for _ in range(5):
        apow = [_dot(apow[i], apow[i]).astype(BF16) for i in each]
        tinv = [tinv[i] + _dot(tinv[i].astype(BF16), apow[i]) for i in each]
    rhs = [jnp.concatenate([stack(v[i] * bx[i]), stack(kb[i] * eg[i])], axis=1).astype(BF16) for i in each]
    uw = [_dot(tinv[i].astype(BF16), rhs[i]) for i in each]
    u = [fold(uw[i][:, :LANES]) for i in each]
    w = [fold(uw[i][:, LANES:]) for i in each]
    wq = [jnp.concatenate([w[i], q[i] * eg[i]], axis=0).astype(BF16) for i in each]
    kd = [(k[i] * jnp.exp(glast[i] - gcx[i])).T.astype(BF16) for i in each]
    gt = [jnp.exp(glast[i]) for i in each]
    ns = len(s_bd)
    state = list(s_bd)
    o = [None] * n
    for sub in range(n // ns):
        js = [sub * ns + i for i in range(ns)]
        ws = [_dot(wq[j], state[i].astype(BF16)) for i, j in enumerate(js)]
        v_new = [u[j] - ws[i][:c] for i, j in enumerate(js)]
        intra = [_dot(aqk[j], stack(v_new[i]).astype(BF16)) for i, j in enumerate(js)]
        upd = [_dot(kd[j], v_new[i].astype(BF16)) for i, j in enumerate(js)]
        for i, j in enumerate(js):
            o[j] = ws[i][c:] + fold(intra[i])
            state[i] = state[i] * gt[j] + jnp.where(same, upd[i], 0.0)
    return o, state


def _delta_kernel(q_ref, k_ref, v_ref, bx_ref, gx_ref, s0_ref, o_ref, sout_ref, s_ref):
    ci = pl.program_id(1)
    nb = q_ref.shape[0]
    npair = N_HEADS // 2

    @pl.when(ci == 0)
    def _():
        s_ref[...] = s0_ref[...]

    streams = [(b, hp) for b in range(nb) for hp in range(npair)]
    chunks = [(slice(sub * DN_C, (sub + 1) * DN_C), b, slice(hp * LANES, (hp + 1) * LANES))
              for sub in range(q_ref.shape[1] // DN_C) for b, hp in streams]
    take = lambda ref: [ref[b, rows, lanes] for rows, b, lanes in chunks]
    o, s_new = _delta_streams(take(q_ref), take(k_ref), take(v_ref), take(bx_ref), take(gx_ref),
                              [s_ref[b, hp] for b, hp in streams])
    for (rows, b, lanes), oc in zip(chunks, o):
        o_ref[b, rows, lanes] = oc
    for (b, hp), sn in zip(streams, s_new):
        s_ref[b, hp] = sn

    @pl.when(ci == pl.num_programs(1) - 1)
    def _():
        sout_ref[...] = s_ref[...]


def _delta(q, k, v, bx, gx, s0_bd, bblk, nsub):
    nbatch, length, _ = q.shape
    npair = N_HEADS // 2
    seq = pl.BlockSpec((bblk, nsub * DN_C, DN_W), lambda g, c: (g, c, 0))
    st = pl.BlockSpec((bblk, npair, LANES, LANES), lambda g, c: (g, 0, 0, 0))
    return pl.pallas_call(
        _delta_kernel,
        name="delta",
        grid=(nbatch // bblk, length // (nsub * DN_C)),
        in_specs=[seq] * 5 + [st],
        out_specs=[seq, st],
        out_shape=[jax.ShapeDtypeStruct((nbatch, length, DN_W), F32),
                   jax.ShapeDtypeStruct((nbatch, npair, LANES, LANES), F32)],
        scratch_shapes=[pltpu.VMEM((bblk, npair, LANES, LANES), F32)],
        compiler_params=_cparams(("arbitrary", "arbitrary")),
    )(q, k, v, bx, gx, s0_bd)


def _state_to_bd(s):
    b = s.shape[0]
    s = s.reshape(b, N_HEADS // 2, 2, HALF, HALF)
    zero = jnp.zeros_like(s[:, :, 0])
    top = jnp.concatenate([s[:, :, 0], zero], axis=-1)
    bot = jnp.concatenate([zero, s[:, :, 1]], axis=-1)
    return jnp.concatenate([top, bot], axis=-2)


def _state_from_bd(s):
    b = s.shape[0]
    return jnp.stack([s[:, :, :HALF, :HALF], s[:, :, HALF:, HALF:]], axis=2).reshape(b, N_HEADS, HALF, HALF)


def _out_kernel(xp_ref, xs_ref, ap_ref, as_ref, dp_ref, ds_ref, z_ref, ng_ref, bavg_ref, wa_ref, wd_ref, g_ref, b_ref,
                o_ref):
    o = _pick(dp_ref, ds_ref)
    oh, ol = _split2(o * o)
    ms = _dot(oh, bavg_ref[...]) + _dot(ol, bavg_ref[...])
    z = z_ref[...]
    dn = o * lax.rsqrt(ms + NORM_EPS) * ng_ref[...] * (z * _sigmoid(z))
    mix = _dot(_pick(ap_ref, as_ref).astype(BF16), wa_ref[...]) + _dot(dn.astype(BF16), wd_ref[...])
    o_ref[...] = _layer_norm(ALPHA * _pick(xp_ref, xs_ref) + mix, g_ref[...], b_ref[...])


def _out_proj(x_p, x_s, att_p, att_s, dn_p, dn_s, z, norm_g, w_out, g, b):
    wa = w_out[:ATTN_W].astype(BF16)
    wd = w_out[ATTN_W:].astype(BF16)
    normg = jnp.tile(norm_g.reshape(1, HEAD_DIM), (1, N_HEADS))
    bavg = jnp.asarray(np.kron(np.eye(N_HEADS), np.ones((HALF, HALF))) / HALF, BF16)
    row = lambda w: pl.BlockSpec((TB, w), lambda i: (i, 0))
    return pl.pallas_call(
        _out_kernel,
        name="out_proj",
        grid=(N_PAD // TB,),
        in_specs=_row2(D_MODEL, TB) + _row2(ATTN_W, TB) + _row2(DN_W, TB) + [
            row(DN_W), _full(normg.shape), _full(bavg.shape),
            _full(wa.shape), _full(wd.shape), _full((1, D_MODEL)), _full((1, D_MODEL))],
        out_specs=row(D_MODEL),
        out_shape=jax.ShapeDtypeStruct((N_PAD, D_MODEL), F32),
        compiler_params=_cparams(("arbitrary",)),
    )(x_p, x_s, att_p, att_s, dn_p, dn_s, z, normg, bavg, wa, wd, g.reshape(1, -1), b.reshape(1, -1))


N_GROUPS = 8
GROUP = N_EXPERTS // N_GROUPS
SLOT_RADIX = 64
_ET_SPEC = pl.BlockSpec((1, 2 * N_EXPERTS, TM), lambda i, *_: (i, 0, 0))
_SEG_SPEC = pl.BlockSpec((1, 1, 2 * N_EXPERTS), lambda i, *_: (i, 0, 0))
TOPK_GROUPS = 4
NEG = -jnp.inf


def _route_kernel(x_ref, wh_ref, wl_ref, bias_ref, upper_ref, lows_ref, st_ref, ge_ref, cnt_ref):
    t = pl.program_id(0)
    xh, xl = _split2(x_ref[...])
    wh = wh_ref[...]
    logits = _dot_nt(wh, xh) + _dot_nt(wh, xl) + _dot_nt(wl_ref[...], xh)
    scores = _sigmoid(logits)
    choice = scores + bias_ref[...]
    i8 = lax.broadcasted_iota(I32, (GROUP, TM), 0)

    def first_max(vals):
        m = jnp.max(vals, axis=0, keepdims=True)
        idx = jnp.min(jnp.where(vals == m, i8, GROUP), axis=0, keepdims=True)
        return m, idx

    pieces = [choice[g * GROUP:(g + 1) * GROUP, :] for g in range(N_GROUPS)]
    gsc = jnp.zeros((N_GROUPS, TM), F32)
    for g in range(N_GROUPS):
        m1, idx1 = first_max(pieces[g])
        m2 = jnp.max(jnp.where(i8 == idx1, NEG, pieces[g]), axis=0, keepdims=True)
        gsc = jnp.where(i8 == g, m1 + m2, gsc)
    gsel = jnp.zeros((N_GROUPS, TM), F32)
    for _ in range(TOPK_GROUPS):
        _, idx = first_max(gsc)
        hit = i8 == idx
        gsel = jnp.where(hit, 1.0, gsel)
        gsc = jnp.where(hit, NEG, gsc)
    masked = [jnp.where(gsel[g:g + 1, :] > 0.5, pieces[g], NEG) for g in range(N_GROUPS)]
    eidx = [i8 + g * GROUP for g in range(N_GROUPS)]
    member = [jnp.zeros((GROUP, TM), F32) for _ in range(N_GROUPS)]
    for _ in range(TOP_K):
        mm = masked[0]
        for g in range(1, N_GROUPS):
            mm = jnp.maximum(mm, masked[g])
        m = jnp.max(mm, axis=0, keepdims=True)
        cand = jnp.where(masked[0] == m, eidx[0], N_EXPERTS)
        for g in range(1, N_GROUPS):
            cand = jnp.minimum(cand, jnp.where(masked[g] == m, eidx[g], N_EXPERTS))
        idx = jnp.min(cand, axis=0, keepdims=True)
        for g in range(N_GROUPS):
            hit = eidx[g] == idx
            member[g] = jnp.where(hit, 1.0, member[g])
            masked[g] = jnp.where(hit, NEG, masked[g])

    tok = t * TM + lax.broadcasted_iota(I32, (1, TM), 1)
    real = jnp.where(tok < N_REAL, 1.0, 0.0)
    mem = jnp.concatenate(member, axis=0) * real
    sel = mem * scores
    gate = (sel / jnp.sum(sel + (1.0 - real), axis=0, keepdims=True) * ROUTED_SCALE).astype(BF16)
    ge_ref[0] = jnp.concatenate([gate, jnp.zeros_like(gate)], axis=0)
    memb = mem.astype(BF16)
    rank = _dot(memb, upper_ref[...])
    cnt = jnp.sum(mem, axis=1, keepdims=True)
    cpad = jnp.maximum(jnp.floor((cnt + (SEG_ALIGN - 1)) * (1.0 / SEG_ALIGN)), 1.0)
    loc = _dot(lows_ref[...], jnp.broadcast_to(cpad, (N_EXPERTS, TM)).astype(BF16)) * SEG_ALIGN
    slot_e = jnp.where(mem > 0.5, loc + rank, float(M_T))
    slot_hi = jnp.floor(slot_e * (1.0 / SLOT_RADIX)) * SLOT_RADIX
    st_ref[0] = jnp.concatenate([slot_hi, slot_e - slot_hi + 1.0], axis=0).astype(BF16)
    ones = jnp.ones((GROUP, TM), BF16)
    cnt_ref[0] = _dot_nt(ones, memb).astype(I32)


def _route(x1, w_router, router_bias):
    wt = w_router.T
    wh, wl = _split2(wt)
    ii = np.arange(TM)
    upper = jnp.asarray(ii[:, None] < ii[None, :], BF16)
    ee = np.arange(N_EXPERTS)
    lows = jnp.asarray(ee[:, None] > ee[None, :], BF16)
    return pl.pallas_call(
        _route_kernel,
        name="route",
        grid=(N_TILES,),
        in_specs=[pl.BlockSpec((TM, D_MODEL), lambda i: (i, 0)), _full(wh.shape), _full(wl.shape),
                  _full((N_EXPERTS, 1)), _full(upper.shape), _full(lows.shape)],
        out_specs=[_ET_SPEC, _ET_SPEC, pl.BlockSpec((1, GROUP, N_EXPERTS), lambda i: (i, 0, 0))],
        out_shape=[jax.ShapeDtypeStruct((N_TILES, 2 * N_EXPERTS, TM), BF16)] * 2 + [
            jax.ShapeDtypeStruct((N_TILES, GROUP, N_EXPERTS), I32)],
        compiler_params=_cparams(("arbitrary",)),
    )(x1, wh, wl, router_bias.reshape(N_EXPERTS, 1), upper, lows)


def _segment_tables(cnt):
    seg = jnp.maximum((cnt + SEG_ALIGN - 1) // SEG_ALIGN, 1)
    cpad = seg * SEG_ALIGN
    loc_off = jnp.cumsum(cpad, axis=1) - cpad
    rows_e = jnp.sum(cpad, axis=0)
    blocks_e = (rows_e + BM - 1) // BM
    bend = jnp.cumsum(blocks_e)
    bstart = bend - blocks_e
    glob_off = (bstart * BM)[None, :] + jnp.cumsum(cpad, axis=0) - cpad
    nb = bend[-1]
    pad_off = bstart * BM + rows_e
    pad_seg = (blocks_e * BM - rows_e) // SEG_ALIGN
    i32 = lambda a: a.reshape(-1).astype(I32)
    bounds = lambda a: jnp.tile(a.astype(F32), (1, 2)).reshape(N_TILES, 1, 2 * N_EXPERTS)
    return (i32(seg), i32(loc_off), i32(glob_off), i32(pad_seg), i32(pad_off), i32(bstart), i32(blocks_e), i32(nb),
            bounds(loc_off), bounds(loc_off + cpad))


def _rows_copy(src_of, dst_of, lo, go, rows, sem):
    lo, go, rows = (v if isinstance(v, int) else pl.multiple_of(v, SEG_ALIGN) for v in (lo, go, rows))
    return pltpu.make_async_copy(src_of(lo, go, rows), dst_of(lo, go, rows), sem)


def _segment_copies(seg_ref, loc_ref, glob_ref, t, src_of, dst_of, sem):
    for e in range(N_EXPERTS):
        i = t * N_EXPERTS + e
        _rows_copy(src_of, dst_of, loc_ref[i], glob_ref[i], seg_ref[i] * SEG_ALIGN, sem).start()


def _wait_rows(src_of, dst_of, rows, sem):
    _rows_copy(src_of, dst_of, 0, 0, rows, sem).wait()


def _local_rows(m0, rows, lo, hi, slot_tab):
    mcol = (lax.broadcasted_iota(I32, (rows, 2 * N_EXPERTS), 0) + m0).astype(F32)
    e01 = jnp.where(mcol >= lo, jnp.where(mcol < hi, 1.0, 0.0), 0.0).astype(BF16)
    mrow = (lax.broadcasted_iota(I32, (rows, TM), 0) + (m0 + 1)).astype(F32)
    return _dot(e01, slot_tab) == mrow, e01


def _row_spans(used):
    yield None, 0, M_MAIN
    for m0 in range(M_MAIN, M_T, M_CHUNK):
        yield m0 < used, m0, M_CHUNK


def _tile_rows(seg_ref, loc_ref, t):
    last = t * N_EXPERTS + N_EXPERTS - 1
    return loc_ref[last] + seg_ref[last] * SEG_ALIGN


def _dispatch_kernel(seg_ref, loc_ref, glob_ref, pseg_ref, poff_ref, nb_ref, x_ref, st_ref, lo_ref, hi_ref,
                     xs_ref, buf_ref, sem):
    t = pl.program_id(0)
    s = t % 2
    used = _tile_rows(seg_ref, loc_ref, t)
    for live, m0, rows in _row_spans(used):
        def sort_rows(m0=m0, rows=rows):
            hit, _ = _local_rows(m0, rows, lo_ref[0], hi_ref[0], st_ref[0])
            onehot = jnp.where(hit, 1.0, 0.0).astype(BF16)
            buf_ref[s, m0:m0 + rows, :] = _dot(onehot, x_ref[...].astype(BF16)).astype(BF16)

        if live is None:
            sort_rows()
        else:
            pl.when(live)(sort_rows)

    src = lambda lo, go, rows: buf_ref.at[s, pl.ds(lo, rows), :]
    dst = lambda lo, go, rows: xs_ref.at[pl.ds(go, rows), :]
    _segment_copies(seg_ref, loc_ref, glob_ref, t, src, dst, sem.at[s])

    @pl.when(t > 0)
    def _():
        prev = lambda lo, go, rows: buf_ref.at[1 - s, pl.ds(lo, rows), :]
        _wait_rows(prev, dst, _tile_rows(seg_ref, loc_ref, t - 1), sem.at[1 - s])

    @pl.when(t == N_TILES - 1)
    def _():
        _wait_rows(src, dst, used, sem.at[s])
        buf_ref[s, 0:BM, :] = jnp.zeros((BM, D_MODEL), BF16)
        zeros = lambda lo, go, rows: buf_ref.at[s, pl.ds(0, rows), :]

        def fill(wait):
            def pad(e, carry):
                k = pseg_ref[e]

                @pl.when(k > 0)
                def _():
                    cp = _rows_copy(zeros, dst, 0, poff_ref[e], k * SEG_ALIGN, sem.at[s])
                    cp.wait() if wait else cp.start()

                return carry

            lax.fori_loop(0, N_EXPERTS, pad, 0)

            def tail(b, carry):
                cp = _rows_copy(zeros, dst, 0, b * BM, BM, sem.at[s])
                cp.wait() if wait else cp.start()
                return carry

            lax.fori_loop(nb_ref[0], NB_MAX, tail, 0)

        fill(False)
        fill(True)


def _dispatch(x1, slot_tab, seg_lo, seg_hi, tables):
    grid_spec = pltpu.PrefetchScalarGridSpec(
        num_scalar_prefetch=len(tables),
        grid=(N_TILES,),
        in_specs=[pl.BlockSpec((TM, D_MODEL), lambda i, *_: (i, 0)), _ET_SPEC, _SEG_SPEC, _SEG_SPEC],
        out_specs=pl.BlockSpec(memory_space=pl.ANY),
        scratch_shapes=[pltpu.VMEM((2, M_T, D_MODEL), BF16), pltpu.SemaphoreType.DMA((2,))],
    )
    return pl.pallas_call(
        _dispatch_kernel,
        name="dispatch",
        grid_spec=grid_spec,
        out_shape=jax.ShapeDtypeStruct((NB_MAX * BM, D_MODEL), BF16),
        compiler_params=_cparams(("arbitrary",)),
    )(*tables, x1, slot_tab, seg_lo, seg_hi)


def _expert_kernel(bstart_ref, bcount_ref, nb_ref, xs_ref, wg_ref, wu_ref, wd_ref, ys_ref,
                   xbuf, ybuf, wgu_s, wd_s, sem_in, sem_out):
    e = pl.program_id(0)
    nb = nb_ref[0]
    b0 = bstart_ref[e]
    n = bcount_ref[e]

    def rows_of(ref, g):
        return ref.at[pl.ds(pl.multiple_of(g * BM, BM), BM), :]

    def x_copy(g):
        return pltpu.make_async_copy(rows_of(xs_ref, g), xbuf.at[g % EXPERT_SLOTS], sem_in.at[g % EXPERT_SLOTS])

    def y_copy(g):
        return pltpu.make_async_copy(ybuf.at[g % EXPERT_SLOTS], rows_of(ys_ref, g), sem_out.at[g % EXPERT_SLOTS])

    @pl.when(e == 0)
    def _():
        for g in range(EXPERT_SLOTS - 1):
            @pl.when(g < nb)
            def _(g=g):
                x_copy(g).start()

    wgu_s[:, :EXPERT_DIM] = wg_ref[0].astype(BF16)
    wgu_s[:, EXPERT_DIM:] = wu_ref[0].astype(BF16)
    wd_s[...] = wd_ref[0].astype(BF16)

    def block(i, carry):
        g = b0 + i
        slot = g % EXPERT_SLOTS

        @pl.when(g + EXPERT_SLOTS - 1 < nb)
        def _():
            x_copy(g + EXPERT_SLOTS - 1).start()

        x_copy(g).wait()

        @pl.when(g >= EXPERT_SLOTS)
        def _():
            y_copy(g - EXPERT_SLOTS).wait()

        h = _dot(xbuf[slot], wgu_s[...])
        gate = h[:, :EXPERT_DIM]
        a = (gate * _sigmoid(gate) * h[:, EXPERT_DIM:]).astype(BF16)
        ybuf[slot] = _dot(a, wd_s[...]).astype(BF16)
        y_copy(g).start()
        return carry

    lax.fori_loop(0, n, block, 0)

    @pl.when(e == N_EXPERTS - 1)
    def _():
        for k in range(EXPERT_SLOTS, 0, -1):
            @pl.when(nb >= k)
            def _(k=k):
                y_copy(nb - k).wait()

        ybuf[0] = jnp.zeros((BM, D_MODEL), BF16)

        def tail(wait):
            def body(g, carry):
                cp = pltpu.make_async_copy(ybuf.at[0], rows_of(ys_ref, g), sem_out.at[0])
                cp.wait() if wait else cp.start()
                return carry

            lax.fori_loop(nb, NB_MAX, body, 0)

        tail(False)
        tail(True)


def _experts(xs, we_gate, we_up, we_down, bstart, bcount, nb):
    weight = lambda shape: pl.BlockSpec((1,) + shape, lambda e, *_: (e, 0, 0))
    grid_spec = pltpu.PrefetchScalarGridSpec(
        num_scalar_prefetch=3,
        grid=(N_EXPERTS,),
        in_specs=[pl.BlockSpec(memory_space=pl.ANY), weight((D_MODEL, EXPERT_DIM)), weight((D_MODEL, EXPERT_DIM)),
                  weight((EXPERT_DIM, D_MODEL))],
        out_specs=pl.BlockSpec(memory_space=pl.ANY),
        scratch_shapes=[pltpu.VMEM((EXPERT_SLOTS, BM, D_MODEL), BF16), pltpu.VMEM((EXPERT_SLOTS, BM, D_MODEL), BF16),
                        pltpu.VMEM((D_MODEL, 2 * EXPERT_DIM), BF16), pltpu.VMEM((EXPERT_DIM, D_MODEL), BF16),
                        pltpu.SemaphoreType.DMA((EXPERT_SLOTS,)), pltpu.SemaphoreType.DMA((EXPERT_SLOTS,))],
    )
    return pl.pallas_call(
        _expert_kernel,
        name="expert",
        grid_spec=grid_spec,
        out_shape=jax.ShapeDtypeStruct((NB_MAX * BM, D_MODEL), BF16),
        compiler_params=_cparams(("arbitrary",)),
    )(bstart, bcount, nb, xs, we_gate, we_up, we_down)


def _combine_kernel(seg_ref, loc_ref, glob_ref, x_ref, st_ref, ge_ref, lo_ref, hi_ref, pp_ref, ps_ref, ys_ref,
                    wsgu_ref, wsd_ref, g2_ref, b2_ref, wpg_ref, wpp_ref, g3_ref, b3_ref, op_ref, os_ref,
                    buf_ref, acc_ref, sem):
    t = pl.program_id(0)
    s = t % 2
    used = _tile_rows(seg_ref, loc_ref, t)
    src = lambda lo, go, rows: ys_ref.at[pl.ds(go, rows), :]
    into = lambda slot: (lambda lo, go, rows: buf_ref.at[slot, pl.ds(lo, rows), :])

    @pl.when(t == 0)
    def _():
        buf_ref[...] = jnp.zeros((2, M_T, D_MODEL), BF16)
        _segment_copies(seg_ref, loc_ref, glob_ref, t, src, into(s), sem.at[s])

    @pl.when(t + 1 < N_TILES)
    def _():
        _segment_copies(seg_ref, loc_ref, glob_ref, t + 1, src, into(1 - s), sem.at[1 - s])

    _wait_rows(src, into(s), used, sem.at[s])

    for live, m0, rows in _row_spans(used):
        def gather_rows(m0=m0, rows=rows):
            hit, e01 = _local_rows(m0, rows, lo_ref[0], hi_ref[0], st_ref[0])
            pg = jnp.where(hit, _dot(e01, ge_ref[0]), 0.0).astype(BF16)
            return lax.dot_general(pg, buf_ref[s, m0:m0 + rows, :], (((0,), (0,)), ((), ())),
                                   preferred_element_type=F32)

        if live is None:
            acc_ref[...] = gather_rows()
        else:
            @pl.when(live)
            def _(gather_rows=gather_rows):
                acc_ref[...] += gather_rows()

    x = x_ref[...]
    routed = acc_ref[...]
    xb = x.astype(BF16)
    h = _dot(xb, wsgu_ref[...])
    g = h[:, :EXPERT_DIM]
    shared = _dot((g * _sigmoid(g) * h[:, EXPERT_DIM:]).astype(BF16), wsd_ref[...])
    x2 = _layer_norm(ALPHA * x + (routed + shared), g2_ref[...], b2_ref[...])
    ple = _sigmoid(_dot(x2.astype(BF16), wpg_ref[...])) * _dot(_pick(pp_ref, ps_ref).astype(BF16), wpp_ref[...])
    y = _layer_norm(ALPHA * x2 + ple, g3_ref[...], b3_ref[...])

    @pl.when(t < NT_P)
    def _():
        op_ref[...] = y

    @pl.when(t >= NT_P)
    def _():
        os_ref[...] = y


def _combine(x1, slot_tab, gate_e, seg_lo, seg_hi, p_p, p_s, ys, seg, loc_off, glob_off, ws_gate, ws_up, ws_down,
             ln2_g, ln2_b, w_ple_gate, w_ple_proj, ln3_g, ln3_b):
    wsgu = jnp.concatenate([ws_gate, ws_up], axis=1).astype(BF16)
    wsd = ws_down.astype(BF16)
    wpg = w_ple_gate.astype(BF16)
    wpp = w_ple_proj.astype(BF16)
    row = lambda w: pl.BlockSpec((TM, w), lambda i, *_: (i, 0))
    full = lambda shape: pl.BlockSpec(shape, lambda i, *_: (0,) * len(shape))
    vec = full((1, D_MODEL))
    grid_spec = pltpu.PrefetchScalarGridSpec(
        num_scalar_prefetch=3,
        grid=(N_TILES,),
        in_specs=[row(D_MODEL), _ET_SPEC, _ET_SPEC, _SEG_SPEC, _SEG_SPEC] + _row2(PLE_DIM) + [
            pl.BlockSpec(memory_space=pl.ANY),
            full(wsgu.shape), full(wsd.shape), vec, vec, full(wpg.shape), full(wpp.shape), vec, vec],
        out_specs=_row2(D_MODEL),
        scratch_shapes=[pltpu.VMEM((2, M_T, D_MODEL), BF16), pltpu.VMEM((TM, D_MODEL), F32),
                        pltpu.SemaphoreType.DMA((2,))],
    )
    r = lambda a: a.reshape(1, -1)
    return pl.pallas_call(
        _combine_kernel,
        name="combine",
        grid_spec=grid_spec,
        out_shape=[jax.ShapeDtypeStruct((N_P, D_MODEL), F32), jax.ShapeDtypeStruct((N_SP, D_MODEL), F32)],
        compiler_params=_cparams(("arbitrary",)),
    )(seg, loc_off, glob_off, x1, slot_tab, gate_e, seg_lo, seg_hi, p_p, p_s, ys, wsgu, wsd, r(ln2_g), r(ln2_b),
      wpg, wpp, r(ln3_g), r(ln3_b))


def _pad_rows(a, rows):
    return jnp.concatenate([a, jnp.zeros((rows - a.shape[0],) + a.shape[1:], a.dtype)], axis=0)


def kernel(x_prompt, x_sample, cache_k_win, cache_v_win, state_conv, state_delta, p_prompt, p_sample, w_in, conv_w, a_log, dt_bias, dn_norm_g, w_out, ln1_g, ln1_b, w_router, router_bias, we_gate, we_up, we_down, ws_gate, ws_up, ws_down, ln2_g, ln2_b, w_ple_gate, w_ple_proj, ln3_g, ln3_b):
    x_p = x_prompt.reshape(N_P, D_MODEL)
    x_s = _pad_rows(x_sample.reshape(N_S, D_MODEL), N_SP)
    p_p = p_prompt[0].reshape(N_P, PLE_DIM)
    p_s = _pad_rows(p_sample[0].reshape(N_S, PLE_DIM), N_SP)

    qkv, dn_in, z, b_logit, a_logit = _project(x_p, x_s, w_in[0])

    att_p = _attn_prompt(qkv)
    qkv_s = qkv[N_P:N_REAL].reshape(N_SAMPLE_B, DEC_SEQ, 3 * ATTN_W)
    qkv_s8 = jnp.pad(qkv_s, ((0, 0), (0, ROWS_S - DEC_SEQ), (0, 0)))
    ck = cache_k_win[0].reshape(N_SAMPLE_B, W_BUF, ATTN_W)
    cv = cache_v_win[0].reshape(N_SAMPLE_B, W_BUF, ATTN_W)
    att_s = _attn_sample(qkv_s8[:, :, :ATTN_W], qkv_s8[:, :, ATTN_W:2 * ATTN_W], qkv_s8[:, :, 2 * ATTN_W:], ck, cv)
    att_s = _pad_rows(att_s[:, :DEC_SEQ].reshape(N_S, ATTN_W), N_SP)

    tails = dn_in.reshape(N_TILES, TM, CONV_CH)[:NT_P - 1, TM - HALO:]
    halo_p = jnp.concatenate([jnp.zeros((1, HALO, CONV_CH), F32), tails], axis=0)
    seq_start = (jnp.arange(NT_P) % (SEQ // TM) == 0)[:, None, None]
    halo_p = jnp.where(seq_start, 0.0, halo_p)
    q_p, k_p, v_p, bx_p, gx_p = _dn_pre(dn_in, halo_p, b_logit, a_logit, jnp.ones((N_P, 1), F32),
                                        conv_w[0], a_log[0], dt_bias[0])
    shp = (N_PROMPT_B, SEQ, DN_W)
    s0_p = jnp.zeros((N_PROMPT_B, N_HEADS // 2, LANES, LANES), F32)
    o_p, s_p = _delta(q_p.reshape(shp), k_p.reshape(shp), v_p.reshape(shp), bx_p.reshape(shp), gx_p.reshape(shp),
                      s0_p, N_PROMPT_B, 4)

    def seq_pad(tok, state=None):
        w = tok.shape[-1]
        tok = tok.reshape(N_SAMPLE_B, DEC_SEQ, w)
        head = jnp.zeros((N_SAMPLE_B, S_TOK0, w), F32)
        if state is not None:
            head = head.at[:, S_TOK0 - (CONV_W - 1):].set(state)
        tail = jnp.zeros((N_SAMPLE_B, SEQ_S - S_TOK0 - DEC_SEQ, w), F32)
        return jnp.concatenate([head, tok, tail], axis=1).reshape(N_SAMPLE_B * SEQ_S, w)

    dn_tok_s = dn_in[N_P:N_REAL]
    dn_s = seq_pad(dn_tok_s, state_conv[0])
    rows_s = N_SAMPLE_B * SEQ_S
    valid_s = seq_pad(jnp.ones((N_S, 1), F32))
    halo_s = jnp.zeros((rows_s // TM, HALO, CONV_CH), F32)
    q_s, k_s, v_s, bx_s, gx_s = _dn_pre(dn_s, halo_s, seq_pad(b_logit[N_P:N_REAL]), seq_pad(a_logit[N_P:N_REAL]),
                                        valid_s, conv_w[0], a_log[0], dt_bias[0])
    shs = (N_SAMPLE_B, SEQ_S, DN_W)
    o_s, s_s = _delta(q_s.reshape(shs), k_s.reshape(shs), v_s.reshape(shs), bx_s.reshape(shs), gx_s.reshape(shs),
                      _state_to_bd(state_delta[0]), 4, 1)
    dn_o_s = _pad_rows(o_s[:, S_TOK0:S_TOK0 + DEC_SEQ].reshape(N_S, DN_W), N_SP)

    x1 = _out_proj(x_p, x_s, att_p, att_s, o_p.reshape(N_P, DN_W), dn_o_s, z, dn_norm_g[0], w_out[0], ln1_g[0], ln1_b[0])

    slot_tab, gate_e, cnt = _route(x1, w_router[0], router_bias[0])
    seg, loc_off, glob_off, pad_seg, pad_off, bstart, bcount, nb, seg_lo, seg_hi = _segment_tables(cnt[:, 0, :])
    xs = _dispatch(x1, slot_tab, seg_lo, seg_hi, (seg, loc_off, glob_off, pad_seg, pad_off, nb))
    ys = _experts(xs, we_gate[0], we_up[0], we_down[0], bstart, bcount, nb)
    y_p, y_s = _combine(x1, slot_tab, gate_e, seg_lo, seg_hi, p_p, p_s, ys, seg, loc_off, glob_off,
                        ws_gate[0], ws_up[0], ws_down[0], ln2_g[0], ln2_b[0], w_ple_gate[0], w_ple_proj[0],
                        ln3_g[0], ln3_b[0])

    y_prompt = y_p.reshape(N_PROMPT_B, SEQ, D_MODEL)
    y_sample = y_s[:N_S].reshape(N_SAMPLE_B, DEC_SEQ, D_MODEL)
    heads = (N_HEADS, HEAD_DIM)
    win = lambda c0: jnp.stack([qkv[(b + 1) * SEQ - W_BUF:(b + 1) * SEQ, c0:c0 + ATTN_W] for b in range(N_PROMPT_B)])
    k_pr = win(ATTN_W).reshape(N_PROMPT_B, W_BUF, *heads)
    v_pr = win(2 * ATTN_W).reshape(N_PROMPT_B, W_BUF, *heads)
    conv_p = jnp.stack([dn_in[(b + 1) * SEQ - (CONV_W - 1):(b + 1) * SEQ] for b in range(N_PROMPT_B)])
    k_new = qkv_s[:, :, ATTN_W:2 * ATTN_W].reshape(N_SAMPLE_B, DEC_SEQ, *heads)
    v_new = qkv_s[:, :, 2 * ATTN_W:].reshape(N_SAMPLE_B, DEC_SEQ, *heads)
    def shifted(cache, new):
        base = jnp.pad(cache[0][:, DEC_SEQ:], ((0, 0), (0, DEC_SEQ), (0, 0), (0, 0)))
        return lax.dynamic_update_slice(base, new, (0, W_BUF - DEC_SEQ, 0, 0))

    k_sm = shifted(cache_k_win, k_new)
    v_sm = shifted(cache_v_win, v_new)
    conv_s = dn_tok_s.reshape(N_SAMPLE_B, DEC_SEQ, CONV_CH)[:, DEC_SEQ - (CONV_W - 1):]
    return (y_prompt, y_sample, k_pr[None], v_pr[None], conv_p[None], _state_from_bd(s_p)[None],
            k_sm[None], v_sm[None], conv_s[None], _state_from_bd(s_s)[None])
```

```python
import jax
import jax.numpy as jnp
import numpy as np
from jax import lax
from jax.experimental import pallas as pl
from jax.experimental.pallas import tpu as pltpu

F32 = jnp.float32
BF16 = jnp.bfloat16
I32 = jnp.int32

D_MODEL = 1024
N_PROMPT_B, SEQ = 2, 8192
N_SAMPLE_B, DEC_SEQ = 32, 4
W_BUF = 2048
N_HEADS = 8
HEAD_DIM = 64
ATTN_W = 512
CONV_CH = 1536
CONV_W = 4
DN_W = 512
N_EXPERTS = 64
TOP_K = 8
EXPERT_DIM = 256
PLE_DIM = 256
BRANCHES = ((128, 1), (512, 4), (2048, 16))
N_BACK = 128
ROUTED_SCALE = 2.5
LN_EPS = 1e-5
NORM_EPS = 1e-6
ALPHA = 2.0 ** 0.25

LANES = 128
HALF = 64
VMEM_LIMIT = 56 * 1024 * 1024

N_P = N_PROMPT_B * SEQ
N_S = N_SAMPLE_B * DEC_SEQ
N_REAL = N_P + N_S
N_PAD = N_P + 512
TM = 256
TB = 512
N_TILES = N_PAD // TM
NT_P = N_P // TM
N_SP = N_PAD - N_P

SEG_ALIGN = 16
M_T = 3072
M_CHUNK = 256
M_MAIN = 2560
BM = 512
EXPERT_SLOTS = 4
R_MAX = N_PAD * TOP_K + N_TILES * N_EXPERTS * SEG_ALIGN + N_EXPERTS * (BM - 1)
NB_MAX = -(-R_MAX // BM)

DN_C = 64
SEQ_S = 64
S_TOK0 = 8


def _dot(a, b):
    return jnp.dot(a, b, preferred_element_type=F32)


def _dot_nt(a, b):
    return lax.dot_general(a, b, (((1,), (1,)), ((), ())), preferred_element_type=F32)


def _split2(x):
    hi = x.astype(BF16)
    lo = (x - hi.astype(F32)).astype(BF16)
    return hi, lo


def _dot_l01(m01, x):
    hi, lo = _split2(x)
    return _dot(m01, hi) + _dot(m01, lo)


def _dot_r01(x, m01):
    hi, lo = _split2(x)
    return _dot(hi, m01) + _dot(lo, m01)


def _sigmoid(x):
    return 1.0 / (1.0 + jnp.exp(-x))


def _layer_norm(x, g, b):
    mu = jnp.mean(x, axis=-1, keepdims=True)
    xc = x - mu
    var = jnp.mean(xc * xc, axis=-1, keepdims=True)
    return xc * lax.rsqrt(var + LN_EPS) * g + b


def _cparams(sem=None):
    return pltpu.CompilerParams(dimension_semantics=sem, vmem_limit_bytes=VMEM_LIMIT)


def _full(shape):
    return pl.BlockSpec(shape, lambda *_: (0,) * len(shape))


def _row2(w, tile=TM):
    ntp = N_P // tile
    return [pl.BlockSpec((tile, w), lambda i, *_: (jnp.minimum(i, ntp - 1), 0)),
            pl.BlockSpec((tile, w), lambda i, *_: (jnp.maximum(i - ntp, 0), 0))]


def _pick(p_ref, s_ref):
    return jnp.where(pl.program_id(0) < N_P // p_ref.shape[0], p_ref[...], s_ref[...])


def _proj_kernel(xp_ref, xs_ref, wqkv_ref, wdn_ref, wz_ref, wg_ref, qkv_ref, dn_ref, z_ref, b_ref, a_ref):
    x = _pick(xp_ref, xs_ref)
    xh, xl = _split2(x)
    qkv_ref[...] = _dot(xh, wqkv_ref[...])
    dn_ref[...] = _dot(xh, wdn_ref[...])
    z_ref[...] = _dot(xh, wz_ref[...])
    ng = 2 * N_HEADS
    o1 = _dot(xh, wg_ref[...])
    o2 = _dot(xl, wg_ref[...])
    gates = o1[:, :ng] + o1[:, ng:2 * ng] + o2[:, :ng]
    b_ref[...] = gates[:, :N_HEADS]
    a_ref[...] = gates[:, N_HEADS:]


def _project(x_p, x_s, w_in):
    wqkv = w_in[:, :3 * ATTN_W].astype(BF16)
    wdn = w_in[:, 3 * ATTN_W:3 * ATTN_W + CONV_CH].astype(BF16)
    c0 = 3 * ATTN_W + CONV_CH
    wz = w_in[:, c0:c0 + DN_W].astype(BF16)
    wgh, wgl = _split2(w_in[:, c0 + DN_W:])
    wg = jnp.concatenate([wgh, wgl, jnp.zeros((D_MODEL, LANES - 4 * N_HEADS), BF16)], axis=1)
    row = lambda w: pl.BlockSpec((TB, w), lambda i: (i, 0))
    outs = (3 * ATTN_W, CONV_CH, DN_W, N_HEADS, N_HEADS)
    return pl.pallas_call(
        _proj_kernel,
        name="proj",
        grid=(N_PAD // TB,),
        in_specs=_row2(D_MODEL, TB) + [_full(wqkv.shape), _full(wdn.shape), _full(wz.shape), _full(wg.shape)],
        out_specs=[row(w) for w in outs],
        out_shape=[jax.ShapeDtypeStruct((N_PAD, w), F32) for w in outs],
        compiler_params=_cparams(("arbitrary",)),
    )(x_p, x_s, wqkv, wdn, wz, wg)


QT = 2048
ATTN_UNROLL = 8


def _attn_bias():
    qi = np.arange(N_BACK)[:, None]
    ki = np.arange(2 * N_BACK)[None, :]
    out = np.zeros((2, len(BRANCHES), N_HEADS, N_BACK, 2 * N_BACK), np.float32)
    for var, off in enumerate((N_BACK, 0)):
        dist = qi - ki + off
        valid = (dist >= 0) & (dist <= N_BACK)
        for di, (_, dil) in enumerate(BRANCHES):
            for h in range(N_HEADS):
                slope = 2.0 ** (-8.0 * (h + 1) / N_HEADS) * dil
                out[var, di, h] = np.where(valid, -slope * dist, -1e30)
    return out


def _attn_prompt_kernel(q_ref, k_ref, v_ref, bias_ref, o_ref, acc_ref, m_ref):
    qt = pl.program_id(2)
    lane = lax.broadcasted_iota(I32, (1, LANES), 1)
    h0 = lane < HALF
    for di, (_, dil) in enumerate(BRANCHES):
        per_r = QT // (N_BACK * dil)

        def body(i, carry, di=di, dil=dil, per_r=per_r):
            def ds(start, n):
                return pl.ds(start, n) if dil == 1 else pl.ds(start, n, stride=dil)

            blocks = []
            for u in range(ATTN_UNROLL):
                blk = i * ATTN_UNROLL + u
                r = blk // per_r
                jb = blk % per_r
                qloc = r + dil * N_BACK * jb
                first = jnp.logical_and(qt == 0, jb == 0)
                kstart = jnp.where(first, r, qt * QT + qloc - N_BACK * dil)
                blocks.append((qloc, kstart, jnp.where(first, 1, 0)))
            q8 = [q_ref[ds(qloc, N_BACK), :] * (HEAD_DIM ** -0.5) for qloc, _, _ in blocks]
            kb = [k_ref[ds(kstart, 2 * N_BACK), :].astype(BF16) for _, kstart, _ in blocks]
            vv = [v_ref[ds(kstart, 2 * N_BACK), :] for _, kstart, _ in blocks]
            heads = [(u, hh) for u in range(ATTN_UNROLL) for hh in range(2)]
            keep = lambda hh: h0 if hh == 0 else jnp.logical_not(h0)
            s = [_dot_nt(jnp.where(keep(hh), q8[u], 0.0).astype(BF16), kb[u]) + bias_ref[blocks[u][2], di, hh]
                 for u, hh in heads]
            m = [jnp.max(x, axis=-1, keepdims=True) for x in s]
            p = [jnp.exp(x - mm).astype(BF16) for x, mm in zip(s, m)]
            res = [_dot(pp, jnp.where(keep(hh), vv[u], 1.0).astype(BF16)) for pp, (u, hh) in zip(p, heads)]
            for u, (qloc, _, _) in enumerate(blocks):
                rows = ds(qloc, N_BACK)
                den = pltpu.roll(jnp.where(h0, res[2 * u + 1], res[2 * u]), HALF, axis=1)
                acc_ref[di, rows, :] = jnp.where(h0, res[2 * u], res[2 * u + 1]) / den
                m_ref[di, rows, :] = jnp.where(h0, m[2 * u], m[2 * u + 1]) + jnp.log(den)
            return carry

        lax.fori_loop(0, QT // (N_BACK * ATTN_UNROLL), body, 0)

    lse = m_ref[...]
    top = jnp.max(lse, axis=0)
    num = jnp.zeros((QT, LANES), F32)
    den = jnp.zeros((QT, LANES), F32)
    for di in range(len(BRANCHES)):
        w = jnp.exp(lse[di] - top)
        num = num + w * acc_ref[di]
        den = den + w
    o_ref[...] = num / den


def _attn_prompt(qkv):
    bias = jnp.asarray(_attn_bias())
    npair = N_HEADS // 2
    nqt = SEQ // QT
    kv_rows = SEQ
    q_spec = pl.BlockSpec((QT, LANES), lambda b, hp, t: (b * nqt + t, hp))
    k_spec = pl.BlockSpec((kv_rows, LANES), lambda b, hp, t: (b, npair + hp))
    v_spec = pl.BlockSpec((kv_rows, LANES), lambda b, hp, t: (b, 2 * npair + hp))
    bias_spec = pl.BlockSpec((2, len(BRANCHES), 2, N_BACK, 2 * N_BACK), lambda b, hp, t: (0, 0, hp, 0, 0))
    return pl.pallas_call(
        _attn_prompt_kernel,
        name="attn_prompt",
        grid=(N_PROMPT_B, npair, nqt),
        in_specs=[q_spec, k_spec, v_spec, bias_spec],
        out_specs=pl.BlockSpec((QT, LANES), lambda b, hp, t: (b * nqt + t, hp)),
        out_shape=jax.ShapeDtypeStruct((N_P, ATTN_W), F32),
        scratch_shapes=[pltpu.VMEM((len(BRANCHES), QT, LANES), F32)] * 2,
        compiler_params=_cparams(("arbitrary", "arbitrary", "arbitrary")),
    )(qkv, qkv, qkv, bias)


ROWS_S = 8


def _attn_sample_kernel(q_ref, kn_ref, vn_ref, ck_ref, cv_ref, e_ref, et_ref, slope_ref, o_ref, kall_ref, vall_ref):
    npair = N_HEADS // 2
    for hp in range(npair):
        sl = slice(hp * LANES, (hp + 1) * LANES)
        kall_ref[hp, 0:W_BUF, :] = ck_ref[0, :, sl]
        vall_ref[hp, 0:W_BUF, :] = cv_ref[0, :, sl]
        kall_ref[hp, W_BUF:W_BUF + ROWS_S, :] = kn_ref[0, :, sl]
        vall_ref[hp, W_BUF:W_BUF + ROWS_S, :] = vn_ref[0, :, sl]

    def rows(ref, start, n, dil):
        idx = pl.ds(start, n) if dil == 1 else pl.ds(start, n, stride=dil)
        return jnp.concatenate([ref[hp, idx, :] for hp in range(npair)], axis=1)

    e01 = e_ref[...]
    et01 = et_ref[...]
    slope = slope_ref[...]
    jrev = (N_BACK - lax.broadcasted_iota(I32, (N_BACK, 1), 0)).astype(F32)
    inew = lax.broadcasted_iota(I32, (ROWS_S, 1), 0)
    scale = HEAD_DIM ** -0.5
    knew = rows(kall_ref, W_BUF, ROWS_S, 1)
    vnew = rows(vall_ref, W_BUF, ROWS_S, 1).astype(BF16).astype(F32)

    def scores(kr, qt):
        ph, plo = _split2(kr * qt)
        return (_dot(ph, e01) + _dot(plo, e01)) * scale

    outs = []
    for t in range(DEC_SEQ):
        qt = q_ref[0, t:t + 1, :]
        s_self = jnp.where(inew == t, scores(knew, qt), -1e30)
        s_br = []
        v_br = []
        for _, dil in BRANCHES:
            start = W_BUF + t - N_BACK * dil
            s_br.append(scores(rows(kall_ref, start, N_BACK, dil), qt) - (slope * dil) * jrev)
            v_br.append(rows(vall_ref, start, N_BACK, dil))
        m = jnp.max(s_self, axis=0, keepdims=True)
        for s in s_br:
            m = jnp.maximum(m, jnp.max(s, axis=0, keepdims=True))
        pb = (jnp.exp(s_self - m) * float(len(BRANCHES))).astype(BF16)
        den = jnp.sum(pb.astype(F32), axis=0, keepdims=True)
        num = jnp.sum(_dot(pb, et01) * vnew, axis=0, keepdims=True)
        for s, vr in zip(s_br, v_br):
            pb = jnp.exp(s - m).astype(BF16)
            den = den + jnp.sum(pb.astype(F32), axis=0, keepdims=True)
            num = num + jnp.sum(_dot(pb, et01) * vr.astype(BF16).astype(F32), axis=0, keepdims=True)
        denx = _dot_r01(jnp.broadcast_to(den, (ROWS_S, N_HEADS)), et01)[0:1]
        outs.append(num / denx)
    pad = jnp.zeros((ROWS_S - DEC_SEQ, ATTN_W), F32)
    o_ref[0] = jnp.concatenate(outs + [pad], axis=0)


def _head_expand():
    e = np.zeros((ATTN_W, N_HEADS), np.float32)
    for h in range(N_HEADS):
        e[h * HEAD_DIM:(h + 1) * HEAD_DIM, h] = 1.0
    return e


def _attn_sample(q_s, k_s, v_s, cache_k, cache_v):
    e = _head_expand()
    e01 = jnp.asarray(e, BF16)
    et01 = jnp.asarray(e.T, BF16)
    slopes = jnp.asarray([[2.0 ** (-8.0 * (h + 1) / N_HEADS) for h in range(N_HEADS)]], F32)
    row = pl.BlockSpec((1, ROWS_S, ATTN_W), lambda b: (b, 0, 0))
    cache = pl.BlockSpec((1, W_BUF, ATTN_W), lambda b: (b, 0, 0))
    return pl.pallas_call(
        _attn_sample_kernel,
        name="attn_sample",
        grid=(N_SAMPLE_B,),
        in_specs=[row, row, row, cache, cache, _full(e01.shape), _full(et01.shape), _full(slopes.shape)],
        out_specs=row,
        out_shape=jax.ShapeDtypeStruct((N_SAMPLE_B, ROWS_S, ATTN_W), F32),
        scratch_shapes=[pltpu.VMEM((N_HEADS // 2, W_BUF + ROWS_S, LANES), F32)] * 2,
        compiler_params=_cparams(("arbitrary",)),
    )(q_s, k_s, v_s, cache_k, cache_v, e01, et01, slopes)


HALO = 8
TD = 512


def _dn_pre_kernel(dn_ref, halo_ref, b_ref, a_ref, valid_ref, cw_ref, alog_ref, dtb_ref, e_ref, et_ref, ltri_ref,
                   q_ref, k_ref, v_ref, bx_ref, gx_ref, buf_ref):
    ts = dn_ref.shape[0]
    buf_ref[0:HALO, :] = halo_ref[0]
    buf_ref[HALO:HALO + ts, :] = dn_ref[...]
    y = jnp.zeros((ts, CONV_CH), F32)
    for i in range(CONV_W):
        y = y + buf_ref[pl.ds(HALO - (CONV_W - 1) + i, ts), :] * cw_ref[i:i + 1, :]
    y = y * _sigmoid(y)
    valid = valid_ref[...]
    e01 = e_ref[...]
    et01 = et_ref[...]

    def l2n(t):
        ss = _dot_r01(t * t, e01)
        inv = lax.rsqrt(ss + NORM_EPS)
        return t * _dot_r01(inv, et01)

    q_ref[...] = l2n(y[:, :DN_W]) * (HEAD_DIM ** -0.5)
    k_ref[...] = l2n(y[:, DN_W:2 * DN_W]) * valid
    v_ref[...] = y[:, 2 * DN_W:] * valid
    beta = _sigmoid(b_ref[...]) * valid
    sp_in = a_ref[...] + dtb_ref[...]
    softplus = jnp.maximum(sp_in, 0.0) + jnp.log1p(jnp.exp(-jnp.abs(sp_in)))
    g = -jnp.exp(alog_ref[...]) * softplus * valid
    bx_ref[...] = _dot_r01(beta, et01)
    gx_ref[...] = _dot_l01(ltri_ref[...], _dot_r01(g, et01))


def _dn_pre(dn, halo, b_logit, a_logit, valid, conv_w, a_log, dt_bias):
    nt = halo.shape[0]
    rows = nt * TD
    e = _head_expand()
    e01 = jnp.asarray(e, BF16)
    et01 = jnp.asarray(e.T, BF16)
    ii = np.arange(TD)
    ltri = jnp.asarray(((ii[:, None] >= ii[None, :]) & (ii[:, None] // DN_C == ii[None, :] // DN_C)), BF16)
    row = lambda w: pl.BlockSpec((TD, w), lambda i: (i, 0))
    return pl.pallas_call(
        _dn_pre_kernel,
        name="dn_pre",
        grid=(nt,),
        in_specs=[row(CONV_CH), pl.BlockSpec((1, HALO, CONV_CH), lambda i: (i, 0, 0)), row(N_HEADS), row(N_HEADS),
                  row(1), _full(conv_w.shape), _full((1, N_HEADS)), _full((1, N_HEADS)),
                  _full(e01.shape), _full(et01.shape), _full(ltri.shape)],
        out_specs=[row(DN_W)] * 5,
        out_shape=[jax.ShapeDtypeStruct((rows, DN_W), F32)] * 5,
        scratch_shapes=[pltpu.VMEM((HALO + TD, CONV_CH), F32)],
        compiler_params=_cparams(("arbitrary",)),
    )(dn, halo, b_logit, a_logit, valid, conv_w, a_log.reshape(1, N_HEADS), dt_bias.reshape(1, N_HEADS),
      e01, et01, ltri)


def _delta_streams(q, k, v, bx, gcx, s_bd):
    c = DN_C
    n = len(q)
    each = range(n)
    lane = lax.broadcasted_iota(I32, (1, LANES), 1)
    h0 = lane < HALF
    m0 = h0.astype(F32)
    m1 = 1.0 - m0
    ii = lax.broadcasted_iota(I32, (2 * c, 2 * c), 0)
    jj = lax.broadcasted_iota(I32, (2 * c, 2 * c), 1)
    same = jnp.where(ii < c, 0, 1) == jnp.where(jj < c, 0, 1)
    order = jnp.where(same, ii - jj, -1)
    tril = order >= 0
    strict = order > 0
    eye = jnp.where(ii == jj, 1.0, 0.0)

    def stack(x):
        return jnp.concatenate([x * m0, x * m1], axis=0)

    def fold(x2):
        return x2[:c] + x2[c:]

    def decay_of(g):
        gsw = pltpu.roll(g, HALF, axis=1)
        row_b = jnp.concatenate([jnp.where(h0, g, gsw), jnp.where(h0, gsw, g)], axis=0)
        return jnp.where(tril, jnp.exp(jnp.where(tril, row_b - row_b.T, 0.0)), 0.0)

    eg = [jnp.exp(gcx[i]) for i in each]
    kb = [k[i] * bx[i] for i in each]
    glast = [gcx[i][c - 1:c, :] for i in each]
    decay = [decay_of(gcx[i]) for i in each]
    k2 = [stack(k[i]).astype(BF16) for i in each]
    lhs = [jnp.concatenate([stack(kb[i]), stack(q[i])], axis=0).astype(BF16) for i in each]
    aq = [_dot_nt(lhs[i], k2[i]) for i in each]
    a_low = [jnp.where(strict, aq[i][:2 * c] * decay[i], 0.0) for i in each]
    aqk = [jnp.where(tril, aq[i][2 * c:] * decay[i], 0.0).astype(BF16) for i in each]
    tinv = [eye - a_low[i] for i in each]
    apow = [a_low[i].astype(BF16) for i in each]
    for _ in range(5):
        apow = [_dot(apow[i], apow[i]).astype(BF16) for i in each]
        tinv = [tinv[i] + _dot(tinv[i].astype(BF16), apow[i]) for i in each]
    rhs = [jnp.concatenate([stack(v[i] * bx[i]), stack(kb[i] * eg[i])], axis=1).astype(BF16) for i in each]
    uw = [_dot(tinv[i].astype(BF16), rhs[i]) for i in each]
    u = [fold(uw[i][:, :LANES]) for i in each]
    w = [fold(uw[i][:, LANES:]) for i in each]
    wq = [jnp.concatenate([w[i], q[i] * eg[i]], axis=0).astype(BF16) for i in each]
    kd = [(k[i] * jnp.exp(glast[i] - gcx[i])).T.astype(BF16) for i in each]
    gt = [jnp.exp(glast[i]) for i in each]
    ns = len(s_bd)
    state = list(s_bd)
    o = [None] * n
    for sub in range(n // ns):
        js = [sub * ns + i for i in range(ns)]
        ws = [_dot(wq[j], state[i].astype(BF16)) for i, j in enumerate(js)]
        v_new = [u[j] - ws[i][:c] for i, j in enumerate(js)]
        intra = [_dot(aqk[j], stack(v_new[i]).astype(BF16)) for i, j in enumerate(js)]
        upd = [_dot(kd[j], v_new[i].astype(BF16)) for i, j in enumerate(js)]
        for i, j in enumerate(js):
            o[j] = ws[i][c:] + fold(intra[i])
            state[i] = state[i] * gt[j] + jnp.where(same, upd[i], 0.0)
    return o, state


def _delta_kernel(q_ref, k_ref, v_ref, bx_ref, gx_ref, s0_ref, o_ref, sout_ref, s_ref):
    ci = pl.program_id(1)
    nb = q_ref.shape[0]
    npair = N_HEADS // 2

    @pl.when(ci == 0)
    def _():
        s_ref[...] = s0_ref[...]

    streams = [(b, hp) for b in range(nb) for hp in range(npair)]
    chunks = [(slice(sub * DN_C, (sub + 1) * DN_C), b, slice(hp * LANES, (hp + 1) * LANES))
              for sub in range(q_ref.shape[1] // DN_C) for b, hp in streams]
    take = lambda ref: [ref[b, rows, lanes] for rows, b, lanes in chunks]
    o, s_new = _delta_streams(take(q_ref), take(k_ref), take(v_ref), take(bx_ref), take(gx_ref),
                              [s_ref[b, hp] for b, hp in streams])
    for (rows, b, lanes), oc in zip(chunks, o):
        o_ref[b, rows, lanes] = oc
    for (b, hp), sn in zip(streams, s_new):
        s_ref[b, hp] = sn

    @pl.when(ci == pl.num_programs(1) - 1)
    def _():
        sout_ref[...] = s_ref[...]


def _delta(q, k, v, bx, gx, s0_bd, bblk, nsub):
    nbatch, length, _ = q.shape
    npair = N_HEADS // 2
    seq = pl.BlockSpec((bblk, nsub * DN_C, DN_W), lambda g, c: (g, c, 0))
    st = pl.BlockSpec((bblk, npair, LANES, LANES), lambda g, c: (g, 0, 0, 0))
    return pl.pallas_call(
        _delta_kernel,
        name="delta",
        grid=(nbatch // bblk, length // (nsub * DN_C)),
        in_specs=[seq] * 5 + [st],
        out_specs=[seq, st],
        out_shape=[jax.ShapeDtypeStruct((nbatch, length, DN_W), F32),
                   jax.ShapeDtypeStruct((nbatch, npair, LANES, LANES), F32)],
        scratch_shapes=[pltpu.VMEM((bblk, npair, LANES, LANES), F32)],
        compiler_params=_cparams(("arbitrary", "arbitrary")),
    )(q, k, v, bx, gx, s0_bd)


def _state_to_bd(s):
    b = s.shape[0]
    s = s.reshape(b, N_HEADS // 2, 2, HALF, HALF)
    zero = jnp.zeros_like(s[:, :, 0])
    top = jnp.concatenate([s[:, :, 0], zero], axis=-1)
    bot = jnp.concatenate([zero, s[:, :, 1]], axis=-1)
    return jnp.concatenate([top, bot], axis=-2)


def _state_from_bd(s):
    b = s.shape[0]
    return jnp.stack([s[:, :, :HALF, :HALF], s[:, :, HALF:, HALF:]], axis=2).reshape(b, N_HEADS, HALF, HALF)


def _out_kernel(xp_ref, xs_ref, ap_ref, as_ref, dp_ref, ds_ref, z_ref, ng_ref, bavg_ref, wa_ref, wd_ref, g_ref, b_ref,
                o_ref):
    o = _pick(dp_ref, ds_ref)
    oh, ol = _split2(o * o)
    ms = _dot(oh, bavg_ref[...]) + _dot(ol, bavg_ref[...])
    z = z_ref[...]
    dn = o * lax.rsqrt(ms + NORM_EPS) * ng_ref[...] * (z * _sigmoid(z))
    mix = _dot(_pick(ap_ref, as_ref).astype(BF16), wa_ref[...]) + _dot(dn.astype(BF16), wd_ref[...])
    o_ref[...] = _layer_norm(ALPHA * _pick(xp_ref, xs_ref) + mix, g_ref[...], b_ref[...])


def _out_proj(x_p, x_s, att_p, att_s, dn_p, dn_s, z, norm_g, w_out, g, b):
    wa = w_out[:ATTN_W].astype(BF16)
    wd = w_out[ATTN_W:].astype(BF16)
    normg = jnp.tile(norm_g.reshape(1, HEAD_DIM), (1, N_HEADS))
    bavg = jnp.asarray(np.kron(np.eye(N_HEADS), np.ones((HALF, HALF))) / HALF, BF16)
    row = lambda w: pl.BlockSpec((TB, w), lambda i: (i, 0))
    return pl.pallas_call(
        _out_kernel,
        name="out_proj",
        grid=(N_PAD // TB,),
        in_specs=_row2(D_MODEL, TB) + _row2(ATTN_W, TB) + _row2(DN_W, TB) + [
            row(DN_W), _full(normg.shape), _full(bavg.shape),
            _full(wa.shape), _full(wd.shape), _full((1, D_MODEL)), _full((1, D_MODEL))],
        out_specs=row(D_MODEL),
        out_shape=jax.ShapeDtypeStruct((N_PAD, D_MODEL), F32),
        compiler_params=_cparams(("arbitrary",)),
    )(x_p, x_s, att_p, att_s, dn_p, dn_s, z, normg, bavg, wa, wd, g.reshape(1, -1), b.reshape(1, -1))


N_GROUPS = 8
GROUP = N_EXPERTS // N_GROUPS
SLOT_RADIX = 64
_ET_SPEC = pl.BlockSpec((1, 2 * N_EXPERTS, TM), lambda i, *_: (i, 0, 0))
_SEG_SPEC = pl.BlockSpec((1, 1, 2 * N_EXPERTS), lambda i, *_: (i, 0, 0))
TOPK_GROUPS = 4
NEG = -jnp.inf


def _route_kernel(x_ref, wh_ref, wl_ref, bias_ref, upper_ref, lows_ref, st_ref, ge_ref, cnt_ref):
    t = pl.program_id(0)
    xh, xl = _split2(x_ref[...])
    wh = wh_ref[...]
    logits = _dot_nt(wh, xh) + _dot_nt(wh, xl) + _dot_nt(wl_ref[...], xh)
    scores = _sigmoid(logits)
    choice = scores + bias_ref[...]
    i8 = lax.broadcasted_iota(I32, (GROUP, TM), 0)

    def first_max(vals):
        m = jnp.max(vals, axis=0, keepdims=True)
        idx = jnp.min(jnp.where(vals == m, i8, GROUP), axis=0, keepdims=True)
        return m, idx

    pieces = [choice[g * GROUP:(g + 1) * GROUP, :] for g in range(N_GROUPS)]
    gsc = jnp.zeros((N_GROUPS, TM), F32)
    for g in range(N_GROUPS):
        m1, idx1 = first_max(pieces[g])
        m2 = jnp.max(jnp.where(i8 == idx1, NEG, pieces[g]), axis=0, keepdims=True)
        gsc = jnp.where(i8 == g, m1 + m2, gsc)
    gsel = jnp.zeros((N_GROUPS, TM), F32)
    for _ in range(TOPK_GROUPS):
        _, idx = first_max(gsc)
        hit = i8 == idx
        gsel = jnp.where(hit, 1.0, gsel)
        gsc = jnp.where(hit, NEG, gsc)
    masked = [jnp.where(gsel[g:g + 1, :] > 0.5, pieces[g], NEG) for g in range(N_GROUPS)]
    eidx = [i8 + g * GROUP for g in range(N_GROUPS)]
    member = [jnp.zeros((GROUP, TM), F32) for _ in range(N_GROUPS)]
    for _ in range(TOP_K):
        mm = masked[0]
        for g in range(1, N_GROUPS):
            mm = jnp.maximum(mm, masked[g])
        m = jnp.max(mm, axis=0, keepdims=True)
        cand = jnp.where(masked[0] == m, eidx[0], N_EXPERTS)
        for g in range(1, N_GROUPS):
            cand = jnp.minimum(cand, jnp.where(masked[g] == m, eidx[g], N_EXPERTS))
        idx = jnp.min(cand, axis=0, keepdims=True)
        for g in range(N_GROUPS):
            hit = eidx[g] == idx
            member[g] = jnp.where(hit, 1.0, member[g])
            masked[g] = jnp.where(hit, NEG, masked[g])

    tok = t * TM + lax.broadcasted_iota(I32, (1, TM), 1)
    real = jnp.where(tok < N_REAL, 1.0, 0.0)
    mem = jnp.concatenate(member, axis=0) * real
    sel = mem * scores
    gate = (sel / jnp.sum(sel + (1.0 - real), axis=0, keepdims=True) * ROUTED_SCALE).astype(BF16)
    ge_ref[0] = jnp.concatenate([gate, jnp.zeros_like(gate)], axis=0)
    memb = mem.astype(BF16)
    rank = _dot(memb, upper_ref[...])
    cnt = jnp.sum(mem, axis=1, keepdims=True)
    cpad = jnp.maximum(jnp.floor((cnt + (SEG_ALIGN - 1)) * (1.0 / SEG_ALIGN)), 1.0)
    loc = _dot(lows_ref[...], jnp.broadcast_to(cpad, (N_EXPERTS, TM)).astype(BF16)) * SEG_ALIGN
    slot_e = jnp.where(mem > 0.5, loc + rank, float(M_T))
    slot_hi = jnp.floor(slot_e * (1.0 / SLOT_RADIX)) * SLOT_RADIX
    st_ref[0] = jnp.concatenate([slot_hi, slot_e - slot_hi + 1.0], axis=0).astype(BF16)
    ones = jnp.ones((GROUP, TM), BF16)
    cnt_ref[0] = _dot_nt(ones, memb).astype(I32)


def _route(x1, w_router, router_bias):
    wt = w_router.T
    wh, wl = _split2(wt)
    ii = np.arange(TM)
    upper = jnp.asarray(ii[:, None] < ii[None, :], BF16)
    ee = np.arange(N_EXPERTS)
    lows = jnp.asarray(ee[:, None] > ee[None, :], BF16)
    return pl.pallas_call(
        _route_kernel,
        name="route",
        grid=(N_TILES,),
        in_specs=[pl.BlockSpec((TM, D_MODEL), lambda i: (i, 0)), _full(wh.shape), _full(wl.shape),
                  _full((N_EXPERTS, 1)), _full(upper.shape), _full(lows.shape)],
        out_specs=[_ET_SPEC, _ET_SPEC, pl.BlockSpec((1, GROUP, N_EXPERTS), lambda i: (i, 0, 0))],
        out_shape=[jax.ShapeDtypeStruct((N_TILES, 2 * N_EXPERTS, TM), BF16)] * 2 + [
            jax.ShapeDtypeStruct((N_TILES, GROUP, N_EXPERTS), I32)],
        compiler_params=_cparams(("arbitrary",)),
    )(x1, wh, wl, router_bias.reshape(N_EXPERTS, 1), upper, lows)


def _segment_tables(cnt):
    seg = jnp.maximum((cnt + SEG_ALIGN - 1) // SEG_ALIGN, 1)
    cpad = seg * SEG_ALIGN
    loc_off = jnp.cumsum(cpad, axis=1) - cpad
    rows_e = jnp.sum(cpad, axis=0)
    blocks_e = (rows_e + BM - 1) // BM
    bend = jnp.cumsum(blocks_e)
    bstart = bend - blocks_e
    glob_off = (bstart * BM)[None, :] + jnp.cumsum(cpad, axis=0) - cpad
    nb = bend[-1]
    pad_off = bstart * BM + rows_e
    pad_seg = (blocks_e * BM - rows_e) // SEG_ALIGN
    i32 = lambda a: a.reshape(-1).astype(I32)
    bounds = lambda a: jnp.tile(a.astype(F32), (1, 2)).reshape(N_TILES, 1, 2 * N_EXPERTS)
    return (i32(seg), i32(loc_off), i32(glob_off), i32(pad_seg), i32(pad_off), i32(bstart), i32(blocks_e), i32(nb),
            bounds(loc_off), bounds(loc_off + cpad))


def _rows_copy(src_of, dst_of, lo, go, rows, sem):
    lo, go, rows = (v if isinstance(v, int) else pl.multiple_of(v, SEG_ALIGN) for v in (lo, go, rows))
    return pltpu.make_async_copy(src_of(lo, go, rows), dst_of(lo, go, rows), sem)


def _segment_copies(seg_ref, loc_ref, glob_ref, t, src_of, dst_of, sem):
    for e in range(N_EXPERTS):
        i = t * N_EXPERTS + e
        _rows_copy(src_of, dst_of, loc_ref[i], glob_ref[i], seg_ref[i] * SEG_ALIGN, sem).start()


def _wait_rows(src_of, dst_of, rows, sem):
    _rows_copy(src_of, dst_of, 0, 0, rows, sem).wait()


def _local_rows(m0, rows, lo, hi, slot_tab):
    mcol = (lax.broadcasted_iota(I32, (rows, 2 * N_EXPERTS), 0) + m0).astype(F32)
    e01 = jnp.where(mcol >= lo, jnp.where(mcol < hi, 1.0, 0.0), 0.0).astype(BF16)
    mrow = (lax.broadcasted_iota(I32, (rows, TM), 0) + (m0 + 1)).astype(F32)
    return _dot(e01, slot_tab) == mrow, e01


def _row_spans(used):
    yield None, 0, M_MAIN
    for m0 in range(M_MAIN, M_T, M_CHUNK):
        yield m0 < used, m0, M_CHUNK


def _tile_rows(seg_ref, loc_ref, t):
    last = t * N_EXPERTS + N_EXPERTS - 1
    return loc_ref[last] + seg_ref[last] * SEG_ALIGN


def _dispatch_kernel(seg_ref, loc_ref, glob_ref, pseg_ref, poff_ref, nb_ref, x_ref, st_ref, lo_ref, hi_ref,
                     xs_ref, buf_ref, sem):
    t = pl.program_id(0)
    s = t % 2
    used = _tile_rows(seg_ref, loc_ref, t)
    for live, m0, rows in _row_spans(used):
        def sort_rows(m0=m0, rows=rows):
            hit, _ = _local_rows(m0, rows, lo_ref[0], hi_ref[0], st_ref[0])
            onehot = jnp.where(hit, 1.0, 0.0).astype(BF16)
            buf_ref[s, m0:m0 + rows, :] = _dot(onehot, x_ref[...].astype(BF16)).astype(BF16)

        if live is None:
            sort_rows()
        else:
            pl.when(live)(sort_rows)

    src = lambda lo, go, rows: buf_ref.at[s, pl.ds(lo, rows), :]
    dst = lambda lo, go, rows: xs_ref.at[pl.ds(go, rows), :]
    _segment_copies(seg_ref, loc_ref, glob_ref, t, src, dst, sem.at[s])

    @pl.when(t > 0)
    def _():
        prev = lambda lo, go, rows: buf_ref.at[1 - s, pl.ds(lo, rows), :]
        _wait_rows(prev, dst, _tile_rows(seg_ref, loc_ref, t - 1), sem.at[1 - s])

    @pl.when(t == N_TILES - 1)
    def _():
        _wait_rows(src, dst, used, sem.at[s])
        buf_ref[s, 0:BM, :] = jnp.zeros((BM, D_MODEL), BF16)
        zeros = lambda lo, go, rows: buf_ref.at[s, pl.ds(0, rows), :]

        def fill(wait):
            def pad(e, carry):
                k = pseg_ref[e]

                @pl.when(k > 0)
                def _():
                    cp = _rows_copy(zeros, dst, 0, poff_ref[e], k * SEG_ALIGN, sem.at[s])
                    cp.wait() if wait else cp.start()

                return carry

            lax.fori_loop(0, N_EXPERTS, pad, 0)

            def tail(b, carry):
                cp = _rows_copy(zeros, dst, 0, b * BM, BM, sem.at[s])
                cp.wait() if wait else cp.start()
                return carry

            lax.fori_loop(nb_ref[0], NB_MAX, tail, 0)

        fill(False)
        fill(True)


def _dispatch(x1, slot_tab, seg_lo, seg_hi, tables):
    grid_spec = pltpu.PrefetchScalarGridSpec(
        num_scalar_prefetch=len(tables),
        grid=(N_TILES,),
        in_specs=[pl.BlockSpec((TM, D_MODEL), lambda i, *_: (i, 0)), _ET_SPEC, _SEG_SPEC, _SEG_SPEC],
        out_specs=pl.BlockSpec(memory_space=pl.ANY),
        scratch_shapes=[pltpu.VMEM((2, M_T, D_MODEL), BF16), pltpu.SemaphoreType.DMA((2,))],
    )
    return pl.pallas_call(
        _dispatch_kernel,
        name="dispatch",
        grid_spec=grid_spec,
        out_shape=jax.ShapeDtypeStruct((NB_MAX * BM, D_MODEL), BF16),
        compiler_params=_cparams(("arbitrary",)),
    )(*tables, x1, slot_tab, seg_lo, seg_hi)


def _expert_kernel(bstart_ref, bcount_ref, nb_ref, xs_ref, wg_ref, wu_ref, wd_ref, ys_ref,
                   xbuf, ybuf, wgu_s, wd_s, sem_in, sem_out):
    e = pl.program_id(0)
    nb = nb_ref[0]
    b0 = bstart_ref[e]
    n = bcount_ref[e]

    def rows_of(ref, g):
        return ref.at[pl.ds(pl.multiple_of(g * BM, BM), BM), :]

    def x_copy(g):
        return pltpu.make_async_copy(rows_of(xs_ref, g), xbuf.at[g % EXPERT_SLOTS], sem_in.at[g % EXPERT_SLOTS])

    def y_copy(g):
        return pltpu.make_async_copy(ybuf.at[g % EXPERT_SLOTS], rows_of(ys_ref, g), sem_out.at[g % EXPERT_SLOTS])

    @pl.when(e == 0)
    def _():
        for g in range(EXPERT_SLOTS - 1):
            @pl.when(g < nb)
            def _(g=g):
                x_copy(g).start()

    wgu_s[:, :EXPERT_DIM] = wg_ref[0].astype(BF16)
    wgu_s[:, EXPERT_DIM:] = wu_ref[0].astype(BF16)
    wd_s[...] = wd_ref[0].astype(BF16)

    def block(i, carry):
        g = b0 + i
        slot = g % EXPERT_SLOTS

        @pl.when(g + EXPERT_SLOTS - 1 < nb)
        def _():
            x_copy(g + EXPERT_SLOTS - 1).start()

        x_copy(g).wait()

        @pl.when(g >= EXPERT_SLOTS)
        def _():
            y_copy(g - EXPERT_SLOTS).wait()

        h = _dot(xbuf[slot], wgu_s[...])
        gate = h[:, :EXPERT_DIM]
        a = (gate * _sigmoid(gate) * h[:, EXPERT_DIM:]).astype(BF16)
        ybuf[slot] = _dot(a, wd_s[...]).astype(BF16)
        y_copy(g).start()
        return carry

    lax.fori_loop(0, n, block, 0)

    @pl.when(e == N_EXPERTS - 1)
    def _():
        for k in range(EXPERT_SLOTS, 0, -1):
            @pl.when(nb >= k)
            def _(k=k):
                y_copy(nb - k).wait()

        ybuf[0] = jnp.zeros((BM, D_MODEL), BF16)

        def tail(wait):
            def body(g, carry):
                cp = pltpu.make_async_copy(ybuf.at[0], rows_of(ys_ref, g), sem_out.at[0])
                cp.wait() if wait else cp.start()
                return carry

            lax.fori_loop(nb, NB_MAX, body, 0)

        tail(False)
        tail(True)


def _experts(xs, we_gate, we_up, we_down, bstart, bcount, nb):
    weight = lambda shape: pl.BlockSpec((1,) + shape, lambda e, *_: (e, 0, 0))
    grid_spec = pltpu.PrefetchScalarGridSpec(
        num_scalar_prefetch=3,
        grid=(N_EXPERTS,),
        in_specs=[pl.BlockSpec(memory_space=pl.ANY), weight((D_MODEL, EXPERT_DIM)), weight((D_MODEL, EXPERT_DIM)),
                  weight((EXPERT_DIM, D_MODEL))],
        out_specs=pl.BlockSpec(memory_space=pl.ANY),
        scratch_shapes=[pltpu.VMEM((EXPERT_SLOTS, BM, D_MODEL), BF16), pltpu.VMEM((EXPERT_SLOTS, BM, D_MODEL), BF16),
                        pltpu.VMEM((D_MODEL, 2 * EXPERT_DIM), BF16), pltpu.VMEM((EXPERT_DIM, D_MODEL), BF16),
                        pltpu.SemaphoreType.DMA((EXPERT_SLOTS,)), pltpu.SemaphoreType.DMA((EXPERT_SLOTS,))],
    )
    return pl.pallas_call(
        _expert_kernel,
        name="expert",
        grid_spec=grid_spec,
        out_shape=jax.ShapeDtypeStruct((NB_MAX * BM, D_MODEL), BF16),
        compiler_params=_cparams(("arbitrary",)),
    )(bstart, bcount, nb, xs, we_gate, we_up, we_down)


def _combine_kernel(seg_ref, loc_ref, glob_ref, x_ref, st_ref, ge_ref, lo_ref, hi_ref, pp_ref, ps_ref, ys_ref,
                    wsgu_ref, wsd_ref, g2_ref, b2_ref, wpg_ref, wpp_ref, g3_ref, b3_ref, op_ref, os_ref,
                    buf_ref, acc_ref, sem):
    t = pl.program_id(0)
    s = t % 2
    used = _tile_rows(seg_ref, loc_ref, t)
    src = lambda lo, go, rows: ys_ref.at[pl.ds(go, rows), :]
    into = lambda slot: (lambda lo, go, rows: buf_ref.at[slot, pl.ds(lo, rows), :])

    @pl.when(t == 0)
    def _():
        buf_ref[...] = jnp.zeros((2, M_T, D_MODEL), BF16)
        _segment_copies(seg_ref, loc_ref, glob_ref, t, src, into(s), sem.at[s])

    @pl.when(t + 1 < N_TILES)
    def _():
        _segment_copies(seg_ref, loc_ref, glob_ref, t + 1, src, into(1 - s), sem.at[1 - s])

    _wait_rows(src, into(s), used, sem.at[s])

    for live, m0, rows in _row_spans(used):
        def gather_rows(m0=m0, rows=rows):
            hit, e01 = _local_rows(m0, rows, lo_ref[0], hi_ref[0], st_ref[0])
            pg = jnp.where(hit, _dot(e01, ge_ref[0]), 0.0).astype(BF16)
            return lax.dot_general(pg, buf_ref[s, m0:m0 + rows, :], (((0,), (0,)), ((), ())),
                                   preferred_element_type=F32)

        if live is None:
            acc_ref[...] = gather_rows()
        else:
            @pl.when(live)
            def _(gather_rows=gather_rows):
                acc_ref[...] += gather_rows()

    x = x_ref[...]
    routed = acc_ref[...]
    xb = x.astype(BF16)
    h = _dot(xb, wsgu_ref[...])
    g = h[:, :EXPERT_DIM]
    shared = _dot((g * _sigmoid(g) * h[:, EXPERT_DIM:]).astype(BF16), wsd_ref[...])
    x2 = _layer_norm(ALPHA * x + (routed + shared), g2_ref[...], b2_ref[...])
    ple = _sigmoid(_dot(x2.astype(BF16), wpg_ref[...])) * _dot(_pick(pp_ref, ps_ref).astype(BF16), wpp_ref[...])
    y = _layer_norm(ALPHA * x2 + ple, g3_ref[...], b3_ref[...])

    @pl.when(t < NT_P)
    def _():
        op_ref[...] = y

    @pl.when(t >= NT_P)
    def _():
        os_ref[...] = y


def _combine(x1, slot_tab, gate_e, seg_lo, seg_hi, p_p, p_s, ys, seg, loc_off, glob_off, ws_gate, ws_up, ws_down,
             ln2_g, ln2_b, w_ple_gate, w_ple_proj, ln3_g, ln3_b):
    wsgu = jnp.concatenate([ws_gate, ws_up], axis=1).astype(BF16)
    wsd = ws_down.astype(BF16)
    wpg = w_ple_gate.astype(BF16)
    wpp = w_ple_proj.astype(BF16)
    row = lambda w: pl.BlockSpec((TM, w), lambda i, *_: (i, 0))
    full = lambda shape: pl.BlockSpec(shape, lambda i, *_: (0,) * len(shape))
    vec = full((1, D_MODEL))
    grid_spec = pltpu.PrefetchScalarGridSpec(
        num_scalar_prefetch=3,
        grid=(N_TILES,),
        in_specs=[row(D_MODEL), _ET_SPEC, _ET_SPEC, _SEG_SPEC, _SEG_SPEC] + _row2(PLE_DIM) + [
            pl.BlockSpec(memory_space=pl.ANY),
            full(wsgu.shape), full(wsd.shape), vec, vec, full(wpg.shape), full(wpp.shape), vec, vec],
        out_specs=_row2(D_MODEL),
        scratch_shapes=[pltpu.VMEM((2, M_T, D_MODEL), BF16), pltpu.VMEM((TM, D_MODEL), F32),
                        pltpu.SemaphoreType.DMA((2,))],
    )
    r = lambda a: a.reshape(1, -1)
    return pl.pallas_call(
        _combine_kernel,
        name="combine",
        grid_spec=grid_spec,
        out_shape=[jax.ShapeDtypeStruct((N_P, D_MODEL), F32), jax.ShapeDtypeStruct((N_SP, D_MODEL), F32)],
        compiler_params=_cparams(("arbitrary",)),
    )(seg, loc_off, glob_off, x1, slot_tab, gate_e, seg_lo, seg_hi, p_p, p_s, ys, wsgu, wsd, r(ln2_g), r(ln2_b),
      wpg, wpp, r(ln3_g), r(ln3_b))


def _pad_rows(a, rows):
    return jnp.concatenate([a, jnp.zeros((rows - a.shape[0],) + a.shape[1:], a.dtype)], axis=0)


def kernel(x_prompt, x_sample, cache_k_win, cache_v_win, state_conv, state_delta, p_prompt, p_sample, w_in, conv_w, a_log, dt_bias, dn_norm_g, w_out, ln1_g, ln1_b, w_router, router_bias, we_gate, we_up, we_down, ws_gate, ws_up, ws_down, ln2_g, ln2_b, w_ple_gate, w_ple_proj, ln3_g, ln3_b):
    x_p = x_prompt.reshape(N_P, D_MODEL)
    x_s = _pad_rows(x_sample.reshape(N_S, D_MODEL), N_SP)
    p_p = p_prompt[0].reshape(N_P, PLE_DIM)
    p_s = _pad_rows(p_sample[0].reshape(N_S, PLE_DIM), N_SP)

    qkv, dn_in, z, b_logit, a_logit = _project(x_p, x_s, w_in[0])

    att_p = _attn_prompt(qkv)
    qkv_s = qkv[N_P:N_REAL].reshape(N_SAMPLE_B, DEC_SEQ, 3 * ATTN_W)
    qkv_s8 = jnp.pad(qkv_s, ((0, 0), (0, ROWS_S - DEC_SEQ), (0, 0)))
    ck = cache_k_win[0].reshape(N_SAMPLE_B, W_BUF, ATTN_W)
    cv = cache_v_win[0].reshape(N_SAMPLE_B, W_BUF, ATTN_W)
    att_s = _attn_sample(qkv_s8[:, :, :ATTN_W], qkv_s8[:, :, ATTN_W:2 * ATTN_W], qkv_s8[:, :, 2 * ATTN_W:], ck, cv)
    att_s = _pad_rows(att_s[:, :DEC_SEQ].reshape(N_S, ATTN_W), N_SP)

    tails = dn_in.reshape(N_PAD // TD, TD, CONV_CH)[:N_P // TD - 1, TD - HALO:]
    halo_p = jnp.concatenate([jnp.zeros((1, HALO, CONV_CH), F32), tails], axis=0)
    seq_start = (jnp.arange(N_P // TD) % (SEQ // TD) == 0)[:, None, None]
    halo_p = jnp.where(seq_start, 0.0, halo_p)
    q_p, k_p, v_p, bx_p, gx_p = _dn_pre(dn_in, halo_p, b_logit, a_logit, jnp.ones((N_P, 1), F32),
                                        conv_w[0], a_log[0], dt_bias[0])
    shp = (N_PROMPT_B, SEQ, DN_W)
    s0_p = jnp.zeros((N_PROMPT_B, N_HEADS // 2, LANES, LANES), F32)
    o_p, s_p = _delta(q_p.reshape(shp), k_p.reshape(shp), v_p.reshape(shp), bx_p.reshape(shp), gx_p.reshape(shp),
                      s0_p, N_PROMPT_B, 4)

    def seq_pad(tok, state=None):
        w = tok.shape[-1]
        tok = tok.reshape(N_SAMPLE_B, DEC_SEQ, w)
        head = jnp.zeros((N_SAMPLE_B, S_TOK0, w), F32)
        if state is not None:
            head = head.at[:, S_TOK0 - (CONV_W - 1):].set(state)
        tail = jnp.zeros((N_SAMPLE_B, SEQ_S - S_TOK0 - DEC_SEQ, w), F32)
        return jnp.concatenate([head, tok, tail], axis=1).reshape(N_SAMPLE_B * SEQ_S, w)

    dn_tok_s = dn_in[N_P:N_REAL]
    dn_s = seq_pad(dn_tok_s, state_conv[0])
    rows_s = N_SAMPLE_B * SEQ_S
    valid_s = seq_pad(jnp.ones((N_S, 1), F32))
    halo_s = jnp.zeros((rows_s // TD, HALO, CONV_CH), F32)
    q_s, k_s, v_s, bx_s, gx_s = _dn_pre(dn_s, halo_s, seq_pad(b_logit[N_P:N_REAL]), seq_pad(a_logit[N_P:N_REAL]),
                                        valid_s, conv_w[0], a_log[0], dt_bias[0])
    shs = (N_SAMPLE_B, SEQ_S, DN_W)
    o_s, s_s = _delta(q_s.reshape(shs), k_s.reshape(shs), v_s.reshape(shs), bx_s.reshape(shs), gx_s.reshape(shs),
                      _state_to_bd(state_delta[0]), 4, 1)
    dn_o_s = _pad_rows(o_s[:, S_TOK0:S_TOK0 + DEC_SEQ].reshape(N_S, DN_W), N_SP)

    x1 = _out_proj(x_p, x_s, att_p, att_s, o_p.reshape(N_P, DN_W), dn_o_s, z, dn_norm_g[0], w_out[0], ln1_g[0], ln1_b[0])

    slot_tab, gate_e, cnt = _route(x1, w_router[0], router_bias[0])
    seg, loc_off, glob_off, pad_seg, pad_off, bstart, bcount, nb, seg_lo, seg_hi = _segment_tables(cnt[:, 0, :])
    xs = _dispatch(x1, slot_tab, seg_lo, seg_hi, (seg, loc_off, glob_off, pad_seg, pad_off, nb))
    ys = _experts(xs, we_gate[0], we_up[0], we_down[0], bstart, bcount, nb)
    y_p, y_s = _combine(x1, slot_tab, gate_e, seg_lo, seg_hi, p_p, p_s, ys, seg, loc_off, glob_off,
                        ws_gate[0], ws_up[0], ws_down[0], ln2_g[0], ln2_b[0], w_ple_gate[0], w_ple_proj[0],
                        ln3_g[0], ln3_b[0])

    y_prompt = y_p.reshape(N_PROMPT_B, SEQ, D_MODEL)
    y_sample = y_s[:N_S].reshape(N_SAMPLE_B, DEC_SEQ, D_MODEL)
    heads = (N_HEADS, HEAD_DIM)
    win = lambda c0: jnp.stack([qkv[(b + 1) * SEQ - W_BUF:(b + 1) * SEQ, c0:c0 + ATTN_W] for b in range(N_PROMPT_B)])
    k_pr = win(ATTN_W).reshape(N_PROMPT_B, W_BUF, *heads)
    v_pr = win(2 * ATTN_W).reshape(N_PROMPT_B, W_BUF, *heads)
    conv_p = jnp.stack([dn_in[(b + 1) * SEQ - (CONV_W - 1):(b + 1) * SEQ] for b in range(N_PROMPT_B)])
    k_new = qkv_s[:, :, ATTN_W:2 * ATTN_W].reshape(N_SAMPLE_B, DEC_SEQ, *heads)
    v_new = qkv_s[:, :, 2 * ATTN_W:].reshape(N_SAMPLE_B, DEC_SEQ, *heads)
    k_sm = jnp.concatenate([cache_k_win[0][:, DEC_SEQ:], k_new], axis=1)
    v_sm = jnp.concatenate([cache_v_win[0][:, DEC_SEQ:], v_new], axis=1)
    conv_s = dn_tok_s.reshape(N_SAMPLE_B, DEC_SEQ, CONV_CH)[:, DEC_SEQ - (CONV_W - 1):]
    return (y_prompt, y_sample, k_pr[None], v_pr[None], conv_p[None], _state_from_bd(s_p)[None],
            k_sm[None], v_sm[None], conv_s[None], _state_from_bd(s_s)[None])
```
